```python
import jax, jax.numpy as jnp
from jax import lax
import numpy as np

D_MODEL = 1024
BATCH = 8
SEQ = 2048
DEPTH = 2
DEC_BATCH = 128
DEC_SEQ = 8
PAST_LEN = 16384
PAGE_SIZE = 128

D_PLE = 256
D_FF = 2816
D_HALF = D_MODEL // 2
H_A = 4
DK_A = D_HALF // H_A
DV_A = D_HALF // H_A
H_B = 4
DK_B = D_HALF // 2 // H_B
DV_B = D_HALF // H_B
GLA_RANK = 16
GLA_GATE_NORM = 16.0
H_C = 4
DK_C = D_MODEL // H_C
DV_C = 2 * D_MODEL // H_C
ROPE_THETA = 10000.0
CHUNK = 64
EPS = 1e-6
N_AB_LAYERS = (DEPTH + 1) // 2
N_RET_LAYERS = DEPTH // 2
AB_SIZES = (H_A * DK_A, H_A * DK_A, H_A * DV_A, H_A * DV_A,
            H_B * DK_B, H_B * DK_B, H_B * DV_B, GLA_RANK, H_B * DV_B)
RET_SIZES = (H_C * DK_C, H_C * DK_C, H_C * DV_C, H_C * DV_C)

kernel_name = "hgrn2_gla_retnet_macaron_step"

F32 = jnp.float32


def split_cols(x, sizes):
    return jnp.split(x, [int(v) for v in np.cumsum(sizes)[:-1]], axis=-1)


def rmsnorm(x, g):
    xf = x.astype(F32)
    y = xf * lax.rsqrt(jnp.mean(xf * xf, axis=-1, keepdims=True) + EPS)
    return (y * g.astype(F32)).astype(x.dtype)


def headnorm(x, g):
    xf = x.astype(F32)
    mu = jnp.mean(xf, axis=-1, keepdims=True)
    xc = xf - mu
    y = xc * lax.rsqrt(jnp.mean(xc * xc, axis=-1, keepdims=True) + EPS)
    return (y * g.astype(F32)).astype(x.dtype)


def swiglu(x, w_in, w_out):
    a, b = jnp.split(x @ w_in, 2, axis=-1)
    return (jax.nn.silu(a) * b) @ w_out


def heads(x, h):
    b, l, _ = x.shape
    return x.reshape(b, l, h, -1).transpose(0, 2, 1, 3)


def merge(x):
    b, h, l, d = x.shape
    return x.transpose(0, 2, 1, 3).reshape(b, l, h * d)


def rope(x, pos):
    half = x.shape[-1] // 2
    inv = ROPE_THETA ** (-jnp.arange(half, dtype=F32) / half)
    ang = pos.astype(F32)[:, None] * inv[None, :]
    cos, sin = jnp.cos(ang), jnp.sin(ang)
    x1 = x[..., :half].astype(F32)
    x2 = x[..., half:].astype(F32)
    return jnp.concatenate([x1 * cos - x2 * sin, x2 * cos + x1 * sin], axis=-1).astype(x.dtype)


def chunk_gated_linear_attention(q, k, v, log_g, s0):
    bsz, h, L, _ = q.shape
    dv = v.shape[-1]
    c = min(CHUNK, L)
    n = -(-L // c)
    pad = n * c - L
    scalar = log_g.shape[-1] == 1

    def prep(t):
        t = jnp.pad(t.astype(F32), ((0, 0), (0, 0), (0, pad), (0, 0)))
        return t.reshape(bsz, h, n, c, t.shape[-1]).transpose(2, 0, 1, 3, 4)

    qs, ks, vs, gs = prep(q), prep(k), prep(v), prep(log_g)
    causal = jnp.tril(jnp.ones((c, c), dtype=bool))

    def step(S, inp):
        qc, kc, vc, gc = inp
        b = jnp.cumsum(gc, axis=-2)
        rel = jnp.exp(jnp.where(causal[:, :, None],
                                b[..., :, None, :] - b[..., None, :, :], -jnp.inf))
        if scalar:
            att = jnp.einsum('bhtk,bhsk->bhts', qc, kc) * rel[..., 0]
        else:
            att = jnp.einsum('bhtk,bhtsk,bhsk->bhts', qc, rel, kc)
        o = (jnp.einsum('bhts,bhsv->bhtv', att, vc)
             + jnp.einsum('bhtk,bhkv->bhtv', qc * jnp.exp(b), S))
        b_last = b[..., -1:, :]
        S = (jnp.exp(b_last)[..., 0, :, None] * S
             + jnp.einsum('bhsk,bhsv->bhkv', kc * jnp.exp(b_last - b), vc))
        return S, o

    S, o = lax.scan(step, s0.astype(F32), (qs, ks, vs, gs))
    o = o.transpose(1, 2, 0, 3, 4).reshape(bsz, h, n * c, dv)[:, :, :L]
    return o.astype(q.dtype), S.astype(s0.dtype)


def mixer_hgrn2_gla(x, s_hgrn, s_gla, lb, w_in, w_gk2, b_gk, gn_a, gn_b, w_out):
    qa, fa, ia, ga, qb, kb, vb, lrb, gb = split_cols(x @ w_in, AB_SIZES)
    fz = fa.astype(F32)
    log_f = jnp.log(lb + (1.0 - lb) * jax.nn.sigmoid(fz))
    ka = (1.0 - lb) * jax.nn.sigmoid(-fz)
    oa, s_hgrn = chunk_gated_linear_attention(heads(jax.nn.silu(qa), H_A), heads(ka, H_A),
                                              heads(ia, H_A), heads(log_f, H_A), s_hgrn)
    oa = merge(rmsnorm(oa, gn_a)) * jax.nn.silu(ga)
    log_a = jax.nn.log_sigmoid((lrb @ w_gk2 + b_gk).astype(F32)) / GLA_GATE_NORM
    ob, s_gla = chunk_gated_linear_attention(heads(qb * (DK_B ** -0.5), H_B), heads(kb, H_B),
                                             heads(vb, H_B), heads(log_a, H_B), s_gla)
    ob = merge(rmsnorm(ob, gn_b)) * jax.nn.silu(gb)
    return jnp.concatenate([oa, ob], axis=-1) @ w_out, s_hgrn, s_gla


def mixer_retention(x, pos, s_ret, w_in, gn, w_out):
    q, k, v, g = split_cols(x @ w_in, RET_SIZES)
    bsz, L, _ = x.shape
    qh = rope(heads(q, H_C), pos)
    kh = rope(heads(k, H_C), pos) * (DK_C ** -0.5)
    log_gamma = jnp.log1p(-jnp.exp2(-5.0 - jnp.arange(H_C, dtype=F32)))
    log_g = jnp.broadcast_to(log_gamma[None, :, None, None], (bsz, H_C, L, 1))
    o, s_ret = chunk_gated_linear_attention(qh, kh, heads(v, H_C), log_g, s_ret)
    o = merge(headnorm(o, gn)) * jax.nn.silu(g)
    return o @ w_out, s_ret


def decoder_trunk(x, p, pos, s_hgrn, s_gla, s_ret,
                  norm_ffn1, ffn1_w_in, ffn1_w_out, norm_mix,
                  ab_w_in, ab_w_gk2, ab_b_gk, hgrn_lb_logits, ab_gn_hgrn, ab_gn_gla, ab_w_out,
                  ret_w_in, ret_gn, ret_w_out,
                  norm_ffn2, ffn2_w_in, ffn2_w_out, norm_ple, ple_w_gate, ple_w_proj, norm_final):
    lb_all = jnp.cumsum(jax.nn.softmax(hgrn_lb_logits.astype(F32), axis=0), axis=0)
    new_hgrn, new_gla, new_ret = [], [], []
    for i in range(DEPTH):
        j = i // 2
        x = x + 0.5 * swiglu(rmsnorm(x, norm_ffn1[i]), ffn1_w_in[i], ffn1_w_out[i])
        h = rmsnorm(x, norm_mix[i])
        if i % 2 == 0:
            m, sa, sb = mixer_hgrn2_gla(h, s_hgrn[j], s_gla[j], lb_all[j], ab_w_in[j], ab_w_gk2[j],
                                        ab_b_gk[j], ab_gn_hgrn[j], ab_gn_gla[j], ab_w_out[j])
            new_hgrn.append(sa)
            new_gla.append(sb)
        else:
            m, sc = mixer_retention(h, pos, s_ret[j], ret_w_in[j], ret_gn[j], ret_w_out[j])
            new_ret.append(sc)
        x = x + m
        x = x + 0.5 * swiglu(rmsnorm(x, norm_ffn2[i]), ffn2_w_in[i], ffn2_w_out[i])
        gate = jax.nn.sigmoid(rmsnorm(x, norm_ple[i]) @ ple_w_gate[i])
        x = x + gate * (p[i] @ ple_w_proj[i])
    return rmsnorm(x, norm_final), jnp.stack(new_hgrn), jnp.stack(new_gla), jnp.stack(new_ret)


def setup_inputs(seed: int = 0) -> dict:
    key = jax.random.key(seed)
    ks = iter(jax.random.split(key, 40))

    def nrm(shape, scale):
        return jax.random.normal(next(ks), shape, F32) * scale

    def gain(shape):
        return 1.0 + nrm(shape, 0.02)

    d_ab_in = sum(AB_SIZES)
    d_ret_in = sum(RET_SIZES)
    return {
        "x_prompt": nrm((BATCH, SEQ, D_MODEL), 1.0),
        "x_sample": nrm((DEC_BATCH, DEC_SEQ, D_MODEL), 1.0),
        "state_hgrn": nrm((N_AB_LAYERS, DEC_BATCH, H_A, DK_A, DV_A), 0.5),
        "state_gla": nrm((N_AB_LAYERS, DEC_BATCH, H_B, DK_B, DV_B), 0.5),
        "state_ret": nrm((N_RET_LAYERS, DEC_BATCH, H_C, DK_C, DV_C), 0.5),
        "p_prompt": nrm((DEPTH, BATCH, SEQ, D_PLE), 1.0),
        "p_sample": nrm((DEPTH, DEC_BATCH, DEC_SEQ, D_PLE), 1.0),
        "norm_ffn1": gain((DEPTH, D_MODEL)),
        "ffn1_w_in": nrm((DEPTH, D_MODEL, 2 * D_FF), D_MODEL ** -0.5),
        "ffn1_w_out": nrm((DEPTH, D_FF, D_MODEL), D_FF ** -0.5),
        "norm_mix": gain((DEPTH, D_MODEL)),
        "ab_w_in": nrm((N_AB_LAYERS, D_MODEL, d_ab_in), D_MODEL ** -0.5),
        "ab_w_gk2": nrm((N_AB_LAYERS, GLA_RANK, H_B * DK_B), GLA_RANK ** -0.5),
        "ab_b_gk": nrm((N_AB_LAYERS, H_B * DK_B), 0.1),
        "hgrn_lb_logits": nrm((N_AB_LAYERS + 1, H_A * DK_A), 0.5),
        "ab_gn_hgrn": gain((N_AB_LAYERS, DV_A)),
        "ab_gn_gla": gain((N_AB_LAYERS, DV_B)),
        "ab_w_out": nrm((N_AB_LAYERS, H_A * DV_A + H_B * DV_B, D_MODEL), D_MODEL ** -0.5),
        "ret_w_in": nrm((N_RET_LAYERS, D_MODEL, d_ret_in), D_MODEL ** -0.5),
        "ret_gn": gain((N_RET_LAYERS, DV_C)),
        "ret_w_out": nrm((N_RET_LAYERS, H_C * DV_C, D_MODEL), (H_C * DV_C) ** -0.5),
        "norm_ffn2": gain((DEPTH, D_MODEL)),
        "ffn2_w_in": nrm((DEPTH, D_MODEL, 2 * D_FF), D_MODEL ** -0.5),
        "ffn2_w_out": nrm((DEPTH, D_FF, D_MODEL), D_FF ** -0.5),
        "norm_ple": gain((DEPTH, D_MODEL)),
        "ple_w_gate": nrm((DEPTH, D_MODEL, D_MODEL), D_MODEL ** -0.5),
        "ple_w_proj": nrm((DEPTH, D_PLE, D_MODEL), D_PLE ** -0.5),
        "norm_final": gain((D_MODEL,)),
    }


def reference(x_prompt, x_sample, state_hgrn, state_gla, state_ret, p_prompt, p_sample,
              norm_ffn1, ffn1_w_in, ffn1_w_out, norm_mix,
              ab_w_in, ab_w_gk2, ab_b_gk, hgrn_lb_logits, ab_gn_hgrn, ab_gn_gla, ab_w_out,
              ret_w_in, ret_gn, ret_w_out,
              norm_ffn2, ffn2_w_in, ffn2_w_out, norm_ple, ple_w_gate, ple_w_proj, norm_final):
    bp = x_prompt.shape[0]
    zero_hgrn = jnp.zeros((N_AB_LAYERS, bp, H_A, DK_A, DV_A), x_prompt.dtype)
    zero_gla = jnp.zeros((N_AB_LAYERS, bp, H_B, DK_B, DV_B), x_prompt.dtype)
    zero_ret = jnp.zeros((N_RET_LAYERS, bp, H_C, DK_C, DV_C), x_prompt.dtype)
    pos_prompt = jnp.arange(x_prompt.shape[1], dtype=jnp.int32)
    pos_sample = PAST_LEN + jnp.arange(x_sample.shape[1], dtype=jnp.int32)

    y_prompt, hgrn_p, gla_p, ret_p = decoder_trunk(
        x_prompt, p_prompt, pos_prompt, zero_hgrn, zero_gla, zero_ret,
        norm_ffn1, ffn1_w_in, ffn1_w_out, norm_mix,
        ab_w_in, ab_w_gk2, ab_b_gk, hgrn_lb_logits, ab_gn_hgrn, ab_gn_gla, ab_w_out,
        ret_w_in, ret_gn, ret_w_out,
        norm_ffn2, ffn2_w_in, ffn2_w_out, norm_ple, ple_w_gate, ple_w_proj, norm_final)
    y_sample, hgrn_s, gla_s, ret_s = decoder_trunk(
        x_sample, p_sample, pos_sample, state_hgrn, state_gla, state_ret,
        norm_ffn1, ffn1_w_in, ffn1_w_out, norm_mix,
        ab_w_in, ab_w_gk2, ab_b_gk, hgrn_lb_logits, ab_gn_hgrn, ab_gn_gla, ab_w_out,
        ret_w_in, ret_gn, ret_w_out,
        norm_ffn2, ffn2_w_in, ffn2_w_out, norm_ple, ple_w_gate, ple_w_proj, norm_final)
    return (y_prompt, y_sample, hgrn_p, gla_p, ret_p, hgrn_s, gla_s, ret_s)
```

```python
import functools
import math

import numpy as np
import jax
import jax.numpy as jnp
from jax import lax
from jax.experimental import pallas as pl
from jax.experimental.pallas import tpu as pltpu

F32 = jnp.float32
BF16 = jnp.bfloat16

H_A = 4
H_B = 4
H_C = 4
GLA_RANK = 16
GLA_GATE_NORM = 16.0
ROPE_THETA = 10000.0
PAST_LEN = 16384
EPS = 1e-6

LANES = 128
V7X_VMEM_BUDGET_BYTES = 56 * 1024 * 1024

ROW_TILE = 512
GLA_CHUNK = 64
GLA_STEP_TOKENS = 256
RET_CHUNK = 256
SAMPLE_SEQS = 8


def _params(n_axes, vmem_bytes):
    return pltpu.CompilerParams(dimension_semantics=("arbitrary",) * n_axes,
                                vmem_limit_bytes=min(int(vmem_bytes), V7X_VMEM_BUDGET_BYTES))


def _resident(shape):
    nd = len(shape)
    return pl.BlockSpec(shape, lambda *_: (0,) * nd, pipeline_mode=pl.Buffered(1))


def _rms(x, g):
    return x * lax.rsqrt(jnp.mean(x * x, axis=-1, keepdims=True) + EPS) * g


def _silu(x):
    return x * jax.nn.sigmoid(x)


def _dot(a, b):
    return jnp.dot(a, b, preferred_element_type=F32)


def _dot_nt(a, b):
    return lax.dot_general(a, b, (((1,), (1,)), ((), ())), preferred_element_type=F32)


def _dot_tn(a, b):
    return lax.dot_general(a, b, (((0,), (0,)), ((), ())), preferred_element_type=F32)


def _ffn_kernel(*refs, d_ff, n_chunks, has_ple, has_final):
    it = iter(refs)
    x_ref, g_ref, win_ref, wout_ref = next(it), next(it), next(it), next(it)
    if has_ple:
        p_ref, gp_ref, wg_ref, wp_ref = next(it), next(it), next(it), next(it)
    if has_final:
        gf_ref = next(it)
    o_ref, act_ref = next(it), next(it)

    x = x_ref[...]
    xn = _rms(x, g_ref[...]).astype(BF16)
    cw = d_ff // n_chunks
    for c in range(n_chunks):
        a = _dot(xn, win_ref[:, c * cw:(c + 1) * cw])
        b = _dot(xn, win_ref[:, d_ff + c * cw:d_ff + (c + 1) * cw])
        act_ref[:, c * cw:(c + 1) * cw] = (_silu(a) * b).astype(BF16)
    x = x + 0.5 * _dot(act_ref[...], wout_ref[...])
    if has_ple:
        gate = jax.nn.sigmoid(_dot(_rms(x, gp_ref[...]).astype(BF16), wg_ref[...]))
        x = x + gate * _dot(p_ref[...].astype(BF16), wp_ref[...])
    if has_final:
        x = _rms(x, gf_ref[...])
    o_ref[...] = x


def _ffn_call(x, gain, w_in, w_out, ple=None, final_gain=None, name="ffn"):
    n, d = x.shape
    d_ff = w_out.shape[0]
    tm = ROW_TILE
    row = lambda i: (i, 0)
    args = [x, gain.reshape(1, d), w_in, w_out]
    specs = [pl.BlockSpec((tm, d), row), _resident((1, d)), _resident(w_in.shape), _resident(w_out.shape)]
    wbytes = 2 * (w_in.size + w_out.size)
    if ple is not None:
        p, gp, wg, wp = ple
        args += [p, gp.reshape(1, d), wg, wp]
        specs += [pl.BlockSpec((tm, p.shape[1]), row), _resident((1, d)), _resident(wg.shape), _resident(wp.shape)]
        wbytes += 2 * (wg.size + wp.size)
    if final_gain is not None:
        args.append(final_gain.reshape(1, d))
        specs.append(_resident((1, d)))
    n_chunks = 2
    tile_bytes = tm * d * 4
    vmem = wbytes + 8 * tile_bytes + 3 * tm * (d_ff // n_chunks) * 4 + tm * d_ff * 2 + (4 << 20)
    return pl.pallas_call(
        functools.partial(_ffn_kernel, d_ff=d_ff, n_chunks=n_chunks, has_ple=ple is not None,
                          has_final=final_gain is not None),
        out_shape=jax.ShapeDtypeStruct((n, d), F32),
        grid=(n // tm,),
        in_specs=specs,
        out_specs=pl.BlockSpec((tm, d), row),
        scratch_shapes=[pltpu.VMEM((tm, d_ff), BF16)],
        compiler_params=_params(1, vmem),
        name=name,
    )(*args)


def _inproj_kernel(*refs, n_out, chunk_cols):
    x_ref, g_ref = refs[0], refs[1]
    w_refs = refs[2:2 + n_out]
    o_refs = refs[2 + n_out:2 + 2 * n_out]
    xn = _rms(x_ref[...], g_ref[...]).astype(BF16)
    for w_ref, o_ref, cw in zip(w_refs, o_refs, chunk_cols):
        for c in range(w_ref.shape[1] // cw):
            o_ref[:, c * cw:(c + 1) * cw] = _dot(xn, w_ref[:, c * cw:(c + 1) * cw]).astype(o_ref.dtype)


def _inproj_call(x, gain, weights, out_dtypes, chunk_cols, name):
    n, d = x.shape
    tm = ROW_TILE
    row = lambda i: (i, 0)
    specs = [pl.BlockSpec((tm, d), row), _resident((1, d))] + [_resident(w.shape) for w in weights]
    out_shape = [jax.ShapeDtypeStruct((n, w.shape[1]), dt) for w, dt in zip(weights, out_dtypes)]
    out_specs = [pl.BlockSpec((tm, w.shape[1]), row) for w in weights]
    vmem = (sum(2 * w.size for w in weights) + 4 * tm * d * 4
            + sum(2 * tm * w.shape[1] * jnp.dtype(dt).itemsize for w, dt in zip(weights, out_dtypes))
            + 2 * tm * max(chunk_cols) * 4 + (4 << 20))
    return pl.pallas_call(
        functools.partial(_inproj_kernel, n_out=len(weights), chunk_cols=tuple(chunk_cols)),
        out_shape=out_shape,
        grid=(n // tm,),
        in_specs=specs,
        out_specs=out_specs,
        compiler_params=_params(1, vmem),
        name=name,
    )(x, gain.reshape(1, d), *weights)


def _outproj_kernel(*refs, n_parts):
    x_ref = refs[0]
    o_ref = refs[-1]
    acc = x_ref[...]
    for i in range(n_parts):
        acc = acc + _dot(refs[1 + i][...], refs[1 + n_parts + i][...])
    o_ref[...] = acc


def _outproj_call(x, parts, weights, name):
    n, d = x.shape
    tm = ROW_TILE
    row = lambda i: (i, 0)
    specs = ([pl.BlockSpec((tm, d), row)] + [pl.BlockSpec((tm, p.shape[1]), row) for p in parts]
             + [_resident(w.shape) for w in weights])
    vmem = (sum(2 * w.size for w in weights) + 6 * tm * d * 4
            + sum(2 * tm * p.shape[1] * 2 for p in parts) + (4 << 20))
    return pl.pallas_call(
        functools.partial(_outproj_kernel, n_parts=len(parts)),
        out_shape=jax.ShapeDtypeStruct((n, d), F32),
        grid=(n // tm,),
        in_specs=specs,
        out_specs=pl.BlockSpec((tm, d), row),
        compiler_params=_params(1, vmem),
        name=name,
    )(x, *parts, *weights)


def _gla_constants(n_seq, ls):
    c = n_seq * ls
    nlev = int(round(math.log2(ls)))
    assert 2 ** nlev == ls
    t = np.arange(c)[:, None]
    i = np.arange(c)[None, :]
    same = (t // ls) == (i // ls)
    w_rows = [same & (i <= t), same & (i > t)]
    masks, uppers = [], []
    for l in range(nlev):
        m = 2 ** l
        upper = ((t // m) % 2) == 1
        r = (t // (2 * m)) * (2 * m) + m - 1
        w_rows.append(np.where(upper, (i > r) & (i <= t), (i > t) & (i <= r)))
        masks.append(((t // (2 * m)) == (i // (2 * m))) & upper & (((i // m) % 2) == 0))
        uppers.append(np.broadcast_to(upper, (c, LANES)))
    w = jnp.asarray(np.concatenate(w_rows, 0).astype(np.float32), BF16)
    return w, jnp.asarray(np.stack(masks).astype(np.float32)), jnp.asarray(np.stack(uppers).astype(np.float32))


def _gla_block(q, k, v, g, st_list, w_ref, m_ref, u_ref, ls):
    c = q.shape[0]
    dk = q.shape[1]
    n_seq = c // ls
    nlev = m_ref.shape[0]
    g_hi = g.astype(BF16)
    g_lo = (g - g_hi.astype(F32)).astype(BF16)
    ex2 = _dot(w_ref[...], jnp.concatenate([g_hi, g_lo], axis=1))
    e = jnp.exp(ex2[:, :dk] + ex2[:, dk:])
    e_cum, e_rest = e[:c], e[c:2 * c]
    att = jnp.zeros((c, c), F32)
    for l in range(nlev):
        x = (jnp.where(u_ref[l] > 0.5, q, k) * e[(2 + l) * c:(3 + l) * c]).astype(BF16)
        att = att + m_ref[l] * _dot_nt(x, x)
    vb = v.astype(BF16)
    o = _dot(att.astype(BF16), vb) + jnp.sum(q * k, axis=1, keepdims=True) * v
    qe = q * e_cum
    kr = k * e_rest
    o_parts, st_new = [], []
    for b in range(n_seq):
        rows = slice(b * ls, (b + 1) * ls)
        st = st_list[b]
        o_parts.append(o[rows] + _dot_nt(qe[rows].astype(BF16), st.astype(BF16)))
        e_tot = e_cum[(b + 1) * ls - 1:(b + 1) * ls, :]
        st_new.append(e_tot * st + _dot_tn(v[rows].astype(BF16), kr[rows].astype(BF16)))
    o = o_parts[0] if n_seq == 1 else jnp.concatenate(o_parts, axis=0)
    return o, st_new


def _hgrn_inputs(refs, rows, q_scale):
    del q_scale
    qa_ref, ia_ref, ga_ref, fa_ref, lg_ref = refs
    lg = lg_ref[...]
    ex = jnp.exp(lg - jnp.max(lg, axis=0, keepdims=True))
    lb = ex[0:1] / jnp.sum(ex, axis=0, keepdims=True)
    fz = fa_ref[rows, :]
    q = _silu(qa_ref[rows, :].astype(F32))
    k = (1.0 - lb) * jax.nn.sigmoid(-fz)
    g = jnp.log(lb + (1.0 - lb) * jax.nn.sigmoid(fz))
    return q, k, ia_ref[rows, :].astype(F32), g, ga_ref[rows, :].astype(F32)


def _gla_inputs(refs, rows, q_scale):
    qb_ref, kb_ref, vb_ref, gb_ref, lr_ref, wgk_ref, bgk_ref = refs
    z = _dot(lr_ref[rows, :].astype(BF16), wgk_ref[...]) + bgk_ref[...]
    g = (jnp.minimum(z, 0.0) - jnp.log1p(jnp.exp(-jnp.abs(z)))) / GLA_GATE_NORM
    q = qb_ref[rows, :].astype(F32) * q_scale
    return q, kb_ref[rows, :].astype(F32), vb_ref[rows, :].astype(F32), g, gb_ref[rows, :].astype(F32)


def _head_out(o, gate, gn):
    return (_rms(o, gn) * _silu(gate)).astype(BF16)


def _ab_prompt_kernel(*refs, kind, n_in, chunk, q_scale):
    in_refs = refs[:n_in]
    gn_ref, w_ref, m_ref, u_ref, o_ref, s_ref, st_ref = refs[n_in:]
    t = pl.program_id(2)

    @pl.when(t == 0)
    def _():
        st_ref[...] = jnp.zeros_like(st_ref)

    load = _hgrn_inputs if kind == "hgrn" else _gla_inputs
    for c in range(o_ref.shape[0] // chunk):
        rows = slice(c * chunk, (c + 1) * chunk)
        q, k, v, g, gate = load(in_refs, rows, q_scale)
        o, (st,) = _gla_block(q, k, v, g, [st_ref[...]], w_ref, m_ref, u_ref, chunk)
        st_ref[...] = st
        o_ref[rows, :] = _head_out(o, gate, gn_ref[...])

    @pl.when(t == pl.num_programs(2) - 1)
    def _():
        s_ref[0, 0] = st_ref[...].T[:s_ref.shape[2], :]


def _ab_sample_kernel(*refs, kind, n_in, ls, q_scale):
    in_refs = refs[:n_in]
    gn_ref, w_ref, m_ref, u_ref, s0_ref, o_ref, s_ref = refs[n_in:]
    load = _hgrn_inputs if kind == "hgrn" else _gla_inputs
    n_seq = o_ref.shape[0] // ls
    dk = s0_ref.shape[2]
    q, k, v, g, gate = load(in_refs, slice(None), q_scale)
    st0 = []
    for b in range(n_seq):
        s0 = s0_ref[b, 0]
        if dk < LANES:
            s0 = jnp.concatenate([s0, jnp.zeros((LANES - dk, s0.shape[1]), F32)], axis=0)
        st0.append(s0.T)
    o, st = _gla_block(q, k, v, g, st0, w_ref, m_ref, u_ref, ls)
    o_ref[...] = _head_out(o, gate, gn_ref[...])
    for b in range(n_seq):
        s_ref[b, 0] = st[b].T[:dk, :]


def _ab_specs(kind, p16, p32, extra, tok_block, tok_index):
    def col(base):
        def index(*ids):
            r, h = tok_index(*ids)
            return (r, base + h)
        return pl.BlockSpec((tok_block, LANES), index)

    def fixed(colblock):
        return pl.BlockSpec((tok_block, LANES), lambda *ids: (tok_index(*ids)[0], colblock))

    def per_head(shape):
        return pl.BlockSpec(shape, lambda *ids: (0, tok_index(*ids)[1]))

    if kind == "hgrn":
        (logits,) = extra
        args = [p16, p16, p16, p32, logits]
        specs = [col(0), col(H_A), col(2 * H_A), col(0), per_head((logits.shape[0], LANES))]
    else:
        w_gk, b_gk = extra
        base = 3 * H_A
        args = [p16, p16, p16, p16, p32, w_gk, b_gk]
        specs = [col(base), col(base + H_B), col(base + 2 * H_B), col(base + 3 * H_B), fixed(H_A),
                 per_head((LANES, LANES)), per_head((1, LANES))]
    return args, specs


def _ab_prompt_call(kind, p16, p32, extra, gn, batch, seq, dk, name):
    nh = H_A if kind == "hgrn" else H_B
    tt = GLA_STEP_TOKENS
    nt = seq // tt
    consts = _gla_constants(1, GLA_CHUNK)
    args, specs = _ab_specs(kind, p16, p32, extra, tt, lambda b, h, t: (b * nt + t, h))
    n_in = len(args)
    args += [gn.reshape(1, LANES), *consts]
    specs += [_resident((1, LANES))] + [_resident(c.shape) for c in consts]
    o, s = pl.pallas_call(
        functools.partial(_ab_prompt_kernel, kind=kind, n_in=n_in, chunk=GLA_CHUNK, q_scale=dk ** -0.5),
        out_shape=[jax.ShapeDtypeStruct((p16.shape[0], nh * LANES), BF16),
                   jax.ShapeDtypeStruct((batch, nh, dk, LANES), F32)],
        grid=(batch, nh, nt),
        in_specs=specs,
        out_specs=[pl.BlockSpec((tt, LANES), lambda b, h, t: (b * nt + t, h)),
                   pl.BlockSpec((1, 1, dk, LANES), lambda b, h, t: (b, h, 0, 0))],
        scratch_shapes=[pltpu.VMEM((LANES, LANES), F32)],
        compiler_params=_params(3, 32 << 20),
        name=name,
    )(*args)
    return o, s


def _ab_sample_call(kind, p16, p32, extra, gn, s0, row0, seq, o_prev, name):
    batch, nh, dk, _ = s0.shape
    nb = SAMPLE_SEQS
    rows = nb * seq
    blk0 = row0 // rows
    consts = _gla_constants(nb, seq)
    tok = lambda i, h: (blk0 + i, h)
    args, specs = _ab_specs(kind, p16, p32, extra, rows, tok)
    n_in = len(args)
    args += [gn.reshape(1, LANES), *consts, s0, o_prev]
    state_spec = pl.BlockSpec((nb, 1, dk, LANES), lambda i, h: (i, h, 0, 0))
    specs += ([_resident((1, LANES))] + [_resident(c.shape) for c in consts]
              + [state_spec, pl.BlockSpec(memory_space=pl.ANY)])

    def kernel(*refs):
        refs = list(refs)
        del refs[n_in + 5]
        _ab_sample_kernel(*refs, kind=kind, n_in=n_in, ls=seq, q_scale=dk ** -0.5)

    o, s = pl.pallas_call(
        kernel,
        out_shape=[jax.ShapeDtypeStruct(o_prev.shape, BF16),
                   jax.ShapeDtypeStruct((batch, nh, dk, LANES), F32)],
        grid=(batch // nb, nh),
        in_specs=specs,
        out_specs=[pl.BlockSpec((rows, LANES), tok), state_spec],
        input_output_aliases={len(args) - 1: 0},
        compiler_params=_params(2, 32 << 20),
        name=name,
    )(*args)
    return o, s


def _ret_constants(n_seq, ls):
    c = n_seq * ls
    log_gamma = np.log1p(-np.exp2(-5.0 - np.arange(H_C, dtype=np.float64)))[:, None, None]
    t = np.arange(c)[:, None]
    s = np.arange(c)[None, :]
    causal = ((t // ls) == (s // ls)) & (s <= t)
    dmask = np.where(causal, np.exp(log_gamma * np.where(causal, t - s, 0)), 0.0)
    pos = (np.arange(c) % ls)[None, :, None]
    scales = np.stack([np.exp(log_gamma * (pos + 1)), np.exp(log_gamma * (ls - 1 - pos))], axis=1)
    scales = np.broadcast_to(scales, (H_C, 2, c, LANES))
    return jnp.asarray(dmask, F32), jnp.asarray(scales, F32)


def _rope_tables(positions, half):
    inv = ROPE_THETA ** (-jnp.arange(half, dtype=F32) / half)
    ang = positions.astype(F32)[:, None] * inv[None, :]
    return jnp.cos(ang), jnp.sin(ang)


def _rope(x, cos, sin):
    half = x.shape[1] // 2
    x1, x2 = x[:, :half], x[:, half:]
    return jnp.concatenate([x1 * cos - x2 * sin, x2 * cos + x1 * sin], axis=1)


def _ret_block(q_ref, k_ref, v_ref, cos_ref, sin_ref, dm_ref, sc_ref, s_list, ls):
    cos, sin = cos_ref[...], sin_ref[...]
    dk = q_ref.shape[1]
    q = _rope(q_ref[...].astype(F32), cos, sin)
    k = _rope(k_ref[...].astype(F32), cos, sin) * (dk ** -0.5)
    v = v_ref[...]
    reps = dk // LANES
    q_sc = jnp.concatenate([sc_ref[0, 0]] * reps, axis=1)
    k_sc = jnp.concatenate([sc_ref[0, 1]] * reps, axis=1)
    att = _dot_nt(q.astype(BF16), k.astype(BF16)) * dm_ref[0]
    o = _dot(att.astype(BF16), v)
    qs = q * q_sc
    ks = k * k_sc
    decay = sc_ref[0, 0][ls - 1:ls, 0:1]
    n_seq = q.shape[0] // ls
    v32 = v if n_seq == 1 else v.astype(F32)
    o_parts, s_new = [], []
    for b in range(n_seq):
        rows = slice(b * ls, (b + 1) * ls)
        s = s_list[b]
        o_parts.append(o[rows] + _dot(qs[rows].astype(BF16), s.astype(BF16)))
        s_new.append(decay * s + _dot_tn(ks[rows].astype(BF16), v32[rows].astype(BF16)))
    o = o_parts[0] if n_seq == 1 else jnp.concatenate(o_parts, axis=0)
    return o, s_new


def _ret_head_out(o, gate, gn):
    oc = o - jnp.mean(o, axis=-1, keepdims=True)
    y = oc * lax.rsqrt(jnp.mean(oc * oc, axis=-1, keepdims=True) + EPS) * gn
    return (y * _silu(gate)).astype(BF16)


def _ret_prompt_kernel(q_ref, k_ref, v_ref, g_ref, cos_ref, sin_ref, dm_ref, sc_ref, gn_ref,
                       o_ref, s_ref, st_ref):
    t = pl.program_id(2)

    @pl.when(t == 0)
    def _():
        st_ref[...] = jnp.zeros_like(st_ref)

    o, (s,) = _ret_block(q_ref, k_ref, v_ref, cos_ref, sin_ref, dm_ref, sc_ref, [st_ref[...]],
                         q_ref.shape[0])
    st_ref[...] = s
    o_ref[...] = _ret_head_out(o, g_ref[...].astype(F32), gn_ref[...])

    @pl.when(t == pl.num_programs(2) - 1)
    def _():
        s_ref[0, 0] = st_ref[...]


def _ret_sample_kernel(q_ref, k_ref, v_ref, g_ref, cos_ref, sin_ref, dm_ref, sc_ref, gn_ref, s0_ref,
                       o_prev_ref, o_ref, s_ref, *, ls):
    del o_prev_ref
    n_seq = q_ref.shape[0] // ls
    o, s = _ret_block(q_ref, k_ref, v_ref, cos_ref, sin_ref, dm_ref, sc_ref,
                      [s0_ref[b, 0] for b in range(n_seq)], ls)
    o_ref[...] = _ret_head_out(o, g_ref[...].astype(F32), gn_ref[...])
    for b in range(n_seq):
        s_ref[b, 0] = s[b]


def _ret_specs(r16, dk, dv, tok_block, tok_index):
    def col(width, base):
        def index(*ids):
            r, h = tok_index(*ids)
            return (r, base + h)
        return pl.BlockSpec((tok_block, width), index)
    nq = H_C * dk
    return [r16] * 4, [col(dk, 0), col(dk, H_C), col(dv, 2 * nq // dv), col(dv, 2 * nq // dv + H_C)]


def _ret_prompt_call(r16, gn, batch, seq, dk, dv, name):
    c = RET_CHUNK
    nt = seq // c
    dmask, scales = _ret_constants(1, c)
    cos, sin = _rope_tables(jnp.arange(seq, dtype=jnp.int32), dk // 2)
    args, specs = _ret_specs(r16, dk, dv, c, lambda b, h, t: (b * nt + t, h))
    args += [cos, sin, dmask, scales, gn.reshape(1, dv)]
    specs += [pl.BlockSpec((c, dk // 2), lambda b, h, t: (t, 0))] * 2
    specs += [pl.BlockSpec((1, c, c), lambda b, h, t: (h, 0, 0)),
              pl.BlockSpec((1, 2, c, LANES), lambda b, h, t: (h, 0, 0, 0)), _resident((1, dv))]
    return pl.pallas_call(
        _ret_prompt_kernel,
        out_shape=[jax.ShapeDtypeStruct((r16.shape[0], H_C * dv), BF16),
                   jax.ShapeDtypeStruct((batch, H_C, dk, dv), F32)],
        grid=(batch, H_C, nt),
        in_specs=specs,
        out_specs=[pl.BlockSpec((c, dv), lambda b, h, t: (b * nt + t, h)),
                   pl.BlockSpec((1, 1, dk, dv), lambda b, h, t: (b, h, 0, 0))],
        scratch_shapes=[pltpu.VMEM((dk, dv), F32)],
        compiler_params=_params(3, 40 << 20),
        name=name,
    )(*args)


def _ret_sample_call(r16, gn, s0, row0, batch, seq, dk, dv, o_prev, name):
    nb = SAMPLE_SEQS
    rows = nb * seq
    blk0 = row0 // rows
    dmask, scales = _ret_constants(nb, seq)
    pos = PAST_LEN + (jnp.arange(rows, dtype=jnp.int32) % seq)
    cos, sin = _rope_tables(pos, dk // 2)
    tok = lambda i, h: (blk0 + i, h)
    args, specs = _ret_specs(r16, dk, dv, rows, tok)
    state_spec = pl.BlockSpec((nb, 1, dk, dv), lambda i, h: (i, h, 0, 0))
    args += [cos, sin, dmask, scales, gn.reshape(1, dv), s0, o_prev]
    specs += [_resident(cos.shape)] * 2
    specs += [pl.BlockSpec((1, rows, rows), lambda i, h: (h, 0, 0)),
              pl.BlockSpec((1, 2, rows, LANES), lambda i, h: (h, 0, 0, 0)), _resident((1, dv)),
              state_spec, pl.BlockSpec(memory_space=pl.ANY)]
    return pl.pallas_call(
        functools.partial(_ret_sample_kernel, ls=seq),
        out_shape=[jax.ShapeDtypeStruct(o_prev.shape, BF16),
                   jax.ShapeDtypeStruct((batch, H_C, dk, dv), F32)],
        grid=(batch // nb, H_C),
        in_specs=specs,
        out_specs=[pl.BlockSpec((rows, dv), tok), state_spec],
        input_output_aliases={len(args) - 1: 0},
        compiler_params=_params(2, 48 << 20),
        name=name,
    )(*args)


def _pad_heads(w, n_heads, width):
    d = w.shape[0]
    w = w.reshape(d, n_heads, -1)
    return jnp.pad(w, ((0, 0), (0, 0), (0, width - w.shape[2]))).reshape(d, n_heads * width)


def _prepare_ab_weights(w_in, w_gk2, b_gk, dk_a, dv_a, dk_b, dv_b):
    sizes = (H_A * dk_a, H_A * dk_a, H_A * dv_a, H_A * dv_a, H_B * dk_b, H_B * dk_b, H_B * dv_b,
             GLA_RANK, H_B * dv_b)
    qa, fa, ia, ga, qb, kb, vb, lrb, gb = jnp.split(w_in, [int(v) for v in np.cumsum(sizes)[:-1]], axis=1)
    w16 = jnp.concatenate([qa, ia, ga, _pad_heads(qb, H_B, LANES), _pad_heads(kb, H_B, LANES), vb, gb],
                          axis=1).astype(BF16)
    w32 = jnp.concatenate([fa, jnp.pad(lrb, ((0, 0), (0, LANES - GLA_RANK)))], axis=1).astype(BF16)
    w_gk = jnp.pad(_pad_heads(w_gk2, H_B, LANES), ((0, LANES - GLA_RANK), (0, 0))).astype(BF16)
    b_gk = _pad_heads(b_gk.reshape(1, -1), H_B, LANES)
    return w16, w32, w_gk, b_gk


def kernel(x_prompt, x_sample, state_hgrn, state_gla, state_ret, p_prompt, p_sample, norm_ffn1, ffn1_w_in, ffn1_w_out, norm_mix, ab_w_in, ab_w_gk2, ab_b_gk, hgrn_lb_logits, ab_gn_hgrn, ab_gn_gla, ab_w_out, ret_w_in, ret_gn, ret_w_out, norm_ffn2, ffn2_w_in, ffn2_w_out, norm_ple, ple_w_gate, ple_w_proj, norm_final):
    bp, lp, d = x_prompt.shape
    bs, ls, _ = x_sample.shape
    depth = norm_ffn1.shape[0]
    n_prompt = bp * lp
    dk_a, dv_a = state_hgrn.shape[-2:]
    dk_b, dv_b = state_gla.shape[-2:]
    dk_c, dv_c = state_ret.shape[-2:]
    assert dk_a == dv_a == dv_b == LANES and dk_b <= LANES and state_hgrn.shape[0] == 1

    x = jnp.concatenate([x_prompt.reshape(n_prompt, d), x_sample.reshape(bs * ls, d)], axis=0)
    p = jnp.concatenate([p_prompt.reshape(depth, n_prompt, -1), p_sample.reshape(depth, bs * ls, -1)], axis=1)

    new_hgrn_p, new_gla_p, new_ret_p, new_hgrn_s, new_gla_s, new_ret_s = [], [], [], [], [], []
    for i in range(depth):
        j = i // 2
        x = _ffn_call(x, norm_ffn1[i], ffn1_w_in[i].astype(BF16), ffn1_w_out[i].astype(BF16), name=f"ffn1_{i}")
        if i % 2 == 0:
            w16, w32, w_gk, b_gk = _prepare_ab_weights(ab_w_in[j], ab_w_gk2[j], ab_b_gk[j], dk_a, dv_a, dk_b, dv_b)
            p16, p32 = _inproj_call(x, norm_mix[i], [w16, w32], [BF16, F32], [w16.shape[1] // 2, w32.shape[1]],
                                    name=f"ab_in_{i}")
            oa, sa_p = _ab_prompt_call("hgrn", p16, p32, (hgrn_lb_logits,), ab_gn_hgrn[j], bp, lp, dk_a,
                                       f"hgrn_p_{i}")
            oa, sa_s = _ab_sample_call("hgrn", p16, p32, (hgrn_lb_logits,), ab_gn_hgrn[j], state_hgrn[j],
                                       n_prompt, ls, oa, f"hgrn_s_{i}")
            ob, sb_p = _ab_prompt_call("gla", p16, p32, (w_gk, b_gk), ab_gn_gla[j], bp, lp, dk_b, f"gla_p_{i}")
            ob, sb_s = _ab_sample_call("gla", p16, p32, (w_gk, b_gk), ab_gn_gla[j], state_gla[j],
                                       n_prompt, ls, ob, f"gla_s_{i}")
            new_hgrn_p.append(sa_p)
            new_hgrn_s.append(sa_s)
            new_gla_p.append(sb_p)
            new_gla_s.append(sb_s)
            w_out = ab_w_out[j].astype(BF16)
            na = H_A * dv_a
            x = _outproj_call(x, [oa, ob], [w_out[:na], w_out[na:]], name=f"ab_out_{i}")
        else:
            r16, = _inproj_call(x, norm_mix[i], [ret_w_in[j].astype(BF16)], [BF16], [2048], name=f"ret_in_{i}")
            oc, sc_p = _ret_prompt_call(r16, ret_gn[j], bp, lp, dk_c, dv_c, f"ret_p_{i}")
            oc, sc_s = _ret_sample_call(r16, ret_gn[j], state_ret[j], n_prompt, bs, ls, dk_c, dv_c, oc, f"ret_s_{i}")
            new_ret_p.append(sc_p)
            new_ret_s.append(sc_s)
            x = _outproj_call(x, [oc], [ret_w_out[j].astype(BF16)], name=f"ret_out_{i}")
        x = _ffn_call(x, norm_ffn2[i], ffn2_w_in[i].astype(BF16), ffn2_w_out[i].astype(BF16),
                      ple=(p[i], norm_ple[i], ple_w_gate[i].astype(BF16), ple_w_proj[i].astype(BF16)),
                      final_gain=norm_final if i == depth - 1 else None, name=f"ffn2_{i}")

    y_prompt = x[:n_prompt].reshape(bp, lp, d)
    y_sample = x[n_prompt:].reshape(bs, ls, d)
    return (y_prompt, y_sample, jnp.stack(new_hgrn_p), jnp.stack(new_gla_p), jnp.stack(new_ret_p),
            jnp.stack(new_hgrn_s), jnp.stack(new_gla_s), jnp.stack(new_ret_s))
```

```python
import functools
import math

import numpy as np
import jax
import jax.numpy as jnp
from jax import lax
from jax.experimental import pallas as pl
from jax.experimental.pallas import tpu as pltpu

F32 = jnp.float32
BF16 = jnp.bfloat16

H_A = 4
H_B = 4
H_C = 4
GLA_RANK = 16
GLA_GATE_NORM = 16.0
ROPE_THETA = 10000.0
PAST_LEN = 16384
EPS = 1e-6

LANES = 128
V7X_VMEM_BUDGET_BYTES = 56 * 1024 * 1024

ROW_TILE = 512
GLA_CHUNK = 64
GLA_STEP_TOKENS = 128
RET_CHUNK = 256
SAMPLE_SEQS = 8


def _params(n_axes, vmem_bytes):
    return pltpu.CompilerParams(dimension_semantics=("arbitrary",) * n_axes,
                                vmem_limit_bytes=min(int(vmem_bytes), V7X_VMEM_BUDGET_BYTES))


def _resident(shape):
    nd = len(shape)
    return pl.BlockSpec(shape, lambda *_: (0,) * nd, pipeline_mode=pl.Buffered(1))


def _rms(x, g):
    return x * lax.rsqrt(jnp.mean(x * x, axis=-1, keepdims=True) + EPS) * g


def _silu(x):
    return x * jax.nn.sigmoid(x)


def _dot(a, b):
    return jnp.dot(a, b, preferred_element_type=F32)


def _dot_nt(a, b):
    return lax.dot_general(a, b, (((1,), (1,)), ((), ())), preferred_element_type=F32)


def _dot_tn(a, b):
    return lax.dot_general(a, b, (((0,), (0,)), ((), ())), preferred_element_type=F32)


def _ffn_kernel(*refs, d_ff, n_chunks, has_ple, has_final):
    it = iter(refs)
    x_ref, g_ref, win_ref, wout_ref = next(it), next(it), next(it), next(it)
    if has_ple:
        p_ref, gp_ref, wg_ref, wp_ref = next(it), next(it), next(it), next(it)
    if has_final:
        gf_ref = next(it)
    o_ref, act_ref = next(it), next(it)

    x = x_ref[...]
    xn = _rms(x, g_ref[...]).astype(BF16)
    cw = d_ff // n_chunks
    for c in range(n_chunks):
        a = _dot(xn, win_ref[:, c * cw:(c + 1) * cw])
        b = _dot(xn, win_ref[:, d_ff + c * cw:d_ff + (c + 1) * cw])
        act_ref[:, c * cw:(c + 1) * cw] = (_silu(a) * b).astype(BF16)
    x = x + 0.5 * _dot(act_ref[...], wout_ref[...])
    if has_ple:
        gate = jax.nn.sigmoid(_dot(_rms(x, gp_ref[...]).astype(BF16), wg_ref[...]))
        x = x + gate * _dot(p_ref[...].astype(BF16), wp_ref[...])
    if has_final:
        x = _rms(x, gf_ref[...])
    o_ref[...] = x


def _ffn_call(x, gain, w_in, w_out, ple=None, final_gain=None, name="ffn"):
    n, d = x.shape
    d_ff = w_out.shape[0]
    tm = ROW_TILE
    row = lambda i: (i, 0)
    args = [x, gain.reshape(1, d), w_in, w_out]
    specs = [pl.BlockSpec((tm, d), row), _resident((1, d)), _resident(w_in.shape), _resident(w_out.shape)]
    wbytes = 2 * (w_in.size + w_out.size)
    if ple is not None:
        p, gp, wg, wp = ple
        args += [p, gp.reshape(1, d), wg, wp]
        specs += [pl.BlockSpec((tm, p.shape[1]), row), _resident((1, d)), _resident(wg.shape), _resident(wp.shape)]
        wbytes += 2 * (wg.size + wp.size)
    if final_gain is not None:
        args.append(final_gain.reshape(1, d))
        specs.append(_resident((1, d)))
    n_chunks = 2
    tile_bytes = tm * d * 4
    vmem = wbytes + 8 * tile_bytes + 3 * tm * (d_ff // n_chunks) * 4 + tm * d_ff * 2 + (4 << 20)
    return pl.pallas_call(
        functools.partial(_ffn_kernel, d_ff=d_ff, n_chunks=n_chunks, has_ple=ple is not None,
                          has_final=final_gain is not None),
        out_shape=jax.ShapeDtypeStruct((n, d), F32),
        grid=(n // tm,),
        in_specs=specs,
        out_specs=pl.BlockSpec((tm, d), row),
        scratch_shapes=[pltpu.VMEM((tm, d_ff), BF16)],
        compiler_params=_params(1, vmem),
        name=name,
    )(*args)


def _inproj_kernel(*refs, n_out, chunk_cols):
    x_ref, g_ref = refs[0], refs[1]
    w_refs = refs[2:2 + n_out]
    o_refs = refs[2 + n_out:2 + 2 * n_out]
    xn = _rms(x_ref[...], g_ref[...]).astype(BF16)
    for w_ref, o_ref, cw in zip(w_refs, o_refs, chunk_cols):
        for c in range(w_ref.shape[1] // cw):
            o_ref[:, c * cw:(c + 1) * cw] = _dot(xn, w_ref[:, c * cw:(c + 1) * cw]).astype(o_ref.dtype)


def _inproj_call(x, gain, weights, out_dtypes, chunk_cols, name):
    n, d = x.shape
    tm = ROW_TILE
    row = lambda i: (i, 0)
    specs = [pl.BlockSpec((tm, d), row), _resident((1, d))] + [_resident(w.shape) for w in weights]
    out_shape = [jax.ShapeDtypeStruct((n, w.shape[1]), dt) for w, dt in zip(weights, out_dtypes)]
    out_specs = [pl.BlockSpec((tm, w.shape[1]), row) for w in weights]
    vmem = (sum(2 * w.size for w in weights) + 4 * tm * d * 4
            + sum(2 * tm * w.shape[1] * jnp.dtype(dt).itemsize for w, dt in zip(weights, out_dtypes))
            + 2 * tm * max(chunk_cols) * 4 + (4 << 20))
    return pl.pallas_call(
        functools.partial(_inproj_kernel, n_out=len(weights), chunk_cols=tuple(chunk_cols)),
        out_shape=out_shape,
        grid=(n // tm,),
        in_specs=specs,
        out_specs=out_specs,
        compiler_params=_params(1, vmem),
        name=name,
    )(x, gain.reshape(1, d), *weights)


def _outproj_kernel(*refs, n_parts):
    x_ref = refs[0]
    o_ref = refs[-1]
    acc = x_ref[...]
    for i in range(n_parts):
        acc = acc + _dot(refs[1 + i][...], refs[1 + n_parts + i][...])
    o_ref[...] = acc


def _outproj_call(x, parts, weights, name):
    n, d = x.shape
    tm = ROW_TILE
    row = lambda i: (i, 0)
    specs = ([pl.BlockSpec((tm, d), row)] + [pl.BlockSpec((tm, p.shape[1]), row) for p in parts]
             + [_resident(w.shape) for w in weights])
    vmem = (sum(2 * w.size for w in weights) + 6 * tm * d * 4
            + sum(2 * tm * p.shape[1] * 2 for p in parts) + (4 << 20))
    return pl.pallas_call(
        functools.partial(_outproj_kernel, n_parts=len(parts)),
        out_shape=jax.ShapeDtypeStruct((n, d), F32),
        grid=(n // tm,),
        in_specs=specs,
        out_specs=pl.BlockSpec((tm, d), row),
        compiler_params=_params(1, vmem),
        name=name,
    )(x, *parts, *weights)


GLA_MATMUL_LEVELS = (1, 2)


def _gla_constants(n_seq, ls, width):
    c = n_seq * ls
    nlev = int(round(math.log2(ls)))
    assert 2 ** nlev == ls and nlev > max(GLA_MATMUL_LEVELS)
    t = np.arange(c)[:, None]
    i = np.arange(c)[None, :]
    w_rows = [((t // ls) == (i // ls)) & (i <= t)]
    masks, uppers = [], []
    for l in range(nlev):
        m = 2 ** l
        upper = ((t // m) % 2) == 1
        r = (t // (2 * m)) * (2 * m) + m - 1
        if l in GLA_MATMUL_LEVELS:
            w_rows.append(np.where(upper, (i > r) & (i <= t), (i > t) & (i <= r)))
        masks.append(((t // (2 * m)) == (i // (2 * m))) & upper & (((i // m) % 2) == 0))
        uppers.append(np.broadcast_to(upper, (c, width)))
    w = jnp.asarray(np.concatenate(w_rows, 0).astype(np.float32), BF16)
    return w, jnp.asarray(np.stack(masks).astype(np.float32)), jnp.asarray(np.stack(uppers).astype(np.float32))


def _gla_block(q, k, v, g, st_list, w_ref, m_ref, u_ref, ls):
    c, width = q.shape
    nh = width // LANES
    n_seq = c // ls
    nlev = m_ref.shape[0]
    g_hi = g.astype(BF16)
    g_lo = (g - g_hi.astype(F32)).astype(BF16)
    ex2 = _dot(w_ref[...], jnp.concatenate([g_hi, g_lo], axis=1))
    ex = ex2[:, :width] + ex2[:, width:]
    b = ex[:c]

    def row(r, n):
        return jnp.broadcast_to(b[r:r + 1, :], (n, width))

    b_end = jnp.concatenate([row((s + 1) * ls - 1, ls) for s in range(n_seq)], axis=0) if n_seq > 1 else row(c - 1, c)
    e_cum = jnp.exp(b)
    qe = q * e_cum
    kr = k * jnp.exp(b_end - b)

    att = [None] * nh
    for l in range(nlev):
        upper = u_ref[l] > 0.5
        if l == 0:
            x = jnp.where(upper, q * jnp.exp(g), k)
        else:
            if l in GLA_MATMUL_LEVELS:
                j = 1 + GLA_MATMUL_LEVELS.index(l)
                e = jnp.exp(ex[j * c:(j + 1) * c])
            else:
                m = 2 ** l
                d = b - jnp.concatenate([row(p * 2 * m + m - 1, 2 * m) for p in range(c // (2 * m))], axis=0) \
                    if c > 2 * m else b - row(m - 1, c)
                e = jnp.exp(jnp.minimum(d, -d))
            x = jnp.where(upper, q, k) * e
        x = x.astype(BF16)
        mask = m_ref[l] > 0.5
        for h in range(nh):
            xh = x[:, h * LANES:(h + 1) * LANES]
            p = _dot_nt(xh, xh)
            att[h] = jnp.where(mask, p, 0.0 if att[h] is None else att[h])

    qk = q * k
    outs, st_new = [], [[None] * nh for _ in range(n_seq)]
    for h in range(nh):
        sl = slice(h * LANES, (h + 1) * LANES)
        vh = v[:, sl]
        o = _dot(att[h].astype(BF16), vh.astype(BF16)) + jnp.sum(qk[:, sl], axis=1, keepdims=True) * vh
        parts = []
        for s in range(n_seq):
            rows = slice(s * ls, (s + 1) * ls)
            st = st_list[s][h]
            parts.append(o[rows] + _dot_nt(qe[rows, sl].astype(BF16), st.astype(BF16)))
            e_tot = e_cum[(s + 1) * ls - 1:(s + 1) * ls, sl]
            st_new[s][h] = e_tot * st + _dot_tn(vh[rows].astype(BF16), kr[rows, sl].astype(BF16))
        outs.append(parts[0] if n_seq == 1 else jnp.concatenate(parts, axis=0))
    return outs, st_new


def _hgrn_inputs(refs, rows, q_scale):
    del q_scale
    qa_ref, ia_ref, ga_ref, fa_ref, lg_ref = refs
    lg = lg_ref[...]
    ex = jnp.exp(lg - jnp.max(lg, axis=0, keepdims=True))
    lb = ex[0:1] / jnp.sum(ex, axis=0, keepdims=True)
    fz = fa_ref[rows, :]
    q = _silu(qa_ref[rows, :].astype(F32))
    k = (1.0 - lb) * jax.nn.sigmoid(-fz)
    g = jnp.log(lb + (1.0 - lb) * jax.nn.sigmoid(fz))
    return q, k, ia_ref[rows, :].astype(F32), g, ga_ref[rows, :].astype(F32)


def _gla_inputs(refs, rows, q_scale):
    qb_ref, kb_ref, vb_ref, gb_ref, lr_ref, wgk_ref, bgk_ref = refs
    z = _dot(lr_ref[rows, :].astype(BF16), wgk_ref[...]) + bgk_ref[...]
    g = (jnp.minimum(z, 0.0) - jnp.log1p(jnp.exp(-jnp.abs(z)))) / GLA_GATE_NORM
    q = qb_ref[rows, :].astype(F32) * q_scale
    return q, kb_ref[rows, :].astype(F32), vb_ref[rows, :].astype(F32), g, gb_ref[rows, :].astype(F32)


def _head_out(outs, gate, gn):
    ys = [_rms(o, gn) * _silu(gate[:, h * LANES:(h + 1) * LANES]) for h, o in enumerate(outs)]
    return jnp.concatenate(ys, axis=1).astype(BF16)


def _ab_prompt_kernel(*refs, kind, n_in, chunk, q_scale):
    in_refs = refs[:n_in]
    gn_ref, w_ref, m_ref, u_ref, o_ref, s_ref, st_ref = refs[n_in:]
    t = pl.program_id(1)
    nh = st_ref.shape[0]

    @pl.when(t == 0)
    def _():
        st_ref[...] = jnp.zeros_like(st_ref)

    load = _hgrn_inputs if kind == "hgrn" else _gla_inputs
    for c in range(o_ref.shape[0] // chunk):
        rows = slice(c * chunk, (c + 1) * chunk)
        q, k, v, g, gate = load(in_refs, rows, q_scale)
        outs, (st,) = _gla_block(q, k, v, g, [[st_ref[h] for h in range(nh)]], w_ref, m_ref, u_ref, chunk)
        for h in range(nh):
            st_ref[h] = st[h]
        o_ref[rows, :] = _head_out(outs, gate, gn_ref[...])

    @pl.when(t == pl.num_programs(1) - 1)
    def _():
        for h in range(nh):
            s_ref[0, h] = st_ref[h].T[:s_ref.shape[2], :]


def _ab_sample_kernel(*refs, kind, n_in, ls, q_scale):
    in_refs = refs[:n_in]
    gn_ref, w_ref, m_ref, u_ref, s0_ref, o_ref, s_ref = refs[n_in:]
    load = _hgrn_inputs if kind == "hgrn" else _gla_inputs
    n_seq, nh, dk = s0_ref.shape[:3]
    q, k, v, g, gate = load(in_refs, slice(None), q_scale)
    st0 = []
    for s in range(n_seq):
        st0.append([])
        for h in range(nh):
            s0 = s0_ref[s, h]
            if dk < LANES:
                s0 = jnp.concatenate([s0, jnp.zeros((LANES - dk, s0.shape[1]), F32)], axis=0)
            st0[s].append(s0.T)
    outs, st = _gla_block(q, k, v, g, st0, w_ref, m_ref, u_ref, ls)
    o_ref[...] = _head_out(outs, gate, gn_ref[...])
    for s in range(n_seq):
        for h in range(nh):
            s_ref[s, h] = st[s][h].T[:dk, :]


def _ab_specs(kind, p16, p32, extra, tok_block, tok_index, nh):
    width = nh * LANES

    def col(group, w=width):
        return pl.BlockSpec((tok_block, w), lambda *ids: (tok_index(*ids), group))

    if kind == "hgrn":
        (logits,) = extra
        args = [p16, p16, p16, p32, logits]
        specs = [col(0), col(1), col(2), col(0), _resident(logits.shape)]
    else:
        w_gk, b_gk = extra
        args = [p16, p16, p16, p16, p32, w_gk, b_gk]
        specs = [col(3), col(4), col(5), col(6), col(H_A, LANES), _resident(w_gk.shape),
                 _resident(b_gk.shape)]
    return args, specs


def _ab_prompt_call(kind, p16, p32, extra, gn, batch, seq, dk, name):
    nh = H_A if kind == "hgrn" else H_B
    tt = GLA_STEP_TOKENS
    nt = seq // tt
    consts = _gla_constants(1, GLA_CHUNK, nh * LANES)
    args, specs = _ab_specs(kind, p16, p32, extra, tt, lambda b, t: b * nt + t, nh)
    n_in = len(args)
    args += [gn.reshape(1, LANES), *consts]
    specs += [_resident((1, LANES))] + [_resident(c.shape) for c in consts]
    o, s = pl.pallas_call(
        functools.partial(_ab_prompt_kernel, kind=kind, n_in=n_in, chunk=GLA_CHUNK, q_scale=dk ** -0.5),
        out_shape=[jax.ShapeDtypeStruct((p16.shape[0], nh * LANES), BF16),
                   jax.ShapeDtypeStruct((batch, nh, dk, LANES), F32)],
        grid=(batch, nt),
        in_specs=specs,
        out_specs=[pl.BlockSpec((tt, nh * LANES), lambda b, t: (b * nt + t, 0)),
                   pl.BlockSpec((1, nh, dk, LANES), lambda b, t: (b, 0, 0, 0))],
        scratch_shapes=[pltpu.VMEM((nh, LANES, LANES), F32)],
        compiler_params=_params(2, 32 << 20),
        name=name,
    )(*args)
    return o, s


def _ab_sample_call(kind, p16, p32, extra, gn, s0, row0, seq, o_prev, name):
    batch, nh, dk, _ = s0.shape
    nb = SAMPLE_SEQS
    rows = nb * seq
    blk0 = row0 // rows
    consts = _gla_constants(nb, seq, nh * LANES)
    tok = lambda i: blk0 + i
    args, specs = _ab_specs(kind, p16, p32, extra, rows, tok, nh)
    n_in = len(args)
    args += [gn.reshape(1, LANES), *consts, s0, o_prev]
    state_spec = pl.BlockSpec((nb, nh, dk, LANES), lambda i: (i, 0, 0, 0))
    specs += ([_resident((1, LANES))] + [_resident(c.shape) for c in consts]
              + [state_spec, pl.BlockSpec(memory_space=pl.ANY)])

    def kernel(*refs):
        refs = list(refs)
        del refs[n_in + 5]
        _ab_sample_kernel(*refs, kind=kind, n_in=n_in, ls=seq, q_scale=dk ** -0.5)

    o, s = pl.pallas_call(
        kernel,
        out_shape=[jax.ShapeDtypeStruct(o_prev.shape, BF16),
                   jax.ShapeDtypeStruct((batch, nh, dk, LANES), F32)],
        grid=(batch // nb,),
        in_specs=specs,
        out_specs=[pl.BlockSpec((rows, nh * LANES), lambda i: (blk0 + i, 0)), state_spec],
        input_output_aliases={len(args) - 1: 0},
        compiler_params=_params(1, 32 << 20),
        name=name,
    )(*args)
    return o, s


def _ret_constants(n_seq, ls):
    c = n_seq * ls
    log_gamma = np.log1p(-np.exp2(-5.0 - np.arange(H_C, dtype=np.float64)))[:, None, None]
    t = np.arange(c)[:, None]
    s = np.arange(c)[None, :]
    causal = ((t // ls) == (s // ls)) & (s <= t)
    dmask = np.where(causal, np.exp(log_gamma * np.where(causal, t - s, 0)), 0.0)
    pos = (np.arange(c) % ls)[None, :, None]
    scales = np.stack([np.exp(log_gamma * (pos + 1)), np.exp(log_gamma * (ls - 1 - pos))], axis=1)
    scales = np.broadcast_to(scales, (H_C, 2, c, LANES))
    return jnp.asarray(dmask, F32), jnp.asarray(scales, F32)


def _rope_tables(positions, half):
    inv = ROPE_THETA ** (-jnp.arange(half, dtype=F32) / half)
    ang = positions.astype(F32)[:, None] * inv[None, :]
    return jnp.cos(ang), jnp.sin(ang)


def _rope(x, cos, sin):
    half = x.shape[1] // 2
    x1, x2 = x[:, :half], x[:, half:]
    return jnp.concatenate([x1 * cos - x2 * sin, x2 * cos + x1 * sin], axis=1)


def _ret_block(q_ref, k_ref, v_ref, cos_ref, sin_ref, dm_ref, sc_ref, s_list, ls):
    cos, sin = cos_ref[...], sin_ref[...]
    dk = q_ref.shape[1]
    q = _rope(q_ref[...].astype(F32), cos, sin)
    k = _rope(k_ref[...].astype(F32), cos, sin) * (dk ** -0.5)
    v = v_ref[...]
    reps = dk // LANES
    q_sc = jnp.concatenate([sc_ref[0, 0]] * reps, axis=1)
    k_sc = jnp.concatenate([sc_ref[0, 1]] * reps, axis=1)
    att = _dot_nt(q.astype(BF16), k.astype(BF16)) * dm_ref[0]
    o = _dot(att.astype(BF16), v)
    qs = q * q_sc
    ks = k * k_sc
    decay = sc_ref[0, 0][ls - 1:ls, 0:1]
    n_seq = q.shape[0] // ls
    v32 = v if n_seq == 1 else v.astype(F32)
    o_parts, s_new = [], []
    for b in range(n_seq):
        rows = slice(b * ls, (b + 1) * ls)
        s = s_list[b]
        o_parts.append(o[rows] + _dot(qs[rows].astype(BF16), s.astype(BF16)))
        s_new.append(decay * s + _dot_tn(ks[rows].astype(BF16), v32[rows].astype(BF16)))
    o = o_parts[0] if n_seq == 1 else jnp.concatenate(o_parts, axis=0)
    return o, s_new


def _ret_head_out(o, gate, gn):
    oc = o - jnp.mean(o, axis=-1, keepdims=True)
    y = oc * lax.rsqrt(jnp.mean(oc * oc, axis=-1, keepdims=True) + EPS) * gn
    return (y * _silu(gate)).astype(BF16)


def _ret_prompt_kernel(q_ref, k_ref, v_ref, g_ref, cos_ref, sin_ref, dm_ref, sc_ref, gn_ref,
                       o_ref, s_ref, st_ref):
    t = pl.program_id(2)

    @pl.when(t == 0)
    def _():
        st_ref[...] = jnp.zeros_like(st_ref)

    o, (s,) = _ret_block(q_ref, k_ref, v_ref, cos_ref, sin_ref, dm_ref, sc_ref, [st_ref[...]],
                         q_ref.shape[0])
    st_ref[...] = s
    o_ref[...] = _ret_head_out(o, g_ref[...].astype(F32), gn_ref[...])

    @pl.when(t == pl.num_programs(2) - 1)
    def _():
        s_ref[0, 0] = st_ref[...]


def _ret_sample_kernel(q_ref, k_ref, v_ref, g_ref, cos_ref, sin_ref, dm_ref, sc_ref, gn_ref, s0_ref,
                       o_prev_ref, o_ref, s_ref, *, ls):
    del o_prev_ref
    n_seq = q_ref.shape[0] // ls
    o, s = _ret_block(q_ref, k_ref, v_ref, cos_ref, sin_ref, dm_ref, sc_ref,
                      [s0_ref[b, 0] for b in range(n_seq)], ls)
    o_ref[...] = _ret_head_out(o, g_ref[...].astype(F32), gn_ref[...])
    for b in range(n_seq):
        s_ref[b, 0] = s[b]


def _ret_specs(r16, dk, dv, tok_block, tok_index):
    def col(width, base):
        def index(*ids):
            r, h = tok_index(*ids)
            return (r, base + h)
        return pl.BlockSpec((tok_block, width), index)
    nq = H_C * dk
    return [r16] * 4, [col(dk, 0), col(dk, H_C), col(dv, 2 * nq // dv), col(dv, 2 * nq // dv + H_C)]


def _ret_prompt_call(r16, gn, batch, seq, dk, dv, name):
    c = RET_CHUNK
    nt = seq // c
    dmask, scales = _ret_constants(1, c)
    cos, sin = _rope_tables(jnp.arange(seq, dtype=jnp.int32), dk // 2)
    args, specs = _ret_specs(r16, dk, dv, c, lambda b, h, t: (b * nt + t, h))
    args += [cos, sin, dmask, scales, gn.reshape(1, dv)]
    specs += [pl.BlockSpec((c, dk // 2), lambda b, h, t: (t, 0))] * 2
    specs += [pl.BlockSpec((1, c, c), lambda b, h, t: (h, 0, 0)),
              pl.BlockSpec((1, 2, c, LANES), lambda b, h, t: (h, 0, 0, 0)), _resident((1, dv))]
    return pl.pallas_call(
        _ret_prompt_kernel,
        out_shape=[jax.ShapeDtypeStruct((r16.shape[0], H_C * dv), BF16),
                   jax.ShapeDtypeStruct((batch, H_C, dk, dv), F32)],
        grid=(batch, H_C, nt),
        in_specs=specs,
        out_specs=[pl.BlockSpec((c, dv), lambda b, h, t: (b * nt + t, h)),
                   pl.BlockSpec((1, 1, dk, dv), lambda b, h, t: (b, h, 0, 0))],
        scratch_shapes=[pltpu.VMEM((dk, dv), F32)],
        compiler_params=_params(3, 40 << 20),
        name=name,
    )(*args)


def _ret_sample_call(r16, gn, s0, row0, batch, seq, dk, dv, o_prev, name):
    nb = SAMPLE_SEQS
    rows = nb * seq
    blk0 = row0 // rows
    dmask, scales = _ret_constants(nb, seq)
    pos = PAST_LEN + (jnp.arange(rows, dtype=jnp.int32) % seq)
    cos, sin = _rope_tables(pos, dk // 2)
    tok = lambda i, h: (blk0 + i, h)
    args, specs = _ret_specs(r16, dk, dv, rows, tok)
    state_spec = pl.BlockSpec((nb, 1, dk, dv), lambda i, h: (i, h, 0, 0))
    args += [cos, sin, dmask, scales, gn.reshape(1, dv), s0, o_prev]
    specs += [_resident(cos.shape)] * 2
    specs += [pl.BlockSpec((1, rows, rows), lambda i, h: (h, 0, 0)),
              pl.BlockSpec((1, 2, rows, LANES), lambda i, h: (h, 0, 0, 0)), _resident((1, dv)),
              state_spec, pl.BlockSpec(memory_space=pl.ANY)]
    return pl.pallas_call(
        functools.partial(_ret_sample_kernel, ls=seq),
        out_shape=[jax.ShapeDtypeStruct(o_prev.shape, BF16),
                   jax.ShapeDtypeStruct((batch, H_C, dk, dv), F32)],
        grid=(batch // nb, H_C),
        in_specs=specs,
        out_specs=[pl.BlockSpec((rows, dv), tok), state_spec],
        input_output_aliases={len(args) - 1: 0},
        compiler_params=_params(2, 48 << 20),
        name=name,
    )(*args)


def _pad_heads(w, n_heads, width):
    d = w.shape[0]
    w = w.reshape(d, n_heads, -1)
    return jnp.pad(w, ((0, 0), (0, 0), (0, width - w.shape[2]))).reshape(d, n_heads * width)


def _prepare_ab_weights(w_in, w_gk2, b_gk, dk_a, dv_a, dk_b, dv_b):
    sizes = (H_A * dk_a, H_A * dk_a, H_A * dv_a, H_A * dv_a, H_B * dk_b, H_B * dk_b, H_B * dv_b,
             GLA_RANK, H_B * dv_b)
    qa, fa, ia, ga, qb, kb, vb, lrb, gb = jnp.split(w_in, [int(v) for v in np.cumsum(sizes)[:-1]], axis=1)
    w16 = jnp.concatenate([qa, ia, ga, _pad_heads(qb, H_B, LANES), _pad_heads(kb, H_B, LANES), vb, gb],
                          axis=1).astype(BF16)
    w32 = jnp.concatenate([fa, jnp.pad(lrb, ((0, 0), (0, LANES - GLA_RANK)))], axis=1).astype(BF16)
    w_gk = jnp.pad(_pad_heads(w_gk2, H_B, LANES), ((0, LANES - GLA_RANK), (0, 0))).astype(BF16)
    b_gk = _pad_heads(b_gk.reshape(1, -1), H_B, LANES)
    return w16, w32, w_gk, b_gk


def kernel(x_prompt, x_sample, state_hgrn, state_gla, state_ret, p_prompt, p_sample, norm_ffn1, ffn1_w_in, ffn1_w_out, norm_mix, ab_w_in, ab_w_gk2, ab_b_gk, hgrn_lb_logits, ab_gn_hgrn, ab_gn_gla, ab_w_out, ret_w_in, ret_gn, ret_w_out, norm_ffn2, ffn2_w_in, ffn2_w_out, norm_ple, ple_w_gate, ple_w_proj, norm_final):
    bp, lp, d = x_prompt.shape
    bs, ls, _ = x_sample.shape
    depth = norm_ffn1.shape[0]
    n_prompt = bp * lp
    dk_a, dv_a = state_hgrn.shape[-2:]
    dk_b, dv_b = state_gla.shape[-2:]
    dk_c, dv_c = state_ret.shape[-2:]
    assert dk_a == dv_a == dv_b == LANES and dk_b <= LANES and state_hgrn.shape[0] == 1

    x = jnp.concatenate([x_prompt.reshape(n_prompt, d), x_sample.reshape(bs * ls, d)], axis=0)
    p = jnp.concatenate([p_prompt.reshape(depth, n_prompt, -1), p_sample.reshape(depth, bs * ls, -1)], axis=1)

    new_hgrn_p, new_gla_p, new_ret_p, new_hgrn_s, new_gla_s, new_ret_s = [], [], [], [], [], []
    for i in range(depth):
        j = i // 2
        x = _ffn_call(x, norm_ffn1[i], ffn1_w_in[i].astype(BF16), ffn1_w_out[i].astype(BF16), name=f"ffn1_{i}")
        if i % 2 == 0:
            w16, w32, w_gk, b_gk = _prepare_ab_weights(ab_w_in[j], ab_w_gk2[j], ab_b_gk[j], dk_a, dv_a, dk_b, dv_b)
            p16, p32 = _inproj_call(x, norm_mix[i], [w16, w32], [BF16, F32], [w16.shape[1] // 2, w32.shape[1]],
                                    name=f"ab_in_{i}")
            oa, sa_p = _ab_prompt_call("hgrn", p16, p32, (hgrn_lb_logits,), ab_gn_hgrn[j], bp, lp, dk_a,
                                       f"hgrn_p_{i}")
            oa, sa_s = _ab_sample_call("hgrn", p16, p32, (hgrn_lb_logits,), ab_gn_hgrn[j], state_hgrn[j],
                                       n_prompt, ls, oa, f"hgrn_s_{i}")
            ob, sb_p = _ab_prompt_call("gla", p16, p32, (w_gk, b_gk), ab_gn_gla[j], bp, lp, dk_b, f"gla_p_{i}")
            ob, sb_s = _ab_sample_call("gla", p16, p32, (w_gk, b_gk), ab_gn_gla[j], state_gla[j],
                                       n_prompt, ls, ob, f"gla_s_{i}")
            new_hgrn_p.append(sa_p)
            new_hgrn_s.append(sa_s)
            new_gla_p.append(sb_p)
            new_gla_s.append(sb_s)
            w_out = ab_w_out[j].astype(BF16)
            na = H_A * dv_a
            x = _outproj_call(x, [oa, ob], [w_out[:na], w_out[na:]], name=f"ab_out_{i}")
        else:
            r16, = _inproj_call(x, norm_mix[i], [ret_w_in[j].astype(BF16)], [BF16], [2048], name=f"ret_in_{i}")
            oc, sc_p = _ret_prompt_call(r16, ret_gn[j], bp, lp, dk_c, dv_c, f"ret_p_{i}")
            oc, sc_s = _ret_sample_call(r16, ret_gn[j], state_ret[j], n_prompt, bs, ls, dk_c, dv_c, oc, f"ret_s_{i}")
            new_ret_p.append(sc_p)
            new_ret_s.append(sc_s)
            x = _outproj_call(x, [oc], [ret_w_out[j].astype(BF16)], name=f"ret_out_{i}")
        x = _ffn_call(x, norm_ffn2[i], ffn2_w_in[i].astype(BF16), ffn2_w_out[i].astype(BF16),
                      ple=(p[i], norm_ple[i], ple_w_gate[i].astype(BF16), ple_w_proj[i].astype(BF16)),
                      final_gain=norm_final if i == depth - 1 else None, name=f"ffn2_{i}")

    y_prompt = x[:n_prompt].reshape(bp, lp, d)
    y_sample = x[n_prompt:].reshape(bs, ls, d)
    return (y_prompt, y_sample, jnp.stack(new_hgrn_p), jnp.stack(new_gla_p), jnp.stack(new_ret_p),
            jnp.stack(new_hgrn_s), jnp.stack(new_gla_s), jnp.stack(new_ret_s))
```

```python
import functools
import math

import numpy as np
import jax
import jax.numpy as jnp
from jax import lax
from jax.experimental import pallas as pl
from jax.experimental.pallas import tpu as pltpu

F32 = jnp.float32
BF16 = jnp.bfloat16

H_A = 4
H_B = 4
H_C = 4
GLA_RANK = 16
GLA_GATE_NORM = 16.0
ROPE_THETA = 10000.0
PAST_LEN = 16384
EPS = 1e-6

LANES = 128
V7X_VMEM_BUDGET_BYTES = 56 * 1024 * 1024

ROW_TILE = 512
GLA_CHUNK = 64
GLA_STEP_TOKENS = 128
RET_CHUNK = 256
SAMPLE_SEQS = 8


def _params(n_axes, vmem_bytes):
    return pltpu.CompilerParams(dimension_semantics=("arbitrary",) * n_axes,
                                vmem_limit_bytes=min(int(vmem_bytes), V7X_VMEM_BUDGET_BYTES))


def _resident(shape):
    nd = len(shape)
    return pl.BlockSpec(shape, lambda *_: (0,) * nd, pipeline_mode=pl.Buffered(1))


def _rms(x, g):
    return x * lax.rsqrt(jnp.mean(x * x, axis=-1, keepdims=True) + EPS) * g


def _silu(x):
    return x * jax.nn.sigmoid(x)


def _dot(a, b):
    return jnp.dot(a, b, preferred_element_type=F32)


def _dot_nt(a, b):
    return lax.dot_general(a, b, (((1,), (1,)), ((), ())), preferred_element_type=F32)


def _dot_tn(a, b):
    return lax.dot_general(a, b, (((0,), (0,)), ((), ())), preferred_element_type=F32)


def _ffn_kernel(*refs, d_ff, n_chunks, n_x, has_ple, has_final, n_o, prompt_tiles):
    it = iter(refs)
    x_refs = [next(it) for _ in range(n_x)]
    g_ref, win_ref, wout_ref = next(it), next(it), next(it)
    if has_ple:
        p_refs = [next(it), next(it)]
        gp_ref, wg_ref, wp_ref = next(it), next(it), next(it)
    if has_final:
        gf_ref = next(it)
    o_refs = [next(it) for _ in range(n_o)]
    act_ref = next(it)
    is_prompt = pl.program_id(0) < prompt_tiles

    x = x_refs[0][...] if n_x == 1 else jnp.where(is_prompt, x_refs[0][...], x_refs[1][...])
    xn = _rms(x, g_ref[...]).astype(BF16)
    cw = d_ff // n_chunks
    for c in range(n_chunks):
        a = _dot(xn, win_ref[:, c * cw:(c + 1) * cw])
        b = _dot(xn, win_ref[:, d_ff + c * cw:d_ff + (c + 1) * cw])
        act_ref[:, c * cw:(c + 1) * cw] = (_silu(a) * b).astype(BF16)
    x = x + 0.5 * _dot(act_ref[...], wout_ref[...])
    if has_ple:
        p = jnp.where(is_prompt, p_refs[0][0], p_refs[1][0])
        gate = jax.nn.sigmoid(_dot(_rms(x, gp_ref[...]).astype(BF16), wg_ref[...]))
        x = x + gate * _dot(p.astype(BF16), wp_ref[...])
    if has_final:
        x = _rms(x, gf_ref[...])
    if n_o == 1:
        o_refs[0][...] = x
    else:
        @pl.when(is_prompt)
        def _():
            o_refs[0][...] = x

        @pl.when(jnp.logical_not(is_prompt))
        def _():
            o_refs[1][...] = x


def _ffn_call(xs, n_rows, gain, w_in, w_out, ple=None, final_gain=None, split_out=False, name="ffn"):
    d = xs[0].shape[1]
    d_ff = w_out.shape[0]
    tm = ROW_TILE
    assert n_rows[0] % tm == 0 and n_rows[1] % tm == 0
    pt, st = n_rows[0] // tm, n_rows[1] // tm
    row = lambda i: (i, 0)
    prompt_row = lambda i: (jnp.minimum(i, pt - 1), 0)
    sample_row = lambda i: (jnp.maximum(i - pt, 0), 0)
    pair = [prompt_row, sample_row]
    args = [*xs, gain.reshape(1, d), w_in, w_out]
    specs = ([pl.BlockSpec((tm, d), row)] if len(xs) == 1 else [pl.BlockSpec((tm, d), m) for m in pair])
    specs += [_resident((1, d)), _resident(w_in.shape), _resident(w_out.shape)]
    wbytes = 2 * (w_in.size + w_out.size)
    if ple is not None:
        ps, layer, gp, wg, wp = ple
        args += [*ps, gp.reshape(1, d), wg, wp]
        specs += [pl.BlockSpec((1, tm, ps[0].shape[2]), lambda i, m=m: (layer, *m(i))) for m in pair]
        specs += [_resident((1, d)), _resident(wg.shape), _resident(wp.shape)]
        wbytes += 2 * (wg.size + wp.size)
    if final_gain is not None:
        args.append(final_gain.reshape(1, d))
        specs.append(_resident((1, d)))
    if split_out:
        out_shape = [jax.ShapeDtypeStruct((r, d), F32) for r in n_rows]
        out_specs = [pl.BlockSpec((tm, d), m) for m in pair]
    else:
        out_shape = jax.ShapeDtypeStruct((sum(n_rows), d), F32)
        out_specs = pl.BlockSpec((tm, d), row)
    n_chunks = 2
    tile_bytes = tm * d * 4
    vmem = wbytes + 12 * tile_bytes + 3 * tm * (d_ff // n_chunks) * 4 + tm * d_ff * 2 + (4 << 20)
    return pl.pallas_call(
        functools.partial(_ffn_kernel, d_ff=d_ff, n_chunks=n_chunks, n_x=len(xs), has_ple=ple is not None,
                          has_final=final_gain is not None, n_o=2 if split_out else 1, prompt_tiles=pt),
        out_shape=out_shape,
        grid=(pt + st,),
        in_specs=specs,
        out_specs=out_specs,
        scratch_shapes=[pltpu.VMEM((tm, d_ff), BF16)],
        compiler_params=_params(1, vmem),
        name=name,
    )(*args)


def _inproj_kernel(*refs, n_out, chunk_cols):
    x_ref, g_ref = refs[0], refs[1]
    w_refs = refs[2:2 + n_out]
    o_refs = refs[2 + n_out:2 + 2 * n_out]
    xn = _rms(x_ref[...], g_ref[...]).astype(BF16)
    for w_ref, o_ref, cw in zip(w_refs, o_refs, chunk_cols):
        for c in range(w_ref.shape[1] // cw):
            o_ref[:, c * cw:(c + 1) * cw] = _dot(xn, w_ref[:, c * cw:(c + 1) * cw]).astype(o_ref.dtype)


def _inproj_call(x, gain, weights, out_dtypes, chunk_cols, name):
    n, d = x.shape
    tm = ROW_TILE
    row = lambda i: (i, 0)
    specs = [pl.BlockSpec((tm, d), row), _resident((1, d))] + [_resident(w.shape) for w in weights]
    out_shape = [jax.ShapeDtypeStruct((n, w.shape[1]), dt) for w, dt in zip(weights, out_dtypes)]
    out_specs = [pl.BlockSpec((tm, w.shape[1]), row) for w in weights]
    vmem = (sum(2 * w.size for w in weights) + 4 * tm * d * 4
            + sum(2 * tm * w.shape[1] * jnp.dtype(dt).itemsize for w, dt in zip(weights, out_dtypes))
            + 2 * tm * max(chunk_cols) * 4 + (4 << 20))
    return pl.pallas_call(
        functools.partial(_inproj_kernel, n_out=len(weights), chunk_cols=tuple(chunk_cols)),
        out_shape=out_shape,
        grid=(n // tm,),
        in_specs=specs,
        out_specs=out_specs,
        compiler_params=_params(1, vmem),
        name=name,
    )(x, gain.reshape(1, d), *weights)


def _outproj_kernel(*refs, n_parts):
    x_ref = refs[0]
    o_ref = refs[-1]
    acc = x_ref[...]
    for i in range(n_parts):
        acc = acc + _dot(refs[1 + i][...], refs[1 + n_parts + i][...])
    o_ref[...] = acc


def _outproj_call(x, parts, weights, name):
    n, d = x.shape
    tm = ROW_TILE
    row = lambda i: (i, 0)
    specs = ([pl.BlockSpec((tm, d), row)] + [pl.BlockSpec((tm, p.shape[1]), row) for p in parts]
             + [_resident(w.shape) for w in weights])
    vmem = (sum(2 * w.size for w in weights) + 6 * tm * d * 4
            + sum(2 * tm * p.shape[1] * 2 for p in parts) + (4 << 20))
    return pl.pallas_call(
        functools.partial(_outproj_kernel, n_parts=len(parts)),
        out_shape=jax.ShapeDtypeStruct((n, d), F32),
        grid=(n // tm,),
        in_specs=specs,
        out_specs=pl.BlockSpec((tm, d), row),
        compiler_params=_params(1, vmem),
        name=name,
    )(x, *parts, *weights)


GLA_MATMUL_LEVELS = (1, 2)


def _gla_constants(n_seq, ls, width):
    c = n_seq * ls
    nlev = int(round(math.log2(ls)))
    assert 2 ** nlev == ls and nlev > max(GLA_MATMUL_LEVELS)
    t = np.arange(c)[:, None]
    i = np.arange(c)[None, :]
    w_rows = [((t // ls) == (i // ls)) & (i <= t)]
    masks, uppers = [], []
    for l in range(nlev):
        m = 2 ** l
        upper = ((t // m) % 2) == 1
        r = (t // (2 * m)) * (2 * m) + m - 1
        if l in GLA_MATMUL_LEVELS:
            w_rows.append(np.where(upper, (i > r) & (i <= t), (i > t) & (i <= r)))
        masks.append(((t // (2 * m)) == (i // (2 * m))) & upper & (((i // m) % 2) == 0))
        uppers.append(np.broadcast_to(upper, (c, width)))
    w = jnp.asarray(np.concatenate(w_rows, 0).astype(np.float32), BF16)
    return w, jnp.asarray(np.stack(masks).astype(np.float32)), jnp.asarray(np.stack(uppers).astype(np.float32))


def _gla_block(q, k, v, g, st_list, w_ref, m_ref, u_ref, ls):
    c, width = q.shape
    nh = width // LANES
    n_seq = c // ls
    nlev = m_ref.shape[0]
    g_hi = g.astype(BF16)
    g_lo = (g - g_hi.astype(F32)).astype(BF16)
    ex2 = _dot(w_ref[...], jnp.concatenate([g_hi, g_lo], axis=1))
    ex = ex2[:, :width] + ex2[:, width:]
    b = ex[:c]

    def row(r, n):
        return jnp.broadcast_to(b[r:r + 1, :], (n, width))

    b_end = jnp.concatenate([row((s + 1) * ls - 1, ls) for s in range(n_seq)], axis=0) if n_seq > 1 else row(c - 1, c)
    e_cum = jnp.exp(b)
    qe = q * e_cum
    kr = k * jnp.exp(b_end - b)

    att = [None] * nh
    for l in range(nlev):
        upper = u_ref[l] > 0.5
        if l == 0:
            x = jnp.where(upper, q * jnp.exp(g), k)
        else:
            if l in GLA_MATMUL_LEVELS:
                j = 1 + GLA_MATMUL_LEVELS.index(l)
                e = jnp.exp(ex[j * c:(j + 1) * c])
            else:
                m = 2 ** l
                d = b - jnp.concatenate([row(p * 2 * m + m - 1, 2 * m) for p in range(c // (2 * m))], axis=0) \
                    if c > 2 * m else b - row(m - 1, c)
                e = jnp.exp(jnp.minimum(d, -d))
            x = jnp.where(upper, q, k) * e
        x = x.astype(BF16)
        mask = m_ref[l] > 0.5
        for h in range(nh):
            xh = x[:, h * LANES:(h + 1) * LANES]
            p = _dot_nt(xh, xh)
            att[h] = jnp.where(mask, p, 0.0 if att[h] is None else att[h])

    qk = q * k
    outs, st_new = [], [[None] * nh for _ in range(n_seq)]
    for h in range(nh):
        sl = slice(h * LANES, (h + 1) * LANES)
        vh = v[:, sl]
        o = _dot(att[h].astype(BF16), vh.astype(BF16)) + jnp.sum(qk[:, sl], axis=1, keepdims=True) * vh
        parts = []
        for s in range(n_seq):
            rows = slice(s * ls, (s + 1) * ls)
            st = st_list[s][h]
            parts.append(o[rows] + _dot_nt(qe[rows, sl].astype(BF16), st.astype(BF16)))
            e_tot = e_cum[(s + 1) * ls - 1:(s + 1) * ls, sl]
            st_new[s][h] = e_tot * st + _dot_tn(vh[rows].astype(BF16), kr[rows, sl].astype(BF16))
        outs.append(parts[0] if n_seq == 1 else jnp.concatenate(parts, axis=0))
    return outs, st_new


def _hgrn_inputs(refs, rows, q_scale):
    del q_scale
    qa_ref, ia_ref, ga_ref, fa_ref, lg_ref = refs
    lg = lg_ref[...]
    ex = jnp.exp(lg - jnp.max(lg, axis=0, keepdims=True))
    lb = ex[0:1] / jnp.sum(ex, axis=0, keepdims=True)
    fz = fa_ref[rows, :]
    q = _silu(qa_ref[rows, :].astype(F32))
    k = (1.0 - lb) * jax.nn.sigmoid(-fz)
    g = jnp.log(lb + (1.0 - lb) * jax.nn.sigmoid(fz))
    return q, k, ia_ref[rows, :].astype(F32), g, ga_ref[rows, :].astype(F32)


def _gla_inputs(refs, rows, q_scale):
    qb_ref, kb_ref, vb_ref, gb_ref, lr_ref, wgk_ref, bgk_ref = refs
    z = _dot(lr_ref[rows, :].astype(BF16), wgk_ref[...]) + bgk_ref[...]
    g = (jnp.minimum(z, 0.0) - jnp.log1p(jnp.exp(-jnp.abs(z)))) / GLA_GATE_NORM
    q = qb_ref[rows, :].astype(F32) * q_scale
    return q, kb_ref[rows, :].astype(F32), vb_ref[rows, :].astype(F32), g, gb_ref[rows, :].astype(F32)


def _head_out(outs, gate, gn):
    ys = [_rms(o, gn) * _silu(gate[:, h * LANES:(h + 1) * LANES]) for h, o in enumerate(outs)]
    return jnp.concatenate(ys, axis=1).astype(BF16)


def _ab_prompt_kernel(*refs, kind, n_in, chunk, q_scale):
    in_refs = refs[:n_in]
    gn_ref, w_ref, m_ref, u_ref, o_ref, s_ref, st_ref = refs[n_in:]
    t = pl.program_id(1)
    nh = st_ref.shape[0]

    @pl.when(t == 0)
    def _():
        st_ref[...] = jnp.zeros_like(st_ref)

    load = _hgrn_inputs if kind == "hgrn" else _gla_inputs
    for c in range(o_ref.shape[0] // chunk):
        rows = slice(c * chunk, (c + 1) * chunk)
        q, k, v, g, gate = load(in_refs, rows, q_scale)
        outs, (st,) = _gla_block(q, k, v, g, [[st_ref[h] for h in range(nh)]], w_ref, m_ref, u_ref, chunk)
        for h in range(nh):
            st_ref[h] = st[h]
        o_ref[rows, :] = _head_out(outs, gate, gn_ref[...])

    @pl.when(t == pl.num_programs(1) - 1)
    def _():
        for h in range(nh):
            s_ref[0, h] = st_ref[h].T[:s_ref.shape[2], :]


def _ab_sample_kernel(*refs, kind, n_in, ls, q_scale):
    in_refs = refs[:n_in]
    gn_ref, w_ref, m_ref, u_ref, s0_ref, o_ref, s_ref = refs[n_in:]
    load = _hgrn_inputs if kind == "hgrn" else _gla_inputs
    n_seq, nh, dk = s0_ref.shape[:3]
    q, k, v, g, gate = load(in_refs, slice(None), q_scale)
    st0 = []
    for s in range(n_seq):
        st0.append([])
        for h in range(nh):
            s0 = s0_ref[s, h]
            if dk < LANES:
                s0 = jnp.concatenate([s0, jnp.zeros((LANES - dk, s0.shape[1]), F32)], axis=0)
            st0[s].append(s0.T)
    outs, st = _gla_block(q, k, v, g, st0, w_ref, m_ref, u_ref, ls)
    o_ref[...] = _head_out(outs, gate, gn_ref[...])
    for s in range(n_seq):
        for h in range(nh):
            s_ref[s, h] = st[s][h].T[:dk, :]


def _ab_specs(kind, p16, p32, extra, tok_block, tok_index, nh):
    width = nh * LANES

    def col(group, w=width):
        return pl.BlockSpec((tok_block, w), lambda *ids: (tok_index(*ids), group))

    if kind == "hgrn":
        (logits,) = extra
        args = [p16, p16, p16, p32, logits]
        specs = [col(0), col(1), col(2), col(0), _resident(logits.shape)]
    else:
        w_gk, b_gk = extra
        args = [p16, p16, p16, p16, p32, w_gk, b_gk]
        specs = [col(3), col(4), col(5), col(6), col(H_A, LANES), _resident(w_gk.shape),
                 _resident(b_gk.shape)]
    return args, specs


def _ab_prompt_call(kind, p16, p32, extra, gn, batch, seq, dk, name):
    nh = H_A if kind == "hgrn" else H_B
    tt = GLA_STEP_TOKENS
    nt = seq // tt
    consts = _gla_constants(1, GLA_CHUNK, nh * LANES)
    args, specs = _ab_specs(kind, p16, p32, extra, tt, lambda b, t: b * nt + t, nh)
    n_in = len(args)
    args += [gn.reshape(1, LANES), *consts]
    specs += [_resident((1, LANES))] + [_resident(c.shape) for c in consts]
    o, s = pl.pallas_call(
        functools.partial(_ab_prompt_kernel, kind=kind, n_in=n_in, chunk=GLA_CHUNK, q_scale=dk ** -0.5),
        out_shape=[jax.ShapeDtypeStruct((p16.shape[0], nh * LANES), BF16),
                   jax.ShapeDtypeStruct((batch, nh, dk, LANES), F32)],
        grid=(batch, nt),
        in_specs=specs,
        out_specs=[pl.BlockSpec((tt, nh * LANES), lambda b, t: (b * nt + t, 0)),
                   pl.BlockSpec((1, nh, dk, LANES), lambda b, t: (b, 0, 0, 0))],
        scratch_shapes=[pltpu.VMEM((nh, LANES, LANES), F32)],
        compiler_params=_params(2, 32 << 20),
        name=name,
    )(*args)
    return o, s


def _ab_sample_call(kind, p16, p32, extra, gn, s0, row0, seq, o_prev, name):
    batch, nh, dk, _ = s0.shape
    nb = SAMPLE_SEQS
    rows = nb * seq
    blk0 = row0 // rows
    consts = _gla_constants(nb, seq, nh * LANES)
    tok = lambda i: blk0 + i
    args, specs = _ab_specs(kind, p16, p32, extra, rows, tok, nh)
    n_in = len(args)
    args += [gn.reshape(1, LANES), *consts, s0, o_prev]
    state_spec = pl.BlockSpec((nb, nh, dk, LANES), lambda i: (i, 0, 0, 0))
    specs += ([_resident((1, LANES))] + [_resident(c.shape) for c in consts]
              + [state_spec, pl.BlockSpec(memory_space=pl.ANY)])

    def kernel(*refs):
        refs = list(refs)
        del refs[n_in + 5]
        _ab_sample_kernel(*refs, kind=kind, n_in=n_in, ls=seq, q_scale=dk ** -0.5)

    o, s = pl.pallas_call(
        kernel,
        out_shape=[jax.ShapeDtypeStruct(o_prev.shape, BF16),
                   jax.ShapeDtypeStruct((batch, nh, dk, LANES), F32)],
        grid=(batch // nb,),
        in_specs=specs,
        out_specs=[pl.BlockSpec((rows, nh * LANES), lambda i: (blk0 + i, 0)), state_spec],
        input_output_aliases={len(args) - 1: 0},
        compiler_params=_params(1, 32 << 20),
        name=name,
    )(*args)
    return o, s


def _ret_constants(n_seq, ls):
    c = n_seq * ls
    log_gamma = np.log1p(-np.exp2(-5.0 - np.arange(H_C, dtype=np.float64)))[:, None, None]
    t = np.arange(c)[:, None]
    s = np.arange(c)[None, :]
    causal = ((t // ls) == (s // ls)) & (s <= t)
    dmask = np.where(causal, np.exp(log_gamma * np.where(causal, t - s, 0)), 0.0)
    pos = (np.arange(c) % ls)[None, :, None]
    scales = np.stack([np.exp(log_gamma * (pos + 1)), np.exp(log_gamma * (ls - 1 - pos))], axis=1)
    scales = np.broadcast_to(scales, (H_C, 2, c, LANES))
    return jnp.asarray(dmask, F32), jnp.asarray(scales, F32)


def _rope_tables(positions, half):
    inv = ROPE_THETA ** (-jnp.arange(half, dtype=F32) / half)
    ang = positions.astype(F32)[:, None] * inv[None, :]
    return jnp.cos(ang), jnp.sin(ang)


def _rope(x, cos, sin):
    half = x.shape[1] // 2
    x1, x2 = x[:, :half], x[:, half:]
    return jnp.concatenate([x1 * cos - x2 * sin, x2 * cos + x1 * sin], axis=1)


def _ret_block(q16, k16, v, cos, sin, dmask, q_scale, k_scale, s_list, ls):
    dk = q16.shape[1]
    q = _rope(q16.astype(F32), cos, sin)
    k = _rope(k16.astype(F32), cos, sin) * (dk ** -0.5)
    reps = dk // LANES
    q_sc = jnp.concatenate([q_scale] * reps, axis=1)
    k_sc = jnp.concatenate([k_scale] * reps, axis=1)
    att = _dot_nt(q.astype(BF16), k.astype(BF16)) * dmask
    o = _dot(att.astype(BF16), v)
    qs = q * q_sc
    ks = k * k_sc
    decay = q_scale[ls - 1:ls, 0:1]
    n_seq = q.shape[0] // ls
    v32 = v if n_seq == 1 else v.astype(F32)
    o_parts, s_new = [], []
    for b in range(n_seq):
        rows = slice(b * ls, (b + 1) * ls)
        s = s_list[b]
        o_parts.append(o[rows] + _dot(qs[rows].astype(BF16), s.astype(BF16)))
        s_new.append(decay * s + _dot_tn(ks[rows].astype(BF16), v32[rows].astype(BF16)))
    o = o_parts[0] if n_seq == 1 else jnp.concatenate(o_parts, axis=0)
    return o, s_new


def _ret_head_out(o, gate, gn):
    oc = o - jnp.mean(o, axis=-1, keepdims=True)
    y = oc * lax.rsqrt(jnp.mean(oc * oc, axis=-1, keepdims=True) + EPS) * gn
    return (y * _silu(gate)).astype(BF16)


def _ret_prompt_kernel(q_ref, k_ref, v_ref, g_ref, cos_ref, sin_ref, dm_ref, sc_ref, gn_ref,
                       o_ref, s_ref, st_ref):
    t = pl.program_id(1)
    nh, dk, dv = st_ref.shape
    c = q_ref.shape[0]

    @pl.when(t == 0)
    def _():
        st_ref[...] = jnp.zeros_like(st_ref)

    cos, sin = cos_ref[...], sin_ref[...]
    for h in range(nh):
        o, (s,) = _ret_block(q_ref[:, h * dk:(h + 1) * dk], k_ref[:, h * dk:(h + 1) * dk],
                             v_ref[:, h * dv:(h + 1) * dv], cos, sin, dm_ref[h], sc_ref[h, 0], sc_ref[h, 1],
                             [st_ref[h]], c)
        st_ref[h] = s
        o_ref[:, h * dv:(h + 1) * dv] = _ret_head_out(o, g_ref[:, h * dv:(h + 1) * dv].astype(F32), gn_ref[...])

    @pl.when(t == pl.num_programs(1) - 1)
    def _():
        s_ref[0] = st_ref[...]


def _ret_sample_kernel(q_ref, k_ref, v_ref, g_ref, cos_ref, sin_ref, dm_ref, sc_ref, gn_ref, s0_ref,
                       o_prev_ref, o_ref, s_ref, *, ls):
    del o_prev_ref
    n_seq = q_ref.shape[0] // ls
    o, s = _ret_block(q_ref[...], k_ref[...], v_ref[...], cos_ref[...], sin_ref[...], dm_ref[0],
                      sc_ref[0, 0], sc_ref[0, 1], [s0_ref[b, 0] for b in range(n_seq)], ls)
    o_ref[...] = _ret_head_out(o, g_ref[...].astype(F32), gn_ref[...])
    for b in range(n_seq):
        s_ref[b, 0] = s[b]


def _ret_specs(r16, dk, dv, tok_block, tok_index):
    def col(width, base):
        def index(*ids):
            r, h = tok_index(*ids)
            return (r, base + h)
        return pl.BlockSpec((tok_block, width), index)
    nq = H_C * dk
    return [r16] * 4, [col(dk, 0), col(dk, H_C), col(dv, 2 * nq // dv), col(dv, 2 * nq // dv + H_C)]


def _ret_prompt_call(r16, gn, batch, seq, dk, dv, name):
    c = RET_CHUNK
    nt = seq // c
    dmask, scales = _ret_constants(1, c)
    cos, sin = _rope_tables(jnp.arange(seq, dtype=jnp.int32), dk // 2)
    tok = lambda b, t: b * nt + t
    nq, nv = H_C * dk, H_C * dv
    args = [r16, r16, r16, r16, cos, sin, dmask, scales, gn.reshape(1, dv)]
    specs = [pl.BlockSpec((c, nq), lambda b, t: (tok(b, t), 0)),
             pl.BlockSpec((c, nq), lambda b, t: (tok(b, t), 1)),
             pl.BlockSpec((c, nv), lambda b, t: (tok(b, t), 2 * nq // nv)),
             pl.BlockSpec((c, nv), lambda b, t: (tok(b, t), 2 * nq // nv + 1)),
             pl.BlockSpec((c, dk // 2), lambda b, t: (t, 0)), pl.BlockSpec((c, dk // 2), lambda b, t: (t, 0)),
             _resident(dmask.shape), _resident(scales.shape), _resident((1, dv))]
    return pl.pallas_call(
        _ret_prompt_kernel,
        out_shape=[jax.ShapeDtypeStruct((r16.shape[0], nv), BF16),
                   jax.ShapeDtypeStruct((batch, H_C, dk, dv), F32)],
        grid=(batch, nt),
        in_specs=specs,
        out_specs=[pl.BlockSpec((c, nv), lambda b, t: (tok(b, t), 0)),
                   pl.BlockSpec((1, H_C, dk, dv), lambda b, t: (b, 0, 0, 0))],
        scratch_shapes=[pltpu.VMEM((H_C, dk, dv), F32)],
        compiler_params=_params(2, 40 << 20),
        name=name,
    )(*args)


def _ret_sample_call(r16, gn, s0, row0, batch, seq, dk, dv, o_prev, name):
    nb = SAMPLE_SEQS
    rows = nb * seq
    blk0 = row0 // rows
    dmask, scales = _ret_constants(nb, seq)
    pos = PAST_LEN + (jnp.arange(rows, dtype=jnp.int32) % seq)
    cos, sin = _rope_tables(pos, dk // 2)
    tok = lambda i, h: (blk0 + i, h)
    args, specs = _ret_specs(r16, dk, dv, rows, tok)
    state_spec = pl.BlockSpec((nb, 1, dk, dv), lambda i, h: (i, h, 0, 0))
    args += [cos, sin, dmask, scales, gn.reshape(1, dv), s0, o_prev]
    specs += [_resident(cos.shape)] * 2
    specs += [pl.BlockSpec((1, rows, rows), lambda i, h: (h, 0, 0)),
              pl.BlockSpec((1, 2, rows, LANES), lambda i, h: (h, 0, 0, 0)), _resident((1, dv)),
              state_spec, pl.BlockSpec(memory_space=pl.ANY)]
    return pl.pallas_call(
        functools.partial(_ret_sample_kernel, ls=seq),
        out_shape=[jax.ShapeDtypeStruct(o_prev.shape, BF16),
                   jax.ShapeDtypeStruct((batch, H_C, dk, dv), F32)],
        grid=(batch // nb, H_C),
        in_specs=specs,
        out_specs=[pl.BlockSpec((rows, dv), tok), state_spec],
        input_output_aliases={len(args) - 1: 0},
        compiler_params=_params(2, 48 << 20),
        name=name,
    )(*args)


def _pad_heads(w, n_heads, width):
    d = w.shape[0]
    w = w.reshape(d, n_heads, -1)
    return jnp.pad(w, ((0, 0), (0, 0), (0, width - w.shape[2]))).reshape(d, n_heads * width)


def _prepare_ab_weights(w_in, w_gk2, b_gk, dk_a, dv_a, dk_b, dv_b):
    sizes = (H_A * dk_a, H_A * dk_a, H_A * dv_a, H_A * dv_a, H_B * dk_b, H_B * dk_b, H_B * dv_b,
             GLA_RANK, H_B * dv_b)
    qa, fa, ia, ga, qb, kb, vb, lrb, gb = jnp.split(w_in, [int(v) for v in np.cumsum(sizes)[:-1]], axis=1)
    w16 = jnp.concatenate([qa, ia, ga, _pad_heads(qb, H_B, LANES), _pad_heads(kb, H_B, LANES), vb, gb],
                          axis=1).astype(BF16)
    w32 = jnp.concatenate([fa, jnp.pad(lrb, ((0, 0), (0, LANES - GLA_RANK)))], axis=1).astype(BF16)
    w_gk = jnp.pad(_pad_heads(w_gk2, H_B, LANES), ((0, LANES - GLA_RANK), (0, 0))).astype(BF16)
    b_gk = _pad_heads(b_gk.reshape(1, -1), H_B, LANES)
    return w16, w32, w_gk, b_gk


def kernel(x_prompt, x_sample, state_hgrn, state_gla, state_ret, p_prompt, p_sample, norm_ffn1, ffn1_w_in, ffn1_w_out, norm_mix, ab_w_in, ab_w_gk2, ab_b_gk, hgrn_lb_logits, ab_gn_hgrn, ab_gn_gla, ab_w_out, ret_w_in, ret_gn, ret_w_out, norm_ffn2, ffn2_w_in, ffn2_w_out, norm_ple, ple_w_gate, ple_w_proj, norm_final):
    bp, lp, d = x_prompt.shape
    bs, ls, _ = x_sample.shape
    depth = norm_ffn1.shape[0]
    n_prompt = bp * lp
    dk_a, dv_a = state_hgrn.shape[-2:]
    dk_b, dv_b = state_gla.shape[-2:]
    dk_c, dv_c = state_ret.shape[-2:]
    assert dk_a == dv_a == dv_b == LANES and dk_b <= LANES and state_hgrn.shape[0] == 1

    n_rows = (n_prompt, bs * ls)
    xs = (x_prompt.reshape(n_prompt, d), x_sample.reshape(bs * ls, d))
    ps = (p_prompt.reshape(depth, n_prompt, -1), p_sample.reshape(depth, bs * ls, -1))

    new_hgrn_p, new_gla_p, new_ret_p, new_hgrn_s, new_gla_s, new_ret_s = [], [], [], [], [], []
    for i in range(depth):
        j = i // 2
        x = _ffn_call(xs if i == 0 else (x,), n_rows, norm_ffn1[i], ffn1_w_in[i].astype(BF16),
                      ffn1_w_out[i].astype(BF16), name=f"ffn1_{i}")
        if i % 2 == 0:
            w16, w32, w_gk, b_gk = _prepare_ab_weights(ab_w_in[j], ab_w_gk2[j], ab_b_gk[j], dk_a, dv_a, dk_b, dv_b)
            p16, p32 = _inproj_call(x, norm_mix[i], [w16, w32], [BF16, F32], [w16.shape[1] // 2, w32.shape[1]],
                                    name=f"ab_in_{i}")
            oa, sa_p = _ab_prompt_call("hgrn", p16, p32, (hgrn_lb_logits,), ab_gn_hgrn[j], bp, lp, dk_a,
                                       f"hgrn_p_{i}")
            oa, sa_s = _ab_sample_call("hgrn", p16, p32, (hgrn_lb_logits,), ab_gn_hgrn[j], state_hgrn[j],
                                       n_prompt, ls, oa, f"hgrn_s_{i}")
            ob, sb_p = _ab_prompt_call("gla", p16, p32, (w_gk, b_gk), ab_gn_gla[j], bp, lp, dk_b, f"gla_p_{i}")
            ob, sb_s = _ab_sample_call("gla", p16, p32, (w_gk, b_gk), ab_gn_gla[j], state_gla[j],
                                       n_prompt, ls, ob, f"gla_s_{i}")
            new_hgrn_p.append(sa_p)
            new_hgrn_s.append(sa_s)
            new_gla_p.append(sb_p)
            new_gla_s.append(sb_s)
            w_out = ab_w_out[j].astype(BF16)
            na = H_A * dv_a
            x = _outproj_call(x, [oa, ob], [w_out[:na], w_out[na:]], name=f"ab_out_{i}")
        else:
            r16, = _inproj_call(x, norm_mix[i], [ret_w_in[j].astype(BF16)], [BF16], [2048], name=f"ret_in_{i}")
            oc, sc_p = _ret_prompt_call(r16, ret_gn[j], bp, lp, dk_c, dv_c, f"ret_p_{i}")
            oc, sc_s = _ret_sample_call(r16, ret_gn[j], state_ret[j], n_prompt, bs, ls, dk_c, dv_c, oc, f"ret_s_{i}")
            new_ret_p.append(sc_p)
            new_ret_s.append(sc_s)
            x = _outproj_call(x, [oc], [ret_w_out[j].astype(BF16)], name=f"ret_out_{i}")
        last = i == depth - 1
        x = _ffn_call((x,), n_rows, norm_ffn2[i], ffn2_w_in[i].astype(BF16), ffn2_w_out[i].astype(BF16),
                      ple=(ps, i, norm_ple[i], ple_w_gate[i].astype(BF16), ple_w_proj[i].astype(BF16)),
                      final_gain=norm_final if last else None, split_out=last, name=f"ffn2_{i}")

    y_prompt = x[0].reshape(bp, lp, d)
    y_sample = x[1].reshape(bs, ls, d)
    return (y_prompt, y_sample, jnp.stack(new_hgrn_p), jnp.stack(new_gla_p), jnp.stack(new_ret_p),
            jnp.stack(new_hgrn_s), jnp.stack(new_gla_s), jnp.stack(new_ret_s))
```

```python
import functools
import math

import numpy as np
import jax
import jax.numpy as jnp
from jax import lax
from jax.experimental import pallas as pl
from jax.experimental.pallas import tpu as pltpu

F32 = jnp.float32
BF16 = jnp.bfloat16

H_A = 4
H_B = 4
H_C = 4
GLA_RANK = 16
GLA_GATE_NORM = 16.0
ROPE_THETA = 10000.0
PAST_LEN = 16384
EPS = 1e-6

LANES = 128
V7X_MXU_COLS = 256
V7X_VMEM_BUDGET_BYTES = 56 * 1024 * 1024

ROW_TILE = 512
GLA_CHUNK = 64
GLA_STEP_TOKENS = 128
RET_CHUNK = 256
SAMPLE_SEQS = 8


def _params(n_axes, vmem_bytes):
    return pltpu.CompilerParams(dimension_semantics=("arbitrary",) * n_axes,
                                vmem_limit_bytes=min(int(vmem_bytes), V7X_VMEM_BUDGET_BYTES))


def _resident(shape):
    nd = len(shape)
    return pl.BlockSpec(shape, lambda *_: (0,) * nd, pipeline_mode=pl.Buffered(1))


def _resident_layer(shape, layer):
    nd = len(shape)
    return pl.BlockSpec((1,) + tuple(shape[1:]), lambda *_: (layer,) + (0,) * (nd - 1),
                        pipeline_mode=pl.Buffered(1))


def _row_maps(prompt_tiles, lead=0):
    stacked = lambda i: (jnp.maximum(i - lead, 0), 0)
    prompt = lambda i: (jnp.clip(i - lead, 0, prompt_tiles - 1), 0)
    sample = lambda i: (jnp.maximum(i - lead - prompt_tiles, 0), 0)
    return stacked, prompt, sample


def _rms(x, g):
    return x * lax.rsqrt(jnp.mean(x * x, axis=-1, keepdims=True) + EPS) * g


def _silu(x):
    return x * jax.nn.sigmoid(x)


def _dot(a, b):
    return jnp.dot(a, b, preferred_element_type=F32)


def _dot_nt(a, b):
    return lax.dot_general(a, b, (((1,), (1,)), ((), ())), preferred_element_type=F32)


def _dot_tn(a, b):
    return lax.dot_general(a, b, (((0,), (0,)), ((), ())), preferred_element_type=F32)


def _pick(is_prompt, refs, index=None):
    vals = [r[...] if index is None else r[index] for r in refs]
    return vals[0] if len(vals) == 1 else jnp.where(is_prompt, vals[0], vals[1])


def _ffn_kernel(*refs, n_x, has_ple, has_final, n_o, prompt_tiles, stage_steps, gate_steps):
    it = iter(refs)
    x_refs = [next(it) for _ in range(n_x)]
    g_ref, win_ref, wout_ref = next(it), next(it), next(it)
    if has_ple:
        p_refs = [next(it), next(it)]
        gp_ref, wg_ref, wp_ref = next(it), next(it), next(it)
    if has_final:
        gf_ref = next(it)
    o_refs = [next(it) for _ in range(n_o)]
    win_s, wout_s, act_ref = next(it), next(it), next(it)
    if has_ple:
        wg_s, wp_s = next(it), next(it)
    i = pl.program_id(0)
    cols = win_s.shape[2]
    n_half = win_s.shape[0] // 2

    @pl.when(i < stage_steps)
    def _():
        blk = win_ref[0]
        win_s[2 * i] = blk[:, :cols].astype(BF16)
        win_s[2 * i + 1] = blk[:, cols:].astype(BF16)
        wout_s[pl.ds(pl.multiple_of(i * cols, cols), cols), :] = wout_ref[0].astype(BF16)

    if has_ple:
        rows = wg_ref.shape[1]

        @pl.when(i < gate_steps)
        def _():
            wg_s[pl.ds(pl.multiple_of(i * rows, rows), rows), :] = wg_ref[0].astype(BF16)

        @pl.when(i == 0)
        def _():
            wp_s[...] = wp_ref[0].astype(BF16)

    @pl.when(i >= stage_steps)
    def _():
        is_prompt = i - stage_steps < prompt_tiles
        x = _pick(is_prompt, x_refs)
        xn = _rms(x, g_ref[...]).astype(BF16)
        for j in range(n_half):
            a = _dot(xn, win_s[j])
            b = _dot(xn, win_s[n_half + j])
            act_ref[:, j * cols:(j + 1) * cols] = (_silu(a) * b).astype(BF16)
        x = x + 0.5 * _dot(act_ref[...], wout_s[...])
        if has_ple:
            gate = jax.nn.sigmoid(_dot(_rms(x, gp_ref[...]).astype(BF16), wg_s[...]))
            x = x + gate * _dot(_pick(is_prompt, p_refs, 0).astype(BF16), wp_s[...])
        if has_final:
            x = _rms(x, gf_ref[...])
        if n_o == 1:
            o_refs[0][...] = x
        else:
            @pl.when(is_prompt)
            def _():
                o_refs[0][...] = x

            @pl.when(jnp.logical_not(is_prompt))
            def _():
                o_refs[1][...] = x


def _ffn_call(xs, n_rows, gain, w_in, w_out, layer, ple=None, final_gain=None, split_out=False, name="ffn"):
    d = xs[0].shape[1]
    d_ff = w_out.shape[1]
    tm = ROW_TILE
    cols = V7X_MXU_COLS
    assert n_rows[0] % tm == 0 and n_rows[1] % tm == 0 and d_ff % cols == 0
    pt, st = n_rows[0] // tm, n_rows[1] // tm
    stage = d_ff // cols
    stacked, prompt, sample = _row_maps(pt, stage)
    pair = [prompt, sample]
    last = stage - 1
    args = [*xs, gain.reshape(1, d), w_in, w_out]
    specs = [pl.BlockSpec((tm, d), m) for m in ([stacked] if len(xs) == 1 else pair)]
    specs += [_resident((1, d)),
              pl.BlockSpec((1, d, 2 * cols), lambda i: (layer, 0, jnp.minimum(i, last))),
              pl.BlockSpec((1, cols, d), lambda i: (layer, jnp.minimum(i, last), 0))]
    scratch = [pltpu.VMEM((2 * stage, d, cols), BF16), pltpu.VMEM((d_ff, d), BF16), pltpu.VMEM((tm, d_ff), BF16)]
    gate_steps = 0
    if ple is not None:
        ps, gp, wg, wp = ple
        gate_rows = LANES
        gate_steps = d // gate_rows
        assert gate_steps <= stage
        args += [*ps, gp.reshape(1, d), wg, wp]
        specs += [pl.BlockSpec((1, tm, ps[0].shape[2]), lambda i, m=m: (layer, *m(i))) for m in pair]
        specs += [_resident((1, d)),
                  pl.BlockSpec((1, gate_rows, d), lambda i: (layer, jnp.minimum(i, gate_steps - 1), 0)),
                  _resident_layer(wp.shape, layer)]
        scratch += [pltpu.VMEM((d, d), BF16), pltpu.VMEM(wp.shape[1:], BF16)]
    if final_gain is not None:
        args.append(final_gain.reshape(1, d))
        specs.append(_resident((1, d)))
    if split_out:
        out_shape = [jax.ShapeDtypeStruct((r, d), F32) for r in n_rows]
        out_specs = [pl.BlockSpec((tm, d), m) for m in pair]
    else:
        out_shape = jax.ShapeDtypeStruct((sum(n_rows), d), F32)
        out_specs = pl.BlockSpec((tm, d), stacked)
    return pl.pallas_call(
        functools.partial(_ffn_kernel, n_x=len(xs), has_ple=ple is not None, has_final=final_gain is not None,
                          n_o=2 if split_out else 1, prompt_tiles=pt, stage_steps=stage, gate_steps=gate_steps),
        out_shape=out_shape,
        grid=(stage + pt + st,),
        in_specs=specs,
        out_specs=out_specs,
        scratch_shapes=scratch,
        compiler_params=_params(1, V7X_VMEM_BUDGET_BYTES),
        name=name,
    )(*args)


def _inproj_kernel(*refs, n_out, chunk_cols):
    x_ref, g_ref = refs[0], refs[1]
    w_refs = refs[2:2 + n_out]
    o_refs = refs[2 + n_out:2 + 2 * n_out]
    xn = _rms(x_ref[...], g_ref[...]).astype(BF16)
    for w_ref, o_ref, cw in zip(w_refs, o_refs, chunk_cols):
        for c in range(w_ref.shape[1] // cw):
            o_ref[:, c * cw:(c + 1) * cw] = _dot(xn, w_ref[:, c * cw:(c + 1) * cw]).astype(o_ref.dtype)


def _inproj_call(x, gain, weights, out_dtypes, chunk_cols, name):
    n, d = x.shape
    tm = ROW_TILE
    row = lambda i: (i, 0)
    specs = [pl.BlockSpec((tm, d), row), _resident((1, d))] + [_resident(w.shape) for w in weights]
    out_shape = [jax.ShapeDtypeStruct((n, w.shape[1]), dt) for w, dt in zip(weights, out_dtypes)]
    out_specs = [pl.BlockSpec((tm, w.shape[1]), row) for w in weights]
    vmem = (sum(2 * w.size for w in weights) + 4 * tm * d * 4
            + sum(2 * tm * w.shape[1] * jnp.dtype(dt).itemsize for w, dt in zip(weights, out_dtypes))
            + 2 * tm * max(chunk_cols) * 4 + (4 << 20))
    return pl.pallas_call(
        functools.partial(_inproj_kernel, n_out=len(weights), chunk_cols=tuple(chunk_cols)),
        out_shape=out_shape,
        grid=(n // tm,),
        in_specs=specs,
        out_specs=out_specs,
        compiler_params=_params(1, vmem),
        name=name,
    )(x, gain.reshape(1, d), *weights)


def _inproj_staged_kernel(x_ref, g_ref, w_ref, o_ref, w_s, *, stage_steps):
    i = pl.program_id(0)
    cols = w_s.shape[2]

    @pl.when(i < stage_steps)
    def _():
        w_s[i] = w_ref[0].astype(BF16)

    @pl.when(i >= stage_steps)
    def _():
        xn = _rms(x_ref[...], g_ref[...]).astype(BF16)
        for c in range(stage_steps):
            o_ref[:, c * cols:(c + 1) * cols] = _dot(xn, w_s[c]).astype(o_ref.dtype)


def _inproj_staged_call(x, gain, w, layer, cols, name):
    n, d = x.shape
    n_out = w.shape[2]
    tm = ROW_TILE
    assert n_out % cols == 0
    stage = n_out // cols
    row = lambda i: (jnp.maximum(i - stage, 0), 0)
    vmem = 2 * d * n_out + 2 * d * cols * 4 + 4 * tm * d * 4 + 2 * tm * n_out * 2 + 2 * tm * cols * 4 + (4 << 20)
    return pl.pallas_call(
        functools.partial(_inproj_staged_kernel, stage_steps=stage),
        out_shape=jax.ShapeDtypeStruct((n, n_out), BF16),
        grid=(stage + n // tm,),
        in_specs=[pl.BlockSpec((tm, d), row), _resident((1, d)),
                  pl.BlockSpec((1, d, cols), lambda i: (layer, 0, jnp.minimum(i, stage - 1)))],
        out_specs=pl.BlockSpec((tm, n_out), row),
        scratch_shapes=[pltpu.VMEM((stage, d, cols), BF16)],
        compiler_params=_params(1, vmem),
        name=name,
    )(x, gain.reshape(1, d), w)


def _outproj_kernel(*refs, n_parts, prompt_tiles):
    x_ref = refs[0]
    part_refs = refs[1:1 + 2 * n_parts]
    w_ref, o_ref, w_s = refs[1 + 2 * n_parts:]
    i = pl.program_id(0)

    @pl.when(i == 0)
    def _():
        w_s[...] = w_ref[0].astype(BF16)

    is_prompt = i < prompt_tiles
    acc = x_ref[...]
    k0 = 0
    for j in range(n_parts):
        part = _pick(is_prompt, part_refs[2 * j:2 * j + 2])
        acc = acc + _dot(part, w_s[k0:k0 + part.shape[1], :])
        k0 += part.shape[1]
    o_ref[...] = acc


def _outproj_call(x, n_rows, parts, w, layer, name):
    n, d = x.shape
    tm = ROW_TILE
    pt = n_rows[0] // tm
    stacked, prompt, sample = _row_maps(pt)
    specs = [pl.BlockSpec((tm, d), stacked)]
    args = [x]
    for pp, sp in parts:
        args += [pp, sp]
        specs += [pl.BlockSpec((tm, pp.shape[1]), prompt), pl.BlockSpec((tm, sp.shape[1]), sample)]
    args.append(w)
    specs.append(_resident_layer(w.shape, layer))
    kdim = w.shape[1]
    vmem = 6 * kdim * d + 6 * tm * d * 4 + 8 * tm * kdim * 2 + (4 << 20)
    return pl.pallas_call(
        functools.partial(_outproj_kernel, n_parts=len(parts), prompt_tiles=pt),
        out_shape=jax.ShapeDtypeStruct((n, d), F32),
        grid=(n // tm,),
        in_specs=specs,
        out_specs=pl.BlockSpec((tm, d), stacked),
        scratch_shapes=[pltpu.VMEM((kdim, d), BF16)],
        compiler_params=_params(1, vmem),
        name=name,
    )(*args)


GLA_MATMUL_LEVELS = (1, 2)


def _gla_constants(n_seq, ls, width):
    c = n_seq * ls
    nlev = int(round(math.log2(ls)))
    assert 2 ** nlev == ls and nlev > max(GLA_MATMUL_LEVELS)
    t = np.arange(c)[:, None]
    i = np.arange(c)[None, :]
    w_rows = [((t // ls) == (i // ls)) & (i <= t)]
    masks, uppers = [], []
    for l in range(nlev):
        m = 2 ** l
        upper = ((t // m) % 2) == 1
        r = (t // (2 * m)) * (2 * m) + m - 1
        if l in GLA_MATMUL_LEVELS:
            w_rows.append(np.where(upper, (i > r) & (i <= t), (i > t) & (i <= r)))
        masks.append(((t // (2 * m)) == (i // (2 * m))) & upper & (((i // m) % 2) == 0))
        uppers.append(np.broadcast_to(upper, (c, width)))
    w = jnp.asarray(np.concatenate(w_rows, 0).astype(np.float32), BF16)
    return w, jnp.asarray(np.stack(masks).astype(np.float32)), jnp.asarray(np.stack(uppers).astype(np.float32))


def _gla_block(q, k, v, g, st_list, w_ref, m_ref, u_ref, ls):
    c, width = q.shape
    nh = width // LANES
    n_seq = c // ls
    nlev = m_ref.shape[0]
    g_hi = g.astype(BF16)
    g_lo = (g - g_hi.astype(F32)).astype(BF16)
    ex2 = _dot(w_ref[...], jnp.concatenate([g_hi, g_lo], axis=1))
    ex = ex2[:, :width] + ex2[:, width:]
    b = ex[:c]

    def row(r, n):
        return jnp.broadcast_to(b[r:r + 1, :], (n, width))

    b_end = jnp.concatenate([row((s + 1) * ls - 1, ls) for s in range(n_seq)], axis=0) if n_seq > 1 else row(c - 1, c)
    e_cum = jnp.exp(b)
    qe = q * e_cum
    kr = k * jnp.exp(b_end - b)

    att = [None] * nh
    for l in range(nlev):
        upper = u_ref[l] > 0.5
        if l == 0:
            x = jnp.where(upper, q * jnp.exp(g), k)
        else:
            if l in GLA_MATMUL_LEVELS:
                j = 1 + GLA_MATMUL_LEVELS.index(l)
                e = jnp.exp(ex[j * c:(j + 1) * c])
            else:
                m = 2 ** l
                d = b - jnp.concatenate([row(p * 2 * m + m - 1, 2 * m) for p in range(c // (2 * m))], axis=0) \
                    if c > 2 * m else b - row(m - 1, c)
                e = jnp.exp(jnp.minimum(d, -d))
            x = jnp.where(upper, q, k) * e
        x = x.astype(BF16)
        mask = m_ref[l] > 0.5
        for h in range(nh):
            xh = x[:, h * LANES:(h + 1) * LANES]
            p = _dot_nt(xh, xh)
            att[h] = jnp.where(mask, p, 0.0 if att[h] is None else att[h])

    qk = q * k
    outs, st_new = [], [[None] * nh for _ in range(n_seq)]
    for h in range(nh):
        sl = slice(h * LANES, (h + 1) * LANES)
        vh = v[:, sl]
        o = _dot(att[h].astype(BF16), vh.astype(BF16)) + jnp.sum(qk[:, sl], axis=1, keepdims=True) * vh
        parts = []
        for s in range(n_seq):
            rows = slice(s * ls, (s + 1) * ls)
            st = st_list[s][h]
            parts.append(o[rows] + _dot_nt(qe[rows, sl].astype(BF16), st.astype(BF16)))
            e_tot = e_cum[(s + 1) * ls - 1:(s + 1) * ls, sl]
            st_new[s][h] = e_tot * st + _dot_tn(vh[rows].astype(BF16), kr[rows, sl].astype(BF16))
        outs.append(parts[0] if n_seq == 1 else jnp.concatenate(parts, axis=0))
    return outs, st_new


def _hgrn_inputs(refs, rows, q_scale):
    del q_scale
    qa_ref, ia_ref, ga_ref, fa_ref, lg_ref = refs
    lg = lg_ref[...]
    ex = jnp.exp(lg - jnp.max(lg, axis=0, keepdims=True))
    lb = ex[0:1] / jnp.sum(ex, axis=0, keepdims=True)
    fz = fa_ref[rows, :]
    q = _silu(qa_ref[rows, :].astype(F32))
    k = (1.0 - lb) * jax.nn.sigmoid(-fz)
    g = jnp.log(lb + (1.0 - lb) * jax.nn.sigmoid(fz))
    return q, k, ia_ref[rows, :].astype(F32), g, ga_ref[rows, :].astype(F32)


def _gla_inputs(refs, rows, q_scale):
    qb_ref, kb_ref, vb_ref, gb_ref, lr_ref, wgk_ref, bgk_ref = refs
    z = _dot(lr_ref[rows, :].astype(BF16), wgk_ref[...]) + bgk_ref[...]
    g = (jnp.minimum(z, 0.0) - jnp.log1p(jnp.exp(-jnp.abs(z)))) / GLA_GATE_NORM
    q = qb_ref[rows, :].astype(F32) * q_scale
    return q, kb_ref[rows, :].astype(F32), vb_ref[rows, :].astype(F32), g, gb_ref[rows, :].astype(F32)


def _head_out(outs, gate, gn):
    ys = [_rms(o, gn) * _silu(gate[:, h * LANES:(h + 1) * LANES]) for h, o in enumerate(outs)]
    return jnp.concatenate(ys, axis=1).astype(BF16)


def _ab_prompt_kernel(*refs, kind, n_in, chunk, q_scale):
    in_refs = refs[:n_in]
    gn_ref, w_ref, m_ref, u_ref, o_ref, s_ref, st_ref = refs[n_in:]
    t = pl.program_id(1)
    nh = st_ref.shape[0]

    @pl.when(t == 0)
    def _():
        st_ref[...] = jnp.zeros_like(st_ref)

    load = _hgrn_inputs if kind == "hgrn" else _gla_inputs
    for c in range(o_ref.shape[0] // chunk):
        rows = slice(c * chunk, (c + 1) * chunk)
        q, k, v, g, gate = load(in_refs, rows, q_scale)
        outs, (st,) = _gla_block(q, k, v, g, [[st_ref[h] for h in range(nh)]], w_ref, m_ref, u_ref, chunk)
        for h in range(nh):
            st_ref[h] = st[h]
        o_ref[rows, :] = _head_out(outs, gate, gn_ref[...])

    @pl.when(t == pl.num_programs(1) - 1)
    def _():
        for h in range(nh):
            s_ref[0, h] = st_ref[h].T[:s_ref.shape[2], :]


def _ab_sample_kernel(*refs, kind, n_in, ls, q_scale):
    in_refs = refs[:n_in]
    gn_ref, w_ref, m_ref, u_ref, s0_ref, o_ref, s_ref = refs[n_in:]
    load = _hgrn_inputs if kind == "hgrn" else _gla_inputs
    n_seq, nh, dk = s0_ref.shape[:3]
    q, k, v, g, gate = load(in_refs, slice(None), q_scale)
    st0 = []
    for s in range(n_seq):
        st0.append([])
        for h in range(nh):
            s0 = s0_ref[s, h]
            if dk < LANES:
                s0 = jnp.concatenate([s0, jnp.zeros((LANES - dk, s0.shape[1]), F32)], axis=0)
            st0[s].append(s0.T)
    outs, st = _gla_block(q, k, v, g, st0, w_ref, m_ref, u_ref, ls)
    o_ref[...] = _head_out(outs, gate, gn_ref[...])
    for s in range(n_seq):
        for h in range(nh):
            s_ref[s, h] = st[s][h].T[:dk, :]


def _ab_specs(kind, p16, p32, extra, tok_block, tok_index, nh):
    width = nh * LANES

    def col(group, w=width):
        return pl.BlockSpec((tok_block, w), lambda *ids: (tok_index(*ids), group))

    if kind == "hgrn":
        (logits,) = extra
        args = [p16, p16, p16, p32, logits]
        specs = [col(0), col(1), col(2), col(0), _resident(logits.shape)]
    else:
        w_gk, b_gk = extra
        args = [p16, p16, p16, p16, p32, w_gk, b_gk]
        specs = [col(3), col(4), col(5), col(6), col(H_A, LANES), _resident(w_gk.shape),
                 _resident(b_gk.shape)]
    return args, specs


def _ab_prompt_call(kind, p16, p32, extra, gn, batch, seq, dk, name):
    nh = H_A if kind == "hgrn" else H_B
    tt = GLA_STEP_TOKENS
    nt = seq // tt
    consts = _gla_constants(1, GLA_CHUNK, nh * LANES)
    args, specs = _ab_specs(kind, p16, p32, extra, tt, lambda b, t: b * nt + t, nh)
    n_in = len(args)
    args += [gn.reshape(1, LANES), *consts]
    specs += [_resident((1, LANES))] + [_resident(c.shape) for c in consts]
    o, s = pl.pallas_call(
        functools.partial(_ab_prompt_kernel, kind=kind, n_in=n_in, chunk=GLA_CHUNK, q_scale=dk ** -0.5),
        out_shape=[jax.ShapeDtypeStruct((batch * seq, nh * LANES), BF16),
                   jax.ShapeDtypeStruct((batch, nh, dk, LANES), F32)],
        grid=(batch, nt),
        in_specs=specs,
        out_specs=[pl.BlockSpec((tt, nh * LANES), lambda b, t: (b * nt + t, 0)),
                   pl.BlockSpec((1, nh, dk, LANES), lambda b, t: (b, 0, 0, 0))],
        scratch_shapes=[pltpu.VMEM((nh, LANES, LANES), F32)],
        compiler_params=_params(2, 32 << 20),
        name=name,
    )(*args)
    return o, s


def _ab_sample_call(kind, p16, p32, extra, gn, s0, row0, seq, name):
    batch, nh, dk, _ = s0.shape
    nb = SAMPLE_SEQS
    rows = nb * seq
    blk0 = row0 // rows
    consts = _gla_constants(nb, seq, nh * LANES)
    args, specs = _ab_specs(kind, p16, p32, extra, rows, lambda i: blk0 + i, nh)
    n_in = len(args)
    args += [gn.reshape(1, LANES), *consts, s0]
    state_spec = pl.BlockSpec((nb, nh, dk, LANES), lambda i: (i, 0, 0, 0))
    specs += [_resident((1, LANES))] + [_resident(c.shape) for c in consts] + [state_spec]
    o, s = pl.pallas_call(
        functools.partial(_ab_sample_kernel, kind=kind, n_in=n_in, ls=seq, q_scale=dk ** -0.5),
        out_shape=[jax.ShapeDtypeStruct((batch * seq, nh * LANES), BF16),
                   jax.ShapeDtypeStruct((batch, nh, dk, LANES), F32)],
        grid=(batch // nb,),
        in_specs=specs,
        out_specs=[pl.BlockSpec((rows, nh * LANES), lambda i: (i, 0)), state_spec],
        compiler_params=_params(1, 32 << 20),
        name=name,
    )(*args)
    return o, s


def _ret_constants(n_seq, ls):
    c = n_seq * ls
    log_gamma = np.log1p(-np.exp2(-5.0 - np.arange(H_C, dtype=np.float64)))[:, None, None]
    t = np.arange(c)[:, None]
    s = np.arange(c)[None, :]
    causal = ((t // ls) == (s // ls)) & (s <= t)
    dmask = np.where(causal, np.exp(log_gamma * np.where(causal, t - s, 0)), 0.0)
    pos = (np.arange(c) % ls)[None, :, None]
    scales = np.stack([np.exp(log_gamma * (pos + 1)), np.exp(log_gamma * (ls - 1 - pos))], axis=1)
    scales = np.broadcast_to(scales, (H_C, 2, c, LANES))
    return jnp.asarray(dmask, F32), jnp.asarray(scales, F32)


def _rope_tables(positions, half):
    inv = ROPE_THETA ** (-jnp.arange(half, dtype=F32) / half)
    ang = positions.astype(F32)[:, None] * inv[None, :]
    return jnp.cos(ang), jnp.sin(ang)


def _rope(x, cos, sin):
    half = x.shape[1] // 2
    x1, x2 = x[:, :half], x[:, half:]
    return jnp.concatenate([x1 * cos - x2 * sin, x2 * cos + x1 * sin], axis=1)


def _ret_block(q16, k16, v, cos, sin, dmask, q_scale, k_scale, s_list, ls):
    dk = q16.shape[1]
    q = _rope(q16.astype(F32), cos, sin)
    k = _rope(k16.astype(F32), cos, sin) * (dk ** -0.5)
    reps = dk // LANES
    q_sc = jnp.concatenate([q_scale] * reps, axis=1)
    k_sc = jnp.concatenate([k_scale] * reps, axis=1)
    att = _dot_nt(q.astype(BF16), k.astype(BF16)) * dmask
    o = _dot(att.astype(BF16), v)
    qs = q * q_sc
    ks = k * k_sc
    decay = q_scale[ls - 1:ls, 0:1]
    n_seq = q.shape[0] // ls
    v32 = v if n_seq == 1 else v.astype(F32)
    o_parts, s_new = [], []
    for b in range(n_seq):
        rows = slice(b * ls, (b + 1) * ls)
        s = s_list[b]
        o_parts.append(o[rows] + _dot(qs[rows].astype(BF16), s.astype(BF16)))
        s_new.append(decay * s + _dot_tn(ks[rows].astype(BF16), v32[rows].astype(BF16)))
    o = o_parts[0] if n_seq == 1 else jnp.concatenate(o_parts, axis=0)
    return o, s_new


def _ret_head_out(o, gate, gn):
    oc = o - jnp.mean(o, axis=-1, keepdims=True)
    y = oc * lax.rsqrt(jnp.mean(oc * oc, axis=-1, keepdims=True) + EPS) * gn
    return (y * _silu(gate)).astype(BF16)


def _ret_prompt_kernel(q_ref, k_ref, v_ref, g_ref, cos_ref, sin_ref, dm_ref, sc_ref, gn_ref,
                       o_ref, s_ref, st_ref):
    t = pl.program_id(1)
    nh, dk, dv = st_ref.shape
    c = q_ref.shape[0]

    @pl.when(t == 0)
    def _():
        st_ref[...] = jnp.zeros_like(st_ref)

    cos, sin = cos_ref[...], sin_ref[...]
    for h in range(nh):
        o, (s,) = _ret_block(q_ref[:, h * dk:(h + 1) * dk], k_ref[:, h * dk:(h + 1) * dk],
                             v_ref[:, h * dv:(h + 1) * dv], cos, sin, dm_ref[h], sc_ref[h, 0], sc_ref[h, 1],
                             [st_ref[h]], c)
        st_ref[h] = s
        o_ref[:, h * dv:(h + 1) * dv] = _ret_head_out(o, g_ref[:, h * dv:(h + 1) * dv].astype(F32), gn_ref[...])

    @pl.when(t == pl.num_programs(1) - 1)
    def _():
        s_ref[0] = st_ref[...]


def _ret_sample_kernel(q_ref, k_ref, v_ref, g_ref, cos_ref, sin_ref, dm_ref, sc_ref, gn_ref, s0_ref,
                       o_ref, s_ref, *, ls):
    n_seq = q_ref.shape[0] // ls
    o, s = _ret_block(q_ref[...], k_ref[...], v_ref[...], cos_ref[...], sin_ref[...], dm_ref[0],
                      sc_ref[0, 0], sc_ref[0, 1], [s0_ref[b, 0] for b in range(n_seq)], ls)
    o_ref[...] = _ret_head_out(o, g_ref[...].astype(F32), gn_ref[...])
    for b in range(n_seq):
        s_ref[b, 0] = s[b]


def _ret_prompt_call(r16, gn, batch, seq, dk, dv, name):
    c = RET_CHUNK
    nt = seq // c
    dmask, scales = _ret_constants(1, c)
    cos, sin = _rope_tables(jnp.arange(seq, dtype=jnp.int32), dk // 2)
    tok = lambda b, t: b * nt + t
    nq, nv = H_C * dk, H_C * dv
    args = [r16, r16, r16, r16, cos, sin, dmask, scales, gn.reshape(1, dv)]
    specs = [pl.BlockSpec((c, nq), lambda b, t: (tok(b, t), 0)),
             pl.BlockSpec((c, nq), lambda b, t: (tok(b, t), 1)),
             pl.BlockSpec((c, nv), lambda b, t: (tok(b, t), 2 * nq // nv)),
             pl.BlockSpec((c, nv), lambda b, t: (tok(b, t), 2 * nq // nv + 1)),
             pl.BlockSpec((c, dk // 2), lambda b, t: (t, 0)), pl.BlockSpec((c, dk // 2), lambda b, t: (t, 0)),
             _resident(dmask.shape), _resident(scales.shape), _resident((1, dv))]
    return pl.pallas_call(
        _ret_prompt_kernel,
        out_shape=[jax.ShapeDtypeStruct((batch * seq, nv), BF16),
                   jax.ShapeDtypeStruct((batch, H_C, dk, dv), F32)],
        grid=(batch, nt),
        in_specs=specs,
        out_specs=[pl.BlockSpec((c, nv), lambda b, t: (tok(b, t), 0)),
                   pl.BlockSpec((1, H_C, dk, dv), lambda b, t: (b, 0, 0, 0))],
        scratch_shapes=[pltpu.VMEM((H_C, dk, dv), F32)],
        compiler_params=_params(2, 40 << 20),
        name=name,
    )(*args)


def _ret_sample_call(r16, gn, s0, row0, batch, seq, dk, dv, name):
    nb = SAMPLE_SEQS
    rows = nb * seq
    blk0 = row0 // rows
    dmask, scales = _ret_constants(nb, seq)
    pos = PAST_LEN + (jnp.arange(rows, dtype=jnp.int32) % seq)
    cos, sin = _rope_tables(pos, dk // 2)
    nq = H_C * dk

    def col(width, base):
        return pl.BlockSpec((rows, width), lambda i, h: (blk0 + i, base + h))

    state_spec = pl.BlockSpec((nb, 1, dk, dv), lambda i, h: (i, h, 0, 0))
    args = [r16, r16, r16, r16, cos, sin, dmask, scales, gn.reshape(1, dv), s0]
    specs = [col(dk, 0), col(dk, H_C), col(dv, 2 * nq // dv), col(dv, 2 * nq // dv + H_C),
             _resident(cos.shape), _resident(sin.shape),
             pl.BlockSpec((1, rows, rows), lambda i, h: (h, 0, 0)),
             pl.BlockSpec((1, 2, rows, LANES), lambda i, h: (h, 0, 0, 0)), _resident((1, dv)), state_spec]
    return pl.pallas_call(
        functools.partial(_ret_sample_kernel, ls=seq),
        out_shape=[jax.ShapeDtypeStruct((batch * seq, H_C * dv), BF16),
                   jax.ShapeDtypeStruct((batch, H_C, dk, dv), F32)],
        grid=(batch // nb, H_C),
        in_specs=specs,
        out_specs=[pl.BlockSpec((rows, dv), lambda i, h: (i, h)), state_spec],
        compiler_params=_params(2, 48 << 20),
        name=name,
    )(*args)


def _pad_heads(w, n_heads, width):
    d = w.shape[0]
    w = w.reshape(d, n_heads, -1)
    return jnp.pad(w, ((0, 0), (0, 0), (0, width - w.shape[2]))).reshape(d, n_heads * width)


def _prepare_ab_weights(w_in, w_gk2, b_gk, dk_a, dv_a, dk_b, dv_b):
    sizes = (H_A * dk_a, H_A * dk_a, H_A * dv_a, H_A * dv_a, H_B * dk_b, H_B * dk_b, H_B * dv_b,
             GLA_RANK, H_B * dv_b)
    qa, fa, ia, ga, qb, kb, vb, lrb, gb = jnp.split(w_in, [int(v) for v in np.cumsum(sizes)[:-1]], axis=1)
    w16 = jnp.concatenate([qa, ia, ga, _pad_heads(qb, H_B, LANES), _pad_heads(kb, H_B, LANES), vb, gb],
                          axis=1).astype(BF16)
    w32 = jnp.concatenate([fa, jnp.pad(lrb, ((0, 0), (0, LANES - GLA_RANK)))], axis=1).astype(BF16)
    w_gk = jnp.pad(_pad_heads(w_gk2, H_B, LANES), ((0, LANES - GLA_RANK), (0, 0))).astype(BF16)
    b_gk = _pad_heads(b_gk.reshape(1, -1), H_B, LANES)
    return w16, w32, w_gk, b_gk


def kernel(x_prompt, x_sample, state_hgrn, state_gla, state_ret, p_prompt, p_sample, norm_ffn1, ffn1_w_in, ffn1_w_out, norm_mix, ab_w_in, ab_w_gk2, ab_b_gk, hgrn_lb_logits, ab_gn_hgrn, ab_gn_gla, ab_w_out, ret_w_in, ret_gn, ret_w_out, norm_ffn2, ffn2_w_in, ffn2_w_out, norm_ple, ple_w_gate, ple_w_proj, norm_final):
    bp, lp, d = x_prompt.shape
    bs, ls, _ = x_sample.shape
    depth = norm_ffn1.shape[0]
    n_prompt = bp * lp
    dk_a, dv_a = state_hgrn.shape[-2:]
    dk_b, dv_b = state_gla.shape[-2:]
    dk_c, dv_c = state_ret.shape[-2:]
    assert dk_a == dv_a == dv_b == LANES and dk_b <= LANES and state_hgrn.shape[0] == 1

    n_rows = (n_prompt, bs * ls)
    xs = (x_prompt.reshape(n_prompt, d), x_sample.reshape(bs * ls, d))
    ps = (p_prompt.reshape(depth, n_prompt, -1), p_sample.reshape(depth, bs * ls, -1))

    new_hgrn_p, new_gla_p, new_ret_p, new_hgrn_s, new_gla_s, new_ret_s = [], [], [], [], [], []
    for i in range(depth):
        j = i // 2
        x = _ffn_call(xs if i == 0 else (x,), n_rows, norm_ffn1[i], ffn1_w_in, ffn1_w_out, i, name=f"ffn1_{i}")
        if i % 2 == 0:
            w16, w32, w_gk, b_gk = _prepare_ab_weights(ab_w_in[j], ab_w_gk2[j], ab_b_gk[j], dk_a, dv_a, dk_b, dv_b)
            p16, p32 = _inproj_call(x, norm_mix[i], [w16, w32], [BF16, F32], [w16.shape[1] // 2, w32.shape[1]],
                                    name=f"ab_in_{i}")
            hgrn = ("hgrn", p16, p32, (hgrn_lb_logits,), ab_gn_hgrn[j])
            gla = ("gla", p16, p32, (w_gk, b_gk), ab_gn_gla[j])
            oa_p, sa_p = _ab_prompt_call(*hgrn, bp, lp, dk_a, f"hgrn_p_{i}")
            oa_s, sa_s = _ab_sample_call(*hgrn, state_hgrn[j], n_prompt, ls, f"hgrn_s_{i}")
            ob_p, sb_p = _ab_prompt_call(*gla, bp, lp, dk_b, f"gla_p_{i}")
            ob_s, sb_s = _ab_sample_call(*gla, state_gla[j], n_prompt, ls, f"gla_s_{i}")
            new_hgrn_p.append(sa_p)
            new_hgrn_s.append(sa_s)
            new_gla_p.append(sb_p)
            new_gla_s.append(sb_s)
            x = _outproj_call(x, n_rows, [(oa_p, oa_s), (ob_p, ob_s)], ab_w_out, j, name=f"ab_out_{i}")
        else:
            r16 = _inproj_staged_call(x, norm_mix[i], ret_w_in, j, 2 * V7X_MXU_COLS, name=f"ret_in_{i}")
            oc_p, sc_p = _ret_prompt_call(r16, ret_gn[j], bp, lp, dk_c, dv_c, f"ret_p_{i}")
            oc_s, sc_s = _ret_sample_call(r16, ret_gn[j], state_ret[j], n_prompt, bs, ls, dk_c, dv_c, f"ret_s_{i}")
            new_ret_p.append(sc_p)
            new_ret_s.append(sc_s)
            x = _outproj_call(x, n_rows, [(oc_p, oc_s)], ret_w_out, j, name=f"ret_out_{i}")
        last = i == depth - 1
        x = _ffn_call((x,), n_rows, norm_ffn2[i], ffn2_w_in, ffn2_w_out, i,
                      ple=(ps, norm_ple[i], ple_w_gate, ple_w_proj),
                      final_gain=norm_final if last else None, split_out=last, name=f"ffn2_{i}")

    y_prompt = x[0].reshape(bp, lp, d)
    y_sample = x[1].reshape(bs, ls, d)
    return (y_prompt, y_sample, jnp.stack(new_hgrn_p), jnp.stack(new_gla_p), jnp.stack(new_ret_p),
            jnp.stack(new_hgrn_s), jnp.stack(new_gla_s), jnp.stack(new_ret_s))
```

```python
import functools
import math

import numpy as np
import jax
import jax.numpy as jnp
from jax import lax
from jax.experimental import pallas as pl
from jax.experimental.pallas import tpu as pltpu

F32 = jnp.float32
BF16 = jnp.bfloat16

H_A = 4
H_B = 4
H_C = 4
GLA_RANK = 16
GLA_GATE_NORM = 16.0
ROPE_THETA = 10000.0
PAST_LEN = 16384
EPS = 1e-6
LOG2_E = math.log2(math.e)

LANES = 128
V7X_MXU_COLS = 256
V7X_VMEM_BUDGET_BYTES = 56 * 1024 * 1024

ROW_TILE = 512
GLA_CHUNK = 64
GLA_STEP_TOKENS = 256
RET_CHUNK = 256
SAMPLE_SEQS = 8


def _params(n_axes, vmem_bytes):
    return pltpu.CompilerParams(dimension_semantics=("arbitrary",) * n_axes,
                                vmem_limit_bytes=min(int(vmem_bytes), V7X_VMEM_BUDGET_BYTES))


def _resident(shape):
    nd = len(shape)
    return pl.BlockSpec(shape, lambda *_: (0,) * nd, pipeline_mode=pl.Buffered(1))


def _resident_layer(shape, layer):
    nd = len(shape)
    return pl.BlockSpec((1,) + tuple(shape[1:]), lambda *_: (layer,) + (0,) * (nd - 1),
                        pipeline_mode=pl.Buffered(1))


def _row_maps(prompt_tiles, lead=0):
    stacked = lambda i: (jnp.maximum(i - lead, 0), 0)
    prompt = lambda i: (jnp.clip(i - lead, 0, prompt_tiles - 1), 0)
    sample = lambda i: (jnp.maximum(i - lead - prompt_tiles, 0), 0)
    return stacked, prompt, sample


def _rms(x, g):
    return x * lax.rsqrt(jnp.mean(x * x, axis=-1, keepdims=True) + EPS) * g


def _silu(x):
    return x * jax.nn.sigmoid(x)


def _dot(a, b):
    return jnp.dot(a, b, preferred_element_type=F32)


def _dot_nt(a, b):
    return lax.dot_general(a, b, (((1,), (1,)), ((), ())), preferred_element_type=F32)


def _dot_tn(a, b):
    return lax.dot_general(a, b, (((0,), (0,)), ((), ())), preferred_element_type=F32)


def _pick(is_prompt, refs, index=None):
    vals = [r[...] if index is None else r[index] for r in refs]
    return vals[0] if len(vals) == 1 else jnp.where(is_prompt, vals[0], vals[1])


def _ffn_kernel(*refs, n_x, has_ple, has_final, n_o, prompt_tiles, stage_steps, gate_steps):
    it = iter(refs)
    x_refs = [next(it) for _ in range(n_x)]
    g_ref, win_ref, wout_ref = next(it), next(it), next(it)
    if has_ple:
        p_refs = [next(it), next(it)]
        gp_ref, wg_ref, wp_ref = next(it), next(it), next(it)
    if has_final:
        gf_ref = next(it)
    o_refs = [next(it) for _ in range(n_o)]
    win_s, wout_s, act_ref = next(it), next(it), next(it)
    if has_ple:
        wg_s, wp_s = next(it), next(it)
    i = pl.program_id(0)
    cols = win_s.shape[2]
    n_half = win_s.shape[0] // 2

    @pl.when(i < stage_steps)
    def _():
        blk = win_ref[0]
        win_s[2 * i] = blk[:, :cols].astype(BF16)
        win_s[2 * i + 1] = blk[:, cols:].astype(BF16)
        wout_s[pl.ds(pl.multiple_of(i * cols, cols), cols), :] = wout_ref[0].astype(BF16)

    if has_ple:
        rows = wg_ref.shape[1]

        @pl.when(i < gate_steps)
        def _():
            wg_s[pl.ds(pl.multiple_of(i * rows, rows), rows), :] = wg_ref[0].astype(BF16)

        @pl.when(i == 0)
        def _():
            wp_s[...] = wp_ref[0].astype(BF16)

    @pl.when(i >= stage_steps)
    def _():
        is_prompt = i - stage_steps < prompt_tiles
        x = _pick(is_prompt, x_refs)
        xn = _rms(x, g_ref[...]).astype(BF16)
        for j in range(n_half):
            a = _dot(xn, win_s[j])
            b = _dot(xn, win_s[n_half + j])
            act_ref[:, j * cols:(j + 1) * cols] = (_silu(a) * b).astype(BF16)
        x = x + 0.5 * _dot(act_ref[...], wout_s[...])
        if has_ple:
            gate = jax.nn.sigmoid(_dot(_rms(x, gp_ref[...]).astype(BF16), wg_s[...]))
            x = x + gate * _dot(_pick(is_prompt, p_refs, 0).astype(BF16), wp_s[...])
        if has_final:
            x = _rms(x, gf_ref[...])
        if n_o == 1:
            o_refs[0][...] = x
        else:
            @pl.when(is_prompt)
            def _():
                o_refs[0][...] = x

            @pl.when(jnp.logical_not(is_prompt))
            def _():
                o_refs[1][...] = x


def _ffn_call(xs, n_rows, gain, w_in, w_out, layer, ple=None, final_gain=None, split_out=False, name="ffn"):
    d = xs[0].shape[1]
    d_ff = w_out.shape[1]
    tm = ROW_TILE
    cols = V7X_MXU_COLS
    assert n_rows[0] % tm == 0 and n_rows[1] % tm == 0 and d_ff % cols == 0
    pt, st = n_rows[0] // tm, n_rows[1] // tm
    stage = d_ff // cols
    stacked, prompt, sample = _row_maps(pt, stage)
    pair = [prompt, sample]
    last = stage - 1
    args = [*xs, gain.reshape(1, d), w_in, w_out]
    specs = [pl.BlockSpec((tm, d), m) for m in ([stacked] if len(xs) == 1 else pair)]
    specs += [_resident((1, d)),
              pl.BlockSpec((1, d, 2 * cols), lambda i: (layer, 0, jnp.minimum(i, last))),
              pl.BlockSpec((1, cols, d), lambda i: (layer, jnp.minimum(i, last), 0))]
    scratch = [pltpu.VMEM((2 * stage, d, cols), BF16), pltpu.VMEM((d_ff, d), BF16), pltpu.VMEM((tm, d_ff), BF16)]
    gate_steps = 0
    if ple is not None:
        ps, gp, wg, wp = ple
        gate_rows = LANES
        gate_steps = d // gate_rows
        assert gate_steps <= stage
        args += [*ps, gp.reshape(1, d), wg, wp]
        specs += [pl.BlockSpec((1, tm, ps[0].shape[2]), lambda i, m=m: (layer, *m(i))) for m in pair]
        specs += [_resident((1, d)),
                  pl.BlockSpec((1, gate_rows, d), lambda i: (layer, jnp.minimum(i, gate_steps - 1), 0)),
                  _resident_layer(wp.shape, layer)]
        scratch += [pltpu.VMEM((d, d), BF16), pltpu.VMEM(wp.shape[1:], BF16)]
    if final_gain is not None:
        args.append(final_gain.reshape(1, d))
        specs.append(_resident((1, d)))
    if split_out:
        out_shape = [jax.ShapeDtypeStruct((r, d), F32) for r in n_rows]
        out_specs = [pl.BlockSpec((tm, d), m) for m in pair]
    else:
        out_shape = jax.ShapeDtypeStruct((sum(n_rows), d), F32)
        out_specs = pl.BlockSpec((tm, d), stacked)
    return pl.pallas_call(
        functools.partial(_ffn_kernel, n_x=len(xs), has_ple=ple is not None, has_final=final_gain is not None,
                          n_o=2 if split_out else 1, prompt_tiles=pt, stage_steps=stage, gate_steps=gate_steps),
        out_shape=out_shape,
        grid=(stage + pt + st,),
        in_specs=specs,
        out_specs=out_specs,
        scratch_shapes=scratch,
        compiler_params=_params(1, V7X_VMEM_BUDGET_BYTES),
        name=name,
    )(*args)


def _inproj_kernel(*refs, n_out, chunk_cols):
    x_ref, g_ref = refs[0], refs[1]
    w_refs = refs[2:2 + n_out]
    o_refs = refs[2 + n_out:2 + 2 * n_out]
    xn = _rms(x_ref[...], g_ref[...]).astype(BF16)
    for w_ref, o_ref, cw in zip(w_refs, o_refs, chunk_cols):
        for c in range(w_ref.shape[1] // cw):
            o_ref[:, c * cw:(c + 1) * cw] = _dot(xn, w_ref[:, c * cw:(c + 1) * cw]).astype(o_ref.dtype)


def _inproj_call(x, gain, weights, out_dtypes, chunk_cols, name):
    n, d = x.shape
    tm = ROW_TILE
    row = lambda i: (i, 0)
    specs = [pl.BlockSpec((tm, d), row), _resident((1, d))] + [_resident(w.shape) for w in weights]
    out_shape = [jax.ShapeDtypeStruct((n, w.shape[1]), dt) for w, dt in zip(weights, out_dtypes)]
    out_specs = [pl.BlockSpec((tm, w.shape[1]), row) for w in weights]
    vmem = (sum(2 * w.size for w in weights) + 4 * tm * d * 4
            + sum(2 * tm * w.shape[1] * jnp.dtype(dt).itemsize for w, dt in zip(weights, out_dtypes))
            + 2 * tm * max(chunk_cols) * 4 + (4 << 20))
    return pl.pallas_call(
        functools.partial(_inproj_kernel, n_out=len(weights), chunk_cols=tuple(chunk_cols)),
        out_shape=out_shape,
        grid=(n // tm,),
        in_specs=specs,
        out_specs=out_specs,
        compiler_params=_params(1, vmem),
        name=name,
    )(x, gain.reshape(1, d), *weights)


def _inproj_staged_kernel(x_ref, g_ref, w_ref, o_ref, w_s, *, stage_steps):
    i = pl.program_id(0)
    cols = w_s.shape[2]

    @pl.when(i < stage_steps)
    def _():
        w_s[i] = w_ref[0].astype(BF16)

    @pl.when(i >= stage_steps)
    def _():
        xn = _rms(x_ref[...], g_ref[...]).astype(BF16)
        for c in range(stage_steps):
            o_ref[:, c * cols:(c + 1) * cols] = _dot(xn, w_s[c]).astype(o_ref.dtype)


def _inproj_staged_call(x, gain, w, layer, cols, name):
    n, d = x.shape
    n_out = w.shape[2]
    tm = ROW_TILE
    assert n_out % cols == 0
    stage = n_out // cols
    row = lambda i: (jnp.maximum(i - stage, 0), 0)
    vmem = 2 * d * n_out + 2 * d * cols * 4 + 4 * tm * d * 4 + 2 * tm * n_out * 2 + 2 * tm * cols * 4 + (4 << 20)
    return pl.pallas_call(
        functools.partial(_inproj_staged_kernel, stage_steps=stage),
        out_shape=jax.ShapeDtypeStruct((n, n_out), BF16),
        grid=(stage + n // tm,),
        in_specs=[pl.BlockSpec((tm, d), row), _resident((1, d)),
                  pl.BlockSpec((1, d, cols), lambda i: (layer, 0, jnp.minimum(i, stage - 1)))],
        out_specs=pl.BlockSpec((tm, n_out), row),
        scratch_shapes=[pltpu.VMEM((stage, d, cols), BF16)],
        compiler_params=_params(1, vmem),
        name=name,
    )(x, gain.reshape(1, d), w)


def _outproj_kernel(*refs, n_parts, prompt_tiles):
    x_ref = refs[0]
    part_refs = refs[1:1 + 2 * n_parts]
    w_ref, o_ref, w_s = refs[1 + 2 * n_parts:]
    i = pl.program_id(0)

    @pl.when(i == 0)
    def _():
        w_s[...] = w_ref[0].astype(BF16)

    is_prompt = i < prompt_tiles
    acc = x_ref[...]
    k0 = 0
    for j in range(n_parts):
        part = _pick(is_prompt, part_refs[2 * j:2 * j + 2])
        acc = acc + _dot(part, w_s[k0:k0 + part.shape[1], :])
        k0 += part.shape[1]
    o_ref[...] = acc


def _outproj_call(x, n_rows, parts, w, layer, name):
    n, d = x.shape
    tm = ROW_TILE
    pt = n_rows[0] // tm
    stacked, prompt, sample = _row_maps(pt)
    specs = [pl.BlockSpec((tm, d), stacked)]
    args = [x]
    for pp, sp in parts:
        args += [pp, sp]
        specs += [pl.BlockSpec((tm, pp.shape[1]), prompt), pl.BlockSpec((tm, sp.shape[1]), sample)]
    args.append(w)
    specs.append(_resident_layer(w.shape, layer))
    kdim = w.shape[1]
    vmem = 6 * kdim * d + 6 * tm * d * 4 + 8 * tm * kdim * 2 + (4 << 20)
    return pl.pallas_call(
        functools.partial(_outproj_kernel, n_parts=len(parts), prompt_tiles=pt),
        out_shape=jax.ShapeDtypeStruct((n, d), F32),
        grid=(n // tm,),
        in_specs=specs,
        out_specs=pl.BlockSpec((tm, d), stacked),
        scratch_shapes=[pltpu.VMEM((kdim, d), BF16)],
        compiler_params=_params(1, vmem),
        name=name,
    )(*args)


GLA_MATMUL_LEVELS = (1, 2)


def _gla_constants(n_seq, ls, width):
    c = n_seq * ls
    nlev = int(round(math.log2(ls)))
    assert 2 ** nlev == ls and nlev > max(GLA_MATMUL_LEVELS)
    t = np.arange(c)[:, None]
    i = np.arange(c)[None, :]
    w_rows = [((t // ls) == (i // ls)) & (i <= t)]
    masks, uppers = [], []
    for l in range(nlev):
        m = 2 ** l
        upper = ((t // m) % 2) == 1
        r = (t // (2 * m)) * (2 * m) + m - 1
        if l in GLA_MATMUL_LEVELS:
            w_rows.append(np.where(upper, (i > r) & (i <= t), (i > t) & (i <= r)))
        masks.append(((t // (2 * m)) == (i // (2 * m))) & upper & (((i // m) % 2) == 0))
        uppers.append(np.broadcast_to(upper, (c, width)))
    w = jnp.asarray(np.concatenate(w_rows, 0).astype(np.float32), BF16)
    return w, jnp.asarray(np.stack(masks).astype(np.float32)), jnp.asarray(np.stack(uppers).astype(np.float32))


def _gla_block(q, k, v, g, st_list, w_ref, m_ref, u_ref, ls):
    c, width = q.shape
    nh = width // LANES
    n_seq = c // ls
    nlev = m_ref.shape[0]
    g_hi = g.astype(BF16)
    g_lo = (g - g_hi.astype(F32)).astype(BF16)
    ex2 = _dot(w_ref[...], jnp.concatenate([g_hi, g_lo], axis=1))
    ex = ex2[:, :width] + ex2[:, width:]
    b = ex[:c]

    def row(r, n):
        return jnp.broadcast_to(b[r:r + 1, :], (n, width))

    b_end = jnp.concatenate([row((s + 1) * ls - 1, ls) for s in range(n_seq)], axis=0) if n_seq > 1 else row(c - 1, c)
    e_cum = jnp.exp2(b)
    qe = q * e_cum
    kr = k * jnp.exp2(b_end - b)

    att = [None] * nh
    for l in range(nlev):
        m = 2 ** l
        if l == 0:
            x = jnp.where(u_ref[l] > 0.5, q * jnp.exp2(g), k)
        elif l in GLA_MATMUL_LEVELS:
            j = 1 + GLA_MATMUL_LEVELS.index(l)
            x = jnp.where(u_ref[l] > 0.5, q, k) * jnp.exp2(ex[j * c:(j + 1) * c])
        else:
            pieces = []
            for p in range(c // (2 * m)):
                lo, mid, hi = p * 2 * m, p * 2 * m + m, (p + 1) * 2 * m
                pivot = row(mid - 1, m)
                pieces.append(k[lo:mid] * jnp.exp2(pivot - b[lo:mid]))
                pieces.append(q[mid:hi] * jnp.exp2(b[mid:hi] - pivot))
            x = jnp.concatenate(pieces, axis=0)
        x = x.astype(BF16)
        mask = m_ref[l] > 0.5
        for h in range(nh):
            xh = x[:, h * LANES:(h + 1) * LANES]
            p = _dot_nt(xh, xh)
            att[h] = jnp.where(mask, p, 0.0 if att[h] is None else att[h])

    qk = q * k
    outs, st_new = [], [[None] * nh for _ in range(n_seq)]
    for h in range(nh):
        sl = slice(h * LANES, (h + 1) * LANES)
        vh = v[:, sl]
        o = _dot(att[h].astype(BF16), vh.astype(BF16)) + jnp.sum(qk[:, sl], axis=1, keepdims=True) * vh
        parts = []
        for s in range(n_seq):
            rows = slice(s * ls, (s + 1) * ls)
            st = st_list[s][h]
            parts.append(o[rows] + _dot_nt(qe[rows, sl].astype(BF16), st.astype(BF16)))
            e_tot = e_cum[(s + 1) * ls - 1:(s + 1) * ls, sl]
            st_new[s][h] = e_tot * st + _dot_tn(vh[rows].astype(BF16), kr[rows, sl].astype(BF16))
        outs.append(parts[0] if n_seq == 1 else jnp.concatenate(parts, axis=0))
    return outs, st_new


def _hgrn_inputs(refs, rows, q_scale):
    del q_scale
    qa_ref, ia_ref, ga_ref, fa_ref, lg_ref = refs
    lg = lg_ref[...]
    ex = jnp.exp(lg - jnp.max(lg, axis=0, keepdims=True))
    lb = ex[0:1] / jnp.sum(ex, axis=0, keepdims=True)
    sig = jax.nn.sigmoid(fa_ref[rows, :])
    q = _silu(qa_ref[rows, :].astype(F32))
    k = (1.0 - lb) * (1.0 - sig)
    g = jnp.log(lb + (1.0 - lb) * sig) * LOG2_E
    return q, k, ia_ref[rows, :].astype(F32), g, ga_ref[rows, :].astype(F32)


def _gla_inputs(refs, rows, q_scale):
    qb_ref, kb_ref, vb_ref, gb_ref, lr_ref, wgk_ref, bgk_ref = refs
    z = _dot(lr_ref[rows, :].astype(BF16), wgk_ref[...]) + bgk_ref[...]
    g = (jnp.minimum(z, 0.0) - jnp.log1p(jnp.exp(-jnp.abs(z)))) * (LOG2_E / GLA_GATE_NORM)
    q = qb_ref[rows, :].astype(F32) * q_scale
    return q, kb_ref[rows, :].astype(F32), vb_ref[rows, :].astype(F32), g, gb_ref[rows, :].astype(F32)


def _head_out(outs, gate, gn):
    ys = [_rms(o, gn) * _silu(gate[:, h * LANES:(h + 1) * LANES]) for h, o in enumerate(outs)]
    return jnp.concatenate(ys, axis=1).astype(BF16)


def _ab_prompt_kernel(*refs, kind, n_in, chunk, q_scale):
    in_refs = refs[:n_in]
    gn_ref, w_ref, m_ref, u_ref, o_ref, s_ref, st_ref = refs[n_in:]
    t = pl.program_id(1)
    nh = st_ref.shape[0]

    @pl.when(t == 0)
    def _():
        st_ref[...] = jnp.zeros_like(st_ref)

    load = _hgrn_inputs if kind == "hgrn" else _gla_inputs
    st = [st_ref[h] for h in range(nh)]
    for c in range(o_ref.shape[0] // chunk):
        rows = slice(c * chunk, (c + 1) * chunk)
        q, k, v, g, gate = load(in_refs, rows, q_scale)
        outs, (st,) = _gla_block(q, k, v, g, [st], w_ref, m_ref, u_ref, chunk)
        o_ref[rows, :] = _head_out(outs, gate, gn_ref[...])
    for h in range(nh):
        st_ref[h] = st[h]

    @pl.when(t == pl.num_programs(1) - 1)
    def _():
        for h in range(nh):
            s_ref[0, h] = st_ref[h].T[:s_ref.shape[2], :]


def _ab_sample_kernel(*refs, kind, n_in, ls, q_scale):
    in_refs = refs[:n_in]
    gn_ref, w_ref, m_ref, u_ref, s0_ref, o_ref, s_ref = refs[n_in:]
    load = _hgrn_inputs if kind == "hgrn" else _gla_inputs
    n_seq, nh, dk = s0_ref.shape[:3]
    q, k, v, g, gate = load(in_refs, slice(None), q_scale)
    st0 = []
    for s in range(n_seq):
        st0.append([])
        for h in range(nh):
            s0 = s0_ref[s, h]
            if dk < LANES:
                s0 = jnp.concatenate([s0, jnp.zeros((LANES - dk, s0.shape[1]), F32)], axis=0)
            st0[s].append(s0.T)
    outs, st = _gla_block(q, k, v, g, st0, w_ref, m_ref, u_ref, ls)
    o_ref[...] = _head_out(outs, gate, gn_ref[...])
    for s in range(n_seq):
        for h in range(nh):
            s_ref[s, h] = st[s][h].T[:dk, :]


def _ab_specs(kind, p16, p32, extra, tok_block, tok_index, nh):
    width = nh * LANES

    def col(group, w=width):
        return pl.BlockSpec((tok_block, w), lambda *ids: (tok_index(*ids), group))

    if kind == "hgrn":
        (logits,) = extra
        args = [p16, p16, p16, p32, logits]
        specs = [col(0), col(1), col(2), col(0), _resident(logits.shape)]
    else:
        w_gk, b_gk = extra
        args = [p16, p16, p16, p16, p32, w_gk, b_gk]
        specs = [col(3), col(4), col(5), col(6), col(H_A, LANES), _resident(w_gk.shape),
                 _resident(b_gk.shape)]
    return args, specs


def _ab_prompt_call(kind, p16, p32, extra, gn, batch, seq, dk, name):
    nh = H_A if kind == "hgrn" else H_B
    tt = GLA_STEP_TOKENS
    nt = seq // tt
    consts = _gla_constants(1, GLA_CHUNK, nh * LANES)
    args, specs = _ab_specs(kind, p16, p32, extra, tt, lambda b, t: b * nt + t, nh)
    n_in = len(args)
    args += [gn.reshape(1, LANES), *consts]
    specs += [_resident((1, LANES))] + [_resident(c.shape) for c in consts]
    o, s = pl.pallas_call(
        functools.partial(_ab_prompt_kernel, kind=kind, n_in=n_in, chunk=GLA_CHUNK, q_scale=dk ** -0.5),
        out_shape=[jax.ShapeDtypeStruct((batch * seq, nh * LANES), BF16),
                   jax.ShapeDtypeStruct((batch, nh, dk, LANES), F32)],
        grid=(batch, nt),
        in_specs=specs,
        out_specs=[pl.BlockSpec((tt, nh * LANES), lambda b, t: (b * nt + t, 0)),
                   pl.BlockSpec((1, nh, dk, LANES), lambda b, t: (b, 0, 0, 0))],
        scratch_shapes=[pltpu.VMEM((nh, LANES, LANES), F32)],
        compiler_params=_params(2, 32 << 20),
        name=name,
    )(*args)
    return o, s


def _ab_sample_call(kind, p16, p32, extra, gn, s0, row0, seq, name):
    batch, nh, dk, _ = s0.shape
    nb = SAMPLE_SEQS
    rows = nb * seq
    blk0 = row0 // rows
    consts = _gla_constants(nb, seq, nh * LANES)
    args, specs = _ab_specs(kind, p16, p32, extra, rows, lambda i: blk0 + i, nh)
    n_in = len(args)
    args += [gn.reshape(1, LANES), *consts, s0]
    state_spec = pl.BlockSpec((nb, nh, dk, LANES), lambda i: (i, 0, 0, 0))
    specs += [_resident((1, LANES))] + [_resident(c.shape) for c in consts] + [state_spec]
    o, s = pl.pallas_call(
        functools.partial(_ab_sample_kernel, kind=kind, n_in=n_in, ls=seq, q_scale=dk ** -0.5),
        out_shape=[jax.ShapeDtypeStruct((batch * seq, nh * LANES), BF16),
                   jax.ShapeDtypeStruct((batch, nh, dk, LANES), F32)],
        grid=(batch // nb,),
        in_specs=specs,
        out_specs=[pl.BlockSpec((rows, nh * LANES), lambda i: (i, 0)), state_spec],
        compiler_params=_params(1, 32 << 20),
        name=name,
    )(*args)
    return o, s


def _ret_constants(n_seq, ls):
    c = n_seq * ls
    log_gamma = np.log1p(-np.exp2(-5.0 - np.arange(H_C, dtype=np.float64)))[:, None, None]
    t = np.arange(c)[:, None]
    s = np.arange(c)[None, :]
    causal = ((t // ls) == (s // ls)) & (s <= t)
    dmask = np.where(causal, np.exp(log_gamma * np.where(causal, t - s, 0)), 0.0)
    pos = (np.arange(c) % ls)[None, :, None]
    scales = np.stack([np.exp(log_gamma * (pos + 1)), np.exp(log_gamma * (ls - 1 - pos))], axis=1)
    scales = np.broadcast_to(scales, (H_C, 2, c, LANES))
    return jnp.asarray(dmask, F32), jnp.asarray(scales, F32)


def _rope_tables(positions, half):
    inv = ROPE_THETA ** (-jnp.arange(half, dtype=F32) / half)
    ang = positions.astype(F32)[:, None] * inv[None, :]
    return jnp.cos(ang), jnp.sin(ang)


def _rope(x, cos, sin):
    half = x.shape[1] // 2
    x1, x2 = x[:, :half], x[:, half:]
    return jnp.concatenate([x1 * cos - x2 * sin, x2 * cos + x1 * sin], axis=1)


def _ret_block(q16, k16, v, cos, sin, dmask, q_scale, k_scale, s_list, ls):
    dk = q16.shape[1]
    q = _rope(q16.astype(F32), cos, sin)
    k = _rope(k16.astype(F32), cos, sin) * (dk ** -0.5)
    reps = dk // LANES
    q_sc = jnp.concatenate([q_scale] * reps, axis=1)
    k_sc = jnp.concatenate([k_scale] * reps, axis=1)
    att = _dot_nt(q.astype(BF16), k.astype(BF16)) * dmask
    o = _dot(att.astype(BF16), v)
    qs = q * q_sc
    ks = k * k_sc
    decay = q_scale[ls - 1:ls, 0:1]
    n_seq = q.shape[0] // ls
    v32 = v if n_seq == 1 else v.astype(F32)
    o_parts, s_new = [], []
    for b in range(n_seq):
        rows = slice(b * ls, (b + 1) * ls)
        s = s_list[b]
        o_parts.append(o[rows] + _dot(qs[rows].astype(BF16), s.astype(BF16)))
        s_new.append(decay * s + _dot_tn(ks[rows].astype(BF16), v32[rows].astype(BF16)))
    o = o_parts[0] if n_seq == 1 else jnp.concatenate(o_parts, axis=0)
    return o, s_new


def _ret_head_out(o, gate, gn):
    oc = o - jnp.mean(o, axis=-1, keepdims=True)
    y = oc * lax.rsqrt(jnp.mean(oc * oc, axis=-1, keepdims=True) + EPS) * gn
    return (y * _silu(gate)).astype(BF16)


def _ret_prompt_kernel(q_ref, k_ref, v_ref, g_ref, cos_ref, sin_ref, dm_ref, sc_ref, gn_ref,
                       o_ref, s_ref, st_ref):
    t = pl.program_id(1)
    nh, dk, dv = st_ref.shape
    c = q_ref.shape[0]

    @pl.when(t == 0)
    def _():
        st_ref[...] = jnp.zeros_like(st_ref)

    cos, sin = cos_ref[...], sin_ref[...]
    for h in range(nh):
        o, (s,) = _ret_block(q_ref[:, h * dk:(h + 1) * dk], k_ref[:, h * dk:(h + 1) * dk],
                             v_ref[:, h * dv:(h + 1) * dv], cos, sin, dm_ref[h], sc_ref[h, 0], sc_ref[h, 1],
                             [st_ref[h]], c)
        st_ref[h] = s
        o_ref[:, h * dv:(h + 1) * dv] = _ret_head_out(o, g_ref[:, h * dv:(h + 1) * dv].astype(F32), gn_ref[...])

    @pl.when(t == pl.num_programs(1) - 1)
    def _():
        s_ref[0] = st_ref[...]


def _ret_sample_kernel(q_ref, k_ref, v_ref, g_ref, cos_ref, sin_ref, dm_ref, sc_ref, gn_ref, s0_ref,
                       o_ref, s_ref, *, ls):
    n_seq = q_ref.shape[0] // ls
    o, s = _ret_block(q_ref[...], k_ref[...], v_ref[...], cos_ref[...], sin_ref[...], dm_ref[0],
                      sc_ref[0, 0], sc_ref[0, 1], [s0_ref[b, 0] for b in range(n_seq)], ls)
    o_ref[...] = _ret_head_out(o, g_ref[...].astype(F32), gn_ref[...])
    for b in range(n_seq):
        s_ref[b, 0] = s[b]


def _ret_prompt_call(r16, gn, batch, seq, dk, dv, name):
    c = RET_CHUNK
    nt = seq // c
    dmask, scales = _ret_constants(1, c)
    cos, sin = _rope_tables(jnp.arange(seq, dtype=jnp.int32), dk // 2)
    tok = lambda b, t: b * nt + t
    nq, nv = H_C * dk, H_C * dv
    args = [r16, r16, r16, r16, cos, sin, dmask, scales, gn.reshape(1, dv)]
    specs = [pl.BlockSpec((c, nq), lambda b, t: (tok(b, t), 0)),
             pl.BlockSpec((c, nq), lambda b, t: (tok(b, t), 1)),
             pl.BlockSpec((c, nv), lambda b, t: (tok(b, t), 2 * nq // nv)),
             pl.BlockSpec((c, nv), lambda b, t: (tok(b, t), 2 * nq // nv + 1)),
             pl.BlockSpec((c, dk // 2), lambda b, t: (t, 0)), pl.BlockSpec((c, dk // 2), lambda b, t: (t, 0)),
             _resident(dmask.shape), _resident(scales.shape), _resident((1, dv))]
    return pl.pallas_call(
        _ret_prompt_kernel,
        out_shape=[jax.ShapeDtypeStruct((batch * seq, nv), BF16),
                   jax.ShapeDtypeStruct((batch, H_C, dk, dv), F32)],
        grid=(batch, nt),
        in_specs=specs,
        out_specs=[pl.BlockSpec((c, nv), lambda b, t: (tok(b, t), 0)),
                   pl.BlockSpec((1, H_C, dk, dv), lambda b, t: (b, 0, 0, 0))],
        scratch_shapes=[pltpu.VMEM((H_C, dk, dv), F32)],
        compiler_params=_params(2, 40 << 20),
        name=name,
    )(*args)


def _ret_sample_call(r16, gn, s0, row0, batch, seq, dk, dv, name):
    nb = SAMPLE_SEQS
    rows = nb * seq
    blk0 = row0 // rows
    dmask, scales = _ret_constants(nb, seq)
    pos = PAST_LEN + (jnp.arange(rows, dtype=jnp.int32) % seq)
    cos, sin = _rope_tables(pos, dk // 2)
    nq = H_C * dk

    def col(width, base):
        return pl.BlockSpec((rows, width), lambda i, h: (blk0 + i, base + h))

    state_spec = pl.BlockSpec((nb, 1, dk, dv), lambda i, h: (i, h, 0, 0))
    args = [r16, r16, r16, r16, cos, sin, dmask, scales, gn.reshape(1, dv), s0]
    specs = [col(dk, 0), col(dk, H_C), col(dv, 2 * nq // dv), col(dv, 2 * nq // dv + H_C),
             _resident(cos.shape), _resident(sin.shape),
             pl.BlockSpec((1, rows, rows), lambda i, h: (h, 0, 0)),
             pl.BlockSpec((1, 2, rows, LANES), lambda i, h: (h, 0, 0, 0)), _resident((1, dv)), state_spec]
    return pl.pallas_call(
        functools.partial(_ret_sample_kernel, ls=seq),
        out_shape=[jax.ShapeDtypeStruct((batch * seq, H_C * dv), BF16),
                   jax.ShapeDtypeStruct((batch, H_C, dk, dv), F32)],
        grid=(batch // nb, H_C),
        in_specs=specs,
        out_specs=[pl.BlockSpec((rows, dv), lambda i, h: (i, h)), state_spec],
        compiler_params=_params(2, 48 << 20),
        name=name,
    )(*args)


def _pad_heads(w, n_heads, width):
    d = w.shape[0]
    w = w.reshape(d, n_heads, -1)
    return jnp.pad(w, ((0, 0), (0, 0), (0, width - w.shape[2]))).reshape(d, n_heads * width)


def _prepare_ab_weights(w_in, w_gk2, b_gk, dk_a, dv_a, dk_b, dv_b):
    sizes = (H_A * dk_a, H_A * dk_a, H_A * dv_a, H_A * dv_a, H_B * dk_b, H_B * dk_b, H_B * dv_b,
             GLA_RANK, H_B * dv_b)
    qa, fa, ia, ga, qb, kb, vb, lrb, gb = jnp.split(w_in, [int(v) for v in np.cumsum(sizes)[:-1]], axis=1)
    w16 = jnp.concatenate([qa, ia, ga, _pad_heads(qb, H_B, LANES), _pad_heads(kb, H_B, LANES), vb, gb],
                          axis=1).astype(BF16)
    w32 = jnp.concatenate([fa, jnp.pad(lrb, ((0, 0), (0, LANES - GLA_RANK)))], axis=1).astype(BF16)
    w_gk = jnp.pad(_pad_heads(w_gk2, H_B, LANES), ((0, LANES - GLA_RANK), (0, 0))).astype(BF16)
    b_gk = _pad_heads(b_gk.reshape(1, -1), H_B, LANES)
    return w16, w32, w_gk, b_gk


def kernel(x_prompt, x_sample, state_hgrn, state_gla, state_ret, p_prompt, p_sample, norm_ffn1, ffn1_w_in, ffn1_w_out, norm_mix, ab_w_in, ab_w_gk2, ab_b_gk, hgrn_lb_logits, ab_gn_hgrn, ab_gn_gla, ab_w_out, ret_w_in, ret_gn, ret_w_out, norm_ffn2, ffn2_w_in, ffn2_w_out, norm_ple, ple_w_gate, ple_w_proj, norm_final):
    bp, lp, d = x_prompt.shape
    bs, ls, _ = x_sample.shape
    depth = norm_ffn1.shape[0]
    n_prompt = bp * lp
    dk_a, dv_a = state_hgrn.shape[-2:]
    dk_b, dv_b = state_gla.shape[-2:]
    dk_c, dv_c = state_ret.shape[-2:]
    assert dk_a == dv_a == dv_b == LANES and dk_b <= LANES and state_hgrn.shape[0] == 1

    n_rows = (n_prompt, bs * ls)
    xs = (x_prompt.reshape(n_prompt, d), x_sample.reshape(bs * ls, d))
    ps = (p_prompt.reshape(depth, n_prompt, -1), p_sample.reshape(depth, bs * ls, -1))

    new_hgrn_p, new_gla_p, new_ret_p, new_hgrn_s, new_gla_s, new_ret_s = [], [], [], [], [], []
    for i in range(depth):
        j = i // 2
        x = _ffn_call(xs if i == 0 else (x,), n_rows, norm_ffn1[i], ffn1_w_in, ffn1_w_out, i, name=f"ffn1_{i}")
        if i % 2 == 0:
            w16, w32, w_gk, b_gk = _prepare_ab_weights(ab_w_in[j], ab_w_gk2[j], ab_b_gk[j], dk_a, dv_a, dk_b, dv_b)
            p16, p32 = _inproj_call(x, norm_mix[i], [w16, w32], [BF16, F32], [w16.shape[1] // 2, w32.shape[1]],
                                    name=f"ab_in_{i}")
            hgrn = ("hgrn", p16, p32, (hgrn_lb_logits,), ab_gn_hgrn[j])
            gla = ("gla", p16, p32, (w_gk, b_gk), ab_gn_gla[j])
            oa_p, sa_p = _ab_prompt_call(*hgrn, bp, lp, dk_a, f"hgrn_p_{i}")
            oa_s, sa_s = _ab_sample_call(*hgrn, state_hgrn[j], n_prompt, ls, f"hgrn_s_{i}")
            ob_p, sb_p = _ab_prompt_call(*gla, bp, lp, dk_b, f"gla_p_{i}")
            ob_s, sb_s = _ab_sample_call(*gla, state_gla[j], n_prompt, ls, f"gla_s_{i}")
            new_hgrn_p.append(sa_p)
            new_hgrn_s.append(sa_s)
            new_gla_p.append(sb_p)
            new_gla_s.append(sb_s)
            x = _outproj_call(x, n_rows, [(oa_p, oa_s), (ob_p, ob_s)], ab_w_out, j, name=f"ab_out_{i}")
        else:
            r16 = _inproj_staged_call(x, norm_mix[i], ret_w_in, j, 2 * V7X_MXU_COLS, name=f"ret_in_{i}")
            oc_p, sc_p = _ret_prompt_call(r16, ret_gn[j], bp, lp, dk_c, dv_c, f"ret_p_{i}")
            oc_s, sc_s = _ret_sample_call(r16, ret_gn[j], state_ret[j], n_prompt, bs, ls, dk_c, dv_c, f"ret_s_{i}")
            new_ret_p.append(sc_p)
            new_ret_s.append(sc_s)
            x = _outproj_call(x, n_rows, [(oc_p, oc_s)], ret_w_out, j, name=f"ret_out_{i}")
        last = i == depth - 1
        x = _ffn_call((x,), n_rows, norm_ffn2[i], ffn2_w_in, ffn2_w_out, i,
                      ple=(ps, norm_ple[i], ple_w_gate, ple_w_proj),
                      final_gain=norm_final if last else None, split_out=last, name=f"ffn2_{i}")

    y_prompt = x[0].reshape(bp, lp, d)
    y_sample = x[1].reshape(bs, ls, d)
    return (y_prompt, y_sample, jnp.stack(new_hgrn_p), jnp.stack(new_gla_p), jnp.stack(new_ret_p),
            jnp.stack(new_hgrn_s), jnp.stack(new_gla_s), jnp.stack(new_ret_s))
```

```python
import functools
import math

import numpy as np
import jax
import jax.numpy as jnp
from jax import lax
from jax.experimental import pallas as pl
from jax.experimental.pallas import tpu as pltpu

F32 = jnp.float32
BF16 = jnp.bfloat16

H_A = 4
H_B = 4
H_C = 4
GLA_RANK = 16
GLA_GATE_NORM = 16.0
ROPE_THETA = 10000.0
PAST_LEN = 16384
EPS = 1e-6
LOG2_E = math.log2(math.e)

LANES = 128
V7X_MXU_COLS = 256
V7X_VMEM_BUDGET_BYTES = 56 * 1024 * 1024

ROW_TILE = 512
GLA_CHUNK = 64
GLA_STEP_TOKENS = 256
RET_CHUNK = 256
SAMPLE_SEQS = 8


def _params(n_axes, vmem_bytes):
    return pltpu.CompilerParams(dimension_semantics=("arbitrary",) * n_axes,
                                vmem_limit_bytes=min(int(vmem_bytes), V7X_VMEM_BUDGET_BYTES))


def _resident(shape):
    nd = len(shape)
    return pl.BlockSpec(shape, lambda *_: (0,) * nd, pipeline_mode=pl.Buffered(1))


def _resident_layer(shape, layer):
    nd = len(shape)
    return pl.BlockSpec((1,) + tuple(shape[1:]), lambda *_: (layer,) + (0,) * (nd - 1),
                        pipeline_mode=pl.Buffered(1))


def _row_maps(prompt_tiles, lead=0):
    stacked = lambda i: (jnp.maximum(i - lead, 0), 0)
    prompt = lambda i: (jnp.clip(i - lead, 0, prompt_tiles - 1), 0)
    sample = lambda i: (jnp.maximum(i - lead - prompt_tiles, 0), 0)
    return stacked, prompt, sample


def _rms(x, g):
    return x * lax.rsqrt(jnp.mean(x * x, axis=-1, keepdims=True) + EPS) * g


def _silu(x):
    return x * jax.nn.sigmoid(x)


def _dot(a, b):
    return jnp.dot(a, b, preferred_element_type=F32)


def _dot_nt(a, b):
    return lax.dot_general(a, b, (((1,), (1,)), ((), ())), preferred_element_type=F32)


def _dot_tn(a, b):
    return lax.dot_general(a, b, (((0,), (0,)), ((), ())), preferred_element_type=F32)


def _pick(is_prompt, refs, index=None):
    vals = [r[...] if index is None else r[index] for r in refs]
    return vals[0] if len(vals) == 1 else jnp.where(is_prompt, vals[0], vals[1])


def _ffn_kernel(*refs, n_x, has_ple, has_final, n_o, prompt_tiles, stage_steps, gate_steps):
    it = iter(refs)
    x_refs = [next(it) for _ in range(n_x)]
    g_ref, win_ref, wout_ref = next(it), next(it), next(it)
    if has_ple:
        p_refs = [next(it), next(it)]
        gp_ref, wg_ref, wp_ref = next(it), next(it), next(it)
    if has_final:
        gf_ref = next(it)
    o_refs = [next(it) for _ in range(n_o)]
    win_s, wout_s, act_ref = next(it), next(it), next(it)
    if has_ple:
        wg_s, wp_s = next(it), next(it)
    i = pl.program_id(0)
    cols = win_s.shape[2]
    n_half = win_s.shape[0] // 2

    @pl.when(i < stage_steps)
    def _():
        blk = win_ref[0]
        win_s[2 * i] = blk[:, :cols].astype(BF16)
        win_s[2 * i + 1] = blk[:, cols:].astype(BF16)
        wout_s[pl.ds(pl.multiple_of(i * cols, cols), cols), :] = wout_ref[0].astype(BF16)

    if has_ple:
        rows = wg_ref.shape[1]

        @pl.when(i < gate_steps)
        def _():
            wg_s[pl.ds(pl.multiple_of(i * rows, rows), rows), :] = wg_ref[0].astype(BF16)

        @pl.when(i == 0)
        def _():
            wp_s[...] = wp_ref[0].astype(BF16)

    @pl.when(i >= stage_steps)
    def _():
        is_prompt = i - stage_steps < prompt_tiles
        x = _pick(is_prompt, x_refs)
        xn = _rms(x, g_ref[...]).astype(BF16)
        for j in range(n_half):
            a = _dot(xn, win_s[j])
            b = _dot(xn, win_s[n_half + j])
            act_ref[:, j * cols:(j + 1) * cols] = (_silu(a) * b).astype(BF16)
        x = x + 0.5 * _dot(act_ref[...], wout_s[...])
        if has_ple:
            gate = jax.nn.sigmoid(_dot(_rms(x, gp_ref[...]).astype(BF16), wg_s[...]))
            x = x + gate * _dot(_pick(is_prompt, p_refs, 0).astype(BF16), wp_s[...])
        if has_final:
            x = _rms(x, gf_ref[...])
        if n_o == 1:
            o_refs[0][...] = x
        else:
            @pl.when(is_prompt)
            def _():
                o_refs[0][...] = x

            @pl.when(jnp.logical_not(is_prompt))
            def _():
                o_refs[1][...] = x


def _ffn_call(xs, n_rows, gain, w_in, w_out, layer, ple=None, final_gain=None, split_out=False, name="ffn"):
    d = xs[0].shape[1]
    d_ff = w_out.shape[1]
    tm = ROW_TILE
    cols = V7X_MXU_COLS
    assert n_rows[0] % tm == 0 and n_rows[1] % tm == 0 and d_ff % cols == 0
    pt, st = n_rows[0] // tm, n_rows[1] // tm
    stage = d_ff // cols
    stacked, prompt, sample = _row_maps(pt, stage)
    pair = [prompt, sample]
    last = stage - 1
    args = [*xs, gain.reshape(1, d), w_in, w_out]
    specs = [pl.BlockSpec((tm, d), m) for m in ([stacked] if len(xs) == 1 else pair)]
    specs += [_resident((1, d)),
              pl.BlockSpec((1, d, 2 * cols), lambda i: (layer, 0, jnp.minimum(i, last))),
              pl.BlockSpec((1, cols, d), lambda i: (layer, jnp.minimum(i, last), 0))]
    scratch = [pltpu.VMEM((2 * stage, d, cols), BF16), pltpu.VMEM((d_ff, d), BF16), pltpu.VMEM((tm, d_ff), BF16)]
    gate_steps = 0
    if ple is not None:
        ps, gp, wg, wp = ple
        gate_rows = LANES
        gate_steps = d // gate_rows
        assert gate_steps <= stage
        args += [*ps, gp.reshape(1, d), wg, wp]
        specs += [pl.BlockSpec((1, tm, ps[0].shape[2]), lambda i, m=m: (layer, *m(i))) for m in pair]
        specs += [_resident((1, d)),
                  pl.BlockSpec((1, gate_rows, d), lambda i: (layer, jnp.minimum(i, gate_steps - 1), 0)),
                  _resident_layer(wp.shape, layer)]
        scratch += [pltpu.VMEM((d, d), BF16), pltpu.VMEM(wp.shape[1:], BF16)]
    if final_gain is not None:
        args.append(final_gain.reshape(1, d))
        specs.append(_resident((1, d)))
    if split_out:
        out_shape = [jax.ShapeDtypeStruct((r, d), F32) for r in n_rows]
        out_specs = [pl.BlockSpec((tm, d), m) for m in pair]
    else:
        out_shape = jax.ShapeDtypeStruct((sum(n_rows), d), F32)
        out_specs = pl.BlockSpec((tm, d), stacked)
    return pl.pallas_call(
        functools.partial(_ffn_kernel, n_x=len(xs), has_ple=ple is not None, has_final=final_gain is not None,
                          n_o=2 if split_out else 1, prompt_tiles=pt, stage_steps=stage, gate_steps=gate_steps),
        out_shape=out_shape,
        grid=(stage + pt + st,),
        in_specs=specs,
        out_specs=out_specs,
        scratch_shapes=scratch,
        compiler_params=_params(1, V7X_VMEM_BUDGET_BYTES),
        name=name,
    )(*args)


def _inproj_kernel(*refs, n_out, chunk_cols):
    x_ref, g_ref = refs[0], refs[1]
    w_refs = refs[2:2 + n_out]
    o_refs = refs[2 + n_out:2 + 2 * n_out]
    xn = _rms(x_ref[...], g_ref[...]).astype(BF16)
    for w_ref, o_ref, cw in zip(w_refs, o_refs, chunk_cols):
        for c in range(w_ref.shape[1] // cw):
            o_ref[:, c * cw:(c + 1) * cw] = _dot(xn, w_ref[:, c * cw:(c + 1) * cw]).astype(o_ref.dtype)


def _inproj_call(x, gain, weights, out_dtypes, chunk_cols, name):
    n, d = x.shape
    tm = ROW_TILE
    row = lambda i: (i, 0)
    specs = [pl.BlockSpec((tm, d), row), _resident((1, d))] + [_resident(w.shape) for w in weights]
    out_shape = [jax.ShapeDtypeStruct((n, w.shape[1]), dt) for w, dt in zip(weights, out_dtypes)]
    out_specs = [pl.BlockSpec((tm, w.shape[1]), row) for w in weights]
    vmem = (sum(2 * w.size for w in weights) + 4 * tm * d * 4
            + sum(2 * tm * w.shape[1] * jnp.dtype(dt).itemsize for w, dt in zip(weights, out_dtypes))
            + 2 * tm * max(chunk_cols) * 4 + (4 << 20))
    return pl.pallas_call(
        functools.partial(_inproj_kernel, n_out=len(weights), chunk_cols=tuple(chunk_cols)),
        out_shape=out_shape,
        grid=(n // tm,),
        in_specs=specs,
        out_specs=out_specs,
        compiler_params=_params(1, vmem),
        name=name,
    )(x, gain.reshape(1, d), *weights)


def _inproj_staged_kernel(x_ref, g_ref, w_ref, o_ref, w_s, *, stage_steps):
    i = pl.program_id(0)
    cols = w_s.shape[2]

    @pl.when(i < stage_steps)
    def _():
        w_s[i] = w_ref[0].astype(BF16)

    @pl.when(i >= stage_steps)
    def _():
        xn = _rms(x_ref[...], g_ref[...]).astype(BF16)
        for c in range(stage_steps):
            o_ref[:, c * cols:(c + 1) * cols] = _dot(xn, w_s[c]).astype(o_ref.dtype)


def _inproj_staged_call(x, gain, w, layer, cols, name):
    n, d = x.shape
    n_out = w.shape[2]
    tm = ROW_TILE
    assert n_out % cols == 0
    stage = n_out // cols
    row = lambda i: (jnp.maximum(i - stage, 0), 0)
    vmem = 2 * d * n_out + 2 * d * cols * 4 + 4 * tm * d * 4 + 2 * tm * n_out * 2 + 2 * tm * cols * 4 + (4 << 20)
    return pl.pallas_call(
        functools.partial(_inproj_staged_kernel, stage_steps=stage),
        out_shape=jax.ShapeDtypeStruct((n, n_out), BF16),
        grid=(stage + n // tm,),
        in_specs=[pl.BlockSpec((tm, d), row), _resident((1, d)),
                  pl.BlockSpec((1, d, cols), lambda i: (layer, 0, jnp.minimum(i, stage - 1)))],
        out_specs=pl.BlockSpec((tm, n_out), row),
        scratch_shapes=[pltpu.VMEM((stage, d, cols), BF16)],
        compiler_params=_params(1, vmem),
        name=name,
    )(x, gain.reshape(1, d), w)


GLA_MATMUL_LEVELS = (1, 2)


def _gla_constants(n_seq, ls, width):
    c = n_seq * ls
    nlev = int(round(math.log2(ls)))
    assert 2 ** nlev == ls and nlev > max(GLA_MATMUL_LEVELS)
    t = np.arange(c)[:, None]
    i = np.arange(c)[None, :]
    w_rows = [((t // ls) == (i // ls)) & (i <= t)]
    masks, uppers = [], []
    for l in range(nlev):
        m = 2 ** l
        upper = ((t // m) % 2) == 1
        r = (t // (2 * m)) * (2 * m) + m - 1
        if l in GLA_MATMUL_LEVELS:
            w_rows.append(np.where(upper, (i > r) & (i <= t), (i > t) & (i <= r)))
        masks.append(((t // (2 * m)) == (i // (2 * m))) & upper & (((i // m) % 2) == 0))
        uppers.append(np.broadcast_to(upper, (c, width)))
    w = jnp.asarray(np.concatenate(w_rows, 0).astype(np.float32), BF16)
    return w, jnp.asarray(np.stack(masks).astype(np.float32)), jnp.asarray(np.stack(uppers).astype(np.float32))


def _gla_block(q, k, v, g, st_list, w_ref, m_ref, u_ref, ls):
    c, width = q.shape
    nh = width // LANES
    n_seq = c // ls
    nlev = m_ref.shape[0]
    g_hi = g.astype(BF16)
    g_lo = (g - g_hi.astype(F32)).astype(BF16)
    ex2 = _dot(w_ref[...], jnp.concatenate([g_hi, g_lo], axis=1))
    ex = ex2[:, :width] + ex2[:, width:]
    b = ex[:c]

    def row(r, n):
        return jnp.broadcast_to(b[r:r + 1, :], (n, width))

    b_end = jnp.concatenate([row((s + 1) * ls - 1, ls) for s in range(n_seq)], axis=0) if n_seq > 1 else row(c - 1, c)
    e_cum = jnp.exp2(b)
    qe = q * e_cum
    kr = k * jnp.exp2(b_end - b)

    att = [None] * nh
    for l in range(nlev):
        m = 2 ** l
        if l == 0:
            x = jnp.where(u_ref[l] > 0.5, q * jnp.exp2(g), k)
        elif l in GLA_MATMUL_LEVELS:
            j = 1 + GLA_MATMUL_LEVELS.index(l)
            x = jnp.where(u_ref[l] > 0.5, q, k) * jnp.exp2(ex[j * c:(j + 1) * c])
        else:
            pieces = []
            for p in range(c // (2 * m)):
                lo, mid, hi = p * 2 * m, p * 2 * m + m, (p + 1) * 2 * m
                pivot = row(mid - 1, m)
                pieces.append(k[lo:mid] * jnp.exp2(pivot - b[lo:mid]))
                pieces.append(q[mid:hi] * jnp.exp2(b[mid:hi] - pivot))
            x = jnp.concatenate(pieces, axis=0)
        x = x.astype(BF16)
        mask = m_ref[l] > 0.5
        for h in range(nh):
            xh = x[:, h * LANES:(h + 1) * LANES]
            p = _dot_nt(xh, xh)
            att[h] = jnp.where(mask, p, 0.0 if att[h] is None else att[h])

    qk = q * k
    outs, st_new = [], [[None] * nh for _ in range(n_seq)]
    for h in range(nh):
        sl = slice(h * LANES, (h + 1) * LANES)
        vh = v[:, sl]
        o = _dot(att[h].astype(BF16), vh.astype(BF16)) + jnp.sum(qk[:, sl], axis=1, keepdims=True) * vh
        parts = []
        for s in range(n_seq):
            rows = slice(s * ls, (s + 1) * ls)
            st = st_list[s][h]
            parts.append(o[rows] + _dot_nt(qe[rows, sl].astype(BF16), st.astype(BF16)))
            e_tot = e_cum[(s + 1) * ls - 1:(s + 1) * ls, sl]
            st_new[s][h] = e_tot * st + _dot_tn(vh[rows].astype(BF16), kr[rows, sl].astype(BF16))
        outs.append(parts[0] if n_seq == 1 else jnp.concatenate(parts, axis=0))
    return outs, st_new


def _hgrn_inputs(refs, rows, q_scale):
    del q_scale
    qa_ref, ia_ref, ga_ref, fa_ref, lg_ref = refs
    lg = lg_ref[...]
    ex = jnp.exp(lg - jnp.max(lg, axis=0, keepdims=True))
    lb = ex[0:1] / jnp.sum(ex, axis=0, keepdims=True)
    sig = jax.nn.sigmoid(fa_ref[rows, :])
    q = _silu(qa_ref[rows, :].astype(F32))
    k = (1.0 - lb) * (1.0 - sig)
    g = jnp.log(lb + (1.0 - lb) * sig) * LOG2_E
    return q, k, ia_ref[rows, :].astype(F32), g, ga_ref[rows, :].astype(F32)


def _gla_inputs(refs, rows, q_scale):
    qb_ref, kb_ref, vb_ref, gb_ref, lr_ref, wgk_ref, bgk_ref = refs
    z = _dot(lr_ref[rows, :].astype(BF16), wgk_ref[...]) + bgk_ref[...]
    g = (jnp.minimum(z, 0.0) - jnp.log1p(jnp.exp(-jnp.abs(z)))) * (LOG2_E / GLA_GATE_NORM)
    q = qb_ref[rows, :].astype(F32) * q_scale
    return q, kb_ref[rows, :].astype(F32), vb_ref[rows, :].astype(F32), g, gb_ref[rows, :].astype(F32)


def _head_out(outs, gate, gn):
    ys = [_rms(o, gn) * _silu(gate[:, h * LANES:(h + 1) * LANES]) for h, o in enumerate(outs)]
    return jnp.concatenate(ys, axis=1).astype(BF16)


def _stage_out_weight(wo_ref, wo_s, first):
    @pl.when(first)
    def _():
        wo_s[...] = wo_ref[0].astype(BF16)


def _ab_prompt_kernel(*refs, kind, n_in, chunk, q_scale):
    in_refs = refs[:n_in]
    gn_ref, w_ref, m_ref, u_ref, x_ref, wo_ref, xo_ref, s_ref, st_ref, y_s, wo_s = refs[n_in:]
    t = pl.program_id(1)
    nh = st_ref.shape[0]
    _stage_out_weight(wo_ref, wo_s, jnp.logical_and(pl.program_id(0) == 0, t == 0))

    @pl.when(t == 0)
    def _():
        st_ref[...] = jnp.zeros_like(st_ref)

    load = _hgrn_inputs if kind == "hgrn" else _gla_inputs
    st = [st_ref[h] for h in range(nh)]
    for c in range(y_s.shape[0] // chunk):
        rows = slice(c * chunk, (c + 1) * chunk)
        q, k, v, g, gate = load(in_refs, rows, q_scale)
        outs, (st,) = _gla_block(q, k, v, g, [st], w_ref, m_ref, u_ref, chunk)
        y_s[rows, :] = _head_out(outs, gate, gn_ref[...])
    for h in range(nh):
        st_ref[h] = st[h]
    xo_ref[...] = x_ref[...] + _dot(y_s[...], wo_s[...])

    @pl.when(t == pl.num_programs(1) - 1)
    def _():
        for h in range(nh):
            s_ref[0, h] = st_ref[h].T[:s_ref.shape[2], :]


def _ab_sample_kernel(*refs, kind, n_in, ls, q_scale):
    in_refs = refs[:n_in]
    gn_ref, w_ref, m_ref, u_ref, s0_ref, x_ref, wo_ref, xo_ref, s_ref, wo_s = refs[n_in:]
    _stage_out_weight(wo_ref, wo_s, pl.program_id(0) == 0)
    load = _hgrn_inputs if kind == "hgrn" else _gla_inputs
    n_seq, nh, dk = s0_ref.shape[:3]
    q, k, v, g, gate = load(in_refs, slice(None), q_scale)
    st0 = []
    for s in range(n_seq):
        st0.append([])
        for h in range(nh):
            s0 = s0_ref[s, h]
            if dk < LANES:
                s0 = jnp.concatenate([s0, jnp.zeros((LANES - dk, s0.shape[1]), F32)], axis=0)
            st0[s].append(s0.T)
    outs, st = _gla_block(q, k, v, g, st0, w_ref, m_ref, u_ref, ls)
    xo_ref[...] = x_ref[...] + _dot(_head_out(outs, gate, gn_ref[...]), wo_s[...])
    for s in range(n_seq):
        for h in range(nh):
            s_ref[s, h] = st[s][h].T[:dk, :]


def _ab_specs(kind, p16, p32, extra, tok_block, tok_index, nh):
    width = nh * LANES

    def col(group, w=width):
        return pl.BlockSpec((tok_block, w), lambda *ids: (tok_index(*ids), group))

    if kind == "hgrn":
        (logits,) = extra
        args = [p16, p16, p16, p32, logits]
        specs = [col(0), col(1), col(2), col(0), _resident(logits.shape)]
    else:
        w_gk, b_gk = extra
        args = [p16, p16, p16, p16, p32, w_gk, b_gk]
        specs = [col(3), col(4), col(5), col(6), col(H_A, LANES), _resident(w_gk.shape),
                 _resident(b_gk.shape)]
    return args, specs


def _out_weight_spec(w_out, layer, part, rows):
    return pl.BlockSpec((1, rows, w_out.shape[2]), lambda *_: (layer, part, 0), pipeline_mode=pl.Buffered(1))


def _ab_prompt_call(kind, p16, p32, extra, gn, x, w_out, layer, part, batch, seq, dk, name):
    nh = H_A if kind == "hgrn" else H_B
    d = x.shape[1]
    tt = GLA_STEP_TOKENS
    nt = seq // tt
    tok = lambda b, t: b * nt + t
    consts = _gla_constants(1, GLA_CHUNK, nh * LANES)
    args, specs = _ab_specs(kind, p16, p32, extra, tt, tok, nh)
    n_in = len(args)
    args += [gn.reshape(1, LANES), *consts, x, w_out]
    specs += [_resident((1, LANES))] + [_resident(c.shape) for c in consts]
    specs += [pl.BlockSpec((tt, d), lambda b, t: (tok(b, t), 0)), _out_weight_spec(w_out, layer, part, nh * LANES)]
    return pl.pallas_call(
        functools.partial(_ab_prompt_kernel, kind=kind, n_in=n_in, chunk=GLA_CHUNK, q_scale=dk ** -0.5),
        out_shape=[jax.ShapeDtypeStruct((batch * seq, d), F32),
                   jax.ShapeDtypeStruct((batch, nh, dk, LANES), F32)],
        grid=(batch, nt),
        in_specs=specs,
        out_specs=[pl.BlockSpec((tt, d), lambda b, t: (tok(b, t), 0)),
                   pl.BlockSpec((1, nh, dk, LANES), lambda b, t: (b, 0, 0, 0))],
        scratch_shapes=[pltpu.VMEM((nh, LANES, LANES), F32), pltpu.VMEM((tt, nh * LANES), BF16),
                        pltpu.VMEM((nh * LANES, d), BF16)],
        compiler_params=_params(2, 40 << 20),
        name=name,
    )(*args)


def _ab_sample_call(kind, p16, p32, extra, gn, s0, row0, seq, x, x_row0, w_out, layer, part, name):
    batch, nh, dk, _ = s0.shape
    d = x.shape[1]
    nb = SAMPLE_SEQS
    rows = nb * seq
    blk0, xblk0 = row0 // rows, x_row0 // rows
    consts = _gla_constants(nb, seq, nh * LANES)
    args, specs = _ab_specs(kind, p16, p32, extra, rows, lambda i: blk0 + i, nh)
    n_in = len(args)
    args += [gn.reshape(1, LANES), *consts, s0, x, w_out]
    state_spec = pl.BlockSpec((nb, nh, dk, LANES), lambda i: (i, 0, 0, 0))
    specs += [_resident((1, LANES))] + [_resident(c.shape) for c in consts] + [state_spec]
    specs += [pl.BlockSpec((rows, d), lambda i: (xblk0 + i, 0)), _out_weight_spec(w_out, layer, part, nh * LANES)]
    return pl.pallas_call(
        functools.partial(_ab_sample_kernel, kind=kind, n_in=n_in, ls=seq, q_scale=dk ** -0.5),
        out_shape=[jax.ShapeDtypeStruct((batch * seq, d), F32),
                   jax.ShapeDtypeStruct((batch, nh, dk, LANES), F32)],
        grid=(batch // nb,),
        in_specs=specs,
        out_specs=[pl.BlockSpec((rows, d), lambda i: (i, 0)), state_spec],
        scratch_shapes=[pltpu.VMEM((nh * LANES, d), BF16)],
        compiler_params=_params(1, 40 << 20),
        name=name,
    )(*args)


def _ret_constants(n_seq, ls):
    c = n_seq * ls
    log_gamma = np.log1p(-np.exp2(-5.0 - np.arange(H_C, dtype=np.float64)))[:, None, None]
    t = np.arange(c)[:, None]
    s = np.arange(c)[None, :]
    causal = ((t // ls) == (s // ls)) & (s <= t)
    dmask = np.where(causal, np.exp(log_gamma * np.where(causal, t - s, 0)), 0.0)
    pos = (np.arange(c) % ls)[None, :, None]
    scales = np.stack([np.exp(log_gamma * (pos + 1)), np.exp(log_gamma * (ls - 1 - pos))], axis=1)
    scales = np.broadcast_to(scales, (H_C, 2, c, LANES))
    return jnp.asarray(dmask, F32), jnp.asarray(scales, F32)


def _rope_tables(positions, half):
    inv = ROPE_THETA ** (-jnp.arange(half, dtype=F32) / half)
    ang = positions.astype(F32)[:, None] * inv[None, :]
    return jnp.cos(ang), jnp.sin(ang)


def _rope(x, cos, sin):
    half = x.shape[1] // 2
    x1, x2 = x[:, :half], x[:, half:]
    return jnp.concatenate([x1 * cos - x2 * sin, x2 * cos + x1 * sin], axis=1)


def _ret_block(q16, k16, v, cos, sin, dmask, q_scale, k_scale, s_list, ls):
    dk = q16.shape[1]
    q = _rope(q16.astype(F32), cos, sin)
    k = _rope(k16.astype(F32), cos, sin) * (dk ** -0.5)
    reps = dk // LANES
    q_sc = jnp.concatenate([q_scale] * reps, axis=1)
    k_sc = jnp.concatenate([k_scale] * reps, axis=1)
    att = _dot_nt(q.astype(BF16), k.astype(BF16)) * dmask
    o = _dot(att.astype(BF16), v)
    qs = q * q_sc
    ks = k * k_sc
    decay = q_scale[ls - 1:ls, 0:1]
    n_seq = q.shape[0] // ls
    v32 = v if n_seq == 1 else v.astype(F32)
    o_parts, s_new = [], []
    for b in range(n_seq):
        rows = slice(b * ls, (b + 1) * ls)
        s = s_list[b]
        o_parts.append(o[rows] + _dot(qs[rows].astype(BF16), s.astype(BF16)))
        s_new.append(decay * s + _dot_tn(ks[rows].astype(BF16), v32[rows].astype(BF16)))
    o = o_parts[0] if n_seq == 1 else jnp.concatenate(o_parts, axis=0)
    return o, s_new


def _ret_head_out(o, gate, gn):
    oc = o - jnp.mean(o, axis=-1, keepdims=True)
    y = oc * lax.rsqrt(jnp.mean(oc * oc, axis=-1, keepdims=True) + EPS) * gn
    return (y * _silu(gate)).astype(BF16)


def _ret_prompt_kernel(q_ref, k_ref, v_ref, g_ref, cos_ref, sin_ref, dm_ref, sc_ref, gn_ref, x_ref, wo_ref,
                       xo_ref, s_ref, st_ref, y_s, wo_s):
    t = pl.program_id(1)
    nh, dk, dv = st_ref.shape
    c = q_ref.shape[0]
    _stage_out_weight(wo_ref, wo_s, jnp.logical_and(pl.program_id(0) == 0, t == 0))

    @pl.when(t == 0)
    def _():
        st_ref[...] = jnp.zeros_like(st_ref)

    cos, sin = cos_ref[...], sin_ref[...]
    for h in range(nh):
        o, (s,) = _ret_block(q_ref[:, h * dk:(h + 1) * dk], k_ref[:, h * dk:(h + 1) * dk],
                             v_ref[:, h * dv:(h + 1) * dv], cos, sin, dm_ref[h], sc_ref[h, 0], sc_ref[h, 1],
                             [st_ref[h]], c)
        st_ref[h] = s
        y_s[:, h * dv:(h + 1) * dv] = _ret_head_out(o, g_ref[:, h * dv:(h + 1) * dv].astype(F32), gn_ref[...])
    xo_ref[...] = x_ref[...] + _dot(y_s[...], wo_s[...])

    @pl.when(t == pl.num_programs(1) - 1)
    def _():
        s_ref[0] = st_ref[...]


def _ret_sample_kernel(q_ref, k_ref, v_ref, g_ref, cos_ref, sin_ref, dm_ref, sc_ref, gn_ref, s0_ref, x_ref,
                       wo_ref, xo_ref, s_ref, wo_s, *, ls):
    h = pl.program_id(1)
    dv = v_ref.shape[1]
    _stage_out_weight(wo_ref, wo_s, jnp.logical_and(pl.program_id(0) == 0, h == 0))
    n_seq = q_ref.shape[0] // ls
    o, s = _ret_block(q_ref[...], k_ref[...], v_ref[...], cos_ref[...], sin_ref[...], dm_ref[0],
                      sc_ref[0, 0], sc_ref[0, 1], [s0_ref[b, 0] for b in range(n_seq)], ls)
    y = _ret_head_out(o, g_ref[...].astype(F32), gn_ref[...])
    proj = _dot(y, wo_s[pl.ds(pl.multiple_of(h * dv, dv), dv), :])

    @pl.when(h == 0)
    def _():
        xo_ref[...] = x_ref[...] + proj

    @pl.when(h > 0)
    def _():
        xo_ref[...] += proj

    for b in range(n_seq):
        s_ref[b, 0] = s[b]


def _ret_prompt_call(r16, gn, x, w_out, layer, batch, seq, dk, dv, name):
    c = RET_CHUNK
    nt = seq // c
    d = x.shape[1]
    dmask, scales = _ret_constants(1, c)
    cos, sin = _rope_tables(jnp.arange(seq, dtype=jnp.int32), dk // 2)
    tok = lambda b, t: b * nt + t
    nq, nv = H_C * dk, H_C * dv
    args = [r16, r16, r16, r16, cos, sin, dmask, scales, gn.reshape(1, dv), x, w_out]
    specs = [pl.BlockSpec((c, nq), lambda b, t: (tok(b, t), 0)),
             pl.BlockSpec((c, nq), lambda b, t: (tok(b, t), 1)),
             pl.BlockSpec((c, nv), lambda b, t: (tok(b, t), 2 * nq // nv)),
             pl.BlockSpec((c, nv), lambda b, t: (tok(b, t), 2 * nq // nv + 1)),
             pl.BlockSpec((c, dk // 2), lambda b, t: (t, 0)), pl.BlockSpec((c, dk // 2), lambda b, t: (t, 0)),
             _resident(dmask.shape), _resident(scales.shape), _resident((1, dv)),
             pl.BlockSpec((c, d), lambda b, t: (tok(b, t), 0)), _out_weight_spec(w_out, layer, 0, nv)]
    return pl.pallas_call(
        _ret_prompt_kernel,
        out_shape=[jax.ShapeDtypeStruct((batch * seq, d), F32),
                   jax.ShapeDtypeStruct((batch, H_C, dk, dv), F32)],
        grid=(batch, nt),
        in_specs=specs,
        out_specs=[pl.BlockSpec((c, d), lambda b, t: (tok(b, t), 0)),
                   pl.BlockSpec((1, H_C, dk, dv), lambda b, t: (b, 0, 0, 0))],
        scratch_shapes=[pltpu.VMEM((H_C, dk, dv), F32), pltpu.VMEM((c, nv), BF16), pltpu.VMEM((nv, d), BF16)],
        compiler_params=_params(2, 48 << 20),
        name=name,
    )(*args)


def _ret_sample_call(r16, gn, s0, row0, batch, seq, dk, dv, x, w_out, layer, name):
    nb = SAMPLE_SEQS
    rows = nb * seq
    blk0 = row0 // rows
    d = x.shape[1]
    dmask, scales = _ret_constants(nb, seq)
    pos = PAST_LEN + (jnp.arange(rows, dtype=jnp.int32) % seq)
    cos, sin = _rope_tables(pos, dk // 2)
    nq = H_C * dk

    def col(width, base):
        return pl.BlockSpec((rows, width), lambda i, h: (blk0 + i, base + h))

    state_spec = pl.BlockSpec((nb, 1, dk, dv), lambda i, h: (i, h, 0, 0))
    args = [r16, r16, r16, r16, cos, sin, dmask, scales, gn.reshape(1, dv), s0, x, w_out]
    specs = [col(dk, 0), col(dk, H_C), col(dv, 2 * nq // dv), col(dv, 2 * nq // dv + H_C),
             _resident(cos.shape), _resident(sin.shape),
             pl.BlockSpec((1, rows, rows), lambda i, h: (h, 0, 0)),
             pl.BlockSpec((1, 2, rows, LANES), lambda i, h: (h, 0, 0, 0)), _resident((1, dv)), state_spec,
             pl.BlockSpec((rows, d), lambda i, h: (blk0 + i, 0)), _out_weight_spec(w_out, layer, 0, H_C * dv)]
    return pl.pallas_call(
        functools.partial(_ret_sample_kernel, ls=seq),
        out_shape=[jax.ShapeDtypeStruct((batch * seq, d), F32),
                   jax.ShapeDtypeStruct((batch, H_C, dk, dv), F32)],
        grid=(batch // nb, H_C),
        in_specs=specs,
        out_specs=[pl.BlockSpec((rows, d), lambda i, h: (i, 0)), state_spec],
        scratch_shapes=[pltpu.VMEM((H_C * dv, d), BF16)],
        compiler_params=_params(2, 56 << 20),
        name=name,
    )(*args)


def _pad_heads(w, n_heads, width):
    d = w.shape[0]
    w = w.reshape(d, n_heads, -1)
    return jnp.pad(w, ((0, 0), (0, 0), (0, width - w.shape[2]))).reshape(d, n_heads * width)


def _prepare_ab_weights(w_in, w_gk2, b_gk, dk_a, dv_a, dk_b, dv_b):
    sizes = (H_A * dk_a, H_A * dk_a, H_A * dv_a, H_A * dv_a, H_B * dk_b, H_B * dk_b, H_B * dv_b,
             GLA_RANK, H_B * dv_b)
    qa, fa, ia, ga, qb, kb, vb, lrb, gb = jnp.split(w_in, [int(v) for v in np.cumsum(sizes)[:-1]], axis=1)
    w16 = jnp.concatenate([qa, ia, ga, _pad_heads(qb, H_B, LANES), _pad_heads(kb, H_B, LANES), vb, gb],
                          axis=1).astype(BF16)
    w32 = jnp.concatenate([fa, jnp.pad(lrb, ((0, 0), (0, LANES - GLA_RANK)))], axis=1).astype(BF16)
    w_gk = jnp.pad(_pad_heads(w_gk2, H_B, LANES), ((0, LANES - GLA_RANK), (0, 0))).astype(BF16)
    b_gk = _pad_heads(b_gk.reshape(1, -1), H_B, LANES)
    return w16, w32, w_gk, b_gk


def kernel(x_prompt, x_sample, state_hgrn, state_gla, state_ret, p_prompt, p_sample, norm_ffn1, ffn1_w_in, ffn1_w_out, norm_mix, ab_w_in, ab_w_gk2, ab_b_gk, hgrn_lb_logits, ab_gn_hgrn, ab_gn_gla, ab_w_out, ret_w_in, ret_gn, ret_w_out, norm_ffn2, ffn2_w_in, ffn2_w_out, norm_ple, ple_w_gate, ple_w_proj, norm_final):
    bp, lp, d = x_prompt.shape
    bs, ls, _ = x_sample.shape
    depth = norm_ffn1.shape[0]
    n_prompt = bp * lp
    dk_a, dv_a = state_hgrn.shape[-2:]
    dk_b, dv_b = state_gla.shape[-2:]
    dk_c, dv_c = state_ret.shape[-2:]
    assert dk_a == dv_a == dv_b == LANES and dk_b <= LANES and state_hgrn.shape[0] == 1

    n_rows = (n_prompt, bs * ls)
    xs = (x_prompt.reshape(n_prompt, d), x_sample.reshape(bs * ls, d))
    ps = (p_prompt.reshape(depth, n_prompt, -1), p_sample.reshape(depth, bs * ls, -1))

    new_hgrn_p, new_gla_p, new_ret_p, new_hgrn_s, new_gla_s, new_ret_s = [], [], [], [], [], []
    for i in range(depth):
        j = i // 2
        x = _ffn_call(xs if i == 0 else (x,), n_rows, norm_ffn1[i], ffn1_w_in, ffn1_w_out, i, name=f"ffn1_{i}")
        if i % 2 == 0:
            w16, w32, w_gk, b_gk = _prepare_ab_weights(ab_w_in[j], ab_w_gk2[j], ab_b_gk[j], dk_a, dv_a, dk_b, dv_b)
            p16, p32 = _inproj_call(x, norm_mix[i], [w16, w32], [BF16, F32], [w16.shape[1] // 2, w32.shape[1]],
                                    name=f"ab_in_{i}")
            hgrn = ("hgrn", p16, p32, (hgrn_lb_logits,), ab_gn_hgrn[j])
            gla = ("gla", p16, p32, (w_gk, b_gk), ab_gn_gla[j])
            xa_p, sa_p = _ab_prompt_call(*hgrn, x, ab_w_out, j, 0, bp, lp, dk_a, f"hgrn_p_{i}")
            xa_s, sa_s = _ab_sample_call(*hgrn, state_hgrn[j], n_prompt, ls, x, n_prompt, ab_w_out, j, 0,
                                         f"hgrn_s_{i}")
            x_p, sb_p = _ab_prompt_call(*gla, xa_p, ab_w_out, j, 1, bp, lp, dk_b, f"gla_p_{i}")
            x_s, sb_s = _ab_sample_call(*gla, state_gla[j], n_prompt, ls, xa_s, 0, ab_w_out, j, 1, f"gla_s_{i}")
            new_hgrn_p.append(sa_p)
            new_hgrn_s.append(sa_s)
            new_gla_p.append(sb_p)
            new_gla_s.append(sb_s)
        else:
            r16 = _inproj_staged_call(x, norm_mix[i], ret_w_in, j, 2 * V7X_MXU_COLS, name=f"ret_in_{i}")
            x_p, sc_p = _ret_prompt_call(r16, ret_gn[j], x, ret_w_out, j, bp, lp, dk_c, dv_c, f"ret_p_{i}")
            x_s, sc_s = _ret_sample_call(r16, ret_gn[j], state_ret[j], n_prompt, bs, ls, dk_c, dv_c, x, ret_w_out, j,
                                         f"ret_s_{i}")
            new_ret_p.append(sc_p)
            new_ret_s.append(sc_s)
        last = i == depth - 1
        x = _ffn_call((x_p, x_s), n_rows, norm_ffn2[i], ffn2_w_in, ffn2_w_out, i,
                      ple=(ps, norm_ple[i], ple_w_gate, ple_w_proj),
                      final_gain=norm_final if last else None, split_out=last, name=f"ffn2_{i}")

    y_prompt = x[0].reshape(bp, lp, d)
    y_sample = x[1].reshape(bs, ls, d)
    return (y_prompt, y_sample, jnp.stack(new_hgrn_p), jnp.stack(new_gla_p), jnp.stack(new_ret_p),
            jnp.stack(new_hgrn_s), jnp.stack(new_gla_s), jnp.stack(new_ret_s))
```

```python
import functools
import math

import numpy as np
import jax
import jax.numpy as jnp
from jax import lax
from jax.experimental import pallas as pl
from jax.experimental.pallas import tpu as pltpu

F32 = jnp.float32
BF16 = jnp.bfloat16

H_A = 4
H_B = 4
H_C = 4
GLA_RANK = 16
GLA_GATE_NORM = 16.0
ROPE_THETA = 10000.0
PAST_LEN = 16384
EPS = 1e-6
LOG2_E = math.log2(math.e)

LANES = 128
V7X_MXU_COLS = 256
V7X_VMEM_BUDGET_BYTES = 56 * 1024 * 1024

ROW_TILE = 512
GLA_CHUNK = 128
GLA_STEP_TOKENS = 1024
RET_CHUNK = 256
RET_STEP_TOKENS = 512
SAMPLE_SEQS = 8


def _params(n_axes, vmem_bytes):
    return pltpu.CompilerParams(dimension_semantics=("arbitrary",) * n_axes,
                                vmem_limit_bytes=min(int(vmem_bytes), V7X_VMEM_BUDGET_BYTES))


def _resident(shape):
    nd = len(shape)
    return pl.BlockSpec(shape, lambda *_: (0,) * nd, pipeline_mode=pl.Buffered(1))


def _resident_layer(shape, layer):
    nd = len(shape)
    return pl.BlockSpec((1,) + tuple(shape[1:]), lambda *_: (layer,) + (0,) * (nd - 1),
                        pipeline_mode=pl.Buffered(1))


def _row_maps(prompt_tiles, lead=0):
    stacked = lambda i: (jnp.maximum(i - lead, 0), 0)
    prompt = lambda i: (jnp.clip(i - lead, 0, prompt_tiles - 1), 0)
    sample = lambda i: (jnp.maximum(i - lead - prompt_tiles, 0), 0)
    return stacked, prompt, sample


def _rms(x, g):
    return x * lax.rsqrt(jnp.mean(x * x, axis=-1, keepdims=True) + EPS) * g


def _silu(x):
    return x * jax.nn.sigmoid(x)


def _dot(a, b):
    return jnp.dot(a, b, preferred_element_type=F32)


def _dot_nt(a, b):
    return lax.dot_general(a, b, (((1,), (1,)), ((), ())), preferred_element_type=F32)


def _dot_tn(a, b):
    return lax.dot_general(a, b, (((0,), (0,)), ((), ())), preferred_element_type=F32)


def _pick(is_prompt, refs, index=None):
    vals = [r[...] if index is None else r[index] for r in refs]
    return vals[0] if len(vals) == 1 else jnp.where(is_prompt, vals[0], vals[1])


def _ffn_kernel(*refs, n_x, has_ple, has_final, n_o, prompt_tiles, stage_steps, gate_steps):
    it = iter(refs)
    x_refs = [next(it) for _ in range(n_x)]
    g_ref, win_ref, wout_ref = next(it), next(it), next(it)
    if has_ple:
        p_refs = [next(it), next(it)]
        gp_ref, wg_ref, wp_ref = next(it), next(it), next(it)
    if has_final:
        gf_ref = next(it)
    o_refs = [next(it) for _ in range(n_o)]
    win_s, wout_s, act_ref = next(it), next(it), next(it)
    if has_ple:
        wg_s, wp_s = next(it), next(it)
    i = pl.program_id(0)
    cols = win_s.shape[2]
    n_half = win_s.shape[0] // 2

    @pl.when(i < stage_steps)
    def _():
        blk = win_ref[0]
        win_s[2 * i] = blk[:, :cols].astype(BF16)
        win_s[2 * i + 1] = blk[:, cols:].astype(BF16)
        wout_s[pl.ds(pl.multiple_of(i * cols, cols), cols), :] = wout_ref[0].astype(BF16)

    if has_ple:
        rows = wg_ref.shape[1]

        @pl.when(i < gate_steps)
        def _():
            wg_s[pl.ds(pl.multiple_of(i * rows, rows), rows), :] = wg_ref[0].astype(BF16)

        @pl.when(i == 0)
        def _():
            wp_s[...] = wp_ref[0].astype(BF16)

    @pl.when(i >= stage_steps)
    def _():
        is_prompt = i - stage_steps < prompt_tiles
        x = _pick(is_prompt, x_refs)
        xn = _rms(x, g_ref[...]).astype(BF16)
        for j in range(n_half):
            a = _dot(xn, win_s[j])
            b = _dot(xn, win_s[n_half + j])
            act_ref[:, j * cols:(j + 1) * cols] = (_silu(a) * b).astype(BF16)
        x = x + 0.5 * _dot(act_ref[...], wout_s[...])
        if has_ple:
            gate = jax.nn.sigmoid(_dot(_rms(x, gp_ref[...]).astype(BF16), wg_s[...]))
            x = x + gate * _dot(_pick(is_prompt, p_refs, 0).astype(BF16), wp_s[...])
        if has_final:
            x = _rms(x, gf_ref[...])
        if n_o == 1:
            o_refs[0][...] = x
        else:
            @pl.when(is_prompt)
            def _():
                o_refs[0][...] = x

            @pl.when(jnp.logical_not(is_prompt))
            def _():
                o_refs[1][...] = x


def _ffn_call(xs, n_rows, gain, w_in, w_out, layer, ple=None, final_gain=None, split_out=False, name="ffn"):
    d = xs[0].shape[1]
    d_ff = w_out.shape[1]
    tm = ROW_TILE
    cols = V7X_MXU_COLS
    assert n_rows[0] % tm == 0 and n_rows[1] % tm == 0 and d_ff % cols == 0
    pt, st = n_rows[0] // tm, n_rows[1] // tm
    stage = d_ff // cols
    stacked, prompt, sample = _row_maps(pt, stage)
    pair = [prompt, sample]
    last = stage - 1
    args = [*xs, gain.reshape(1, d), w_in, w_out]
    specs = [pl.BlockSpec((tm, d), m) for m in ([stacked] if len(xs) == 1 else pair)]
    specs += [_resident((1, d)),
              pl.BlockSpec((1, d, 2 * cols), lambda i: (layer, 0, jnp.minimum(i, last))),
              pl.BlockSpec((1, cols, d), lambda i: (layer, jnp.minimum(i, last), 0))]
    scratch = [pltpu.VMEM((2 * stage, d, cols), BF16), pltpu.VMEM((d_ff, d), BF16), pltpu.VMEM((tm, d_ff), BF16)]
    gate_steps = 0
    if ple is not None:
        ps, gp, wg, wp = ple
        gate_rows = LANES
        gate_steps = d // gate_rows
        assert gate_steps <= stage
        args += [*ps, gp.reshape(1, d), wg, wp]
        specs += [pl.BlockSpec((1, tm, ps[0].shape[2]), lambda i, m=m: (layer, *m(i))) for m in pair]
        specs += [_resident((1, d)),
                  pl.BlockSpec((1, gate_rows, d), lambda i: (layer, jnp.minimum(i, gate_steps - 1), 0)),
                  _resident_layer(wp.shape, layer)]
        scratch += [pltpu.VMEM((d, d), BF16), pltpu.VMEM(wp.shape[1:], BF16)]
    if final_gain is not None:
        args.append(final_gain.reshape(1, d))
        specs.append(_resident((1, d)))
    if split_out:
        out_shape = [jax.ShapeDtypeStruct((r, d), F32) for r in n_rows]
        out_specs = [pl.BlockSpec((tm, d), m) for m in pair]
    else:
        out_shape = jax.ShapeDtypeStruct((sum(n_rows), d), F32)
        out_specs = pl.BlockSpec((tm, d), stacked)
    return pl.pallas_call(
        functools.partial(_ffn_kernel, n_x=len(xs), has_ple=ple is not None, has_final=final_gain is not None,
                          n_o=2 if split_out else 1, prompt_tiles=pt, stage_steps=stage, gate_steps=gate_steps),
        out_shape=out_shape,
        grid=(stage + pt + st,),
        in_specs=specs,
        out_specs=out_specs,
        scratch_shapes=scratch,
        compiler_params=_params(1, V7X_VMEM_BUDGET_BYTES),
        name=name,
    )(*args)


def _inproj_kernel(*refs, n_out, chunk_cols):
    x_ref, g_ref = refs[0], refs[1]
    w_refs = refs[2:2 + n_out]
    o_refs = refs[2 + n_out:2 + 2 * n_out]
    xn = _rms(x_ref[...], g_ref[...]).astype(BF16)
    for w_ref, o_ref, cw in zip(w_refs, o_refs, chunk_cols):
        for c in range(w_ref.shape[1] // cw):
            o_ref[:, c * cw:(c + 1) * cw] = _dot(xn, w_ref[:, c * cw:(c + 1) * cw]).astype(o_ref.dtype)


def _inproj_call(x, gain, weights, out_dtypes, chunk_cols, name):
    n, d = x.shape
    tm = ROW_TILE
    row = lambda i: (i, 0)
    specs = [pl.BlockSpec((tm, d), row), _resident((1, d))] + [_resident(w.shape) for w in weights]
    out_shape = [jax.ShapeDtypeStruct((n, w.shape[1]), dt) for w, dt in zip(weights, out_dtypes)]
    out_specs = [pl.BlockSpec((tm, w.shape[1]), row) for w in weights]
    vmem = (sum(2 * w.size for w in weights) + 4 * tm * d * 4
            + sum(2 * tm * w.shape[1] * jnp.dtype(dt).itemsize for w, dt in zip(weights, out_dtypes))
            + 2 * tm * max(chunk_cols) * 4 + (4 << 20))
    return pl.pallas_call(
        functools.partial(_inproj_kernel, n_out=len(weights), chunk_cols=tuple(chunk_cols)),
        out_shape=out_shape,
        grid=(n // tm,),
        in_specs=specs,
        out_specs=out_specs,
        compiler_params=_params(1, vmem),
        name=name,
    )(x, gain.reshape(1, d), *weights)


def _inproj_staged_kernel(x_ref, g_ref, w_ref, o_ref, w_s, *, stage_steps):
    i = pl.program_id(0)
    cols = w_s.shape[2]

    @pl.when(i < stage_steps)
    def _():
        w_s[i] = w_ref[0].astype(BF16)

    @pl.when(i >= stage_steps)
    def _():
        xn = _rms(x_ref[...], g_ref[...]).astype(BF16)
        for c in range(stage_steps):
            o_ref[:, c * cols:(c + 1) * cols] = _dot(xn, w_s[c]).astype(o_ref.dtype)


def _inproj_staged_call(x, gain, w, layer, cols, name):
    n, d = x.shape
    n_out = w.shape[2]
    tm = ROW_TILE
    assert n_out % cols == 0
    stage = n_out // cols
    row = lambda i: (jnp.maximum(i - stage, 0), 0)
    vmem = 2 * d * n_out + 2 * d * cols * 4 + 4 * tm * d * 4 + 2 * tm * n_out * 2 + 2 * tm * cols * 4 + (4 << 20)
    return pl.pallas_call(
        functools.partial(_inproj_staged_kernel, stage_steps=stage),
        out_shape=jax.ShapeDtypeStruct((n, n_out), BF16),
        grid=(stage + n // tm,),
        in_specs=[pl.BlockSpec((tm, d), row), _resident((1, d)),
                  pl.BlockSpec((1, d, cols), lambda i: (layer, 0, jnp.minimum(i, stage - 1)))],
        out_specs=pl.BlockSpec((tm, n_out), row),
        scratch_shapes=[pltpu.VMEM((stage, d, cols), BF16)],
        compiler_params=_params(1, vmem),
        name=name,
    )(x, gain.reshape(1, d), w)


GLA_MATMUL_LEVELS = (1, 2)


def _gla_constants(n_seq, ls, width):
    c = n_seq * ls
    nlev = int(round(math.log2(ls)))
    assert 2 ** nlev == ls and nlev > max(GLA_MATMUL_LEVELS)
    t = np.arange(c)[:, None]
    i = np.arange(c)[None, :]
    w_rows = [((t // ls) == (i // ls)) & (i <= t)]
    masks, uppers = [], []
    for l in range(nlev):
        m = 2 ** l
        upper = ((t // m) % 2) == 1
        r = (t // (2 * m)) * (2 * m) + m - 1
        if l in GLA_MATMUL_LEVELS:
            w_rows.append(np.where(upper, (i > r) & (i <= t), (i > t) & (i <= r)))
        masks.append(((t // (2 * m)) == (i // (2 * m))) & upper & (((i // m) % 2) == 0))
        uppers.append(np.broadcast_to(upper, (c, width)))
    w = jnp.asarray(np.concatenate(w_rows, 0).astype(np.float32), BF16)
    return w, jnp.asarray(np.stack(masks).astype(np.float32)), jnp.asarray(np.stack(uppers).astype(np.float32))


def _gla_block(q, k, v, g, st_list, w_ref, m_ref, u_ref, ls):
    c, width = q.shape
    nh = width // LANES
    n_seq = c // ls
    nlev = m_ref.shape[0]
    g_hi = g.astype(BF16)
    g_lo = (g - g_hi.astype(F32)).astype(BF16)
    ex2 = _dot(w_ref[...], jnp.concatenate([g_hi, g_lo], axis=1))
    ex = ex2[:, :width] + ex2[:, width:]
    b = ex[:c]

    def row(r, n):
        return jnp.broadcast_to(b[r:r + 1, :], (n, width))

    b_end = jnp.concatenate([row((s + 1) * ls - 1, ls) for s in range(n_seq)], axis=0) if n_seq > 1 else row(c - 1, c)
    e_cum = jnp.exp2(b)
    qe = q * e_cum
    kr = k * jnp.exp2(b_end - b)

    att = [None] * nh
    for l in range(nlev):
        m = 2 ** l
        if l == 0:
            x = jnp.where(u_ref[l] > 0.5, q * jnp.exp2(g), k)
        elif l in GLA_MATMUL_LEVELS:
            j = 1 + GLA_MATMUL_LEVELS.index(l)
            x = jnp.where(u_ref[l] > 0.5, q, k) * jnp.exp2(ex[j * c:(j + 1) * c])
        else:
            pieces = []
            for p in range(c // (2 * m)):
                lo, mid, hi = p * 2 * m, p * 2 * m + m, (p + 1) * 2 * m
                pivot = row(mid - 1, m)
                pieces.append(k[lo:mid] * jnp.exp2(pivot - b[lo:mid]))
                pieces.append(q[mid:hi] * jnp.exp2(b[mid:hi] - pivot))
            x = jnp.concatenate(pieces, axis=0)
        x = x.astype(BF16)
        mask = m_ref[l] > 0.5
        for h in range(nh):
            xh = x[:, h * LANES:(h + 1) * LANES]
            p = _dot_nt(xh, xh)
            att[h] = jnp.where(mask, p, 0.0 if att[h] is None else att[h])

    qk = q * k
    outs, st_new = [], [[None] * nh for _ in range(n_seq)]
    for h in range(nh):
        sl = slice(h * LANES, (h + 1) * LANES)
        vh = v[:, sl]
        o = _dot(att[h].astype(BF16), vh.astype(BF16)) + jnp.sum(qk[:, sl], axis=1, keepdims=True) * vh
        parts = []
        for s in range(n_seq):
            rows = slice(s * ls, (s + 1) * ls)
            st = st_list[s][h]
            parts.append(o[rows] + _dot_nt(qe[rows, sl].astype(BF16), st.astype(BF16)))
            e_tot = e_cum[(s + 1) * ls - 1:(s + 1) * ls, sl]
            st_new[s][h] = e_tot * st + _dot_tn(vh[rows].astype(BF16), kr[rows, sl].astype(BF16))
        outs.append(parts[0] if n_seq == 1 else jnp.concatenate(parts, axis=0))
    return outs, st_new


def _hgrn_inputs(refs, rows, q_scale):
    del q_scale
    qa_ref, ia_ref, ga_ref, fa_ref, lg_ref = refs
    lg = lg_ref[...]
    ex = jnp.exp(lg - jnp.max(lg, axis=0, keepdims=True))
    lb = ex[0:1] / jnp.sum(ex, axis=0, keepdims=True)
    sig = jax.nn.sigmoid(fa_ref[rows, :])
    q = _silu(qa_ref[rows, :].astype(F32))
    k = (1.0 - lb) * (1.0 - sig)
    g = jnp.log(lb + (1.0 - lb) * sig) * LOG2_E
    return q, k, ia_ref[rows, :].astype(F32), g, ga_ref[rows, :].astype(F32)


def _gla_inputs(refs, rows, q_scale):
    qb_ref, kb_ref, vb_ref, gb_ref, lr_ref, wgk_ref, bgk_ref = refs
    z = _dot(lr_ref[rows, :].astype(BF16), wgk_ref[...]) + bgk_ref[...]
    g = (jnp.minimum(z, 0.0) - jnp.log1p(jnp.exp(-jnp.abs(z)))) * (LOG2_E / GLA_GATE_NORM)
    q = qb_ref[rows, :].astype(F32) * q_scale
    return q, kb_ref[rows, :].astype(F32), vb_ref[rows, :].astype(F32), g, gb_ref[rows, :].astype(F32)


def _head_out(outs, gate, gn):
    ys = [_rms(o, gn) * _silu(gate[:, h * LANES:(h + 1) * LANES]) for h, o in enumerate(outs)]
    return jnp.concatenate(ys, axis=1).astype(BF16)


def _stage_out_weight(wo_ref, wo_s, first):
    @pl.when(first)
    def _():
        wo_s[...] = wo_ref[0].astype(BF16)


def _ab_prompt_kernel(*refs, kind, n_in, chunk, q_scale):
    in_refs = refs[:n_in]
    gn_ref, w_ref, m_ref, u_ref, x_ref, wo_ref, xo_ref, s_ref, st_ref, y_s, wo_s = refs[n_in:]
    t = pl.program_id(1)
    nh = st_ref.shape[0]
    _stage_out_weight(wo_ref, wo_s, jnp.logical_and(pl.program_id(0) == 0, t == 0))

    @pl.when(t == 0)
    def _():
        st_ref[...] = jnp.zeros_like(st_ref)

    load = _hgrn_inputs if kind == "hgrn" else _gla_inputs
    st = [st_ref[h] for h in range(nh)]
    for c in range(y_s.shape[0] // chunk):
        rows = slice(c * chunk, (c + 1) * chunk)
        q, k, v, g, gate = load(in_refs, rows, q_scale)
        outs, (st,) = _gla_block(q, k, v, g, [st], w_ref, m_ref, u_ref, chunk)
        y_s[rows, :] = _head_out(outs, gate, gn_ref[...])
    for h in range(nh):
        st_ref[h] = st[h]
    xo_ref[...] = x_ref[...] + _dot(y_s[...], wo_s[...])

    @pl.when(t == pl.num_programs(1) - 1)
    def _():
        for h in range(nh):
            s_ref[0, h] = st_ref[h].T[:s_ref.shape[2], :]


def _ab_sample_kernel(*refs, kind, n_in, ls, q_scale):
    in_refs = refs[:n_in]
    gn_ref, w_ref, m_ref, u_ref, s0_ref, x_ref, wo_ref, xo_ref, s_ref, wo_s = refs[n_in:]
    _stage_out_weight(wo_ref, wo_s, pl.program_id(0) == 0)
    load = _hgrn_inputs if kind == "hgrn" else _gla_inputs
    n_seq, nh, dk = s0_ref.shape[:3]
    q, k, v, g, gate = load(in_refs, slice(None), q_scale)
    st0 = []
    for s in range(n_seq):
        st0.append([])
        for h in range(nh):
            s0 = s0_ref[s, h]
            if dk < LANES:
                s0 = jnp.concatenate([s0, jnp.zeros((LANES - dk, s0.shape[1]), F32)], axis=0)
            st0[s].append(s0.T)
    outs, st = _gla_block(q, k, v, g, st0, w_ref, m_ref, u_ref, ls)
    xo_ref[...] = x_ref[...] + _dot(_head_out(outs, gate, gn_ref[...]), wo_s[...])
    for s in range(n_seq):
        for h in range(nh):
            s_ref[s, h] = st[s][h].T[:dk, :]


def _ab_specs(kind, p16, p32, extra, tok_block, tok_index, nh):
    width = nh * LANES

    def col(group, w=width):
        return pl.BlockSpec((tok_block, w), lambda *ids: (tok_index(*ids), group))

    if kind == "hgrn":
        (logits,) = extra
        args = [p16, p16, p16, p32, logits]
        specs = [col(0), col(1), col(2), col(0), _resident(logits.shape)]
    else:
        w_gk, b_gk = extra
        args = [p16, p16, p16, p16, p32, w_gk, b_gk]
        specs = [col(3), col(4), col(5), col(6), col(H_A, LANES), _resident(w_gk.shape),
                 _resident(b_gk.shape)]
    return args, specs


def _out_weight_spec(w_out, layer, part, rows):
    return pl.BlockSpec((1, rows, w_out.shape[2]), lambda *_: (layer, part, 0), pipeline_mode=pl.Buffered(1))


def _ab_prompt_call(kind, p16, p32, extra, gn, x, w_out, layer, part, batch, seq, dk, name):
    nh = H_A if kind == "hgrn" else H_B
    d = x.shape[1]
    tt = GLA_STEP_TOKENS
    nt = seq // tt
    tok = lambda b, t: b * nt + t
    consts = _gla_constants(1, GLA_CHUNK, nh * LANES)
    args, specs = _ab_specs(kind, p16, p32, extra, tt, tok, nh)
    n_in = len(args)
    args += [gn.reshape(1, LANES), *consts, x, w_out]
    specs += [_resident((1, LANES))] + [_resident(c.shape) for c in consts]
    specs += [pl.BlockSpec((tt, d), lambda b, t: (tok(b, t), 0)), _out_weight_spec(w_out, layer, part, nh * LANES)]
    return pl.pallas_call(
        functools.partial(_ab_prompt_kernel, kind=kind, n_in=n_in, chunk=GLA_CHUNK, q_scale=dk ** -0.5),
        out_shape=[jax.ShapeDtypeStruct((batch * seq, d), F32),
                   jax.ShapeDtypeStruct((batch, nh, dk, LANES), F32)],
        grid=(batch, nt),
        in_specs=specs,
        out_specs=[pl.BlockSpec((tt, d), lambda b, t: (tok(b, t), 0)),
                   pl.BlockSpec((1, nh, dk, LANES), lambda b, t: (b, 0, 0, 0))],
        scratch_shapes=[pltpu.VMEM((nh, LANES, LANES), F32), pltpu.VMEM((tt, nh * LANES), BF16),
                        pltpu.VMEM((nh * LANES, d), BF16)],
        compiler_params=_params(2, 40 << 20),
        name=name,
    )(*args)


def _ab_sample_call(kind, p16, p32, extra, gn, s0, row0, seq, x, x_row0, w_out, layer, part, name):
    batch, nh, dk, _ = s0.shape
    d = x.shape[1]
    nb = SAMPLE_SEQS
    rows = nb * seq
    blk0, xblk0 = row0 // rows, x_row0 // rows
    consts = _gla_constants(nb, seq, nh * LANES)
    args, specs = _ab_specs(kind, p16, p32, extra, rows, lambda i: blk0 + i, nh)
    n_in = len(args)
    args += [gn.reshape(1, LANES), *consts, s0, x, w_out]
    state_spec = pl.BlockSpec((nb, nh, dk, LANES), lambda i: (i, 0, 0, 0))
    specs += [_resident((1, LANES))] + [_resident(c.shape) for c in consts] + [state_spec]
    specs += [pl.BlockSpec((rows, d), lambda i: (xblk0 + i, 0)), _out_weight_spec(w_out, layer, part, nh * LANES)]
    return pl.pallas_call(
        functools.partial(_ab_sample_kernel, kind=kind, n_in=n_in, ls=seq, q_scale=dk ** -0.5),
        out_shape=[jax.ShapeDtypeStruct((batch * seq, d), F32),
                   jax.ShapeDtypeStruct((batch, nh, dk, LANES), F32)],
        grid=(batch // nb,),
        in_specs=specs,
        out_specs=[pl.BlockSpec((rows, d), lambda i: (i, 0)), state_spec],
        scratch_shapes=[pltpu.VMEM((nh * LANES, d), BF16)],
        compiler_params=_params(1, 40 << 20),
        name=name,
    )(*args)


def _ret_constants(n_seq, ls):
    c = n_seq * ls
    log_gamma = np.log1p(-np.exp2(-5.0 - np.arange(H_C, dtype=np.float64)))[:, None, None]
    t = np.arange(c)[:, None]
    s = np.arange(c)[None, :]
    causal = ((t // ls) == (s // ls)) & (s <= t)
    dmask = np.where(causal, np.exp(log_gamma * np.where(causal, t - s, 0)), 0.0)
    pos = (np.arange(c) % ls)[None, :, None]
    scales = np.stack([np.exp(log_gamma * (pos + 1)), np.exp(log_gamma * (ls - 1 - pos))], axis=1)
    scales = np.broadcast_to(scales, (H_C, 2, c, LANES))
    return jnp.asarray(dmask, F32), jnp.asarray(scales, F32)


def _rope_tables(positions, half):
    inv = ROPE_THETA ** (-jnp.arange(half, dtype=F32) / half)
    ang = positions.astype(F32)[:, None] * inv[None, :]
    return jnp.cos(ang), jnp.sin(ang)


def _rope(x, cos, sin):
    half = x.shape[1] // 2
    x1, x2 = x[:, :half], x[:, half:]
    return jnp.concatenate([x1 * cos - x2 * sin, x2 * cos + x1 * sin], axis=1)


def _ret_block(q16, k16, v, cos, sin, dmask, q_scale, k_scale, s_list, ls):
    dk = q16.shape[1]
    q = _rope(q16.astype(F32), cos, sin)
    k = _rope(k16.astype(F32), cos, sin) * (dk ** -0.5)
    reps = dk // LANES
    q_sc = jnp.concatenate([q_scale] * reps, axis=1)
    k_sc = jnp.concatenate([k_scale] * reps, axis=1)
    att = _dot_nt(q.astype(BF16), k.astype(BF16)) * dmask
    o = _dot(att.astype(BF16), v)
    qs = q * q_sc
    ks = k * k_sc
    decay = q_scale[ls - 1:ls, 0:1]
    n_seq = q.shape[0] // ls
    v32 = v if n_seq == 1 else v.astype(F32)
    o_parts, s_new = [], []
    for b in range(n_seq):
        rows = slice(b * ls, (b + 1) * ls)
        s = s_list[b]
        o_parts.append(o[rows] + _dot(qs[rows].astype(BF16), s.astype(BF16)))
        s_new.append(decay * s + _dot_tn(ks[rows].astype(BF16), v32[rows].astype(BF16)))
    o = o_parts[0] if n_seq == 1 else jnp.concatenate(o_parts, axis=0)
    return o, s_new


def _ret_head_out(o, gate, gn):
    oc = o - jnp.mean(o, axis=-1, keepdims=True)
    y = oc * lax.rsqrt(jnp.mean(oc * oc, axis=-1, keepdims=True) + EPS) * gn
    return (y * _silu(gate)).astype(BF16)


def _ret_prompt_kernel(q_ref, k_ref, v_ref, g_ref, cos_ref, sin_ref, dm_ref, sc_ref, gn_ref, x_ref, wo_ref,
                       xo_ref, s_ref, st_ref, y_s, wo_s):
    t = pl.program_id(1)
    nh, dk, dv = st_ref.shape
    c = dm_ref.shape[1]
    _stage_out_weight(wo_ref, wo_s, jnp.logical_and(pl.program_id(0) == 0, t == 0))

    @pl.when(t == 0)
    def _():
        st_ref[...] = jnp.zeros_like(st_ref)

    for h in range(nh):
        s = st_ref[h]
        for j in range(q_ref.shape[0] // c):
            rows = slice(j * c, (j + 1) * c)
            o, (s,) = _ret_block(q_ref[rows, h * dk:(h + 1) * dk], k_ref[rows, h * dk:(h + 1) * dk],
                                 v_ref[rows, h * dv:(h + 1) * dv], cos_ref[rows, :], sin_ref[rows, :], dm_ref[h],
                                 sc_ref[h, 0], sc_ref[h, 1], [s], c)
            y_s[rows, h * dv:(h + 1) * dv] = _ret_head_out(o, g_ref[rows, h * dv:(h + 1) * dv].astype(F32),
                                                           gn_ref[...])
        st_ref[h] = s
    xo_ref[...] = x_ref[...] + _dot(y_s[...], wo_s[...])

    @pl.when(t == pl.num_programs(1) - 1)
    def _():
        s_ref[0] = st_ref[...]


def _ret_sample_kernel(q_ref, k_ref, v_ref, g_ref, cos_ref, sin_ref, dm_ref, sc_ref, gn_ref, s0_ref, x_ref,
                       wo_ref, xo_ref, s_ref, wo_s, *, ls):
    h = pl.program_id(1)
    dv = v_ref.shape[1]
    _stage_out_weight(wo_ref, wo_s, jnp.logical_and(pl.program_id(0) == 0, h == 0))
    n_seq = q_ref.shape[0] // ls
    o, s = _ret_block(q_ref[...], k_ref[...], v_ref[...], cos_ref[...], sin_ref[...], dm_ref[0],
                      sc_ref[0, 0], sc_ref[0, 1], [s0_ref[b, 0] for b in range(n_seq)], ls)
    y = _ret_head_out(o, g_ref[...].astype(F32), gn_ref[...])
    proj = _dot(y, wo_s[pl.ds(pl.multiple_of(h * dv, dv), dv), :])

    @pl.when(h == 0)
    def _():
        xo_ref[...] = x_ref[...] + proj

    @pl.when(h > 0)
    def _():
        xo_ref[...] += proj

    for b in range(n_seq):
        s_ref[b, 0] = s[b]


def _ret_prompt_call(r16, gn, x, w_out, layer, batch, seq, dk, dv, name):
    c = RET_STEP_TOKENS
    nt = seq // c
    d = x.shape[1]
    dmask, scales = _ret_constants(1, RET_CHUNK)
    cos, sin = _rope_tables(jnp.arange(seq, dtype=jnp.int32), dk // 2)
    tok = lambda b, t: b * nt + t
    nq, nv = H_C * dk, H_C * dv
    args = [r16, r16, r16, r16, cos, sin, dmask, scales, gn.reshape(1, dv), x, w_out]
    specs = [pl.BlockSpec((c, nq), lambda b, t: (tok(b, t), 0)),
             pl.BlockSpec((c, nq), lambda b, t: (tok(b, t), 1)),
             pl.BlockSpec((c, nv), lambda b, t: (tok(b, t), 2 * nq // nv)),
             pl.BlockSpec((c, nv), lambda b, t: (tok(b, t), 2 * nq // nv + 1)),
             pl.BlockSpec((c, dk // 2), lambda b, t: (t, 0)), pl.BlockSpec((c, dk // 2), lambda b, t: (t, 0)),
             _resident(dmask.shape), _resident(scales.shape), _resident((1, dv)),
             pl.BlockSpec((c, d), lambda b, t: (tok(b, t), 0)), _out_weight_spec(w_out, layer, 0, nv)]
    return pl.pallas_call(
        _ret_prompt_kernel,
        out_shape=[jax.ShapeDtypeStruct((batch * seq, d), F32),
                   jax.ShapeDtypeStruct((batch, H_C, dk, dv), F32)],
        grid=(batch, nt),
        in_specs=specs,
        out_specs=[pl.BlockSpec((c, d), lambda b, t: (tok(b, t), 0)),
                   pl.BlockSpec((1, H_C, dk, dv), lambda b, t: (b, 0, 0, 0))],
        scratch_shapes=[pltpu.VMEM((H_C, dk, dv), F32), pltpu.VMEM((c, nv), BF16), pltpu.VMEM((nv, d), BF16)],
        compiler_params=_params(2, 48 << 20),
        name=name,
    )(*args)


def _ret_sample_call(r16, gn, s0, row0, batch, seq, dk, dv, x, w_out, layer, name):
    nb = SAMPLE_SEQS
    rows = nb * seq
    blk0 = row0 // rows
    d = x.shape[1]
    dmask, scales = _ret_constants(nb, seq)
    pos = PAST_LEN + (jnp.arange(rows, dtype=jnp.int32) % seq)
    cos, sin = _rope_tables(pos, dk // 2)
    nq = H_C * dk

    def col(width, base):
        return pl.BlockSpec((rows, width), lambda i, h: (blk0 + i, base + h))

    state_spec = pl.BlockSpec((nb, 1, dk, dv), lambda i, h: (i, h, 0, 0))
    args = [r16, r16, r16, r16, cos, sin, dmask, scales, gn.reshape(1, dv), s0, x, w_out]
    specs = [col(dk, 0), col(dk, H_C), col(dv, 2 * nq // dv), col(dv, 2 * nq // dv + H_C),
             _resident(cos.shape), _resident(sin.shape),
             pl.BlockSpec((1, rows, rows), lambda i, h: (h, 0, 0)),
             pl.BlockSpec((1, 2, rows, LANES), lambda i, h: (h, 0, 0, 0)), _resident((1, dv)), state_spec,
             pl.BlockSpec((rows, d), lambda i, h: (blk0 + i, 0)), _out_weight_spec(w_out, layer, 0, H_C * dv)]
    return pl.pallas_call(
        functools.partial(_ret_sample_kernel, ls=seq),
        out_shape=[jax.ShapeDtypeStruct((batch * seq, d), F32),
                   jax.ShapeDtypeStruct((batch, H_C, dk, dv), F32)],
        grid=(batch // nb, H_C),
        in_specs=specs,
        out_specs=[pl.BlockSpec((rows, d), lambda i, h: (i, 0)), state_spec],
        scratch_shapes=[pltpu.VMEM((H_C * dv, d), BF16)],
        compiler_params=_params(2, 56 << 20),
        name=name,
    )(*args)


def _pad_heads(w, n_heads, width):
    d = w.shape[0]
    w = w.reshape(d, n_heads, -1)
    return jnp.pad(w, ((0, 0), (0, 0), (0, width - w.shape[2]))).reshape(d, n_heads * width)


def _prepare_ab_weights(w_in, w_gk2, b_gk, dk_a, dv_a, dk_b, dv_b):
    sizes = (H_A * dk_a, H_A * dk_a, H_A * dv_a, H_A * dv_a, H_B * dk_b, H_B * dk_b, H_B * dv_b,
             GLA_RANK, H_B * dv_b)
    qa, fa, ia, ga, qb, kb, vb, lrb, gb = jnp.split(w_in, [int(v) for v in np.cumsum(sizes)[:-1]], axis=1)
    w16 = jnp.concatenate([qa, ia, ga, _pad_heads(qb, H_B, LANES), _pad_heads(kb, H_B, LANES), vb, gb],
                          axis=1).astype(BF16)
    w32 = jnp.concatenate([fa, jnp.pad(lrb, ((0, 0), (0, LANES - GLA_RANK)))], axis=1).astype(BF16)
    w_gk = jnp.pad(_pad_heads(w_gk2, H_B, LANES), ((0, LANES - GLA_RANK), (0, 0))).astype(BF16)
    b_gk = _pad_heads(b_gk.reshape(1, -1), H_B, LANES)
    return w16, w32, w_gk, b_gk


def kernel(x_prompt, x_sample, state_hgrn, state_gla, state_ret, p_prompt, p_sample, norm_ffn1, ffn1_w_in, ffn1_w_out, norm_mix, ab_w_in, ab_w_gk2, ab_b_gk, hgrn_lb_logits, ab_gn_hgrn, ab_gn_gla, ab_w_out, ret_w_in, ret_gn, ret_w_out, norm_ffn2, ffn2_w_in, ffn2_w_out, norm_ple, ple_w_gate, ple_w_proj, norm_final):
    bp, lp, d = x_prompt.shape
    bs, ls, _ = x_sample.shape
    depth = norm_ffn1.shape[0]
    n_prompt = bp * lp
    dk_a, dv_a = state_hgrn.shape[-2:]
    dk_b, dv_b = state_gla.shape[-2:]
    dk_c, dv_c = state_ret.shape[-2:]
    assert dk_a == dv_a == dv_b == LANES and dk_b <= LANES and state_hgrn.shape[0] == 1

    n_rows = (n_prompt, bs * ls)
    xs = (x_prompt.reshape(n_prompt, d), x_sample.reshape(bs * ls, d))
    ps = (p_prompt.reshape(depth, n_prompt, -1), p_sample.reshape(depth, bs * ls, -1))

    new_hgrn_p, new_gla_p, new_ret_p, new_hgrn_s, new_gla_s, new_ret_s = [], [], [], [], [], []
    for i in range(depth):
        j = i // 2
        x = _ffn_call(xs if i == 0 else (x,), n_rows, norm_ffn1[i], ffn1_w_in, ffn1_w_out, i, name=f"ffn1_{i}")
        if i % 2 == 0:
            w16, w32, w_gk, b_gk = _prepare_ab_weights(ab_w_in[j], ab_w_gk2[j], ab_b_gk[j], dk_a, dv_a, dk_b, dv_b)
            p16, p32 = _inproj_call(x, norm_mix[i], [w16, w32], [BF16, F32], [w16.shape[1] // 2, w32.shape[1]],
                                    name=f"ab_in_{i}")
            hgrn = ("hgrn", p16, p32, (hgrn_lb_logits,), ab_gn_hgrn[j])
            gla = ("gla", p16, p32, (w_gk, b_gk), ab_gn_gla[j])
            xa_p, sa_p = _ab_prompt_call(*hgrn, x, ab_w_out, j, 0, bp, lp, dk_a, f"hgrn_p_{i}")
            xa_s, sa_s = _ab_sample_call(*hgrn, state_hgrn[j], n_prompt, ls, x, n_prompt, ab_w_out, j, 0,
                                         f"hgrn_s_{i}")
            x_p, sb_p = _ab_prompt_call(*gla, xa_p, ab_w_out, j, 1, bp, lp, dk_b, f"gla_p_{i}")
            x_s, sb_s = _ab_sample_call(*gla, state_gla[j], n_prompt, ls, xa_s, 0, ab_w_out, j, 1, f"gla_s_{i}")
            new_hgrn_p.append(sa_p)
            new_hgrn_s.append(sa_s)
            new_gla_p.append(sb_p)
            new_gla_s.append(sb_s)
        else:
            r16 = _inproj_staged_call(x, norm_mix[i], ret_w_in, j, 2 * V7X_MXU_COLS, name=f"ret_in_{i}")
            x_p, sc_p = _ret_prompt_call(r16, ret_gn[j], x, ret_w_out, j, bp, lp, dk_c, dv_c, f"ret_p_{i}")
            x_s, sc_s = _ret_sample_call(r16, ret_gn[j], state_ret[j], n_prompt, bs, ls, dk_c, dv_c, x, ret_w_out, j,
                                         f"ret_s_{i}")
            new_ret_p.append(sc_p)
            new_ret_s.append(sc_s)
        last = i == depth - 1
        x = _ffn_call((x_p, x_s), n_rows, norm_ffn2[i], ffn2_w_in, ffn2_w_out, i,
                      ple=(ps, norm_ple[i], ple_w_gate, ple_w_proj),
                      final_gain=norm_final if last else None, split_out=last, name=f"ffn2_{i}")

    y_prompt = x[0].reshape(bp, lp, d)
    y_sample = x[1].reshape(bs, ls, d)
    return (y_prompt, y_sample, jnp.stack(new_hgrn_p), jnp.stack(new_gla_p), jnp.stack(new_ret_p),
            jnp.stack(new_hgrn_s), jnp.stack(new_gla_s), jnp.stack(new_ret_s))
```

```python
import functools
import math

import numpy as np
import jax
import jax.numpy as jnp
from jax import lax
from jax.experimental import pallas as pl
from jax.experimental.pallas import tpu as pltpu

F32 = jnp.float32
BF16 = jnp.bfloat16

H_A = 4
H_B = 4
H_C = 4
GLA_RANK = 16
GLA_GATE_NORM = 16.0
ROPE_THETA = 10000.0
PAST_LEN = 16384
EPS = 1e-6
LOG2_E = math.log2(math.e)

LANES = 128
V7X_MXU_COLS = 256
V7X_VMEM_BUDGET_BYTES = 56 * 1024 * 1024

ROW_TILE = 512
GLA_CHUNK = 128
GLA_STEP_TOKENS = 1024
RET_CHUNK = 256
RET_STEP_TOKENS = 512
SAMPLE_SEQS = 16


def _params(n_axes, vmem_bytes):
    return pltpu.CompilerParams(dimension_semantics=("arbitrary",) * n_axes,
                                vmem_limit_bytes=min(int(vmem_bytes), V7X_VMEM_BUDGET_BYTES))


def _resident(shape):
    nd = len(shape)
    return pl.BlockSpec(shape, lambda *_: (0,) * nd, pipeline_mode=pl.Buffered(1))


def _resident_layer(shape, layer):
    nd = len(shape)
    return pl.BlockSpec((1,) + tuple(shape[1:]), lambda *_: (layer,) + (0,) * (nd - 1),
                        pipeline_mode=pl.Buffered(1))


def _row_maps(prompt_tiles, lead=0):
    stacked = lambda i: (jnp.maximum(i - lead, 0), 0)
    prompt = lambda i: (jnp.clip(i - lead, 0, prompt_tiles - 1), 0)
    sample = lambda i: (jnp.maximum(i - lead - prompt_tiles, 0), 0)
    return stacked, prompt, sample


def _rms(x, g):
    return x * lax.rsqrt(jnp.mean(x * x, axis=-1, keepdims=True) + EPS) * g


def _silu(x):
    return x * jax.nn.sigmoid(x)


def _dot(a, b):
    return jnp.dot(a, b, preferred_element_type=F32)


def _dot_nt(a, b):
    return lax.dot_general(a, b, (((1,), (1,)), ((), ())), preferred_element_type=F32)


def _dot_tn(a, b):
    return lax.dot_general(a, b, (((0,), (0,)), ((), ())), preferred_element_type=F32)


def _pick(is_prompt, refs, index=None):
    vals = [r[...] if index is None else r[index] for r in refs]
    return vals[0] if len(vals) == 1 else jnp.where(is_prompt, vals[0], vals[1])


def _ffn_kernel(*refs, n_x, has_ple, has_final, n_o, prompt_tiles, stage_steps, gate_steps):
    it = iter(refs)
    x_refs = [next(it) for _ in range(n_x)]
    g_ref, win_ref, wout_ref = next(it), next(it), next(it)
    if has_ple:
        p_refs = [next(it), next(it)]
        gp_ref, wg_ref, wp_ref = next(it), next(it), next(it)
    if has_final:
        gf_ref = next(it)
    o_refs = [next(it) for _ in range(n_o)]
    win_s, wout_s, act_ref = next(it), next(it), next(it)
    if has_ple:
        wg_s, wp_s = next(it), next(it)
    i = pl.program_id(0)
    cols = win_s.shape[2]
    n_half = win_s.shape[0] // 2

    @pl.when(i < stage_steps)
    def _():
        blk = win_ref[0]
        win_s[2 * i] = blk[:, :cols].astype(BF16)
        win_s[2 * i + 1] = blk[:, cols:].astype(BF16)
        wout_s[pl.ds(pl.multiple_of(i * cols, cols), cols), :] = wout_ref[0].astype(BF16)

    if has_ple:
        rows = wg_ref.shape[1]

        @pl.when(i < gate_steps)
        def _():
            wg_s[pl.ds(pl.multiple_of(i * rows, rows), rows), :] = wg_ref[0].astype(BF16)

        @pl.when(i == 0)
        def _():
            wp_s[...] = wp_ref[0].astype(BF16)

    @pl.when(i >= stage_steps)
    def _():
        is_prompt = i - stage_steps < prompt_tiles
        x = _pick(is_prompt, x_refs)
        xn = _rms(x, g_ref[...]).astype(BF16)
        for j in range(n_half):
            a = _dot(xn, win_s[j])
            b = _dot(xn, win_s[n_half + j])
            act_ref[:, j * cols:(j + 1) * cols] = (_silu(a) * b).astype(BF16)
        x = x + 0.5 * _dot(act_ref[...], wout_s[...])
        if has_ple:
            gate = jax.nn.sigmoid(_dot(_rms(x, gp_ref[...]).astype(BF16), wg_s[...]))
            x = x + gate * _dot(_pick(is_prompt, p_refs, 0).astype(BF16), wp_s[...])
        if has_final:
            x = _rms(x, gf_ref[...])
        if n_o == 1:
            o_refs[0][...] = x
        else:
            @pl.when(is_prompt)
            def _():
                o_refs[0][...] = x

            @pl.when(jnp.logical_not(is_prompt))
            def _():
                o_refs[1][...] = x


def _ffn_call(xs, n_rows, gain, w_in, w_out, layer, ple=None, final_gain=None, split_out=False, name="ffn"):
    d = xs[0].shape[1]
    d_ff = w_out.shape[1]
    tm = ROW_TILE
    cols = V7X_MXU_COLS
    assert n_rows[0] % tm == 0 and n_rows[1] % tm == 0 and d_ff % cols == 0
    pt, st = n_rows[0] // tm, n_rows[1] // tm
    stage = d_ff // cols
    stacked, prompt, sample = _row_maps(pt, stage)
    pair = [prompt, sample]
    last = stage - 1
    args = [*xs, gain.reshape(1, d), w_in, w_out]
    specs = [pl.BlockSpec((tm, d), m) for m in ([stacked] if len(xs) == 1 else pair)]
    specs += [_resident((1, d)),
              pl.BlockSpec((1, d, 2 * cols), lambda i: (layer, 0, jnp.minimum(i, last))),
              pl.BlockSpec((1, cols, d), lambda i: (layer, jnp.minimum(i, last), 0))]
    scratch = [pltpu.VMEM((2 * stage, d, cols), BF16), pltpu.VMEM((d_ff, d), BF16), pltpu.VMEM((tm, d_ff), BF16)]
    gate_steps = 0
    if ple is not None:
        ps, gp, wg, wp = ple
        gate_rows = LANES
        gate_steps = d // gate_rows
        assert gate_steps <= stage
        args += [*ps, gp.reshape(1, d), wg, wp]
        specs += [pl.BlockSpec((1, tm, ps[0].shape[2]), lambda i, m=m: (layer, *m(i))) for m in pair]
        specs += [_resident((1, d)),
                  pl.BlockSpec((1, gate_rows, d), lambda i: (layer, jnp.minimum(i, gate_steps - 1), 0)),
                  _resident_layer(wp.shape, layer)]
        scratch += [pltpu.VMEM((d, d), BF16), pltpu.VMEM(wp.shape[1:], BF16)]
    if final_gain is not None:
        args.append(final_gain.reshape(1, d))
        specs.append(_resident((1, d)))
    if split_out:
        out_shape = [jax.ShapeDtypeStruct((r, d), F32) for r in n_rows]
        out_specs = [pl.BlockSpec((tm, d), m) for m in pair]
    else:
        out_shape = jax.ShapeDtypeStruct((sum(n_rows), d), F32)
        out_specs = pl.BlockSpec((tm, d), stacked)
    return pl.pallas_call(
        functools.partial(_ffn_kernel, n_x=len(xs), has_ple=ple is not None, has_final=final_gain is not None,
                          n_o=2 if split_out else 1, prompt_tiles=pt, stage_steps=stage, gate_steps=gate_steps),
        out_shape=out_shape,
        grid=(stage + pt + st,),
        in_specs=specs,
        out_specs=out_specs,
        scratch_shapes=scratch,
        compiler_params=_params(1, V7X_VMEM_BUDGET_BYTES),
        name=name,
    )(*args)


def _inproj_kernel(*refs, n_out, chunk_cols):
    x_ref, g_ref = refs[0], refs[1]
    w_refs = refs[2:2 + n_out]
    o_refs = refs[2 + n_out:2 + 2 * n_out]
    xn = _rms(x_ref[...], g_ref[...]).astype(BF16)
    for w_ref, o_ref, cw in zip(w_refs, o_refs, chunk_cols):
        for c in range(w_ref.shape[1] // cw):
            o_ref[:, c * cw:(c + 1) * cw] = _dot(xn, w_ref[:, c * cw:(c + 1) * cw]).astype(o_ref.dtype)


def _inproj_call(x, gain, weights, out_dtypes, chunk_cols, name):
    n, d = x.shape
    tm = ROW_TILE
    row = lambda i: (i, 0)
    specs = [pl.BlockSpec((tm, d), row), _resident((1, d))] + [_resident(w.shape) for w in weights]
    out_shape = [jax.ShapeDtypeStruct((n, w.shape[1]), dt) for w, dt in zip(weights, out_dtypes)]
    out_specs = [pl.BlockSpec((tm, w.shape[1]), row) for w in weights]
    vmem = (sum(2 * w.size for w in weights) + 4 * tm * d * 4
            + sum(2 * tm * w.shape[1] * jnp.dtype(dt).itemsize for w, dt in zip(weights, out_dtypes))
            + 2 * tm * max(chunk_cols) * 4 + (4 << 20))
    return pl.pallas_call(
        functools.partial(_inproj_kernel, n_out=len(weights), chunk_cols=tuple(chunk_cols)),
        out_shape=out_shape,
        grid=(n // tm,),
        in_specs=specs,
        out_specs=out_specs,
        compiler_params=_params(1, vmem),
        name=name,
    )(x, gain.reshape(1, d), *weights)


def _inproj_staged_kernel(x_ref, g_ref, w_ref, o_ref, w_s, *, stage_steps):
    i = pl.program_id(0)
    cols = w_s.shape[2]

    @pl.when(i < stage_steps)
    def _():
        w_s[i] = w_ref[0].astype(BF16)

    @pl.when(i >= stage_steps)
    def _():
        xn = _rms(x_ref[...], g_ref[...]).astype(BF16)
        for c in range(stage_steps):
            o_ref[:, c * cols:(c + 1) * cols] = _dot(xn, w_s[c]).astype(o_ref.dtype)


def _inproj_staged_call(x, gain, w, layer, cols, name):
    n, d = x.shape
    n_out = w.shape[2]
    tm = ROW_TILE
    assert n_out % cols == 0
    stage = n_out // cols
    row = lambda i: (jnp.maximum(i - stage, 0), 0)
    vmem = 2 * d * n_out + 2 * d * cols * 4 + 4 * tm * d * 4 + 2 * tm * n_out * 2 + 2 * tm * cols * 4 + (4 << 20)
    return pl.pallas_call(
        functools.partial(_inproj_staged_kernel, stage_steps=stage),
        out_shape=jax.ShapeDtypeStruct((n, n_out), BF16),
        grid=(stage + n // tm,),
        in_specs=[pl.BlockSpec((tm, d), row), _resident((1, d)),
                  pl.BlockSpec((1, d, cols), lambda i: (layer, 0, jnp.minimum(i, stage - 1)))],
        out_specs=pl.BlockSpec((tm, n_out), row),
        scratch_shapes=[pltpu.VMEM((stage, d, cols), BF16)],
        compiler_params=_params(1, vmem),
        name=name,
    )(x, gain.reshape(1, d), w)


GLA_MATMUL_LEVELS = (1, 2)


def _gla_constants(n_seq, ls, width):
    c = n_seq * ls
    nlev = int(round(math.log2(ls)))
    assert 2 ** nlev == ls and nlev > max(GLA_MATMUL_LEVELS)
    t = np.arange(c)[:, None]
    i = np.arange(c)[None, :]
    w_rows = [((t // ls) == (i // ls)) & (i <= t)]
    masks, uppers = [], []
    for l in range(nlev):
        m = 2 ** l
        upper = ((t // m) % 2) == 1
        r = (t // (2 * m)) * (2 * m) + m - 1
        if l in GLA_MATMUL_LEVELS:
            w_rows.append(np.where(upper, (i > r) & (i <= t), (i > t) & (i <= r)))
        masks.append(((t // (2 * m)) == (i // (2 * m))) & upper & (((i // m) % 2) == 0))
        uppers.append(np.broadcast_to(upper, (c, width)))
    w = jnp.asarray(np.concatenate(w_rows, 0).astype(np.float32), BF16)
    return w, jnp.asarray(np.stack(masks).astype(np.float32)), jnp.asarray(np.stack(uppers).astype(np.float32))


def _gla_block(q, k, v, g, st_list, w_ref, m_ref, u_ref, ls):
    c, width = q.shape
    nh = width // LANES
    n_seq = c // ls
    nlev = m_ref.shape[0]
    g_hi = g.astype(BF16)
    g_lo = (g - g_hi.astype(F32)).astype(BF16)
    ex2 = _dot(w_ref[...], jnp.concatenate([g_hi, g_lo], axis=1))
    ex = ex2[:, :width] + ex2[:, width:]
    b = ex[:c]

    def row(r, n):
        return jnp.broadcast_to(b[r:r + 1, :], (n, width))

    b_end = jnp.concatenate([row((s + 1) * ls - 1, ls) for s in range(n_seq)], axis=0) if n_seq > 1 else row(c - 1, c)
    e_cum = jnp.exp2(b)
    qe = q * e_cum
    kr = k * jnp.exp2(b_end - b)

    att = [None] * nh
    for l in range(nlev):
        m = 2 ** l
        if l == 0:
            x = jnp.where(u_ref[l] > 0.5, q * jnp.exp2(g), k)
        elif l in GLA_MATMUL_LEVELS:
            j = 1 + GLA_MATMUL_LEVELS.index(l)
            x = jnp.where(u_ref[l] > 0.5, q, k) * jnp.exp2(ex[j * c:(j + 1) * c])
        else:
            pieces = []
            for p in range(c // (2 * m)):
                lo, mid, hi = p * 2 * m, p * 2 * m + m, (p + 1) * 2 * m
                pivot = row(mid - 1, m)
                pieces.append(k[lo:mid] * jnp.exp2(pivot - b[lo:mid]))
                pieces.append(q[mid:hi] * jnp.exp2(b[mid:hi] - pivot))
            x = jnp.concatenate(pieces, axis=0)
        x = x.astype(BF16)
        mask = m_ref[l] > 0.5
        for h in range(nh):
            xh = x[:, h * LANES:(h + 1) * LANES]
            p = _dot_nt(xh, xh)
            att[h] = jnp.where(mask, p, 0.0 if att[h] is None else att[h])

    qk = q * k
    outs, st_new = [], [[None] * nh for _ in range(n_seq)]
    for h in range(nh):
        sl = slice(h * LANES, (h + 1) * LANES)
        vh = v[:, sl]
        o = _dot(att[h].astype(BF16), vh.astype(BF16)) + jnp.sum(qk[:, sl], axis=1, keepdims=True) * vh
        parts = []
        for s in range(n_seq):
            rows = slice(s * ls, (s + 1) * ls)
            st = st_list[s][h]
            parts.append(o[rows] + _dot_nt(qe[rows, sl].astype(BF16), st.astype(BF16)))
            e_tot = e_cum[(s + 1) * ls - 1:(s + 1) * ls, sl]
            st_new[s][h] = e_tot * st + _dot_tn(vh[rows].astype(BF16), kr[rows, sl].astype(BF16))
        outs.append(parts[0] if n_seq == 1 else jnp.concatenate(parts, axis=0))
    return outs, st_new


def _hgrn_inputs(refs, rows, q_scale):
    del q_scale
    qa_ref, ia_ref, ga_ref, fa_ref, lg_ref = refs
    lg = lg_ref[...]
    ex = jnp.exp(lg - jnp.max(lg, axis=0, keepdims=True))
    lb = ex[0:1] / jnp.sum(ex, axis=0, keepdims=True)
    sig = jax.nn.sigmoid(fa_ref[rows, :])
    q = _silu(qa_ref[rows, :].astype(F32))
    k = (1.0 - lb) * (1.0 - sig)
    g = jnp.log(lb + (1.0 - lb) * sig) * LOG2_E
    return q, k, ia_ref[rows, :].astype(F32), g, ga_ref[rows, :].astype(F32)


def _gla_inputs(refs, rows, q_scale):
    qb_ref, kb_ref, vb_ref, gb_ref, lr_ref, wgk_ref, bgk_ref = refs
    z = _dot(lr_ref[rows, :].astype(BF16), wgk_ref[...]) + bgk_ref[...]
    g = (jnp.minimum(z, 0.0) - jnp.log1p(jnp.exp(-jnp.abs(z)))) * (LOG2_E / GLA_GATE_NORM)
    q = qb_ref[rows, :].astype(F32) * q_scale
    return q, kb_ref[rows, :].astype(F32), vb_ref[rows, :].astype(F32), g, gb_ref[rows, :].astype(F32)


def _head_out(outs, gate, gn):
    ys = [_rms(o, gn) * _silu(gate[:, h * LANES:(h + 1) * LANES]) for h, o in enumerate(outs)]
    return jnp.concatenate(ys, axis=1).astype(BF16)


def _stage_out_weight(wo_ref, wo_s, first):
    @pl.when(first)
    def _():
        wo_s[...] = wo_ref[0].astype(BF16)


def _ab_prompt_kernel(*refs, kind, n_in, chunk, q_scale):
    in_refs = refs[:n_in]
    gn_ref, w_ref, m_ref, u_ref, x_ref, wo_ref, xo_ref, s_ref, st_ref, y_s, wo_s = refs[n_in:]
    t = pl.program_id(1)
    nh = st_ref.shape[0]
    _stage_out_weight(wo_ref, wo_s, jnp.logical_and(pl.program_id(0) == 0, t == 0))

    @pl.when(t == 0)
    def _():
        st_ref[...] = jnp.zeros_like(st_ref)

    load = _hgrn_inputs if kind == "hgrn" else _gla_inputs
    st = [st_ref[h] for h in range(nh)]
    for c in range(y_s.shape[0] // chunk):
        rows = slice(c * chunk, (c + 1) * chunk)
        q, k, v, g, gate = load(in_refs, rows, q_scale)
        outs, (st,) = _gla_block(q, k, v, g, [st], w_ref, m_ref, u_ref, chunk)
        y_s[rows, :] = _head_out(outs, gate, gn_ref[...])
    for h in range(nh):
        st_ref[h] = st[h]
    xo_ref[...] = x_ref[...] + _dot(y_s[...], wo_s[...])

    @pl.when(t == pl.num_programs(1) - 1)
    def _():
        for h in range(nh):
            s_ref[0, h] = st_ref[h].T[:s_ref.shape[2], :]


def _ab_sample_kernel(*refs, kind, n_in, ls, q_scale):
    in_refs = refs[:n_in]
    gn_ref, w_ref, m_ref, u_ref, s0_ref, x_ref, wo_ref, xo_ref, s_ref, wo_s = refs[n_in:]
    _stage_out_weight(wo_ref, wo_s, pl.program_id(0) == 0)
    load = _hgrn_inputs if kind == "hgrn" else _gla_inputs
    n_seq, nh, dk = s0_ref.shape[:3]
    q, k, v, g, gate = load(in_refs, slice(None), q_scale)
    st0 = []
    for s in range(n_seq):
        st0.append([])
        for h in range(nh):
            s0 = s0_ref[s, h]
            if dk < LANES:
                s0 = jnp.concatenate([s0, jnp.zeros((LANES - dk, s0.shape[1]), F32)], axis=0)
            st0[s].append(s0.T)
    outs, st = _gla_block(q, k, v, g, st0, w_ref, m_ref, u_ref, ls)
    xo_ref[...] = x_ref[...] + _dot(_head_out(outs, gate, gn_ref[...]), wo_s[...])
    for s in range(n_seq):
        for h in range(nh):
            s_ref[s, h] = st[s][h].T[:dk, :]


def _ab_specs(kind, p16, p32, extra, tok_block, tok_index, nh):
    width = nh * LANES

    def col(group, w=width):
        return pl.BlockSpec((tok_block, w), lambda *ids: (tok_index(*ids), group))

    if kind == "hgrn":
        (logits,) = extra
        args = [p16, p16, p16, p32, logits]
        specs = [col(0), col(1), col(2), col(0), _resident(logits.shape)]
    else:
        w_gk, b_gk = extra
        args = [p16, p16, p16, p16, p32, w_gk, b_gk]
        specs = [col(3), col(4), col(5), col(6), col(H_A, LANES), _resident(w_gk.shape),
                 _resident(b_gk.shape)]
    return args, specs


def _out_weight_spec(w_out, layer, part, rows):
    return pl.BlockSpec((1, rows, w_out.shape[2]), lambda *_: (layer, part, 0), pipeline_mode=pl.Buffered(1))


def _ab_prompt_call(kind, p16, p32, extra, gn, x, w_out, layer, part, batch, seq, dk, name):
    nh = H_A if kind == "hgrn" else H_B
    d = x.shape[1]
    tt = GLA_STEP_TOKENS
    nt = seq // tt
    tok = lambda b, t: b * nt + t
    consts = _gla_constants(1, GLA_CHUNK, nh * LANES)
    args, specs = _ab_specs(kind, p16, p32, extra, tt, tok, nh)
    n_in = len(args)
    args += [gn.reshape(1, LANES), *consts, x, w_out]
    specs += [_resident((1, LANES))] + [_resident(c.shape) for c in consts]
    specs += [pl.BlockSpec((tt, d), lambda b, t: (tok(b, t), 0)), _out_weight_spec(w_out, layer, part, nh * LANES)]
    return pl.pallas_call(
        functools.partial(_ab_prompt_kernel, kind=kind, n_in=n_in, chunk=GLA_CHUNK, q_scale=dk ** -0.5),
        out_shape=[jax.ShapeDtypeStruct((batch * seq, d), F32),
                   jax.ShapeDtypeStruct((batch, nh, dk, LANES), F32)],
        grid=(batch, nt),
        in_specs=specs,
        out_specs=[pl.BlockSpec((tt, d), lambda b, t: (tok(b, t), 0)),
                   pl.BlockSpec((1, nh, dk, LANES), lambda b, t: (b, 0, 0, 0))],
        scratch_shapes=[pltpu.VMEM((nh, LANES, LANES), F32), pltpu.VMEM((tt, nh * LANES), BF16),
                        pltpu.VMEM((nh * LANES, d), BF16)],
        compiler_params=_params(2, 40 << 20),
        name=name,
    )(*args)


def _ab_sample_call(kind, p16, p32, extra, gn, s0, row0, seq, x, x_row0, w_out, layer, part, name):
    batch, nh, dk, _ = s0.shape
    d = x.shape[1]
    nb = SAMPLE_SEQS
    rows = nb * seq
    blk0, xblk0 = row0 // rows, x_row0 // rows
    consts = _gla_constants(nb, seq, nh * LANES)
    args, specs = _ab_specs(kind, p16, p32, extra, rows, lambda i: blk0 + i, nh)
    n_in = len(args)
    args += [gn.reshape(1, LANES), *consts, s0, x, w_out]
    state_spec = pl.BlockSpec((nb, nh, dk, LANES), lambda i: (i, 0, 0, 0))
    specs += [_resident((1, LANES))] + [_resident(c.shape) for c in consts] + [state_spec]
    specs += [pl.BlockSpec((rows, d), lambda i: (xblk0 + i, 0)), _out_weight_spec(w_out, layer, part, nh * LANES)]
    return pl.pallas_call(
        functools.partial(_ab_sample_kernel, kind=kind, n_in=n_in, ls=seq, q_scale=dk ** -0.5),
        out_shape=[jax.ShapeDtypeStruct((batch * seq, d), F32),
                   jax.ShapeDtypeStruct((batch, nh, dk, LANES), F32)],
        grid=(batch // nb,),
        in_specs=specs,
        out_specs=[pl.BlockSpec((rows, d), lambda i: (i, 0)), state_spec],
        scratch_shapes=[pltpu.VMEM((nh * LANES, d), BF16)],
        compiler_params=_params(1, 40 << 20),
        name=name,
    )(*args)


def _ret_constants(n_seq, ls):
    c = n_seq * ls
    log_gamma = np.log1p(-np.exp2(-5.0 - np.arange(H_C, dtype=np.float64)))[:, None, None]
    t = np.arange(c)[:, None]
    s = np.arange(c)[None, :]
    causal = ((t // ls) == (s // ls)) & (s <= t)
    dmask = np.where(causal, np.exp(log_gamma * np.where(causal, t - s, 0)), 0.0)
    pos = (np.arange(c) % ls)[None, :, None]
    scales = np.stack([np.exp(log_gamma * (pos + 1)), np.exp(log_gamma * (ls - 1 - pos))], axis=1)
    scales = np.broadcast_to(scales, (H_C, 2, c, LANES))
    return jnp.asarray(dmask, F32), jnp.asarray(scales, F32)


def _rope_tables(positions, half):
    inv = ROPE_THETA ** (-jnp.arange(half, dtype=F32) / half)
    ang = positions.astype(F32)[:, None] * inv[None, :]
    return jnp.cos(ang), jnp.sin(ang)


def _rope(x, cos, sin):
    half = x.shape[1] // 2
    x1, x2 = x[:, :half], x[:, half:]
    return jnp.concatenate([x1 * cos - x2 * sin, x2 * cos + x1 * sin], axis=1)


def _ret_block(q16, k16, v, cos, sin, dmask, q_scale, k_scale, s_list, ls):
    dk = q16.shape[1]
    q = _rope(q16.astype(F32), cos, sin)
    k = _rope(k16.astype(F32), cos, sin) * (dk ** -0.5)
    reps = dk // LANES
    q_sc = jnp.concatenate([q_scale] * reps, axis=1)
    k_sc = jnp.concatenate([k_scale] * reps, axis=1)
    att = _dot_nt(q.astype(BF16), k.astype(BF16)) * dmask
    o = _dot(att.astype(BF16), v)
    qs = q * q_sc
    ks = k * k_sc
    decay = q_scale[ls - 1:ls, 0:1]
    n_seq = q.shape[0] // ls
    v32 = v if n_seq == 1 else v.astype(F32)
    o_parts, s_new = [], []
    for b in range(n_seq):
        rows = slice(b * ls, (b + 1) * ls)
        s = s_list[b]
        o_parts.append(o[rows] + _dot(qs[rows].astype(BF16), s.astype(BF16)))
        s_new.append(decay * s + _dot_tn(ks[rows].astype(BF16), v32[rows].astype(BF16)))
    o = o_parts[0] if n_seq == 1 else jnp.concatenate(o_parts, axis=0)
    return o, s_new


def _ret_head_out(o, gate, gn):
    oc = o - jnp.mean(o, axis=-1, keepdims=True)
    y = oc * lax.rsqrt(jnp.mean(oc * oc, axis=-1, keepdims=True) + EPS) * gn
    return (y * _silu(gate)).astype(BF16)


def _ret_prompt_kernel(q_ref, k_ref, v_ref, g_ref, cos_ref, sin_ref, dm_ref, sc_ref, gn_ref, x_ref, wo_ref,
                       xo_ref, s_ref, st_ref, y_s, wo_s):
    t = pl.program_id(1)
    nh, dk, dv = st_ref.shape
    c = dm_ref.shape[1]
    _stage_out_weight(wo_ref, wo_s, jnp.logical_and(pl.program_id(0) == 0, t == 0))

    @pl.when(t == 0)
    def _():
        st_ref[...] = jnp.zeros_like(st_ref)

    for h in range(nh):
        s = st_ref[h]
        for j in range(q_ref.shape[0] // c):
            rows = slice(j * c, (j + 1) * c)
            o, (s,) = _ret_block(q_ref[rows, h * dk:(h + 1) * dk], k_ref[rows, h * dk:(h + 1) * dk],
                                 v_ref[rows, h * dv:(h + 1) * dv], cos_ref[rows, :], sin_ref[rows, :], dm_ref[h],
                                 sc_ref[h, 0], sc_ref[h, 1], [s], c)
            y_s[rows, h * dv:(h + 1) * dv] = _ret_head_out(o, g_ref[rows, h * dv:(h + 1) * dv].astype(F32),
                                                           gn_ref[...])
        st_ref[h] = s
    xo_ref[...] = x_ref[...] + _dot(y_s[...], wo_s[...])

    @pl.when(t == pl.num_programs(1) - 1)
    def _():
        s_ref[0] = st_ref[...]


def _ret_sample_kernel(q_ref, k_ref, v_ref, g_ref, cos_ref, sin_ref, dm_ref, sc_ref, gn_ref, s0_ref, x_ref,
                       wo_ref, xo_ref, s_ref, wo_s, *, ls):
    h = pl.program_id(1)
    dv = v_ref.shape[1]
    _stage_out_weight(wo_ref, wo_s, jnp.logical_and(pl.program_id(0) == 0, h == 0))
    n_seq = q_ref.shape[0] // ls
    o, s = _ret_block(q_ref[...], k_ref[...], v_ref[...], cos_ref[...], sin_ref[...], dm_ref[0],
                      sc_ref[0, 0], sc_ref[0, 1], [s0_ref[b, 0] for b in range(n_seq)], ls)
    y = _ret_head_out(o, g_ref[...].astype(F32), gn_ref[...])
    proj = _dot(y, wo_s[pl.ds(pl.multiple_of(h * dv, dv), dv), :])

    @pl.when(h == 0)
    def _():
        xo_ref[...] = x_ref[...] + proj

    @pl.when(h > 0)
    def _():
        xo_ref[...] += proj

    for b in range(n_seq):
        s_ref[b, 0] = s[b]


def _ret_prompt_call(r16, gn, x, w_out, layer, batch, seq, dk, dv, name):
    c = RET_STEP_TOKENS
    nt = seq // c
    d = x.shape[1]
    dmask, scales = _ret_constants(1, RET_CHUNK)
    cos, sin = _rope_tables(jnp.arange(seq, dtype=jnp.int32), dk // 2)
    tok = lambda b, t: b * nt + t
    nq, nv = H_C * dk, H_C * dv
    args = [r16, r16, r16, r16, cos, sin, dmask, scales, gn.reshape(1, dv), x, w_out]
    specs = [pl.BlockSpec((c, nq), lambda b, t: (tok(b, t), 0)),
             pl.BlockSpec((c, nq), lambda b, t: (tok(b, t), 1)),
             pl.BlockSpec((c, nv), lambda b, t: (tok(b, t), 2 * nq // nv)),
             pl.BlockSpec((c, nv), lambda b, t: (tok(b, t), 2 * nq // nv + 1)),
             pl.BlockSpec((c, dk // 2), lambda b, t: (t, 0)), pl.BlockSpec((c, dk // 2), lambda b, t: (t, 0)),
             _resident(dmask.shape), _resident(scales.shape), _resident((1, dv)),
             pl.BlockSpec((c, d), lambda b, t: (tok(b, t), 0)), _out_weight_spec(w_out, layer, 0, nv)]
    return pl.pallas_call(
        _ret_prompt_kernel,
        out_shape=[jax.ShapeDtypeStruct((batch * seq, d), F32),
                   jax.ShapeDtypeStruct((batch, H_C, dk, dv), F32)],
        grid=(batch, nt),
        in_specs=specs,
        out_specs=[pl.BlockSpec((c, d), lambda b, t: (tok(b, t), 0)),
                   pl.BlockSpec((1, H_C, dk, dv), lambda b, t: (b, 0, 0, 0))],
        scratch_shapes=[pltpu.VMEM((H_C, dk, dv), F32), pltpu.VMEM((c, nv), BF16), pltpu.VMEM((nv, d), BF16)],
        compiler_params=_params(2, 48 << 20),
        name=name,
    )(*args)


def _ret_sample_call(r16, gn, s0, row0, batch, seq, dk, dv, x, w_out, layer, name):
    nb = SAMPLE_SEQS
    rows = nb * seq
    blk0 = row0 // rows
    d = x.shape[1]
    dmask, scales = _ret_constants(nb, seq)
    pos = PAST_LEN + (jnp.arange(rows, dtype=jnp.int32) % seq)
    cos, sin = _rope_tables(pos, dk // 2)
    nq = H_C * dk

    def col(width, base):
        return pl.BlockSpec((rows, width), lambda i, h: (blk0 + i, base + h))

    state_spec = pl.BlockSpec((nb, 1, dk, dv), lambda i, h: (i, h, 0, 0))
    args = [r16, r16, r16, r16, cos, sin, dmask, scales, gn.reshape(1, dv), s0, x, w_out]
    specs = [col(dk, 0), col(dk, H_C), col(dv, 2 * nq // dv), col(dv, 2 * nq // dv + H_C),
             _resident(cos.shape), _resident(sin.shape),
             pl.BlockSpec((1, rows, rows), lambda i, h: (h, 0, 0)),
             pl.BlockSpec((1, 2, rows, LANES), lambda i, h: (h, 0, 0, 0)), _resident((1, dv)), state_spec,
             pl.BlockSpec((rows, d), lambda i, h: (blk0 + i, 0)), _out_weight_spec(w_out, layer, 0, H_C * dv)]
    return pl.pallas_call(
        functools.partial(_ret_sample_kernel, ls=seq),
        out_shape=[jax.ShapeDtypeStruct((batch * seq, d), F32),
                   jax.ShapeDtypeStruct((batch, H_C, dk, dv), F32)],
        grid=(batch // nb, H_C),
        in_specs=specs,
        out_specs=[pl.BlockSpec((rows, d), lambda i, h: (i, 0)), state_spec],
        scratch_shapes=[pltpu.VMEM((H_C * dv, d), BF16)],
        compiler_params=_params(2, 56 << 20),
        name=name,
    )(*args)


def _pad_heads(w, n_heads, width):
    d = w.shape[0]
    w = w.reshape(d, n_heads, -1)
    return jnp.pad(w, ((0, 0), (0, 0), (0, width - w.shape[2]))).reshape(d, n_heads * width)


def _prepare_ab_weights(w_in, w_gk2, b_gk, dk_a, dv_a, dk_b, dv_b):
    sizes = (H_A * dk_a, H_A * dk_a, H_A * dv_a, H_A * dv_a, H_B * dk_b, H_B * dk_b, H_B * dv_b,
             GLA_RANK, H_B * dv_b)
    qa, fa, ia, ga, qb, kb, vb, lrb, gb = jnp.split(w_in, [int(v) for v in np.cumsum(sizes)[:-1]], axis=1)
    w16 = jnp.concatenate([qa, ia, ga, _pad_heads(qb, H_B, LANES), _pad_heads(kb, H_B, LANES), vb, gb],
                          axis=1).astype(BF16)
    w32 = jnp.concatenate([fa, jnp.pad(lrb, ((0, 0), (0, LANES - GLA_RANK)))], axis=1).astype(BF16)
    w_gk = jnp.pad(_pad_heads(w_gk2, H_B, LANES), ((0, LANES - GLA_RANK), (0, 0))).astype(BF16)
    b_gk = _pad_heads(b_gk.reshape(1, -1), H_B, LANES)
    return w16, w32, w_gk, b_gk


def kernel(x_prompt, x_sample, state_hgrn, state_gla, state_ret, p_prompt, p_sample, norm_ffn1, ffn1_w_in, ffn1_w_out, norm_mix, ab_w_in, ab_w_gk2, ab_b_gk, hgrn_lb_logits, ab_gn_hgrn, ab_gn_gla, ab_w_out, ret_w_in, ret_gn, ret_w_out, norm_ffn2, ffn2_w_in, ffn2_w_out, norm_ple, ple_w_gate, ple_w_proj, norm_final):
    bp, lp, d = x_prompt.shape
    bs, ls, _ = x_sample.shape
    depth = norm_ffn1.shape[0]
    n_prompt = bp * lp
    dk_a, dv_a = state_hgrn.shape[-2:]
    dk_b, dv_b = state_gla.shape[-2:]
    dk_c, dv_c = state_ret.shape[-2:]
    assert dk_a == dv_a == dv_b == LANES and dk_b <= LANES and state_hgrn.shape[0] == 1

    n_rows = (n_prompt, bs * ls)
    xs = (x_prompt.reshape(n_prompt, d), x_sample.reshape(bs * ls, d))
    ps = (p_prompt.reshape(depth, n_prompt, -1), p_sample.reshape(depth, bs * ls, -1))

    new_hgrn_p, new_gla_p, new_ret_p, new_hgrn_s, new_gla_s, new_ret_s = [], [], [], [], [], []
    for i in range(depth):
        j = i // 2
        x = _ffn_call(xs if i == 0 else (x,), n_rows, norm_ffn1[i], ffn1_w_in, ffn1_w_out, i, name=f"ffn1_{i}")
        if i % 2 == 0:
            w16, w32, w_gk, b_gk = _prepare_ab_weights(ab_w_in[j], ab_w_gk2[j], ab_b_gk[j], dk_a, dv_a, dk_b, dv_b)
            p16, p32 = _inproj_call(x, norm_mix[i], [w16, w32], [BF16, F32], [w16.shape[1] // 2, w32.shape[1]],
                                    name=f"ab_in_{i}")
            hgrn = ("hgrn", p16, p32, (hgrn_lb_logits,), ab_gn_hgrn[j])
            gla = ("gla", p16, p32, (w_gk, b_gk), ab_gn_gla[j])
            xa_p, sa_p = _ab_prompt_call(*hgrn, x, ab_w_out, j, 0, bp, lp, dk_a, f"hgrn_p_{i}")
            xa_s, sa_s = _ab_sample_call(*hgrn, state_hgrn[j], n_prompt, ls, x, n_prompt, ab_w_out, j, 0,
                                         f"hgrn_s_{i}")
            x_p, sb_p = _ab_prompt_call(*gla, xa_p, ab_w_out, j, 1, bp, lp, dk_b, f"gla_p_{i}")
            x_s, sb_s = _ab_sample_call(*gla, state_gla[j], n_prompt, ls, xa_s, 0, ab_w_out, j, 1, f"gla_s_{i}")
            new_hgrn_p.append(sa_p)
            new_hgrn_s.append(sa_s)
            new_gla_p.append(sb_p)
            new_gla_s.append(sb_s)
        else:
            r16 = _inproj_staged_call(x, norm_mix[i], ret_w_in, j, 2 * V7X_MXU_COLS, name=f"ret_in_{i}")
            x_p, sc_p = _ret_prompt_call(r16, ret_gn[j], x, ret_w_out, j, bp, lp, dk_c, dv_c, f"ret_p_{i}")
            x_s, sc_s = _ret_sample_call(r16, ret_gn[j], state_ret[j], n_prompt, bs, ls, dk_c, dv_c, x, ret_w_out, j,
                                         f"ret_s_{i}")
            new_ret_p.append(sc_p)
            new_ret_s.append(sc_s)
        last = i == depth - 1
        x = _ffn_call((x_p, x_s), n_rows, norm_ffn2[i], ffn2_w_in, ffn2_w_out, i,
                      ple=(ps, norm_ple[i], ple_w_gate, ple_w_proj),
                      final_gain=norm_final if last else None, split_out=last, name=f"ffn2_{i}")

    y_prompt = x[0].reshape(bp, lp, d)
    y_sample = x[1].reshape(bs, ls, d)
    def layers(states):
        return states[0][None] if len(states) == 1 else jnp.stack(states)

    return (y_prompt, y_sample, layers(new_hgrn_p), layers(new_gla_p), layers(new_ret_p),
            layers(new_hgrn_s), layers(new_gla_s), layers(new_ret_s))
```

```python
import functools
import math

import numpy as np
import jax
import jax.numpy as jnp
from jax import lax
from jax.experimental import pallas as pl
from jax.experimental.pallas import tpu as pltpu

F32 = jnp.float32
BF16 = jnp.bfloat16

H_A = 4
H_B = 4
H_C = 4
GLA_RANK = 16
GLA_GATE_NORM = 16.0
ROPE_THETA = 10000.0
PAST_LEN = 16384
EPS = 1e-6
LOG2_E = math.log2(math.e)

LANES = 128
V7X_MXU_COLS = 256
V7X_VMEM_BUDGET_BYTES = 56 * 1024 * 1024

ROW_TILE = 512
GLA_CHUNK = 128
GLA_STEP_TOKENS = 1024
RET_CHUNK = 256
RET_STEP_TOKENS = 512
SAMPLE_SEQS = 16


def _params(n_axes, vmem_bytes):
    return pltpu.CompilerParams(dimension_semantics=("arbitrary",) * n_axes,
                                vmem_limit_bytes=min(int(vmem_bytes), V7X_VMEM_BUDGET_BYTES))


def _resident(shape):
    nd = len(shape)
    return pl.BlockSpec(shape, lambda *_: (0,) * nd, pipeline_mode=pl.Buffered(1))


def _resident_layer(shape, layer):
    nd = len(shape)
    return pl.BlockSpec((1,) + tuple(shape[1:]), lambda *_: (layer,) + (0,) * (nd - 1),
                        pipeline_mode=pl.Buffered(1))


def _row_maps(prompt_tiles, lead=0):
    stacked = lambda i: (jnp.maximum(i - lead, 0), 0)
    prompt = lambda i: (jnp.clip(i - lead, 0, prompt_tiles - 1), 0)
    sample = lambda i: (jnp.maximum(i - lead - prompt_tiles, 0), 0)
    return stacked, prompt, sample


def _rms(x, g):
    return x * lax.rsqrt(jnp.mean(x * x, axis=-1, keepdims=True) + EPS) * g


def _silu(x):
    return x * jax.nn.sigmoid(x)


def _dot(a, b):
    return jnp.dot(a, b, preferred_element_type=F32)


def _dot_nt(a, b):
    return lax.dot_general(a, b, (((1,), (1,)), ((), ())), preferred_element_type=F32)


def _dot_tn(a, b):
    return lax.dot_general(a, b, (((0,), (0,)), ((), ())), preferred_element_type=F32)


def _pick(is_prompt, refs, index=None):
    vals = [r[...] if index is None else r[index] for r in refs]
    return vals[0] if len(vals) == 1 else jnp.where(is_prompt, vals[0], vals[1])


def _ffn_kernel(*refs, n_x, has_ple, has_final, n_o, prompt_tiles, stage_steps, gate_steps):
    it = iter(refs)
    x_refs = [next(it) for _ in range(n_x)]
    g_ref, win_ref, wout_ref = next(it), next(it), next(it)
    if has_ple:
        p_refs = [next(it), next(it)]
        gp_ref, wg_ref, wp_ref = next(it), next(it), next(it)
    if has_final:
        gf_ref = next(it)
    o_refs = [next(it) for _ in range(n_o)]
    win_s, wout_s, act_ref = next(it), next(it), next(it)
    if has_ple:
        wg_s, wp_s = next(it), next(it)
    i = pl.program_id(0)
    cols = win_s.shape[2]
    n_half = win_s.shape[0] // 2

    @pl.when(i < stage_steps)
    def _():
        blk = win_ref[0]
        win_s[2 * i] = blk[:, :cols].astype(BF16)
        win_s[2 * i + 1] = blk[:, cols:].astype(BF16)
        wout_s[pl.ds(pl.multiple_of(i * cols, cols), cols), :] = wout_ref[0].astype(BF16)

    if has_ple:
        rows = wg_ref.shape[1]

        @pl.when(i < gate_steps)
        def _():
            wg_s[pl.ds(pl.multiple_of(i * rows, rows), rows), :] = wg_ref[0].astype(BF16)

        @pl.when(i == 0)
        def _():
            wp_s[...] = wp_ref[0].astype(BF16)

    @pl.when(i >= stage_steps)
    def _():
        is_prompt = i - stage_steps < prompt_tiles
        x = _pick(is_prompt, x_refs)
        xn = _rms(x, g_ref[...]).astype(BF16)
        for j in range(n_half):
            a = _dot(xn, win_s[j])
            b = _dot(xn, win_s[n_half + j])
            act_ref[:, j * cols:(j + 1) * cols] = (_silu(a) * b).astype(BF16)
        x = x + 0.5 * _dot(act_ref[...], wout_s[...])
        if has_ple:
            gate = jax.nn.sigmoid(_dot(_rms(x, gp_ref[...]).astype(BF16), wg_s[...]))
            x = x + gate * _dot(_pick(is_prompt, p_refs, 0).astype(BF16), wp_s[...])
        if has_final:
            x = _rms(x, gf_ref[...])
        if n_o == 1:
            o_refs[0][...] = x
        else:
            @pl.when(is_prompt)
            def _():
                o_refs[0][...] = x

            @pl.when(jnp.logical_not(is_prompt))
            def _():
                o_refs[1][...] = x


def _ffn_call(xs, n_rows, gain, w_in, w_out, layer, ple=None, final_gain=None, split_out=False, name="ffn"):
    d = xs[0].shape[1]
    d_ff = w_out.shape[1]
    tm = ROW_TILE
    cols = V7X_MXU_COLS
    assert n_rows[0] % tm == 0 and n_rows[1] % tm == 0 and d_ff % cols == 0
    pt, st = n_rows[0] // tm, n_rows[1] // tm
    stage = d_ff // cols
    stacked, prompt, sample = _row_maps(pt, stage)
    pair = [prompt, sample]
    last = stage - 1
    args = [*xs, gain.reshape(1, d), w_in, w_out]
    specs = [pl.BlockSpec((tm, d), m) for m in ([stacked] if len(xs) == 1 else pair)]
    specs += [_resident((1, d)),
              pl.BlockSpec((1, d, 2 * cols), lambda i: (layer, 0, jnp.minimum(i, last))),
              pl.BlockSpec((1, cols, d), lambda i: (layer, jnp.minimum(i, last), 0))]
    scratch = [pltpu.VMEM((2 * stage, d, cols), BF16), pltpu.VMEM((d_ff, d), BF16), pltpu.VMEM((tm, d_ff), BF16)]
    gate_steps = 0
    if ple is not None:
        ps, gp, wg, wp = ple
        gate_rows = LANES
        gate_steps = d // gate_rows
        assert gate_steps <= stage
        args += [*ps, gp.reshape(1, d), wg, wp]
        specs += [pl.BlockSpec((1, tm, ps[0].shape[2]), lambda i, m=m: (layer, *m(i))) for m in pair]
        specs += [_resident((1, d)),
                  pl.BlockSpec((1, gate_rows, d), lambda i: (layer, jnp.minimum(i, gate_steps - 1), 0)),
                  _resident_layer(wp.shape, layer)]
        scratch += [pltpu.VMEM((d, d), BF16), pltpu.VMEM(wp.shape[1:], BF16)]
    if final_gain is not None:
        args.append(final_gain.reshape(1, d))
        specs.append(_resident((1, d)))
    if split_out:
        out_shape = [jax.ShapeDtypeStruct((r, d), F32) for r in n_rows]
        out_specs = [pl.BlockSpec((tm, d), m) for m in pair]
    else:
        out_shape = jax.ShapeDtypeStruct((sum(n_rows), d), F32)
        out_specs = pl.BlockSpec((tm, d), stacked)
    return pl.pallas_call(
        functools.partial(_ffn_kernel, n_x=len(xs), has_ple=ple is not None, has_final=final_gain is not None,
                          n_o=2 if split_out else 1, prompt_tiles=pt, stage_steps=stage, gate_steps=gate_steps),
        out_shape=out_shape,
        grid=(stage + pt + st,),
        in_specs=specs,
        out_specs=out_specs,
        scratch_shapes=scratch,
        compiler_params=_params(1, V7X_VMEM_BUDGET_BYTES),
        name=name,
    )(*args)


def _inproj_kernel(*refs, n_out, chunk_cols):
    x_ref, g_ref = refs[0], refs[1]
    w_refs = refs[2:2 + n_out]
    o_refs = refs[2 + n_out:2 + 2 * n_out]
    xn = _rms(x_ref[...], g_ref[...]).astype(BF16)
    for w_ref, o_ref, cw in zip(w_refs, o_refs, chunk_cols):
        for c in range(w_ref.shape[1] // cw):
            o_ref[:, c * cw:(c + 1) * cw] = _dot(xn, w_ref[:, c * cw:(c + 1) * cw]).astype(o_ref.dtype)


def _inproj_call(x, gain, weights, out_dtypes, chunk_cols, name):
    n, d = x.shape
    tm = ROW_TILE
    row = lambda i: (i, 0)
    specs = [pl.BlockSpec((tm, d), row), _resident((1, d))] + [_resident(w.shape) for w in weights]
    out_shape = [jax.ShapeDtypeStruct((n, w.shape[1]), dt) for w, dt in zip(weights, out_dtypes)]
    out_specs = [pl.BlockSpec((tm, w.shape[1]), row) for w in weights]
    vmem = (sum(2 * w.size for w in weights) + 4 * tm * d * 4
            + sum(2 * tm * w.shape[1] * jnp.dtype(dt).itemsize for w, dt in zip(weights, out_dtypes))
            + 2 * tm * max(chunk_cols) * 4 + (4 << 20))
    return pl.pallas_call(
        functools.partial(_inproj_kernel, n_out=len(weights), chunk_cols=tuple(chunk_cols)),
        out_shape=out_shape,
        grid=(n // tm,),
        in_specs=specs,
        out_specs=out_specs,
        compiler_params=_params(1, vmem),
        name=name,
    )(x, gain.reshape(1, d), *weights)


def _inproj_staged_kernel(x_ref, g_ref, w_ref, o_ref, w_s, *, stage_steps):
    i = pl.program_id(0)
    cols = w_s.shape[2]

    @pl.when(i < stage_steps)
    def _():
        w_s[i] = w_ref[0].astype(BF16)

    @pl.when(i >= stage_steps)
    def _():
        xn = _rms(x_ref[...], g_ref[...]).astype(BF16)
        for c in range(stage_steps):
            o_ref[:, c * cols:(c + 1) * cols] = _dot(xn, w_s[c]).astype(o_ref.dtype)


def _inproj_staged_call(x, gain, w, layer, cols, name):
    n, d = x.shape
    n_out = w.shape[2]
    tm = ROW_TILE
    assert n_out % cols == 0
    stage = n_out // cols
    row = lambda i: (jnp.maximum(i - stage, 0), 0)
    vmem = 2 * d * n_out + 2 * d * cols * 4 + 4 * tm * d * 4 + 2 * tm * n_out * 2 + 2 * tm * cols * 4 + (4 << 20)
    return pl.pallas_call(
        functools.partial(_inproj_staged_kernel, stage_steps=stage),
        out_shape=jax.ShapeDtypeStruct((n, n_out), BF16),
        grid=(stage + n // tm,),
        in_specs=[pl.BlockSpec((tm, d), row), _resident((1, d)),
                  pl.BlockSpec((1, d, cols), lambda i: (layer, 0, jnp.minimum(i, stage - 1)))],
        out_specs=pl.BlockSpec((tm, n_out), row),
        scratch_shapes=[pltpu.VMEM((stage, d, cols), BF16)],
        compiler_params=_params(1, vmem),
        name=name,
    )(x, gain.reshape(1, d), w)


GLA_MATMUL_LEVELS = (1, 2)


def _gla_constants(n_seq, ls, width):
    c = n_seq * ls
    nlev = int(round(math.log2(ls)))
    assert 2 ** nlev == ls and nlev > max(GLA_MATMUL_LEVELS)
    t = np.arange(c)[:, None]
    i = np.arange(c)[None, :]
    w_rows = [((t // ls) == (i // ls)) & (i <= t)]
    masks, uppers = [], []
    for l in range(nlev):
        m = 2 ** l
        upper = ((t // m) % 2) == 1
        r = (t // (2 * m)) * (2 * m) + m - 1
        if l in GLA_MATMUL_LEVELS:
            w_rows.append(np.where(upper, (i > r) & (i <= t), (i > t) & (i <= r)))
        masks.append(((t // (2 * m)) == (i // (2 * m))) & upper & (((i // m) % 2) == 0))
        uppers.append(np.broadcast_to(upper, (c, width)))
    w = jnp.asarray(np.concatenate(w_rows, 0).astype(np.float32), BF16)
    return w, jnp.asarray(np.stack(masks).astype(np.float32)), jnp.asarray(np.stack(uppers).astype(np.float32))


def _gla_block(q, k, v, g, st_list, w_ref, m_ref, u_ref, ls):
    c, width = q.shape
    nh = width // LANES
    n_seq = c // ls
    nlev = m_ref.shape[0]
    g_hi = g.astype(BF16)
    g_lo = (g - g_hi.astype(F32)).astype(BF16)
    ex2 = _dot(w_ref[...], jnp.concatenate([g_hi, g_lo], axis=1))
    ex = ex2[:, :width] + ex2[:, width:]
    b = ex[:c]

    def row(r, n):
        return jnp.broadcast_to(b[r:r + 1, :], (n, width))

    b_end = jnp.concatenate([row((s + 1) * ls - 1, ls) for s in range(n_seq)], axis=0) if n_seq > 1 else row(c - 1, c)
    e_cum = jnp.exp2(b)
    qe = q * e_cum
    kr = k * jnp.exp2(b_end - b)

    att = [None] * nh
    for l in range(nlev):
        m = 2 ** l
        if l == 0:
            x = jnp.where(u_ref[l] > 0.5, q * jnp.exp2(g), k)
        elif l in GLA_MATMUL_LEVELS:
            j = 1 + GLA_MATMUL_LEVELS.index(l)
            x = jnp.where(u_ref[l] > 0.5, q, k) * jnp.exp2(ex[j * c:(j + 1) * c])
        else:
            pieces = []
            for p in range(c // (2 * m)):
                lo, mid, hi = p * 2 * m, p * 2 * m + m, (p + 1) * 2 * m
                pivot = row(mid - 1, m)
                pieces.append(k[lo:mid] * jnp.exp2(pivot - b[lo:mid]))
                pieces.append(q[mid:hi] * jnp.exp2(b[mid:hi] - pivot))
            x = jnp.concatenate(pieces, axis=0)
        mask = m_ref[l] > 0.5
        for h in range(nh):
            xh = x[:, h * LANES:(h + 1) * LANES]
            p = _dot(xh.astype(BF16), xh.T.astype(BF16))
            att[h] = jnp.where(mask, p, 0.0 if att[h] is None else att[h])

    qk = q * k
    outs, st_new = [], [[None] * nh for _ in range(n_seq)]
    for h in range(nh):
        sl = slice(h * LANES, (h + 1) * LANES)
        vh = v[:, sl]
        o = _dot(att[h].astype(BF16), vh.astype(BF16)) + jnp.sum(qk[:, sl], axis=1, keepdims=True) * vh
        parts = []
        for s in range(n_seq):
            rows = slice(s * ls, (s + 1) * ls)
            st = st_list[s][h]
            parts.append(o[rows] + _dot_nt(qe[rows, sl].astype(BF16), st.astype(BF16)))
            e_tot = e_cum[(s + 1) * ls - 1:(s + 1) * ls, sl]
            st_new[s][h] = e_tot * st + _dot_tn(vh[rows].astype(BF16), kr[rows, sl].astype(BF16))
        outs.append(parts[0] if n_seq == 1 else jnp.concatenate(parts, axis=0))
    return outs, st_new


def _hgrn_inputs(refs, rows, q_scale):
    del q_scale
    qa_ref, ia_ref, ga_ref, fa_ref, lg_ref = refs
    lg = lg_ref[...]
    ex = jnp.exp(lg - jnp.max(lg, axis=0, keepdims=True))
    lb = ex[0:1] / jnp.sum(ex, axis=0, keepdims=True)
    sig = jax.nn.sigmoid(fa_ref[rows, :])
    q = _silu(qa_ref[rows, :].astype(F32))
    k = (1.0 - lb) * (1.0 - sig)
    g = jnp.log(lb + (1.0 - lb) * sig) * LOG2_E
    return q, k, ia_ref[rows, :].astype(F32), g, ga_ref[rows, :].astype(F32)


def _gla_inputs(refs, rows, q_scale):
    qb_ref, kb_ref, vb_ref, gb_ref, lr_ref, wgk_ref, bgk_ref = refs
    z = _dot(lr_ref[rows, :].astype(BF16), wgk_ref[...]) + bgk_ref[...]
    g = (jnp.minimum(z, 0.0) - jnp.log1p(jnp.exp(-jnp.abs(z)))) * (LOG2_E / GLA_GATE_NORM)
    q = qb_ref[rows, :].astype(F32) * q_scale
    return q, kb_ref[rows, :].astype(F32), vb_ref[rows, :].astype(F32), g, gb_ref[rows, :].astype(F32)


def _head_out(outs, gate, gn):
    ys = [_rms(o, gn) * _silu(gate[:, h * LANES:(h + 1) * LANES]) for h, o in enumerate(outs)]
    return jnp.concatenate(ys, axis=1).astype(BF16)


def _stage_out_weight(wo_ref, wo_s, first):
    @pl.when(first)
    def _():
        wo_s[...] = wo_ref[0].astype(BF16)


def _ab_prompt_kernel(*refs, kind, n_in, chunk, q_scale):
    in_refs = refs[:n_in]
    gn_ref, w_ref, m_ref, u_ref, x_ref, wo_ref, xo_ref, s_ref, st_ref, y_s, wo_s = refs[n_in:]
    t = pl.program_id(1)
    nh = st_ref.shape[0]
    _stage_out_weight(wo_ref, wo_s, jnp.logical_and(pl.program_id(0) == 0, t == 0))

    @pl.when(t == 0)
    def _():
        st_ref[...] = jnp.zeros_like(st_ref)

    load = _hgrn_inputs if kind == "hgrn" else _gla_inputs
    st = [st_ref[h] for h in range(nh)]
    for c in range(y_s.shape[0] // chunk):
        rows = slice(c * chunk, (c + 1) * chunk)
        q, k, v, g, gate = load(in_refs, rows, q_scale)
        outs, (st,) = _gla_block(q, k, v, g, [st], w_ref, m_ref, u_ref, chunk)
        y_s[rows, :] = _head_out(outs, gate, gn_ref[...])
    for h in range(nh):
        st_ref[h] = st[h]
    xo_ref[...] = x_ref[...] + _dot(y_s[...], wo_s[...])

    @pl.when(t == pl.num_programs(1) - 1)
    def _():
        for h in range(nh):
            s_ref[0, h] = st_ref[h].T[:s_ref.shape[2], :]


def _ab_sample_kernel(*refs, kind, n_in, ls, q_scale):
    in_refs = refs[:n_in]
    gn_ref, w_ref, m_ref, u_ref, s0_ref, x_ref, wo_ref, xo_ref, s_ref, wo_s = refs[n_in:]
    _stage_out_weight(wo_ref, wo_s, pl.program_id(0) == 0)
    load = _hgrn_inputs if kind == "hgrn" else _gla_inputs
    n_seq, nh, dk = s0_ref.shape[:3]
    q, k, v, g, gate = load(in_refs, slice(None), q_scale)
    st0 = []
    for s in range(n_seq):
        st0.append([])
        for h in range(nh):
            s0 = s0_ref[s, h]
            if dk < LANES:
                s0 = jnp.concatenate([s0, jnp.zeros((LANES - dk, s0.shape[1]), F32)], axis=0)
            st0[s].append(s0.T)
    outs, st = _gla_block(q, k, v, g, st0, w_ref, m_ref, u_ref, ls)
    xo_ref[...] = x_ref[...] + _dot(_head_out(outs, gate, gn_ref[...]), wo_s[...])
    for s in range(n_seq):
        for h in range(nh):
            s_ref[s, h] = st[s][h].T[:dk, :]


def _ab_specs(kind, p16, p32, extra, tok_block, tok_index, nh):
    width = nh * LANES

    def col(group, w=width):
        return pl.BlockSpec((tok_block, w), lambda *ids: (tok_index(*ids), group))

    if kind == "hgrn":
        (logits,) = extra
        args = [p16, p16, p16, p32, logits]
        specs = [col(0), col(1), col(2), col(0), _resident(logits.shape)]
    else:
        w_gk, b_gk = extra
        args = [p16, p16, p16, p16, p32, w_gk, b_gk]
        specs = [col(3), col(4), col(5), col(6), col(H_A, LANES), _resident(w_gk.shape),
                 _resident(b_gk.shape)]
    return args, specs


def _out_weight_spec(w_out, layer, part, rows):
    return pl.BlockSpec((1, rows, w_out.shape[2]), lambda *_: (layer, part, 0), pipeline_mode=pl.Buffered(1))


def _ab_prompt_call(kind, p16, p32, extra, gn, x, w_out, layer, part, batch, seq, dk, name):
    nh = H_A if kind == "hgrn" else H_B
    d = x.shape[1]
    tt = GLA_STEP_TOKENS
    nt = seq // tt
    tok = lambda b, t: b * nt + t
    consts = _gla_constants(1, GLA_CHUNK, nh * LANES)
    args, specs = _ab_specs(kind, p16, p32, extra, tt, tok, nh)
    n_in = len(args)
    args += [gn.reshape(1, LANES), *consts, x, w_out]
    specs += [_resident((1, LANES))] + [_resident(c.shape) for c in consts]
    specs += [pl.BlockSpec((tt, d), lambda b, t: (tok(b, t), 0)), _out_weight_spec(w_out, layer, part, nh * LANES)]
    return pl.pallas_call(
        functools.partial(_ab_prompt_kernel, kind=kind, n_in=n_in, chunk=GLA_CHUNK, q_scale=dk ** -0.5),
        out_shape=[jax.ShapeDtypeStruct((batch * seq, d), F32),
                   jax.ShapeDtypeStruct((batch, nh, dk, LANES), F32)],
        grid=(batch, nt),
        in_specs=specs,
        out_specs=[pl.BlockSpec((tt, d), lambda b, t: (tok(b, t), 0)),
                   pl.BlockSpec((1, nh, dk, LANES), lambda b, t: (b, 0, 0, 0))],
        scratch_shapes=[pltpu.VMEM((nh, LANES, LANES), F32), pltpu.VMEM((tt, nh * LANES), BF16),
                        pltpu.VMEM((nh * LANES, d), BF16)],
        compiler_params=_params(2, 40 << 20),
        name=name,
    )(*args)


def _ab_sample_call(kind, p16, p32, extra, gn, s0, row0, seq, x, x_row0, w_out, layer, part, name):
    batch, nh, dk, _ = s0.shape
    d = x.shape[1]
    nb = SAMPLE_SEQS
    rows = nb * seq
    blk0, xblk0 = row0 // rows, x_row0 // rows
    consts = _gla_constants(nb, seq, nh * LANES)
    args, specs = _ab_specs(kind, p16, p32, extra, rows, lambda i: blk0 + i, nh)
    n_in = len(args)
    args += [gn.reshape(1, LANES), *consts, s0, x, w_out]
    state_spec = pl.BlockSpec((nb, nh, dk, LANES), lambda i: (i, 0, 0, 0))
    specs += [_resident((1, LANES))] + [_resident(c.shape) for c in consts] + [state_spec]
    specs += [pl.BlockSpec((rows, d), lambda i: (xblk0 + i, 0)), _out_weight_spec(w_out, layer, part, nh * LANES)]
    return pl.pallas_call(
        functools.partial(_ab_sample_kernel, kind=kind, n_in=n_in, ls=seq, q_scale=dk ** -0.5),
        out_shape=[jax.ShapeDtypeStruct((batch * seq, d), F32),
                   jax.ShapeDtypeStruct((batch, nh, dk, LANES), F32)],
        grid=(batch // nb,),
        in_specs=specs,
        out_specs=[pl.BlockSpec((rows, d), lambda i: (i, 0)), state_spec],
        scratch_shapes=[pltpu.VMEM((nh * LANES, d), BF16)],
        compiler_params=_params(1, 40 << 20),
        name=name,
    )(*args)


def _ret_constants(n_seq, ls):
    c = n_seq * ls
    log_gamma = np.log1p(-np.exp2(-5.0 - np.arange(H_C, dtype=np.float64)))[:, None, None]
    t = np.arange(c)[:, None]
    s = np.arange(c)[None, :]
    causal = ((t // ls) == (s // ls)) & (s <= t)
    dmask = np.where(causal, np.exp(log_gamma * np.where(causal, t - s, 0)), 0.0)
    pos = (np.arange(c) % ls)[None, :, None]
    scales = np.stack([np.exp(log_gamma * (pos + 1)), np.exp(log_gamma * (ls - 1 - pos))], axis=1)
    scales = np.broadcast_to(scales, (H_C, 2, c, LANES))
    return jnp.asarray(dmask, F32), jnp.asarray(scales, F32)


def _rope_tables(positions, half):
    inv = ROPE_THETA ** (-jnp.arange(half, dtype=F32) / half)
    ang = positions.astype(F32)[:, None] * inv[None, :]
    return jnp.cos(ang), jnp.sin(ang)


def _rope(x, cos, sin):
    half = x.shape[1] // 2
    x1, x2 = x[:, :half], x[:, half:]
    return jnp.concatenate([x1 * cos - x2 * sin, x2 * cos + x1 * sin], axis=1)


def _ret_block(q16, k16, v, cos, sin, dmask, q_scale, k_scale, s_list, ls):
    dk = q16.shape[1]
    q = _rope(q16.astype(F32), cos, sin)
    k = _rope(k16.astype(F32), cos, sin) * (dk ** -0.5)
    reps = dk // LANES
    q_sc = jnp.concatenate([q_scale] * reps, axis=1)
    k_sc = jnp.concatenate([k_scale] * reps, axis=1)
    att = _dot_nt(q.astype(BF16), k.astype(BF16)) * dmask
    o = _dot(att.astype(BF16), v)
    qs = q * q_sc
    ks = k * k_sc
    decay = q_scale[ls - 1:ls, 0:1]
    n_seq = q.shape[0] // ls
    v32 = v if n_seq == 1 else v.astype(F32)
    o_parts, s_new = [], []
    for b in range(n_seq):
        rows = slice(b * ls, (b + 1) * ls)
        s = s_list[b]
        o_parts.append(o[rows] + _dot(qs[rows].astype(BF16), s.astype(BF16)))
        s_new.append(decay * s + _dot_tn(ks[rows].astype(BF16), v32[rows].astype(BF16)))
    o = o_parts[0] if n_seq == 1 else jnp.concatenate(o_parts, axis=0)
    return o, s_new


def _ret_head_out(o, gate, gn):
    oc = o - jnp.mean(o, axis=-1, keepdims=True)
    y = oc * lax.rsqrt(jnp.mean(oc * oc, axis=-1, keepdims=True) + EPS) * gn
    return (y * _silu(gate)).astype(BF16)


def _ret_prompt_kernel(q_ref, k_ref, v_ref, g_ref, cos_ref, sin_ref, dm_ref, sc_ref, gn_ref, x_ref, wo_ref,
                       xo_ref, s_ref, st_ref, y_s, wo_s):
    t = pl.program_id(1)
    nh, dk, dv = st_ref.shape
    c = dm_ref.shape[1]
    _stage_out_weight(wo_ref, wo_s, jnp.logical_and(pl.program_id(0) == 0, t == 0))

    @pl.when(t == 0)
    def _():
        st_ref[...] = jnp.zeros_like(st_ref)

    for h in range(nh):
        s = st_ref[h]
        for j in range(q_ref.shape[0] // c):
            rows = slice(j * c, (j + 1) * c)
            o, (s,) = _ret_block(q_ref[rows, h * dk:(h + 1) * dk], k_ref[rows, h * dk:(h + 1) * dk],
                                 v_ref[rows, h * dv:(h + 1) * dv], cos_ref[rows, :], sin_ref[rows, :], dm_ref[h],
                                 sc_ref[h, 0], sc_ref[h, 1], [s], c)
            y_s[rows, h * dv:(h + 1) * dv] = _ret_head_out(o, g_ref[rows, h * dv:(h + 1) * dv].astype(F32),
                                                           gn_ref[...])
        st_ref[h] = s
    xo_ref[...] = x_ref[...] + _dot(y_s[...], wo_s[...])

    @pl.when(t == pl.num_programs(1) - 1)
    def _():
        s_ref[0] = st_ref[...]


def _ret_sample_kernel(q_ref, k_ref, v_ref, g_ref, cos_ref, sin_ref, dm_ref, sc_ref, gn_ref, s0_ref, x_ref,
                       wo_ref, xo_ref, s_ref, wo_s, *, ls):
    h = pl.program_id(1)
    dv = v_ref.shape[1]
    _stage_out_weight(wo_ref, wo_s, jnp.logical_and(pl.program_id(0) == 0, h == 0))
    n_seq = q_ref.shape[0] // ls
    o, s = _ret_block(q_ref[...], k_ref[...], v_ref[...], cos_ref[...], sin_ref[...], dm_ref[0],
                      sc_ref[0, 0], sc_ref[0, 1], [s0_ref[b, 0] for b in range(n_seq)], ls)
    y = _ret_head_out(o, g_ref[...].astype(F32), gn_ref[...])
    proj = _dot(y, wo_s[pl.ds(pl.multiple_of(h * dv, dv), dv), :])

    @pl.when(h == 0)
    def _():
        xo_ref[...] = x_ref[...] + proj

    @pl.when(h > 0)
    def _():
        xo_ref[...] += proj

    for b in range(n_seq):
        s_ref[b, 0] = s[b]


def _ret_prompt_call(r16, gn, x, w_out, layer, batch, seq, dk, dv, name):
    c = RET_STEP_TOKENS
    nt = seq // c
    d = x.shape[1]
    dmask, scales = _ret_constants(1, RET_CHUNK)
    cos, sin = _rope_tables(jnp.arange(seq, dtype=jnp.int32), dk // 2)
    tok = lambda b, t: b * nt + t
    nq, nv = H_C * dk, H_C * dv
    args = [r16, r16, r16, r16, cos, sin, dmask, scales, gn.reshape(1, dv), x, w_out]
    specs = [pl.BlockSpec((c, nq), lambda b, t: (tok(b, t), 0)),
             pl.BlockSpec((c, nq), lambda b, t: (tok(b, t), 1)),
             pl.BlockSpec((c, nv), lambda b, t: (tok(b, t), 2 * nq // nv)),
             pl.BlockSpec((c, nv), lambda b, t: (tok(b, t), 2 * nq // nv + 1)),
             pl.BlockSpec((c, dk // 2), lambda b, t: (t, 0)), pl.BlockSpec((c, dk // 2), lambda b, t: (t, 0)),
             _resident(dmask.shape), _resident(scales.shape), _resident((1, dv)),
             pl.BlockSpec((c, d), lambda b, t: (tok(b, t), 0)), _out_weight_spec(w_out, layer, 0, nv)]
    return pl.pallas_call(
        _ret_prompt_kernel,
        out_shape=[jax.ShapeDtypeStruct((batch * seq, d), F32),
                   jax.ShapeDtypeStruct((batch, H_C, dk, dv), F32)],
        grid=(batch, nt),
        in_specs=specs,
        out_specs=[pl.BlockSpec((c, d), lambda b, t: (tok(b, t), 0)),
                   pl.BlockSpec((1, H_C, dk, dv), lambda b, t: (b, 0, 0, 0))],
        scratch_shapes=[pltpu.VMEM((H_C, dk, dv), F32), pltpu.VMEM((c, nv), BF16), pltpu.VMEM((nv, d), BF16)],
        compiler_params=_params(2, 48 << 20),
        name=name,
    )(*args)


def _ret_sample_call(r16, gn, s0, row0, batch, seq, dk, dv, x, w_out, layer, name):
    nb = SAMPLE_SEQS
    rows = nb * seq
    blk0 = row0 // rows
    d = x.shape[1]
    dmask, scales = _ret_constants(nb, seq)
    pos = PAST_LEN + (jnp.arange(rows, dtype=jnp.int32) % seq)
    cos, sin = _rope_tables(pos, dk // 2)
    nq = H_C * dk

    def col(width, base):
        return pl.BlockSpec((rows, width), lambda i, h: (blk0 + i, base + h))

    state_spec = pl.BlockSpec((nb, 1, dk, dv), lambda i, h: (i, h, 0, 0))
    args = [r16, r16, r16, r16, cos, sin, dmask, scales, gn.reshape(1, dv), s0, x, w_out]
    specs = [col(dk, 0), col(dk, H_C), col(dv, 2 * nq // dv), col(dv, 2 * nq // dv + H_C),
             _resident(cos.shape), _resident(sin.shape),
             pl.BlockSpec((1, rows, rows), lambda i, h: (h, 0, 0)),
             pl.BlockSpec((1, 2, rows, LANES), lambda i, h: (h, 0, 0, 0)), _resident((1, dv)), state_spec,
             pl.BlockSpec((rows, d), lambda i, h: (blk0 + i, 0)), _out_weight_spec(w_out, layer, 0, H_C * dv)]
    return pl.pallas_call(
        functools.partial(_ret_sample_kernel, ls=seq),
        out_shape=[jax.ShapeDtypeStruct((batch * seq, d), F32),
                   jax.ShapeDtypeStruct((batch, H_C, dk, dv), F32)],
        grid=(batch // nb, H_C),
        in_specs=specs,
        out_specs=[pl.BlockSpec((rows, d), lambda i, h: (i, 0)), state_spec],
        scratch_shapes=[pltpu.VMEM((H_C * dv, d), BF16)],
        compiler_params=_params(2, 56 << 20),
        name=name,
    )(*args)


def _pad_heads(w, n_heads, width):
    d = w.shape[0]
    w = w.reshape(d, n_heads, -1)
    return jnp.pad(w, ((0, 0), (0, 0), (0, width - w.shape[2]))).reshape(d, n_heads * width)


def _prepare_ab_weights(w_in, w_gk2, b_gk, dk_a, dv_a, dk_b, dv_b):
    sizes = (H_A * dk_a, H_A * dk_a, H_A * dv_a, H_A * dv_a, H_B * dk_b, H_B * dk_b, H_B * dv_b,
             GLA_RANK, H_B * dv_b)
    qa, fa, ia, ga, qb, kb, vb, lrb, gb = jnp.split(w_in, [int(v) for v in np.cumsum(sizes)[:-1]], axis=1)
    w16 = jnp.concatenate([qa, ia, ga, _pad_heads(qb, H_B, LANES), _pad_heads(kb, H_B, LANES), vb, gb],
                          axis=1).astype(BF16)
    w32 = jnp.concatenate([fa, jnp.pad(lrb, ((0, 0), (0, LANES - GLA_RANK)))], axis=1).astype(BF16)
    w_gk = jnp.pad(_pad_heads(w_gk2, H_B, LANES), ((0, LANES - GLA_RANK), (0, 0))).astype(BF16)
    b_gk = _pad_heads(b_gk.reshape(1, -1), H_B, LANES)
    return w16, w32, w_gk, b_gk


def kernel(x_prompt, x_sample, state_hgrn, state_gla, state_ret, p_prompt, p_sample, norm_ffn1, ffn1_w_in, ffn1_w_out, norm_mix, ab_w_in, ab_w_gk2, ab_b_gk, hgrn_lb_logits, ab_gn_hgrn, ab_gn_gla, ab_w_out, ret_w_in, ret_gn, ret_w_out, norm_ffn2, ffn2_w_in, ffn2_w_out, norm_ple, ple_w_gate, ple_w_proj, norm_final):
    bp, lp, d = x_prompt.shape
    bs, ls, _ = x_sample.shape
    depth = norm_ffn1.shape[0]
    n_prompt = bp * lp
    dk_a, dv_a = state_hgrn.shape[-2:]
    dk_b, dv_b = state_gla.shape[-2:]
    dk_c, dv_c = state_ret.shape[-2:]
    assert dk_a == dv_a == dv_b == LANES and dk_b <= LANES and state_hgrn.shape[0] == 1

    n_rows = (n_prompt, bs * ls)
    xs = (x_prompt.reshape(n_prompt, d), x_sample.reshape(bs * ls, d))
    ps = (p_prompt.reshape(depth, n_prompt, -1), p_sample.reshape(depth, bs * ls, -1))

    new_hgrn_p, new_gla_p, new_ret_p, new_hgrn_s, new_gla_s, new_ret_s = [], [], [], [], [], []
    for i in range(depth):
        j = i // 2
        x = _ffn_call(xs if i == 0 else (x,), n_rows, norm_ffn1[i], ffn1_w_in, ffn1_w_out, i, name=f"ffn1_{i}")
        if i % 2 == 0:
            w16, w32, w_gk, b_gk = _prepare_ab_weights(ab_w_in[j], ab_w_gk2[j], ab_b_gk[j], dk_a, dv_a, dk_b, dv_b)
            p16, p32 = _inproj_call(x, norm_mix[i], [w16, w32], [BF16, F32], [w16.shape[1] // 2, w32.shape[1]],
                                    name=f"ab_in_{i}")
            hgrn = ("hgrn", p16, p32, (hgrn_lb_logits,), ab_gn_hgrn[j])
            gla = ("gla", p16, p32, (w_gk, b_gk), ab_gn_gla[j])
            xa_p, sa_p = _ab_prompt_call(*hgrn, x, ab_w_out, j, 0, bp, lp, dk_a, f"hgrn_p_{i}")
            xa_s, sa_s = _ab_sample_call(*hgrn, state_hgrn[j], n_prompt, ls, x, n_prompt, ab_w_out, j, 0,
                                         f"hgrn_s_{i}")
            x_p, sb_p = _ab_prompt_call(*gla, xa_p, ab_w_out, j, 1, bp, lp, dk_b, f"gla_p_{i}")
            x_s, sb_s = _ab_sample_call(*gla, state_gla[j], n_prompt, ls, xa_s, 0, ab_w_out, j, 1, f"gla_s_{i}")
            new_hgrn_p.append(sa_p)
            new_hgrn_s.append(sa_s)
            new_gla_p.append(sb_p)
            new_gla_s.append(sb_s)
        else:
            r16 = _inproj_staged_call(x, norm_mix[i], ret_w_in, j, 2 * V7X_MXU_COLS, name=f"ret_in_{i}")
            x_p, sc_p = _ret_prompt_call(r16, ret_gn[j], x, ret_w_out, j, bp, lp, dk_c, dv_c, f"ret_p_{i}")
            x_s, sc_s = _ret_sample_call(r16, ret_gn[j], state_ret[j], n_prompt, bs, ls, dk_c, dv_c, x, ret_w_out, j,
                                         f"ret_s_{i}")
            new_ret_p.append(sc_p)
            new_ret_s.append(sc_s)
        last = i == depth - 1
        x = _ffn_call((x_p, x_s), n_rows, norm_ffn2[i], ffn2_w_in, ffn2_w_out, i,
                      ple=(ps, norm_ple[i], ple_w_gate, ple_w_proj),
                      final_gain=norm_final if last else None, split_out=last, name=f"ffn2_{i}")

    y_prompt = x[0].reshape(bp, lp, d)
    y_sample = x[1].reshape(bs, ls, d)
    def layers(states):
        return states[0][None] if len(states) == 1 else jnp.stack(states)

    return (y_prompt, y_sample, layers(new_hgrn_p), layers(new_gla_p), layers(new_ret_p),
            layers(new_hgrn_s), layers(new_gla_s), layers(new_ret_s))
```

```python
import functools
import math

import numpy as np
import jax
import jax.numpy as jnp
from jax import lax
from jax.experimental import pallas as pl
from jax.experimental.pallas import tpu as pltpu

F32 = jnp.float32
BF16 = jnp.bfloat16

H_A = 4
H_B = 4
H_C = 4
GLA_RANK = 16
GLA_GATE_NORM = 16.0
ROPE_THETA = 10000.0
PAST_LEN = 16384
EPS = 1e-6
LOG2_E = math.log2(math.e)

LANES = 128
V7X_MXU_COLS = 256
V7X_VMEM_BUDGET_BYTES = 56 * 1024 * 1024

ROW_TILE = 512
GLA_CHUNK = 128
GLA_STEP_TOKENS = 1024
RET_CHUNK = 256
RET_STEP_TOKENS = 512
SAMPLE_SEQS = 16


def _params(n_axes, vmem_bytes):
    return pltpu.CompilerParams(dimension_semantics=("arbitrary",) * n_axes,
                                vmem_limit_bytes=min(int(vmem_bytes), V7X_VMEM_BUDGET_BYTES))


def _resident(shape):
    nd = len(shape)
    return pl.BlockSpec(shape, lambda *_: (0,) * nd, pipeline_mode=pl.Buffered(1))


def _resident_layer(shape, layer):
    nd = len(shape)
    return pl.BlockSpec((1,) + tuple(shape[1:]), lambda *_: (layer,) + (0,) * (nd - 1),
                        pipeline_mode=pl.Buffered(1))


def _row_maps(prompt_tiles, lead=0):
    stacked = lambda i: (jnp.maximum(i - lead, 0), 0)
    prompt = lambda i: (jnp.clip(i - lead, 0, prompt_tiles - 1), 0)
    sample = lambda i: (jnp.maximum(i - lead - prompt_tiles, 0), 0)
    return stacked, prompt, sample


def _rms(x, g):
    return x * lax.rsqrt(jnp.mean(x * x, axis=-1, keepdims=True) + EPS) * g


def _silu(x):
    return x * jax.nn.sigmoid(x)


def _dot(a, b):
    return jnp.dot(a, b, preferred_element_type=F32)


def _dot_nt(a, b):
    return lax.dot_general(a, b, (((1,), (1,)), ((), ())), preferred_element_type=F32)


def _dot_tn(a, b):
    return lax.dot_general(a, b, (((0,), (0,)), ((), ())), preferred_element_type=F32)


def _pick(is_prompt, refs, index=None):
    vals = [r[...] if index is None else r[index] for r in refs]
    return vals[0] if len(vals) == 1 else jnp.where(is_prompt, vals[0], vals[1])


def _ffn_kernel(*refs, n_x, has_ple, has_final, n_o, prompt_tiles, stage_steps, gate_steps):
    it = iter(refs)
    x_refs = [next(it) for _ in range(n_x)]
    g_ref, win_ref, wout_ref = next(it), next(it), next(it)
    if has_ple:
        p_refs = [next(it), next(it)]
        gp_ref, wg_ref, wp_ref = next(it), next(it), next(it)
    if has_final:
        gf_ref = next(it)
    o_refs = [next(it) for _ in range(n_o)]
    win_s, wout_s, act_ref = next(it), next(it), next(it)
    if has_ple:
        wg_s, wp_s = next(it), next(it)
    i = pl.program_id(0)
    cols = win_s.shape[2]
    n_half = win_s.shape[0] // 2

    @pl.when(i < stage_steps)
    def _():
        blk = win_ref[0]
        win_s[2 * i] = blk[:, :cols].astype(BF16)
        win_s[2 * i + 1] = blk[:, cols:].astype(BF16)
        wout_s[pl.ds(pl.multiple_of(i * cols, cols), cols), :] = wout_ref[0].astype(BF16)

    if has_ple:
        rows = wg_ref.shape[1]

        @pl.when(i < gate_steps)
        def _():
            wg_s[pl.ds(pl.multiple_of(i * rows, rows), rows), :] = wg_ref[0].astype(BF16)

        @pl.when(i == 0)
        def _():
            wp_s[...] = wp_ref[0].astype(BF16)

    @pl.when(i >= stage_steps)
    def _():
        is_prompt = i - stage_steps < prompt_tiles
        x = _pick(is_prompt, x_refs)
        xn = _rms(x, g_ref[...]).astype(BF16)
        for j in range(n_half):
            a = _dot(xn, win_s[j])
            b = _dot(xn, win_s[n_half + j])
            act_ref[:, j * cols:(j + 1) * cols] = (_silu(a) * b).astype(BF16)
        x = x + 0.5 * _dot(act_ref[...], wout_s[...])
        if has_ple:
            gate = jax.nn.sigmoid(_dot(_rms(x, gp_ref[...]).astype(BF16), wg_s[...]))
            x = x + gate * _dot(_pick(is_prompt, p_refs, 0).astype(BF16), wp_s[...])
        if has_final:
            x = _rms(x, gf_ref[...])
        if n_o == 1:
            o_refs[0][...] = x
        else:
            @pl.when(is_prompt)
            def _():
                o_refs[0][...] = x

            @pl.when(jnp.logical_not(is_prompt))
            def _():
                o_refs[1][...] = x


def _ffn_call(xs, n_rows, gain, w_in, w_out, layer, ple=None, final_gain=None, split_out=False, name="ffn"):
    d = xs[0].shape[1]
    d_ff = w_out.shape[1]
    tm = ROW_TILE
    cols = V7X_MXU_COLS
    assert n_rows[0] % tm == 0 and n_rows[1] % tm == 0 and d_ff % cols == 0
    pt, st = n_rows[0] // tm, n_rows[1] // tm
    stage = d_ff // cols
    stacked, prompt, sample = _row_maps(pt, stage)
    pair = [prompt, sample]
    last = stage - 1
    args = [*xs, gain.reshape(1, d), w_in, w_out]
    specs = [pl.BlockSpec((tm, d), m) for m in ([stacked] if len(xs) == 1 else pair)]
    specs += [_resident((1, d)),
              pl.BlockSpec((1, d, 2 * cols), lambda i: (layer, 0, jnp.minimum(i, last))),
              pl.BlockSpec((1, cols, d), lambda i: (layer, jnp.minimum(i, last), 0))]
    scratch = [pltpu.VMEM((2 * stage, d, cols), BF16), pltpu.VMEM((d_ff, d), BF16), pltpu.VMEM((tm, d_ff), BF16)]
    gate_steps = 0
    if ple is not None:
        ps, gp, wg, wp = ple
        gate_rows = LANES
        gate_steps = d // gate_rows
        assert gate_steps <= stage
        args += [*ps, gp.reshape(1, d), wg, wp]
        specs += [pl.BlockSpec((1, tm, ps[0].shape[2]), lambda i, m=m: (layer, *m(i))) for m in pair]
        specs += [_resident((1, d)),
                  pl.BlockSpec((1, gate_rows, d), lambda i: (layer, jnp.minimum(i, gate_steps - 1), 0)),
                  _resident_layer(wp.shape, layer)]
        scratch += [pltpu.VMEM((d, d), BF16), pltpu.VMEM(wp.shape[1:], BF16)]
    if final_gain is not None:
        args.append(final_gain.reshape(1, d))
        specs.append(_resident((1, d)))
    if split_out:
        out_shape = [jax.ShapeDtypeStruct((r, d), F32) for r in n_rows]
        out_specs = [pl.BlockSpec((tm, d), m) for m in pair]
    else:
        out_shape = jax.ShapeDtypeStruct((sum(n_rows), d), F32)
        out_specs = pl.BlockSpec((tm, d), stacked)
    return pl.pallas_call(
        functools.partial(_ffn_kernel, n_x=len(xs), has_ple=ple is not None, has_final=final_gain is not None,
                          n_o=2 if split_out else 1, prompt_tiles=pt, stage_steps=stage, gate_steps=gate_steps),
        out_shape=out_shape,
        grid=(stage + pt + st,),
        in_specs=specs,
        out_specs=out_specs,
        scratch_shapes=scratch,
        compiler_params=_params(1, V7X_VMEM_BUDGET_BYTES),
        name=name,
    )(*args)


def _inproj_kernel(*refs, n_out, chunk_cols):
    x_ref, g_ref = refs[0], refs[1]
    w_refs = refs[2:2 + n_out]
    o_refs = refs[2 + n_out:2 + 2 * n_out]
    xn = _rms(x_ref[...], g_ref[...]).astype(BF16)
    for w_ref, o_ref, cw in zip(w_refs, o_refs, chunk_cols):
        for c in range(w_ref.shape[1] // cw):
            o_ref[:, c * cw:(c + 1) * cw] = _dot(xn, w_ref[:, c * cw:(c + 1) * cw]).astype(o_ref.dtype)


def _inproj_call(x, gain, weights, out_dtypes, chunk_cols, name):
    n, d = x.shape
    tm = ROW_TILE
    row = lambda i: (i, 0)
    specs = [pl.BlockSpec((tm, d), row), _resident((1, d))] + [_resident(w.shape) for w in weights]
    out_shape = [jax.ShapeDtypeStruct((n, w.shape[1]), dt) for w, dt in zip(weights, out_dtypes)]
    out_specs = [pl.BlockSpec((tm, w.shape[1]), row) for w in weights]
    vmem = (sum(2 * w.size for w in weights) + 4 * tm * d * 4
            + sum(2 * tm * w.shape[1] * jnp.dtype(dt).itemsize for w, dt in zip(weights, out_dtypes))
            + 2 * tm * max(chunk_cols) * 4 + (4 << 20))
    return pl.pallas_call(
        functools.partial(_inproj_kernel, n_out=len(weights), chunk_cols=tuple(chunk_cols)),
        out_shape=out_shape,
        grid=(n // tm,),
        in_specs=specs,
        out_specs=out_specs,
        compiler_params=_params(1, vmem),
        name=name,
    )(x, gain.reshape(1, d), *weights)


def _inproj_staged_kernel(x_ref, g_ref, w_ref, o_ref, w_s, *, stage_steps):
    i = pl.program_id(0)
    cols = w_s.shape[2]

    @pl.when(i < stage_steps)
    def _():
        w_s[i] = w_ref[0].astype(BF16)

    @pl.when(i >= stage_steps)
    def _():
        xn = _rms(x_ref[...], g_ref[...]).astype(BF16)
        for c in range(stage_steps):
            o_ref[:, c * cols:(c + 1) * cols] = _dot(xn, w_s[c]).astype(o_ref.dtype)


def _inproj_staged_call(x, gain, w, layer, cols, name):
    n, d = x.shape
    n_out = w.shape[2]
    tm = ROW_TILE
    assert n_out % cols == 0
    stage = n_out // cols
    row = lambda i: (jnp.maximum(i - stage, 0), 0)
    vmem = 2 * d * n_out + 2 * d * cols * 4 + 4 * tm * d * 4 + 2 * tm * n_out * 2 + 2 * tm * cols * 4 + (4 << 20)
    return pl.pallas_call(
        functools.partial(_inproj_staged_kernel, stage_steps=stage),
        out_shape=jax.ShapeDtypeStruct((n, n_out), BF16),
        grid=(stage + n // tm,),
        in_specs=[pl.BlockSpec((tm, d), row), _resident((1, d)),
                  pl.BlockSpec((1, d, cols), lambda i: (layer, 0, jnp.minimum(i, stage - 1)))],
        out_specs=pl.BlockSpec((tm, n_out), row),
        scratch_shapes=[pltpu.VMEM((stage, d, cols), BF16)],
        compiler_params=_params(1, vmem),
        name=name,
    )(x, gain.reshape(1, d), w)


GLA_MATMUL_LEVELS = (1, 2)


def _gla_constants(n_seq, ls, width):
    c = n_seq * ls
    nlev = int(round(math.log2(ls)))
    assert 2 ** nlev == ls and nlev > max(GLA_MATMUL_LEVELS)
    t = np.arange(c)[:, None]
    i = np.arange(c)[None, :]
    w_rows = [((t // ls) == (i // ls)) & (i <= t)]
    masks, uppers = [], []
    for l in range(nlev):
        m = 2 ** l
        upper = ((t // m) % 2) == 1
        r = (t // (2 * m)) * (2 * m) + m - 1
        if l in GLA_MATMUL_LEVELS:
            w_rows.append(np.where(upper, (i > r) & (i <= t), (i > t) & (i <= r)))
        masks.append(((t // (2 * m)) == (i // (2 * m))) & upper & (((i // m) % 2) == 0))
        uppers.append(np.broadcast_to(upper, (c, width)))
    w = jnp.asarray(np.concatenate(w_rows, 0).astype(np.float32), BF16)
    return w, jnp.asarray(np.stack(masks).astype(np.float32)), jnp.asarray(np.stack(uppers).astype(np.float32))


def _gla_block(q, k, v, g, st_list, w_ref, m_ref, u_ref, ls, transposed_state):
    c, width = q.shape
    nh = width // LANES
    n_seq = c // ls
    nlev = m_ref.shape[0]
    g_hi32 = g.astype(BF16).astype(F32)
    g_lo32 = g - g_hi32
    g_hi, g_lo = g_hi32.astype(BF16), g_lo32.astype(BF16)
    ex2 = _dot(w_ref[...], jnp.concatenate([g_hi, g_lo], axis=1))
    ex = ex2[:, :width] + ex2[:, width:]
    b = ex[:c]

    def row(r, n):
        return jnp.broadcast_to(b[r:r + 1, :], (n, width))

    b_end = jnp.concatenate([row((s + 1) * ls - 1, ls) for s in range(n_seq)], axis=0) if n_seq > 1 else row(c - 1, c)
    e_cum = jnp.exp2(b)
    qe = q * e_cum
    kr = k * jnp.exp2(b_end - b)

    att = [None] * nh
    for l in range(nlev):
        m = 2 ** l
        if l == 0:
            x = jnp.where(u_ref[l] > 0.5, q * jnp.exp2(g), k)
        elif l in GLA_MATMUL_LEVELS:
            j = 1 + GLA_MATMUL_LEVELS.index(l)
            x = jnp.where(u_ref[l] > 0.5, q, k) * jnp.exp2(ex[j * c:(j + 1) * c])
        else:
            pieces = []
            for p in range(c // (2 * m)):
                lo, mid, hi = p * 2 * m, p * 2 * m + m, (p + 1) * 2 * m
                pivot = row(mid - 1, m)
                pieces.append(k[lo:mid] * jnp.exp2(pivot - b[lo:mid]))
                pieces.append(q[mid:hi] * jnp.exp2(b[mid:hi] - pivot))
            x = jnp.concatenate(pieces, axis=0)
        x = x.astype(BF16)
        mask = m_ref[l] > 0.5
        for h in range(nh):
            xh = x[:, h * LANES:(h + 1) * LANES]
            p = _dot_nt(xh, xh)
            att[h] = jnp.where(mask, p, 0.0 if att[h] is None else att[h])

    qk = q * k
    outs, st_new = [], [[None] * nh for _ in range(n_seq)]
    for h in range(nh):
        sl = slice(h * LANES, (h + 1) * LANES)
        vh = v[:, sl]
        o = _dot(att[h].astype(BF16), vh.astype(BF16)) + jnp.sum(qk[:, sl], axis=1, keepdims=True) * vh
        parts = []
        for s in range(n_seq):
            rows = slice(s * ls, (s + 1) * ls)
            st = st_list[s][h]
            if transposed_state:
                parts.append(o[rows] + _dot_nt(qe[rows, sl].astype(BF16), st.astype(BF16)))
                e_tot = e_cum[(s + 1) * ls - 1:(s + 1) * ls, sl]
                st_new[s][h] = e_tot * st + _dot_tn(vh[rows].astype(BF16), kr[rows, sl].astype(BF16))
            else:
                parts.append(o[rows] + _dot(qe[rows, sl].astype(BF16), st.astype(BF16)))
                g2 = jnp.concatenate([g_hi32[rows, sl], g_lo32[rows, sl]], axis=0).astype(BF16)
                e_tot = jnp.exp2(_dot_tn(g2, jnp.ones((2 * ls, LANES), BF16)))
                st_new[s][h] = e_tot * st + _dot_tn(kr[rows, sl].astype(BF16), vh[rows].astype(BF16))
        outs.append(parts[0] if n_seq == 1 else jnp.concatenate(parts, axis=0))
    return outs, st_new


def _hgrn_inputs(refs, rows, q_scale):
    del q_scale
    qa_ref, ia_ref, ga_ref, fa_ref, lg_ref = refs
    lg = lg_ref[...]
    ex = jnp.exp(lg - jnp.max(lg, axis=0, keepdims=True))
    lb = ex[0:1] / jnp.sum(ex, axis=0, keepdims=True)
    sig = jax.nn.sigmoid(fa_ref[rows, :])
    q = _silu(qa_ref[rows, :].astype(F32))
    k = (1.0 - lb) * (1.0 - sig)
    g = jnp.log(lb + (1.0 - lb) * sig) * LOG2_E
    return q, k, ia_ref[rows, :].astype(F32), g, ga_ref[rows, :].astype(F32)


def _gla_inputs(refs, rows, q_scale):
    qb_ref, kb_ref, vb_ref, gb_ref, lr_ref, wgk_ref, bgk_ref = refs
    z = _dot(lr_ref[rows, :].astype(BF16), wgk_ref[...]) + bgk_ref[...]
    g = (jnp.minimum(z, 0.0) - jnp.log1p(jnp.exp(-jnp.abs(z)))) * (LOG2_E / GLA_GATE_NORM)
    q = qb_ref[rows, :].astype(F32) * q_scale
    return q, kb_ref[rows, :].astype(F32), vb_ref[rows, :].astype(F32), g, gb_ref[rows, :].astype(F32)


def _head_out(outs, gate, gn):
    ys = [_rms(o, gn) * _silu(gate[:, h * LANES:(h + 1) * LANES]) for h, o in enumerate(outs)]
    return jnp.concatenate(ys, axis=1).astype(BF16)


def _stage_out_weight(wo_ref, wo_s, first):
    @pl.when(first)
    def _():
        wo_s[...] = wo_ref[0].astype(BF16)


def _ab_prompt_kernel(*refs, kind, n_in, chunk, q_scale):
    in_refs = refs[:n_in]
    gn_ref, w_ref, m_ref, u_ref, x_ref, wo_ref, xo_ref, s_ref, st_ref, y_s, wo_s = refs[n_in:]
    t = pl.program_id(1)
    nh = st_ref.shape[0]
    _stage_out_weight(wo_ref, wo_s, jnp.logical_and(pl.program_id(0) == 0, t == 0))

    @pl.when(t == 0)
    def _():
        st_ref[...] = jnp.zeros_like(st_ref)

    load = _hgrn_inputs if kind == "hgrn" else _gla_inputs
    st = [st_ref[h] for h in range(nh)]
    for c in range(y_s.shape[0] // chunk):
        rows = slice(c * chunk, (c + 1) * chunk)
        q, k, v, g, gate = load(in_refs, rows, q_scale)
        outs, (st,) = _gla_block(q, k, v, g, [st], w_ref, m_ref, u_ref, chunk, True)
        y_s[rows, :] = _head_out(outs, gate, gn_ref[...])
    for h in range(nh):
        st_ref[h] = st[h]
    xo_ref[...] = x_ref[...] + _dot(y_s[...], wo_s[...])

    @pl.when(t == pl.num_programs(1) - 1)
    def _():
        for h in range(nh):
            s_ref[0, h] = st_ref[h].T[:s_ref.shape[2], :]


def _ab_sample_kernel(*refs, kind, n_in, ls, q_scale):
    in_refs = refs[:n_in]
    gn_ref, w_ref, m_ref, u_ref, s0_ref, x_ref, wo_ref, xo_ref, s_ref, wo_s = refs[n_in:]
    _stage_out_weight(wo_ref, wo_s, pl.program_id(0) == 0)
    load = _hgrn_inputs if kind == "hgrn" else _gla_inputs
    n_seq, nh, dk = s0_ref.shape[:3]
    q, k, v, g, gate = load(in_refs, slice(None), q_scale)
    st0 = []
    for s in range(n_seq):
        st0.append([])
        for h in range(nh):
            s0 = s0_ref[s, h]
            if dk < LANES:
                s0 = jnp.concatenate([s0, jnp.zeros((LANES - dk, s0.shape[1]), F32)], axis=0)
            st0[s].append(s0)
    outs, st = _gla_block(q, k, v, g, st0, w_ref, m_ref, u_ref, ls, False)
    xo_ref[...] = x_ref[...] + _dot(_head_out(outs, gate, gn_ref[...]), wo_s[...])
    for s in range(n_seq):
        for h in range(nh):
            s_ref[s, h] = st[s][h][:dk, :]


def _ab_specs(kind, p16, p32, extra, tok_block, tok_index, nh):
    width = nh * LANES

    def col(group, w=width):
        return pl.BlockSpec((tok_block, w), lambda *ids: (tok_index(*ids), group))

    if kind == "hgrn":
        (logits,) = extra
        args = [p16, p16, p16, p32, logits]
        specs = [col(0), col(1), col(2), col(0), _resident(logits.shape)]
    else:
        w_gk, b_gk = extra
        args = [p16, p16, p16, p16, p32, w_gk, b_gk]
        specs = [col(3), col(4), col(5), col(6), col(H_A, LANES), _resident(w_gk.shape),
                 _resident(b_gk.shape)]
    return args, specs


def _out_weight_spec(w_out, layer, part, rows):
    return pl.BlockSpec((1, rows, w_out.shape[2]), lambda *_: (layer, part, 0), pipeline_mode=pl.Buffered(1))


def _ab_prompt_call(kind, p16, p32, extra, gn, x, w_out, layer, part, batch, seq, dk, name):
    nh = H_A if kind == "hgrn" else H_B
    d = x.shape[1]
    tt = GLA_STEP_TOKENS
    nt = seq // tt
    tok = lambda b, t: b * nt + t
    consts = _gla_constants(1, GLA_CHUNK, nh * LANES)
    args, specs = _ab_specs(kind, p16, p32, extra, tt, tok, nh)
    n_in = len(args)
    args += [gn.reshape(1, LANES), *consts, x, w_out]
    specs += [_resident((1, LANES))] + [_resident(c.shape) for c in consts]
    specs += [pl.BlockSpec((tt, d), lambda b, t: (tok(b, t), 0)), _out_weight_spec(w_out, layer, part, nh * LANES)]
    return pl.pallas_call(
        functools.partial(_ab_prompt_kernel, kind=kind, n_in=n_in, chunk=GLA_CHUNK, q_scale=dk ** -0.5),
        out_shape=[jax.ShapeDtypeStruct((batch * seq, d), F32),
                   jax.ShapeDtypeStruct((batch, nh, dk, LANES), F32)],
        grid=(batch, nt),
        in_specs=specs,
        out_specs=[pl.BlockSpec((tt, d), lambda b, t: (tok(b, t), 0)),
                   pl.BlockSpec((1, nh, dk, LANES), lambda b, t: (b, 0, 0, 0))],
        scratch_shapes=[pltpu.VMEM((nh, LANES, LANES), F32), pltpu.VMEM((tt, nh * LANES), BF16),
                        pltpu.VMEM((nh * LANES, d), BF16)],
        compiler_params=_params(2, 40 << 20),
        name=name,
    )(*args)


def _ab_sample_call(kind, p16, p32, extra, gn, s0, row0, seq, x, x_row0, w_out, layer, part, name):
    batch, nh, dk, _ = s0.shape
    d = x.shape[1]
    nb = SAMPLE_SEQS
    rows = nb * seq
    blk0, xblk0 = row0 // rows, x_row0 // rows
    consts = _gla_constants(nb, seq, nh * LANES)
    args, specs = _ab_specs(kind, p16, p32, extra, rows, lambda i: blk0 + i, nh)
    n_in = len(args)
    args += [gn.reshape(1, LANES), *consts, s0, x, w_out]
    state_spec = pl.BlockSpec((nb, nh, dk, LANES), lambda i: (i, 0, 0, 0))
    specs += [_resident((1, LANES))] + [_resident(c.shape) for c in consts] + [state_spec]
    specs += [pl.BlockSpec((rows, d), lambda i: (xblk0 + i, 0)), _out_weight_spec(w_out, layer, part, nh * LANES)]
    return pl.pallas_call(
        functools.partial(_ab_sample_kernel, kind=kind, n_in=n_in, ls=seq, q_scale=dk ** -0.5),
        out_shape=[jax.ShapeDtypeStruct((batch * seq, d), F32),
                   jax.ShapeDtypeStruct((batch, nh, dk, LANES), F32)],
        grid=(batch // nb,),
        in_specs=specs,
        out_specs=[pl.BlockSpec((rows, d), lambda i: (i, 0)), state_spec],
        scratch_shapes=[pltpu.VMEM((nh * LANES, d), BF16)],
        compiler_params=_params(1, 40 << 20),
        name=name,
    )(*args)


def _ret_constants(n_seq, ls):
    c = n_seq * ls
    log_gamma = np.log1p(-np.exp2(-5.0 - np.arange(H_C, dtype=np.float64)))[:, None, None]
    t = np.arange(c)[:, None]
    s = np.arange(c)[None, :]
    causal = ((t // ls) == (s // ls)) & (s <= t)
    dmask = np.where(causal, np.exp(log_gamma * np.where(causal, t - s, 0)), 0.0)
    pos = (np.arange(c) % ls)[None, :, None]
    scales = np.stack([np.exp(log_gamma * (pos + 1)), np.exp(log_gamma * (ls - 1 - pos))], axis=1)
    scales = np.broadcast_to(scales, (H_C, 2, c, LANES))
    return jnp.asarray(dmask, F32), jnp.asarray(scales, F32)


def _rope_tables(positions, half):
    inv = ROPE_THETA ** (-jnp.arange(half, dtype=F32) / half)
    ang = positions.astype(F32)[:, None] * inv[None, :]
    return jnp.cos(ang), jnp.sin(ang)


def _rope(x, cos, sin):
    half = x.shape[1] // 2
    x1, x2 = x[:, :half], x[:, half:]
    return jnp.concatenate([x1 * cos - x2 * sin, x2 * cos + x1 * sin], axis=1)


def _ret_block(q16, k16, v, cos, sin, dmask, q_scale, k_scale, s_list, ls):
    dk = q16.shape[1]
    q = _rope(q16.astype(F32), cos, sin)
    k = _rope(k16.astype(F32), cos, sin) * (dk ** -0.5)
    reps = dk // LANES
    q_sc = jnp.concatenate([q_scale] * reps, axis=1)
    k_sc = jnp.concatenate([k_scale] * reps, axis=1)
    att = _dot_nt(q.astype(BF16), k.astype(BF16)) * dmask
    o = _dot(att.astype(BF16), v)
    qs = q * q_sc
    ks = k * k_sc
    decay = q_scale[ls - 1:ls, 0:1]
    n_seq = q.shape[0] // ls
    v32 = v if n_seq == 1 else v.astype(F32)
    o_parts, s_new = [], []
    for b in range(n_seq):
        rows = slice(b * ls, (b + 1) * ls)
        s = s_list[b]
        o_parts.append(o[rows] + _dot(qs[rows].astype(BF16), s.astype(BF16)))
        s_new.append(decay * s + _dot_tn(ks[rows].astype(BF16), v32[rows].astype(BF16)))
    o = o_parts[0] if n_seq == 1 else jnp.concatenate(o_parts, axis=0)
    return o, s_new


def _ret_head_out(o, gate, gn):
    oc = o - jnp.mean(o, axis=-1, keepdims=True)
    y = oc * lax.rsqrt(jnp.mean(oc * oc, axis=-1, keepdims=True) + EPS) * gn
    return (y * _silu(gate)).astype(BF16)


def _ret_prompt_kernel(q_ref, k_ref, v_ref, g_ref, cos_ref, sin_ref, dm_ref, sc_ref, gn_ref, x_ref, wo_ref,
                       xo_ref, s_ref, st_ref, y_s, wo_s):
    t = pl.program_id(1)
    nh, dk, dv = st_ref.shape
    c = dm_ref.shape[1]
    _stage_out_weight(wo_ref, wo_s, jnp.logical_and(pl.program_id(0) == 0, t == 0))

    @pl.when(t == 0)
    def _():
        st_ref[...] = jnp.zeros_like(st_ref)

    for h in range(nh):
        s = st_ref[h]
        for j in range(q_ref.shape[0] // c):
            rows = slice(j * c, (j + 1) * c)
            o, (s,) = _ret_block(q_ref[rows, h * dk:(h + 1) * dk], k_ref[rows, h * dk:(h + 1) * dk],
                                 v_ref[rows, h * dv:(h + 1) * dv], cos_ref[rows, :], sin_ref[rows, :], dm_ref[h],
                                 sc_ref[h, 0], sc_ref[h, 1], [s], c)
            y_s[rows, h * dv:(h + 1) * dv] = _ret_head_out(o, g_ref[rows, h * dv:(h + 1) * dv].astype(F32),
                                                           gn_ref[...])
        st_ref[h] = s
    xo_ref[...] = x_ref[...] + _dot(y_s[...], wo_s[...])

    @pl.when(t == pl.num_programs(1) - 1)
    def _():
        s_ref[0] = st_ref[...]


def _ret_sample_kernel(q_ref, k_ref, v_ref, g_ref, cos_ref, sin_ref, dm_ref, sc_ref, gn_ref, s0_ref, x_ref,
                       wo_ref, xo_ref, s_ref, wo_s, *, ls):
    h = pl.program_id(1)
    dv = v_ref.shape[1]
    _stage_out_weight(wo_ref, wo_s, jnp.logical_and(pl.program_id(0) == 0, h == 0))
    n_seq = q_ref.shape[0] // ls
    o, s = _ret_block(q_ref[...], k_ref[...], v_ref[...], cos_ref[...], sin_ref[...], dm_ref[0],
                      sc_ref[0, 0], sc_ref[0, 1], [s0_ref[b, 0] for b in range(n_seq)], ls)
    y = _ret_head_out(o, g_ref[...].astype(F32), gn_ref[...])
    proj = _dot(y, wo_s[pl.ds(pl.multiple_of(h * dv, dv), dv), :])

    @pl.when(h == 0)
    def _():
        xo_ref[...] = x_ref[...] + proj

    @pl.when(h > 0)
    def _():
        xo_ref[...] += proj

    for b in range(n_seq):
        s_ref[b, 0] = s[b]


def _ret_prompt_call(r16, gn, x, w_out, layer, batch, seq, dk, dv, name):
    c = RET_STEP_TOKENS
    nt = seq // c
    d = x.shape[1]
    dmask, scales = _ret_constants(1, RET_CHUNK)
    cos, sin = _rope_tables(jnp.arange(seq, dtype=jnp.int32), dk // 2)
    tok = lambda b, t: b * nt + t
    nq, nv = H_C * dk, H_C * dv
    args = [r16, r16, r16, r16, cos, sin, dmask, scales, gn.reshape(1, dv), x, w_out]
    specs = [pl.BlockSpec((c, nq), lambda b, t: (tok(b, t), 0)),
             pl.BlockSpec((c, nq), lambda b, t: (tok(b, t), 1)),
             pl.BlockSpec((c, nv), lambda b, t: (tok(b, t), 2 * nq // nv)),
             pl.BlockSpec((c, nv), lambda b, t: (tok(b, t), 2 * nq // nv + 1)),
             pl.BlockSpec((c, dk // 2), lambda b, t: (t, 0)), pl.BlockSpec((c, dk // 2), lambda b, t: (t, 0)),
             _resident(dmask.shape), _resident(scales.shape), _resident((1, dv)),
             pl.BlockSpec((c, d), lambda b, t: (tok(b, t), 0)), _out_weight_spec(w_out, layer, 0, nv)]
    return pl.pallas_call(
        _ret_prompt_kernel,
        out_shape=[jax.ShapeDtypeStruct((batch * seq, d), F32),
                   jax.ShapeDtypeStruct((batch, H_C, dk, dv), F32)],
        grid=(batch, nt),
        in_specs=specs,
        out_specs=[pl.BlockSpec((c, d), lambda b, t: (tok(b, t), 0)),
                   pl.BlockSpec((1, H_C, dk, dv), lambda b, t: (b, 0, 0, 0))],
        scratch_shapes=[pltpu.VMEM((H_C, dk, dv), F32), pltpu.VMEM((c, nv), BF16), pltpu.VMEM((nv, d), BF16)],
        compiler_params=_params(2, 48 << 20),
        name=name,
    )(*args)


def _ret_sample_call(r16, gn, s0, row0, batch, seq, dk, dv, x, w_out, layer, name):
    nb = SAMPLE_SEQS
    rows = nb * seq
    blk0 = row0 // rows
    d = x.shape[1]
    dmask, scales = _ret_constants(nb, seq)
    pos = PAST_LEN + (jnp.arange(rows, dtype=jnp.int32) % seq)
    cos, sin = _rope_tables(pos, dk // 2)
    nq = H_C * dk

    def col(width, base):
        return pl.BlockSpec((rows, width), lambda i, h: (blk0 + i, base + h))

    state_spec = pl.BlockSpec((nb, 1, dk, dv), lambda i, h: (i, h, 0, 0))
    args = [r16, r16, r16, r16, cos, sin, dmask, scales, gn.reshape(1, dv), s0, x, w_out]
    specs = [col(dk, 0), col(dk, H_C), col(dv, 2 * nq // dv), col(dv, 2 * nq // dv + H_C),
             _resident(cos.shape), _resident(sin.shape),
             pl.BlockSpec((1, rows, rows), lambda i, h: (h, 0, 0)),
             pl.BlockSpec((1, 2, rows, LANES), lambda i, h: (h, 0, 0, 0)), _resident((1, dv)), state_spec,
             pl.BlockSpec((rows, d), lambda i, h: (blk0 + i, 0)), _out_weight_spec(w_out, layer, 0, H_C * dv)]
    return pl.pallas_call(
        functools.partial(_ret_sample_kernel, ls=seq),
        out_shape=[jax.ShapeDtypeStruct((batch * seq, d), F32),
                   jax.ShapeDtypeStruct((batch, H_C, dk, dv), F32)],
        grid=(batch // nb, H_C),
        in_specs=specs,
        out_specs=[pl.BlockSpec((rows, d), lambda i, h: (i, 0)), state_spec],
        scratch_shapes=[pltpu.VMEM((H_C * dv, d), BF16)],
        compiler_params=_params(2, 56 << 20),
        name=name,
    )(*args)


def _pad_heads(w, n_heads, width):
    d = w.shape[0]
    w = w.reshape(d, n_heads, -1)
    return jnp.pad(w, ((0, 0), (0, 0), (0, width - w.shape[2]))).reshape(d, n_heads * width)


def _prepare_ab_weights(w_in, w_gk2, b_gk, dk_a, dv_a, dk_b, dv_b):
    sizes = (H_A * dk_a, H_A * dk_a, H_A * dv_a, H_A * dv_a, H_B * dk_b, H_B * dk_b, H_B * dv_b,
             GLA_RANK, H_B * dv_b)
    qa, fa, ia, ga, qb, kb, vb, lrb, gb = jnp.split(w_in, [int(v) for v in np.cumsum(sizes)[:-1]], axis=1)
    w16 = jnp.concatenate([qa, ia, ga, _pad_heads(qb, H_B, LANES), _pad_heads(kb, H_B, LANES), vb, gb],
                          axis=1).astype(BF16)
    w32 = jnp.concatenate([fa, jnp.pad(lrb, ((0, 0), (0, LANES - GLA_RANK)))], axis=1).astype(BF16)
    w_gk = jnp.pad(_pad_heads(w_gk2, H_B, LANES), ((0, LANES - GLA_RANK), (0, 0))).astype(BF16)
    b_gk = _pad_heads(b_gk.reshape(1, -1), H_B, LANES)
    return w16, w32, w_gk, b_gk


def kernel(x_prompt, x_sample, state_hgrn, state_gla, state_ret, p_prompt, p_sample, norm_ffn1, ffn1_w_in, ffn1_w_out, norm_mix, ab_w_in, ab_w_gk2, ab_b_gk, hgrn_lb_logits, ab_gn_hgrn, ab_gn_gla, ab_w_out, ret_w_in, ret_gn, ret_w_out, norm_ffn2, ffn2_w_in, ffn2_w_out, norm_ple, ple_w_gate, ple_w_proj, norm_final):
    bp, lp, d = x_prompt.shape
    bs, ls, _ = x_sample.shape
    depth = norm_ffn1.shape[0]
    n_prompt = bp * lp
    dk_a, dv_a = state_hgrn.shape[-2:]
    dk_b, dv_b = state_gla.shape[-2:]
    dk_c, dv_c = state_ret.shape[-2:]
    assert dk_a == dv_a == dv_b == LANES and dk_b <= LANES and state_hgrn.shape[0] == 1

    n_rows = (n_prompt, bs * ls)
    xs = (x_prompt.reshape(n_prompt, d), x_sample.reshape(bs * ls, d))
    ps = (p_prompt.reshape(depth, n_prompt, -1), p_sample.reshape(depth, bs * ls, -1))

    new_hgrn_p, new_gla_p, new_ret_p, new_hgrn_s, new_gla_s, new_ret_s = [], [], [], [], [], []
    for i in range(depth):
        j = i // 2
        x = _ffn_call(xs if i == 0 else (x,), n_rows, norm_ffn1[i], ffn1_w_in, ffn1_w_out, i, name=f"ffn1_{i}")
        if i % 2 == 0:
            w16, w32, w_gk, b_gk = _prepare_ab_weights(ab_w_in[j], ab_w_gk2[j], ab_b_gk[j], dk_a, dv_a, dk_b, dv_b)
            p16, p32 = _inproj_call(x, norm_mix[i], [w16, w32], [BF16, F32], [w16.shape[1] // 2, w32.shape[1]],
                                    name=f"ab_in_{i}")
            hgrn = ("hgrn", p16, p32, (hgrn_lb_logits,), ab_gn_hgrn[j])
            gla = ("gla", p16, p32, (w_gk, b_gk), ab_gn_gla[j])
            xa_p, sa_p = _ab_prompt_call(*hgrn, x, ab_w_out, j, 0, bp, lp, dk_a, f"hgrn_p_{i}")
            xa_s, sa_s = _ab_sample_call(*hgrn, state_hgrn[j], n_prompt, ls, x, n_prompt, ab_w_out, j, 0,
                                         f"hgrn_s_{i}")
            x_p, sb_p = _ab_prompt_call(*gla, xa_p, ab_w_out, j, 1, bp, lp, dk_b, f"gla_p_{i}")
            x_s, sb_s = _ab_sample_call(*gla, state_gla[j], n_prompt, ls, xa_s, 0, ab_w_out, j, 1, f"gla_s_{i}")
            new_hgrn_p.append(sa_p)
            new_hgrn_s.append(sa_s)
            new_gla_p.append(sb_p)
            new_gla_s.append(sb_s)
        else:
            r16 = _inproj_staged_call(x, norm_mix[i], ret_w_in, j, 2 * V7X_MXU_COLS, name=f"ret_in_{i}")
            x_p, sc_p = _ret_prompt_call(r16, ret_gn[j], x, ret_w_out, j, bp, lp, dk_c, dv_c, f"ret_p_{i}")
            x_s, sc_s = _ret_sample_call(r16, ret_gn[j], state_ret[j], n_prompt, bs, ls, dk_c, dv_c, x, ret_w_out, j,
                                         f"ret_s_{i}")
            new_ret_p.append(sc_p)
            new_ret_s.append(sc_s)
        last = i == depth - 1
        x = _ffn_call((x_p, x_s), n_rows, norm_ffn2[i], ffn2_w_in, ffn2_w_out, i,
                      ple=(ps, norm_ple[i], ple_w_gate, ple_w_proj),
                      final_gain=norm_final if last else None, split_out=last, name=f"ffn2_{i}")

    y_prompt = x[0].reshape(bp, lp, d)
    y_sample = x[1].reshape(bs, ls, d)
    def layers(states):
        return states[0][None] if len(states) == 1 else jnp.stack(states)

    return (y_prompt, y_sample, layers(new_hgrn_p), layers(new_gla_p), layers(new_ret_p),
            layers(new_hgrn_s), layers(new_gla_s), layers(new_ret_s))
```

```python
import functools
import math

import numpy as np
import jax
import jax.numpy as jnp
from jax import lax
from jax.experimental import pallas as pl
from jax.experimental.pallas import tpu as pltpu

F32 = jnp.float32
BF16 = jnp.bfloat16

H_A = 4
H_B = 4
H_C = 4
GLA_RANK = 16
GLA_GATE_NORM = 16.0
ROPE_THETA = 10000.0
PAST_LEN = 16384
EPS = 1e-6
LOG2_E = math.log2(math.e)

LANES = 128
V7X_MXU_COLS = 256
V7X_VMEM_BUDGET_BYTES = 56 * 1024 * 1024

ROW_TILE = 512
GLA_CHUNK = 128
GLA_STEP_TOKENS = 1024
RET_CHUNK = 256
RET_STEP_TOKENS = 512
SAMPLE_SEQS = 16


def _params(n_axes, vmem_bytes):
    return pltpu.CompilerParams(dimension_semantics=("arbitrary",) * n_axes,
                                vmem_limit_bytes=min(int(vmem_bytes), V7X_VMEM_BUDGET_BYTES))


def _resident(shape):
    nd = len(shape)
    return pl.BlockSpec(shape, lambda *_: (0,) * nd, pipeline_mode=pl.Buffered(1))


def _resident_layer(shape, layer):
    nd = len(shape)
    return pl.BlockSpec((1,) + tuple(shape[1:]), lambda *_: (layer,) + (0,) * (nd - 1),
                        pipeline_mode=pl.Buffered(1))


def _row_maps(prompt_tiles, lead=0):
    stacked = lambda i: (jnp.maximum(i - lead, 0), 0)
    prompt = lambda i: (jnp.clip(i - lead, 0, prompt_tiles - 1), 0)
    sample = lambda i: (jnp.maximum(i - lead - prompt_tiles, 0), 0)
    return stacked, prompt, sample


def _rms(x, g):
    return x * lax.rsqrt(jnp.mean(x * x, axis=-1, keepdims=True) + EPS) * g


def _silu(x):
    return x * jax.nn.sigmoid(x)


def _dot(a, b):
    return jnp.dot(a, b, preferred_element_type=F32)


def _dot_nt(a, b):
    return lax.dot_general(a, b, (((1,), (1,)), ((), ())), preferred_element_type=F32)


def _dot_tn(a, b):
    return lax.dot_general(a, b, (((0,), (0,)), ((), ())), preferred_element_type=F32)


def _pick(is_prompt, refs, index=None):
    vals = [r[...] if index is None else r[index] for r in refs]
    return vals[0] if len(vals) == 1 else jnp.where(is_prompt, vals[0], vals[1])


def _ffn_kernel(*refs, n_x, has_ple, has_final, n_o, prompt_tiles, stage_steps, gate_steps):
    it = iter(refs)
    x_refs = [next(it) for _ in range(n_x)]
    g_ref, win_ref, wout_ref = next(it), next(it), next(it)
    if has_ple:
        p_refs = [next(it), next(it)]
        gp_ref, wg_ref, wp_ref = next(it), next(it), next(it)
    if has_final:
        gf_ref = next(it)
    o_refs = [next(it) for _ in range(n_o)]
    win_s, wout_s, act_ref = next(it), next(it), next(it)
    if has_ple:
        wg_s, wp_s = next(it), next(it)
    i = pl.program_id(0)
    cols = win_s.shape[2]
    n_half = win_s.shape[0] // 2

    @pl.when(i < stage_steps)
    def _():
        blk = win_ref[0]
        win_s[2 * i] = blk[:, :cols].astype(BF16)
        win_s[2 * i + 1] = blk[:, cols:].astype(BF16)
        wout_s[pl.ds(pl.multiple_of(i * cols, cols), cols), :] = wout_ref[0].astype(BF16)

    if has_ple:
        rows = wg_ref.shape[1]

        @pl.when(i < gate_steps)
        def _():
            wg_s[pl.ds(pl.multiple_of(i * rows, rows), rows), :] = wg_ref[0].astype(BF16)

        @pl.when(i == 0)
        def _():
            wp_s[...] = wp_ref[0].astype(BF16)

    @pl.when(i >= stage_steps)
    def _():
        is_prompt = i - stage_steps < prompt_tiles
        x = _pick(is_prompt, x_refs)
        xn = _rms(x, g_ref[...]).astype(BF16)
        for j in range(n_half):
            a = _dot(xn, win_s[j])
            b = _dot(xn, win_s[n_half + j])
            act_ref[:, j * cols:(j + 1) * cols] = (_silu(a) * b).astype(BF16)
        x = x + 0.5 * _dot(act_ref[...], wout_s[...])
        if has_ple:
            gate = jax.nn.sigmoid(_dot(_rms(x, gp_ref[...]).astype(BF16), wg_s[...]))
            x = x + gate * _dot(_pick(is_prompt, p_refs, 0).astype(BF16), wp_s[...])
        if has_final:
            x = _rms(x, gf_ref[...])
        if n_o == 1:
            o_refs[0][...] = x
        else:
            @pl.when(is_prompt)
            def _():
                o_refs[0][...] = x

            @pl.when(jnp.logical_not(is_prompt))
            def _():
                o_refs[1][...] = x


def _ffn_call(xs, n_rows, gain, w_in, w_out, layer, ple=None, final_gain=None, split_out=False, name="ffn"):
    d = xs[0].shape[1]
    d_ff = w_out.shape[1]
    tm = ROW_TILE
    cols = V7X_MXU_COLS
    assert n_rows[0] % tm == 0 and n_rows[1] % tm == 0 and d_ff % cols == 0
    pt, st = n_rows[0] // tm, n_rows[1] // tm
    stage = d_ff // cols
    stacked, prompt, sample = _row_maps(pt, stage)
    pair = [prompt, sample]
    last = stage - 1
    args = [*xs, gain.reshape(1, d), w_in, w_out]
    specs = [pl.BlockSpec((tm, d), m) for m in ([stacked] if len(xs) == 1 else pair)]
    specs += [_resident((1, d)),
              pl.BlockSpec((1, d, 2 * cols), lambda i: (layer, 0, jnp.minimum(i, last))),
              pl.BlockSpec((1, cols, d), lambda i: (layer, jnp.minimum(i, last), 0))]
    scratch = [pltpu.VMEM((2 * stage, d, cols), BF16), pltpu.VMEM((d_ff, d), BF16), pltpu.VMEM((tm, d_ff), BF16)]
    gate_steps = 0
    if ple is not None:
        ps, gp, wg, wp = ple
        gate_rows = LANES
        gate_steps = d // gate_rows
        assert gate_steps <= stage
        args += [*ps, gp.reshape(1, d), wg, wp]
        specs += [pl.BlockSpec((1, tm, ps[0].shape[2]), lambda i, m=m: (layer, *m(i))) for m in pair]
        specs += [_resident((1, d)),
                  pl.BlockSpec((1, gate_rows, d), lambda i: (layer, jnp.minimum(i, gate_steps - 1), 0)),
                  _resident_layer(wp.shape, layer)]
        scratch += [pltpu.VMEM((d, d), BF16), pltpu.VMEM(wp.shape[1:], BF16)]
    if final_gain is not None:
        args.append(final_gain.reshape(1, d))
        specs.append(_resident((1, d)))
    if split_out:
        out_shape = [jax.ShapeDtypeStruct((r, d), F32) for r in n_rows]
        out_specs = [pl.BlockSpec((tm, d), m) for m in pair]
    else:
        out_shape = jax.ShapeDtypeStruct((sum(n_rows), d), F32)
        out_specs = pl.BlockSpec((tm, d), stacked)
    return pl.pallas_call(
        functools.partial(_ffn_kernel, n_x=len(xs), has_ple=ple is not None, has_final=final_gain is not None,
                          n_o=2 if split_out else 1, prompt_tiles=pt, stage_steps=stage, gate_steps=gate_steps),
        out_shape=out_shape,
        grid=(stage + pt + st,),
        in_specs=specs,
        out_specs=out_specs,
        scratch_shapes=scratch,
        compiler_params=_params(1, V7X_VMEM_BUDGET_BYTES),
        name=name,
    )(*args)


def _inproj_kernel(*refs, n_out, chunk_cols):
    x_ref, g_ref = refs[0], refs[1]
    w_refs = refs[2:2 + n_out]
    o_refs = refs[2 + n_out:2 + 2 * n_out]
    xn = _rms(x_ref[...], g_ref[...]).astype(BF16)
    for w_ref, o_ref, cw in zip(w_refs, o_refs, chunk_cols):
        for c in range(w_ref.shape[1] // cw):
            o_ref[:, c * cw:(c + 1) * cw] = _dot(xn, w_ref[:, c * cw:(c + 1) * cw]).astype(o_ref.dtype)


def _inproj_call(x, gain, weights, out_dtypes, chunk_cols, name):
    n, d = x.shape
    tm = ROW_TILE
    row = lambda i: (i, 0)
    specs = [pl.BlockSpec((tm, d), row), _resident((1, d))] + [_resident(w.shape) for w in weights]
    out_shape = [jax.ShapeDtypeStruct((n, w.shape[1]), dt) for w, dt in zip(weights, out_dtypes)]
    out_specs = [pl.BlockSpec((tm, w.shape[1]), row) for w in weights]
    vmem = (sum(2 * w.size for w in weights) + 4 * tm * d * 4
            + sum(2 * tm * w.shape[1] * jnp.dtype(dt).itemsize for w, dt in zip(weights, out_dtypes))
            + 2 * tm * max(chunk_cols) * 4 + (4 << 20))
    return pl.pallas_call(
        functools.partial(_inproj_kernel, n_out=len(weights), chunk_cols=tuple(chunk_cols)),
        out_shape=out_shape,
        grid=(n // tm,),
        in_specs=specs,
        out_specs=out_specs,
        compiler_params=_params(1, vmem),
        name=name,
    )(x, gain.reshape(1, d), *weights)


def _inproj_staged_kernel(x_ref, g_ref, w_ref, o_ref, w_s, *, stage_steps):
    i = pl.program_id(0)
    cols = w_s.shape[2]

    @pl.when(i < stage_steps)
    def _():
        w_s[i] = w_ref[0].astype(BF16)

    @pl.when(i >= stage_steps)
    def _():
        xn = _rms(x_ref[...], g_ref[...]).astype(BF16)
        for c in range(stage_steps):
            o_ref[:, c * cols:(c + 1) * cols] = _dot(xn, w_s[c]).astype(o_ref.dtype)


def _inproj_staged_call(x, gain, w, layer, cols, name):
    n, d = x.shape
    n_out = w.shape[2]
    tm = ROW_TILE
    assert n_out % cols == 0
    stage = n_out // cols
    row = lambda i: (jnp.maximum(i - stage, 0), 0)
    vmem = 2 * d * n_out + 2 * d * cols * 4 + 4 * tm * d * 4 + 2 * tm * n_out * 2 + 2 * tm * cols * 4 + (4 << 20)
    return pl.pallas_call(
        functools.partial(_inproj_staged_kernel, stage_steps=stage),
        out_shape=jax.ShapeDtypeStruct((n, n_out), BF16),
        grid=(stage + n // tm,),
        in_specs=[pl.BlockSpec((tm, d), row), _resident((1, d)),
                  pl.BlockSpec((1, d, cols), lambda i: (layer, 0, jnp.minimum(i, stage - 1)))],
        out_specs=pl.BlockSpec((tm, n_out), row),
        scratch_shapes=[pltpu.VMEM((stage, d, cols), BF16)],
        compiler_params=_params(1, vmem),
        name=name,
    )(x, gain.reshape(1, d), w)


GLA_MATMUL_LEVELS = (1, 2)


def _gla_constants(n_seq, ls, width):
    c = n_seq * ls
    nlev = int(round(math.log2(ls)))
    assert 2 ** nlev == ls and nlev > max(GLA_MATMUL_LEVELS)
    t = np.arange(c)[:, None]
    i = np.arange(c)[None, :]
    w_rows = [((t // ls) == (i // ls)) & (i <= t)]
    masks, uppers = [], []
    for l in range(nlev):
        m = 2 ** l
        upper = ((t // m) % 2) == 1
        r = (t // (2 * m)) * (2 * m) + m - 1
        if l in GLA_MATMUL_LEVELS:
            w_rows.append(np.where(upper, (i > r) & (i <= t), (i > t) & (i <= r)))
        masks.append(((t // (2 * m)) == (i // (2 * m))) & upper & (((i // m) % 2) == 0))
        uppers.append(np.broadcast_to(upper, (c, width)))
    w = jnp.asarray(np.concatenate(w_rows, 0).astype(np.float32), BF16)
    return w, jnp.asarray(np.stack(masks).astype(np.float32)), jnp.asarray(np.stack(uppers).astype(np.float32))


def _gla_block(q, k, v, g, st_list, w_ref, m_ref, u_ref, ls, transposed_state, dk):
    c, width = q.shape
    nh = v.shape[1] // LANES
    hpg = LANES // dk
    n_groups = width // LANES
    n_seq = c // ls
    nlev = m_ref.shape[0]
    lane = lax.broadcasted_iota(jnp.int32, (1, LANES), 1)
    own = [jnp.where((lane >= i * dk) & (lane < (i + 1) * dk), 1.0, 0.0) for i in range(hpg)]
    g_hi32 = g.astype(BF16).astype(F32)
    g_lo32 = g - g_hi32
    g_hi, g_lo = g_hi32.astype(BF16), g_lo32.astype(BF16)
    ex2 = _dot(w_ref[...], jnp.concatenate([g_hi, g_lo], axis=1))
    ex = ex2[:, :width] + ex2[:, width:]
    b = ex[:c]

    def row(r, n):
        return jnp.broadcast_to(b[r:r + 1, :], (n, width))

    b_end = jnp.concatenate([row((s + 1) * ls - 1, ls) for s in range(n_seq)], axis=0) if n_seq > 1 else row(c - 1, c)
    e_cum = jnp.exp2(b)
    qe = q * e_cum
    kr = k * jnp.exp2(b_end - b)

    att = [None] * nh
    for l in range(nlev):
        m = 2 ** l
        if l == 0:
            x = jnp.where(u_ref[l] > 0.5, q * jnp.exp2(g), k)
        elif l in GLA_MATMUL_LEVELS:
            j = 1 + GLA_MATMUL_LEVELS.index(l)
            x = jnp.where(u_ref[l] > 0.5, q, k) * jnp.exp2(ex[j * c:(j + 1) * c])
        else:
            pieces = []
            for p in range(c // (2 * m)):
                lo, mid, hi = p * 2 * m, p * 2 * m + m, (p + 1) * 2 * m
                pivot = row(mid - 1, m)
                pieces.append(k[lo:mid] * jnp.exp2(pivot - b[lo:mid]))
                pieces.append(q[mid:hi] * jnp.exp2(b[mid:hi] - pivot))
            x = jnp.concatenate(pieces, axis=0)
        x = x.astype(BF16)
        mask = m_ref[l] > 0.5
        for h in range(nh):
            grp, i = divmod(h, hpg)
            xg = x[:, grp * LANES:(grp + 1) * LANES]
            xl = xg if hpg == 1 else xg * own[i].astype(BF16)
            att[h] = jnp.where(mask, _dot_nt(xl, xg), 0.0 if att[h] is None else att[h])

    qk = q * k
    outs = [[None] * n_seq for _ in range(nh)]
    st_new = [[None] * n_groups for _ in range(n_seq)]
    for grp in range(n_groups):
        gl = slice(grp * LANES, (grp + 1) * LANES)
        heads = []
        for i in range(hpg):
            h = grp * hpg + i
            vh = v[:, h * LANES:(h + 1) * LANES]
            sel = 1.0 if hpg == 1 else own[i]
            o = (_dot(att[h].astype(BF16), vh.astype(BF16))
                 + jnp.sum(qk[:, gl] * sel, axis=1, keepdims=True) * vh)
            heads.append((h, vh, o, qe[:, gl] * sel, kr[:, gl] * sel))
        for s in range(n_seq):
            rows = slice(s * ls, (s + 1) * ls)
            st = st_list[s][grp]
            if transposed_state:
                new = e_cum[(s + 1) * ls - 1:(s + 1) * ls, gl] * st
            else:
                g2 = jnp.concatenate([g_hi32[rows, gl], g_lo32[rows, gl]], axis=0).astype(BF16)
                new = jnp.exp2(_dot_tn(g2, jnp.ones((2 * ls, LANES), BF16))) * st
            for h, vh, o, qe_h, kr_h in heads:
                if transposed_state:
                    outs[h][s] = o[rows] + _dot_nt(qe_h[rows].astype(BF16), st.astype(BF16))
                    new = new + _dot_tn(vh[rows].astype(BF16), kr_h[rows].astype(BF16))
                else:
                    outs[h][s] = o[rows] + _dot(qe_h[rows].astype(BF16), st.astype(BF16))
                    new = new + _dot_tn(kr_h[rows].astype(BF16), vh[rows].astype(BF16))
            st_new[s][grp] = new
    outs = [parts[0] if n_seq == 1 else jnp.concatenate(parts, axis=0) for parts in outs]
    return outs, st_new


def _hgrn_inputs(refs, rows, q_scale):
    del q_scale
    qa_ref, ia_ref, ga_ref, fa_ref, lg_ref = refs
    lg = lg_ref[...]
    ex = jnp.exp(lg - jnp.max(lg, axis=0, keepdims=True))
    lb = ex[0:1] / jnp.sum(ex, axis=0, keepdims=True)
    sig = jax.nn.sigmoid(fa_ref[rows, :])
    q = _silu(qa_ref[rows, :].astype(F32))
    k = (1.0 - lb) * (1.0 - sig)
    g = jnp.log(lb + (1.0 - lb) * sig) * LOG2_E
    return q, k, ia_ref[rows, :].astype(F32), g, ga_ref[rows, :].astype(F32)


def _gla_inputs(refs, rows, q_scale):
    qb_ref, kb_ref, vb_ref, gb_ref, lr_ref, wgk_ref, bgk_ref = refs
    z = _dot(lr_ref[rows, :].astype(BF16), wgk_ref[...]) + bgk_ref[...]
    g = (jnp.minimum(z, 0.0) - jnp.log1p(jnp.exp(-jnp.abs(z)))) * (LOG2_E / GLA_GATE_NORM)
    q = qb_ref[rows, :].astype(F32) * q_scale
    return q, kb_ref[rows, :].astype(F32), vb_ref[rows, :].astype(F32), g, gb_ref[rows, :].astype(F32)


def _head_out(outs, gate, gn):
    ys = [_rms(o, gn) * _silu(gate[:, h * LANES:(h + 1) * LANES]) for h, o in enumerate(outs)]
    return jnp.concatenate(ys, axis=1).astype(BF16)


def _stage_out_weight(wo_ref, wo_s, first):
    @pl.when(first)
    def _():
        wo_s[...] = wo_ref[0].astype(BF16)


def _ab_prompt_kernel(*refs, kind, n_in, chunk, q_scale):
    in_refs = refs[:n_in]
    gn_ref, w_ref, m_ref, u_ref, x_ref, wo_ref, xo_ref, s_ref, st_ref, y_s, wo_s = refs[n_in:]
    t = pl.program_id(1)
    n_groups = st_ref.shape[0]
    nh, dk = s_ref.shape[1:3]
    hpg = nh // n_groups
    _stage_out_weight(wo_ref, wo_s, jnp.logical_and(pl.program_id(0) == 0, t == 0))

    @pl.when(t == 0)
    def _():
        st_ref[...] = jnp.zeros_like(st_ref)

    load = _hgrn_inputs if kind == "hgrn" else _gla_inputs
    st = [st_ref[j] for j in range(n_groups)]
    for c in range(y_s.shape[0] // chunk):
        rows = slice(c * chunk, (c + 1) * chunk)
        q, k, v, g, gate = load(in_refs, rows, q_scale)
        outs, (st,) = _gla_block(q, k, v, g, [st], w_ref, m_ref, u_ref, chunk, True, dk)
        y_s[rows, :] = _head_out(outs, gate, gn_ref[...])
    for j in range(n_groups):
        st_ref[j] = st[j]
    xo_ref[...] = x_ref[...] + _dot(y_s[...], wo_s[...])

    @pl.when(t == pl.num_programs(1) - 1)
    def _():
        for h in range(nh):
            grp, i = divmod(h, hpg)
            s_ref[0, h] = st_ref[grp].T[i * dk:(i + 1) * dk, :]


def _ab_sample_kernel(*refs, kind, n_in, ls, q_scale):
    in_refs = refs[:n_in]
    gn_ref, w_ref, m_ref, u_ref, s0_ref, x_ref, wo_ref, xo_ref, s_ref, wo_s = refs[n_in:]
    _stage_out_weight(wo_ref, wo_s, pl.program_id(0) == 0)
    load = _hgrn_inputs if kind == "hgrn" else _gla_inputs
    n_seq, nh, dk = s0_ref.shape[:3]
    hpg = LANES // dk
    q, k, v, g, gate = load(in_refs, slice(None), q_scale)
    st0 = [[jnp.concatenate([s0_ref[s, j * hpg + i] for i in range(hpg)], axis=0) if hpg > 1 else s0_ref[s, j]
            for j in range(nh // hpg)] for s in range(n_seq)]
    outs, st = _gla_block(q, k, v, g, st0, w_ref, m_ref, u_ref, ls, False, dk)
    xo_ref[...] = x_ref[...] + _dot(_head_out(outs, gate, gn_ref[...]), wo_s[...])
    for s in range(n_seq):
        for h in range(nh):
            grp, i = divmod(h, hpg)
            s_ref[s, h] = st[s][grp][i * dk:(i + 1) * dk, :]


def _ab_specs(kind, p16, p32, extra, tok_block, tok_index, nh, dk):
    wide = nh * LANES
    kw = nh * dk

    def col(start, w):
        assert start % w == 0
        return pl.BlockSpec((tok_block, w), lambda *ids: (tok_index(*ids), start // w))

    if kind == "hgrn":
        (logits,) = extra
        args = [p16, p16, p16, p32, logits]
        specs = [col(0, wide), col(wide, wide), col(2 * wide, wide), col(0, kw), _resident(logits.shape)]
    else:
        w_gk, b_gk = extra
        a16 = 3 * H_A * LANES
        args = [p16, p16, p16, p16, p32, w_gk, b_gk]
        specs = [col(a16, kw), col(a16 + kw, kw), col(a16 + 2 * kw, wide), col(a16 + 2 * kw + wide, wide),
                 col(H_A * LANES, LANES), _resident(w_gk.shape), _resident(b_gk.shape)]
    return args, specs


def _out_weight_spec(w_out, layer, part, rows):
    return pl.BlockSpec((1, rows, w_out.shape[2]), lambda *_: (layer, part, 0), pipeline_mode=pl.Buffered(1))


def _ab_prompt_call(kind, p16, p32, extra, gn, x, w_out, layer, part, batch, seq, dk, name):
    nh = H_A if kind == "hgrn" else H_B
    d = x.shape[1]
    tt = GLA_STEP_TOKENS
    nt = seq // tt
    tok = lambda b, t: b * nt + t
    consts = _gla_constants(1, GLA_CHUNK, nh * dk)
    args, specs = _ab_specs(kind, p16, p32, extra, tt, tok, nh, dk)
    n_in = len(args)
    args += [gn.reshape(1, LANES), *consts, x, w_out]
    specs += [_resident((1, LANES))] + [_resident(c.shape) for c in consts]
    specs += [pl.BlockSpec((tt, d), lambda b, t: (tok(b, t), 0)), _out_weight_spec(w_out, layer, part, nh * LANES)]
    return pl.pallas_call(
        functools.partial(_ab_prompt_kernel, kind=kind, n_in=n_in, chunk=GLA_CHUNK, q_scale=dk ** -0.5),
        out_shape=[jax.ShapeDtypeStruct((batch * seq, d), F32),
                   jax.ShapeDtypeStruct((batch, nh, dk, LANES), F32)],
        grid=(batch, nt),
        in_specs=specs,
        out_specs=[pl.BlockSpec((tt, d), lambda b, t: (tok(b, t), 0)),
                   pl.BlockSpec((1, nh, dk, LANES), lambda b, t: (b, 0, 0, 0))],
        scratch_shapes=[pltpu.VMEM((nh * dk // LANES, LANES, LANES), F32), pltpu.VMEM((tt, nh * LANES), BF16),
                        pltpu.VMEM((nh * LANES, d), BF16)],
        compiler_params=_params(2, 40 << 20),
        name=name,
    )(*args)


def _ab_sample_call(kind, p16, p32, extra, gn, s0, row0, seq, x, x_row0, w_out, layer, part, name):
    batch, nh, dk, _ = s0.shape
    d = x.shape[1]
    nb = SAMPLE_SEQS
    rows = nb * seq
    blk0, xblk0 = row0 // rows, x_row0 // rows
    consts = _gla_constants(nb, seq, nh * dk)
    args, specs = _ab_specs(kind, p16, p32, extra, rows, lambda i: blk0 + i, nh, dk)
    n_in = len(args)
    args += [gn.reshape(1, LANES), *consts, s0, x, w_out]
    state_spec = pl.BlockSpec((nb, nh, dk, LANES), lambda i: (i, 0, 0, 0))
    specs += [_resident((1, LANES))] + [_resident(c.shape) for c in consts] + [state_spec]
    specs += [pl.BlockSpec((rows, d), lambda i: (xblk0 + i, 0)), _out_weight_spec(w_out, layer, part, nh * LANES)]
    return pl.pallas_call(
        functools.partial(_ab_sample_kernel, kind=kind, n_in=n_in, ls=seq, q_scale=dk ** -0.5),
        out_shape=[jax.ShapeDtypeStruct((batch * seq, d), F32),
                   jax.ShapeDtypeStruct((batch, nh, dk, LANES), F32)],
        grid=(batch // nb,),
        in_specs=specs,
        out_specs=[pl.BlockSpec((rows, d), lambda i: (i, 0)), state_spec],
        scratch_shapes=[pltpu.VMEM((nh * LANES, d), BF16)],
        compiler_params=_params(1, 40 << 20),
        name=name,
    )(*args)


def _ret_constants(n_seq, ls):
    c = n_seq * ls
    log_gamma = np.log1p(-np.exp2(-5.0 - np.arange(H_C, dtype=np.float64)))[:, None, None]
    t = np.arange(c)[:, None]
    s = np.arange(c)[None, :]
    causal = ((t // ls) == (s // ls)) & (s <= t)
    dmask = np.where(causal, np.exp(log_gamma * np.where(causal, t - s, 0)), 0.0)
    pos = (np.arange(c) % ls)[None, :, None]
    scales = np.stack([np.exp(log_gamma * (pos + 1)), np.exp(log_gamma * (ls - 1 - pos))], axis=1)
    scales = np.broadcast_to(scales, (H_C, 2, c, LANES))
    return jnp.asarray(dmask, F32), jnp.asarray(scales, F32)


def _rope_tables(positions, half):
    inv = ROPE_THETA ** (-jnp.arange(half, dtype=F32) / half)
    ang = positions.astype(F32)[:, None] * inv[None, :]
    return jnp.cos(ang), jnp.sin(ang)


def _rope(x, cos, sin):
    half = x.shape[1] // 2
    x1, x2 = x[:, :half], x[:, half:]
    return jnp.concatenate([x1 * cos - x2 * sin, x2 * cos + x1 * sin], axis=1)


def _ret_block(q16, k16, v, cos, sin, dmask, q_scale, k_scale, s_list, ls):
    dk = q16.shape[1]
    q = _rope(q16.astype(F32), cos, sin)
    k = _rope(k16.astype(F32), cos, sin) * (dk ** -0.5)
    reps = dk // LANES
    q_sc = jnp.concatenate([q_scale] * reps, axis=1)
    k_sc = jnp.concatenate([k_scale] * reps, axis=1)
    att = _dot_nt(q.astype(BF16), k.astype(BF16)) * dmask
    o = _dot(att.astype(BF16), v)
    qs = q * q_sc
    ks = k * k_sc
    decay = q_scale[ls - 1:ls, 0:1]
    n_seq = q.shape[0] // ls
    v32 = v if n_seq == 1 else v.astype(F32)
    o_parts, s_new = [], []
    for b in range(n_seq):
        rows = slice(b * ls, (b + 1) * ls)
        s = s_list[b]
        o_parts.append(o[rows] + _dot(qs[rows].astype(BF16), s.astype(BF16)))
        s_new.append(decay * s + _dot_tn(ks[rows].astype(BF16), v32[rows].astype(BF16)))
    o = o_parts[0] if n_seq == 1 else jnp.concatenate(o_parts, axis=0)
    return o, s_new


def _ret_head_out(o, gate, gn):
    oc = o - jnp.mean(o, axis=-1, keepdims=True)
    y = oc * lax.rsqrt(jnp.mean(oc * oc, axis=-1, keepdims=True) + EPS) * gn
    return (y * _silu(gate)).astype(BF16)


def _ret_prompt_kernel(q_ref, k_ref, v_ref, g_ref, cos_ref, sin_ref, dm_ref, sc_ref, gn_ref, x_ref, wo_ref,
                       xo_ref, s_ref, st_ref, y_s, wo_s):
    t = pl.program_id(1)
    nh, dk, dv = st_ref.shape
    c = dm_ref.shape[1]
    _stage_out_weight(wo_ref, wo_s, jnp.logical_and(pl.program_id(0) == 0, t == 0))

    @pl.when(t == 0)
    def _():
        st_ref[...] = jnp.zeros_like(st_ref)

    for h in range(nh):
        s = st_ref[h]
        for j in range(q_ref.shape[0] // c):
            rows = slice(j * c, (j + 1) * c)
            o, (s,) = _ret_block(q_ref[rows, h * dk:(h + 1) * dk], k_ref[rows, h * dk:(h + 1) * dk],
                                 v_ref[rows, h * dv:(h + 1) * dv], cos_ref[rows, :], sin_ref[rows, :], dm_ref[h],
                                 sc_ref[h, 0], sc_ref[h, 1], [s], c)
            y_s[rows, h * dv:(h + 1) * dv] = _ret_head_out(o, g_ref[rows, h * dv:(h + 1) * dv].astype(F32),
                                                           gn_ref[...])
        st_ref[h] = s
    xo_ref[...] = x_ref[...] + _dot(y_s[...], wo_s[...])

    @pl.when(t == pl.num_programs(1) - 1)
    def _():
        s_ref[0] = st_ref[...]


def _ret_sample_kernel(q_ref, k_ref, v_ref, g_ref, cos_ref, sin_ref, dm_ref, sc_ref, gn_ref, s0_ref, x_ref,
                       wo_ref, xo_ref, s_ref, wo_s, *, ls):
    h = pl.program_id(1)
    dv = v_ref.shape[1]
    _stage_out_weight(wo_ref, wo_s, jnp.logical_and(pl.program_id(0) == 0, h == 0))
    n_seq = q_ref.shape[0] // ls
    o, s = _ret_block(q_ref[...], k_ref[...], v_ref[...], cos_ref[...], sin_ref[...], dm_ref[0],
                      sc_ref[0, 0], sc_ref[0, 1], [s0_ref[b, 0] for b in range(n_seq)], ls)
    y = _ret_head_out(o, g_ref[...].astype(F32), gn_ref[...])
    proj = _dot(y, wo_s[pl.ds(pl.multiple_of(h * dv, dv), dv), :])

    @pl.when(h == 0)
    def _():
        xo_ref[...] = x_ref[...] + proj

    @pl.when(h > 0)
    def _():
        xo_ref[...] += proj

    for b in range(n_seq):
        s_ref[b, 0] = s[b]


def _ret_prompt_call(r16, gn, x, w_out, layer, batch, seq, dk, dv, name):
    c = RET_STEP_TOKENS
    nt = seq // c
    d = x.shape[1]
    dmask, scales = _ret_constants(1, RET_CHUNK)
    cos, sin = _rope_tables(jnp.arange(seq, dtype=jnp.int32), dk // 2)
    tok = lambda b, t: b * nt + t
    nq, nv = H_C * dk, H_C * dv
    args = [r16, r16, r16, r16, cos, sin, dmask, scales, gn.reshape(1, dv), x, w_out]
    specs = [pl.BlockSpec((c, nq), lambda b, t: (tok(b, t), 0)),
             pl.BlockSpec((c, nq), lambda b, t: (tok(b, t), 1)),
             pl.BlockSpec((c, nv), lambda b, t: (tok(b, t), 2 * nq // nv)),
             pl.BlockSpec((c, nv), lambda b, t: (tok(b, t), 2 * nq // nv + 1)),
             pl.BlockSpec((c, dk // 2), lambda b, t: (t, 0)), pl.BlockSpec((c, dk // 2), lambda b, t: (t, 0)),
             _resident(dmask.shape), _resident(scales.shape), _resident((1, dv)),
             pl.BlockSpec((c, d), lambda b, t: (tok(b, t), 0)), _out_weight_spec(w_out, layer, 0, nv)]
    return pl.pallas_call(
        _ret_prompt_kernel,
        out_shape=[jax.ShapeDtypeStruct((batch * seq, d), F32),
                   jax.ShapeDtypeStruct((batch, H_C, dk, dv), F32)],
        grid=(batch, nt),
        in_specs=specs,
        out_specs=[pl.BlockSpec((c, d), lambda b, t: (tok(b, t), 0)),
                   pl.BlockSpec((1, H_C, dk, dv), lambda b, t: (b, 0, 0, 0))],
        scratch_shapes=[pltpu.VMEM((H_C, dk, dv), F32), pltpu.VMEM((c, nv), BF16), pltpu.VMEM((nv, d), BF16)],
        compiler_params=_params(2, 48 << 20),
        name=name,
    )(*args)


def _ret_sample_call(r16, gn, s0, row0, batch, seq, dk, dv, x, w_out, layer, name):
    nb = SAMPLE_SEQS
    rows = nb * seq
    blk0 = row0 // rows
    d = x.shape[1]
    dmask, scales = _ret_constants(nb, seq)
    pos = PAST_LEN + (jnp.arange(rows, dtype=jnp.int32) % seq)
    cos, sin = _rope_tables(pos, dk // 2)
    nq = H_C * dk

    def col(width, base):
        return pl.BlockSpec((rows, width), lambda i, h: (blk0 + i, base + h))

    state_spec = pl.BlockSpec((nb, 1, dk, dv), lambda i, h: (i, h, 0, 0))
    args = [r16, r16, r16, r16, cos, sin, dmask, scales, gn.reshape(1, dv), s0, x, w_out]
    specs = [col(dk, 0), col(dk, H_C), col(dv, 2 * nq // dv), col(dv, 2 * nq // dv + H_C),
             _resident(cos.shape), _resident(sin.shape),
             pl.BlockSpec((1, rows, rows), lambda i, h: (h, 0, 0)),
             pl.BlockSpec((1, 2, rows, LANES), lambda i, h: (h, 0, 0, 0)), _resident((1, dv)), state_spec,
             pl.BlockSpec((rows, d), lambda i, h: (blk0 + i, 0)), _out_weight_spec(w_out, layer, 0, H_C * dv)]
    return pl.pallas_call(
        functools.partial(_ret_sample_kernel, ls=seq),
        out_shape=[jax.ShapeDtypeStruct((batch * seq, d), F32),
                   jax.ShapeDtypeStruct((batch, H_C, dk, dv), F32)],
        grid=(batch // nb, H_C),
        in_specs=specs,
        out_specs=[pl.BlockSpec((rows, d), lambda i, h: (i, 0)), state_spec],
        scratch_shapes=[pltpu.VMEM((H_C * dv, d), BF16)],
        compiler_params=_params(2, 56 << 20),
        name=name,
    )(*args)


def _prepare_ab_weights(w_in, w_gk2, b_gk, dk_a, dv_a, dk_b, dv_b):
    sizes = (H_A * dk_a, H_A * dk_a, H_A * dv_a, H_A * dv_a, H_B * dk_b, H_B * dk_b, H_B * dv_b,
             GLA_RANK, H_B * dv_b)
    qa, fa, ia, ga, qb, kb, vb, lrb, gb = jnp.split(w_in, [int(v) for v in np.cumsum(sizes)[:-1]], axis=1)
    w16 = jnp.concatenate([qa, ia, ga, qb, kb, vb, gb], axis=1).astype(BF16)
    w32 = jnp.concatenate([fa, jnp.pad(lrb, ((0, 0), (0, LANES - GLA_RANK)))], axis=1).astype(BF16)
    w_gk = jnp.pad(w_gk2, ((0, LANES - GLA_RANK), (0, 0))).astype(BF16)
    return w16, w32, w_gk, b_gk.reshape(1, -1)


def kernel(x_prompt, x_sample, state_hgrn, state_gla, state_ret, p_prompt, p_sample, norm_ffn1, ffn1_w_in, ffn1_w_out, norm_mix, ab_w_in, ab_w_gk2, ab_b_gk, hgrn_lb_logits, ab_gn_hgrn, ab_gn_gla, ab_w_out, ret_w_in, ret_gn, ret_w_out, norm_ffn2, ffn2_w_in, ffn2_w_out, norm_ple, ple_w_gate, ple_w_proj, norm_final):
    bp, lp, d = x_prompt.shape
    bs, ls, _ = x_sample.shape
    depth = norm_ffn1.shape[0]
    n_prompt = bp * lp
    dk_a, dv_a = state_hgrn.shape[-2:]
    dk_b, dv_b = state_gla.shape[-2:]
    dk_c, dv_c = state_ret.shape[-2:]
    assert dk_a == dv_a == dv_b == LANES and LANES % dk_b == 0 and H_B % (LANES // dk_b) == 0
    assert state_hgrn.shape[0] == 1

    n_rows = (n_prompt, bs * ls)
    xs = (x_prompt.reshape(n_prompt, d), x_sample.reshape(bs * ls, d))
    ps = (p_prompt.reshape(depth, n_prompt, -1), p_sample.reshape(depth, bs * ls, -1))

    new_hgrn_p, new_gla_p, new_ret_p, new_hgrn_s, new_gla_s, new_ret_s = [], [], [], [], [], []
    for i in range(depth):
        j = i // 2
        x = _ffn_call(xs if i == 0 else (x,), n_rows, norm_ffn1[i], ffn1_w_in, ffn1_w_out, i, name=f"ffn1_{i}")
        if i % 2 == 0:
            w16, w32, w_gk, b_gk = _prepare_ab_weights(ab_w_in[j], ab_w_gk2[j], ab_b_gk[j], dk_a, dv_a, dk_b, dv_b)
            p16, p32 = _inproj_call(x, norm_mix[i], [w16, w32], [BF16, F32], [w16.shape[1] // 2, w32.shape[1]],
                                    name=f"ab_in_{i}")
            hgrn = ("hgrn", p16, p32, (hgrn_lb_logits,), ab_gn_hgrn[j])
            gla = ("gla", p16, p32, (w_gk, b_gk), ab_gn_gla[j])
            xa_p, sa_p = _ab_prompt_call(*hgrn, x, ab_w_out, j, 0, bp, lp, dk_a, f"hgrn_p_{i}")
            xa_s, sa_s = _ab_sample_call(*hgrn, state_hgrn[j], n_prompt, ls, x, n_prompt, ab_w_out, j, 0,
                                         f"hgrn_s_{i}")
            x_p, sb_p = _ab_prompt_call(*gla, xa_p, ab_w_out, j, 1, bp, lp, dk_b, f"gla_p_{i}")
            x_s, sb_s = _ab_sample_call(*gla, state_gla[j], n_prompt, ls, xa_s, 0, ab_w_out, j, 1, f"gla_s_{i}")
            new_hgrn_p.append(sa_p)
            new_hgrn_s.append(sa_s)
            new_gla_p.append(sb_p)
            new_gla_s.append(sb_s)
        else:
            r16 = _inproj_staged_call(x, norm_mix[i], ret_w_in, j, 2 * V7X_MXU_COLS, name=f"ret_in_{i}")
            x_p, sc_p = _ret_prompt_call(r16, ret_gn[j], x, ret_w_out, j, bp, lp, dk_c, dv_c, f"ret_p_{i}")
            x_s, sc_s = _ret_sample_call(r16, ret_gn[j], state_ret[j], n_prompt, bs, ls, dk_c, dv_c, x, ret_w_out, j,
                                         f"ret_s_{i}")
            new_ret_p.append(sc_p)
            new_ret_s.append(sc_s)
        last = i == depth - 1
        x = _ffn_call((x_p, x_s), n_rows, norm_ffn2[i], ffn2_w_in, ffn2_w_out, i,
                      ple=(ps, norm_ple[i], ple_w_gate, ple_w_proj),
                      final_gain=norm_final if last else None, split_out=last, name=f"ffn2_{i}")

    y_prompt = x[0].reshape(bp, lp, d)
    y_sample = x[1].reshape(bs, ls, d)
    def layers(states):
        return states[0][None] if len(states) == 1 else jnp.stack(states)

    return (y_prompt, y_sample, layers(new_hgrn_p), layers(new_gla_p), layers(new_ret_p),
            layers(new_hgrn_s), layers(new_gla_s), layers(new_ret_s))
```

```python
import functools
import math

import numpy as np
import jax
import jax.numpy as jnp
from jax import lax
from jax.experimental import pallas as pl
from jax.experimental.pallas import tpu as pltpu

F32 = jnp.float32
BF16 = jnp.bfloat16

H_A = 4
H_B = 4
H_C = 4
GLA_RANK = 16
GLA_GATE_NORM = 16.0
ROPE_THETA = 10000.0
PAST_LEN = 16384
EPS = 1e-6
LOG2_E = math.log2(math.e)

LANES = 128
V7X_MXU_COLS = 256
V7X_VMEM_BUDGET_BYTES = 56 * 1024 * 1024

ROW_TILE = 512
GLA_CHUNK = 128
GLA_STEP_TOKENS = 1024
RET_CHUNK = 256
RET_STEP_TOKENS = 512
SAMPLE_SEQS = 16


def _params(n_axes, vmem_bytes):
    return pltpu.CompilerParams(dimension_semantics=("arbitrary",) * n_axes,
                                vmem_limit_bytes=min(int(vmem_bytes), V7X_VMEM_BUDGET_BYTES))


def _resident(shape):
    nd = len(shape)
    return pl.BlockSpec(shape, lambda *_: (0,) * nd, pipeline_mode=pl.Buffered(1))


def _resident_layer(shape, layer):
    nd = len(shape)
    return pl.BlockSpec((1,) + tuple(shape[1:]), lambda *_: (layer,) + (0,) * (nd - 1),
                        pipeline_mode=pl.Buffered(1))


def _row_maps(prompt_tiles, lead=0):
    stacked = lambda i: (jnp.maximum(i - lead, 0), 0)
    prompt = lambda i: (jnp.clip(i - lead, 0, prompt_tiles - 1), 0)
    sample = lambda i: (jnp.maximum(i - lead - prompt_tiles, 0), 0)
    return stacked, prompt, sample


def _rms(x, g):
    return x * lax.rsqrt(jnp.mean(x * x, axis=-1, keepdims=True) + EPS) * g


def _silu(x):
    return x * jax.nn.sigmoid(x)


def _dot(a, b):
    return jnp.dot(a, b, preferred_element_type=F32)


def _dot_nt(a, b):
    return lax.dot_general(a, b, (((1,), (1,)), ((), ())), preferred_element_type=F32)


def _dot_tn(a, b):
    return lax.dot_general(a, b, (((0,), (0,)), ((), ())), preferred_element_type=F32)


def _pick(is_prompt, refs, index=None):
    vals = [r[...] if index is None else r[index] for r in refs]
    return vals[0] if len(vals) == 1 else jnp.where(is_prompt, vals[0], vals[1])


def _ffn_kernel(*refs, n_x, has_ple, has_final, n_o, prompt_tiles, stage_steps, gate_steps):
    it = iter(refs)
    x_refs = [next(it) for _ in range(n_x)]
    g_ref, win_ref, wout_ref = next(it), next(it), next(it)
    if has_ple:
        p_refs = [next(it), next(it)]
        gp_ref, wg_ref, wp_ref = next(it), next(it), next(it)
    if has_final:
        gf_ref = next(it)
    o_refs = [next(it) for _ in range(n_o)]
    win_s, wout_s, act_ref = next(it), next(it), next(it)
    if has_ple:
        wg_s, wp_s = next(it), next(it)
    i = pl.program_id(0)
    cols = win_s.shape[2]
    n_half = win_s.shape[0] // 2

    @pl.when(i < stage_steps)
    def _():
        blk = win_ref[0]
        win_s[2 * i] = blk[:, :cols].astype(BF16)
        win_s[2 * i + 1] = blk[:, cols:].astype(BF16)
        wout_s[pl.ds(pl.multiple_of(i * cols, cols), cols), :] = wout_ref[0].astype(BF16)

    if has_ple:
        rows = wg_ref.shape[1]

        @pl.when(i < gate_steps)
        def _():
            wg_s[pl.ds(pl.multiple_of(i * rows, rows), rows), :] = wg_ref[0].astype(BF16)

        @pl.when(i == 0)
        def _():
            wp_s[...] = wp_ref[0].astype(BF16)

    @pl.when(i >= stage_steps)
    def _():
        is_prompt = i - stage_steps < prompt_tiles
        x = _pick(is_prompt, x_refs)
        xn = _rms(x, g_ref[...]).astype(BF16)
        for j in range(n_half):
            a = _dot(xn, win_s[j])
            b = _dot(xn, win_s[n_half + j])
            act_ref[:, j * cols:(j + 1) * cols] = (_silu(a) * b).astype(BF16)
        x = x + 0.5 * _dot(act_ref[...], wout_s[...])
        if has_ple:
            gate = jax.nn.sigmoid(_dot(_rms(x, gp_ref[...]).astype(BF16), wg_s[...]))
            x = x + gate * _dot(_pick(is_prompt, p_refs, 0).astype(BF16), wp_s[...])
        if has_final:
            x = _rms(x, gf_ref[...])
        if n_o == 1:
            o_refs[0][...] = x
        else:
            @pl.when(is_prompt)
            def _():
                o_refs[0][...] = x

            @pl.when(jnp.logical_not(is_prompt))
            def _():
                o_refs[1][...] = x


def _ffn_call(xs, n_rows, gain, w_in, w_out, layer, ple=None, final_gain=None, split_out=False, name="ffn"):
    d = xs[0].shape[1]
    d_ff = w_out.shape[1]
    tm = ROW_TILE
    cols = V7X_MXU_COLS
    assert n_rows[0] % tm == 0 and n_rows[1] % tm == 0 and d_ff % cols == 0
    pt, st = n_rows[0] // tm, n_rows[1] // tm
    stage = d_ff // cols
    stacked, prompt, sample = _row_maps(pt, stage)
    pair = [prompt, sample]
    last = stage - 1
    args = [*xs, gain.reshape(1, d), w_in, w_out]
    specs = [pl.BlockSpec((tm, d), m) for m in ([stacked] if len(xs) == 1 else pair)]
    specs += [_resident((1, d)),
              pl.BlockSpec((1, d, 2 * cols), lambda i: (layer, 0, jnp.minimum(i, last))),
              pl.BlockSpec((1, cols, d), lambda i: (layer, jnp.minimum(i, last), 0))]
    scratch = [pltpu.VMEM((2 * stage, d, cols), BF16), pltpu.VMEM((d_ff, d), BF16), pltpu.VMEM((tm, d_ff), BF16)]
    gate_steps = 0
    if ple is not None:
        ps, gp, wg, wp = ple
        gate_rows = LANES
        gate_steps = d // gate_rows
        assert gate_steps <= stage
        args += [*ps, gp.reshape(1, d), wg, wp]
        specs += [pl.BlockSpec((1, tm, ps[0].shape[2]), lambda i, m=m: (layer, *m(i))) for m in pair]
        specs += [_resident((1, d)),
                  pl.BlockSpec((1, gate_rows, d), lambda i: (layer, jnp.minimum(i, gate_steps - 1), 0)),
                  _resident_layer(wp.shape, layer)]
        scratch += [pltpu.VMEM((d, d), BF16), pltpu.VMEM(wp.shape[1:], BF16)]
    if final_gain is not None:
        args.append(final_gain.reshape(1, d))
        specs.append(_resident((1, d)))
    if split_out:
        out_shape = [jax.ShapeDtypeStruct((r, d), F32) for r in n_rows]
        out_specs = [pl.BlockSpec((tm, d), m) for m in pair]
    else:
        out_shape = jax.ShapeDtypeStruct((sum(n_rows), d), F32)
        out_specs = pl.BlockSpec((tm, d), stacked)
    return pl.pallas_call(
        functools.partial(_ffn_kernel, n_x=len(xs), has_ple=ple is not None, has_final=final_gain is not None,
                          n_o=2 if split_out else 1, prompt_tiles=pt, stage_steps=stage, gate_steps=gate_steps),
        out_shape=out_shape,
        grid=(stage + pt + st,),
        in_specs=specs,
        out_specs=out_specs,
        scratch_shapes=scratch,
        compiler_params=_params(1, V7X_VMEM_BUDGET_BYTES),
        name=name,
    )(*args)


def _inproj_kernel(*refs, n_out, chunk_cols):
    x_ref, g_ref = refs[0], refs[1]
    w_refs = refs[2:2 + n_out]
    o_refs = refs[2 + n_out:2 + 2 * n_out]
    xn = _rms(x_ref[...], g_ref[...]).astype(BF16)
    for w_ref, o_ref, cw in zip(w_refs, o_refs, chunk_cols):
        for c in range(w_ref.shape[1] // cw):
            o_ref[:, c * cw:(c + 1) * cw] = _dot(xn, w_ref[:, c * cw:(c + 1) * cw]).astype(o_ref.dtype)


def _inproj_call(x, gain, weights, out_dtypes, chunk_cols, name):
    n, d = x.shape
    tm = 2 * ROW_TILE
    assert n % tm == 0
    row = lambda i: (i, 0)
    specs = [pl.BlockSpec((tm, d), row), _resident((1, d))] + [_resident(w.shape) for w in weights]
    out_shape = [jax.ShapeDtypeStruct((n, w.shape[1]), dt) for w, dt in zip(weights, out_dtypes)]
    out_specs = [pl.BlockSpec((tm, w.shape[1]), row) for w in weights]
    vmem = (sum(2 * w.size for w in weights) + 4 * tm * d * 4
            + sum(2 * tm * w.shape[1] * jnp.dtype(dt).itemsize for w, dt in zip(weights, out_dtypes))
            + 2 * tm * max(chunk_cols) * 4 + (4 << 20))
    return pl.pallas_call(
        functools.partial(_inproj_kernel, n_out=len(weights), chunk_cols=tuple(chunk_cols)),
        out_shape=out_shape,
        grid=(n // tm,),
        in_specs=specs,
        out_specs=out_specs,
        compiler_params=_params(1, vmem),
        name=name,
    )(x, gain.reshape(1, d), *weights)


def _inproj_staged_kernel(x_ref, g_ref, w_ref, o_ref, w_s, *, stage_steps):
    i = pl.program_id(0)
    cols = w_s.shape[2]

    @pl.when(i < stage_steps)
    def _():
        w_s[i] = w_ref[0].astype(BF16)

    @pl.when(i >= stage_steps)
    def _():
        xn = _rms(x_ref[...], g_ref[...]).astype(BF16)
        for c in range(stage_steps):
            o_ref[:, c * cols:(c + 1) * cols] = _dot(xn, w_s[c]).astype(o_ref.dtype)


def _inproj_staged_call(x, gain, w, layer, cols, name):
    n, d = x.shape
    n_out = w.shape[2]
    tm = ROW_TILE
    assert n_out % cols == 0
    stage = n_out // cols
    row = lambda i: (jnp.maximum(i - stage, 0), 0)
    vmem = 2 * d * n_out + 2 * d * cols * 4 + 4 * tm * d * 4 + 2 * tm * n_out * 2 + 2 * tm * cols * 4 + (4 << 20)
    return pl.pallas_call(
        functools.partial(_inproj_staged_kernel, stage_steps=stage),
        out_shape=jax.ShapeDtypeStruct((n, n_out), BF16),
        grid=(stage + n // tm,),
        in_specs=[pl.BlockSpec((tm, d), row), _resident((1, d)),
                  pl.BlockSpec((1, d, cols), lambda i: (layer, 0, jnp.minimum(i, stage - 1)))],
        out_specs=pl.BlockSpec((tm, n_out), row),
        scratch_shapes=[pltpu.VMEM((stage, d, cols), BF16)],
        compiler_params=_params(1, vmem),
        name=name,
    )(x, gain.reshape(1, d), w)


GLA_MATMUL_LEVELS = (1, 2)


def _gla_constants(n_seq, ls, width):
    c = n_seq * ls
    nlev = int(round(math.log2(ls)))
    assert 2 ** nlev == ls and nlev > max(GLA_MATMUL_LEVELS)
    t = np.arange(c)[:, None]
    i = np.arange(c)[None, :]
    w_rows = [((t // ls) == (i // ls)) & (i <= t)]
    masks, uppers = [], []
    for l in range(nlev):
        m = 2 ** l
        upper = ((t // m) % 2) == 1
        r = (t // (2 * m)) * (2 * m) + m - 1
        if l in GLA_MATMUL_LEVELS:
            w_rows.append(np.where(upper, (i > r) & (i <= t), (i > t) & (i <= r)))
        masks.append(((t // (2 * m)) == (i // (2 * m))) & upper & (((i // m) % 2) == 0))
        uppers.append(np.broadcast_to(upper, (c, width)))
    w = jnp.asarray(np.concatenate(w_rows, 0).astype(np.float32), BF16)
    return w, jnp.asarray(np.stack(masks).astype(np.float32)), jnp.asarray(np.stack(uppers).astype(np.float32))


def _gla_block(q, k, v, g, st_list, w_ref, m_ref, u_ref, ls, transposed_state, dk):
    c, width = q.shape
    nh = v.shape[1] // LANES
    hpg = LANES // dk
    n_groups = width // LANES
    n_seq = c // ls
    nlev = m_ref.shape[0]
    lane = lax.broadcasted_iota(jnp.int32, (1, LANES), 1)
    own = [jnp.where((lane >= i * dk) & (lane < (i + 1) * dk), 1.0, 0.0) for i in range(hpg)]
    g_hi32 = g.astype(BF16).astype(F32)
    g_lo32 = g - g_hi32
    g_hi, g_lo = g_hi32.astype(BF16), g_lo32.astype(BF16)
    ex2 = _dot(w_ref[...], jnp.concatenate([g_hi, g_lo], axis=1))
    ex = ex2[:, :width] + ex2[:, width:]
    b = ex[:c]

    def row(r, n):
        return jnp.broadcast_to(b[r:r + 1, :], (n, width))

    b_end = jnp.concatenate([row((s + 1) * ls - 1, ls) for s in range(n_seq)], axis=0) if n_seq > 1 else row(c - 1, c)
    e_cum = jnp.exp2(b)
    qe = q * e_cum
    kr = k * jnp.exp2(b_end - b)

    att = [None] * nh
    for l in range(nlev):
        m = 2 ** l
        if l == 0:
            x = jnp.where(u_ref[l] > 0.5, q * jnp.exp2(g), k)
        elif l in GLA_MATMUL_LEVELS:
            j = 1 + GLA_MATMUL_LEVELS.index(l)
            x = jnp.where(u_ref[l] > 0.5, q, k) * jnp.exp2(ex[j * c:(j + 1) * c])
        else:
            pieces = []
            for p in range(c // (2 * m)):
                lo, mid, hi = p * 2 * m, p * 2 * m + m, (p + 1) * 2 * m
                pivot = row(mid - 1, m)
                pieces.append(k[lo:mid] * jnp.exp2(pivot - b[lo:mid]))
                pieces.append(q[mid:hi] * jnp.exp2(b[mid:hi] - pivot))
            x = jnp.concatenate(pieces, axis=0)
        x = x.astype(BF16)
        mask = m_ref[l] > 0.5
        for h in range(nh):
            grp, i = divmod(h, hpg)
            xg = x[:, grp * LANES:(grp + 1) * LANES]
            xl = xg if hpg == 1 else xg * own[i].astype(BF16)
            att[h] = jnp.where(mask, _dot_nt(xl, xg), 0.0 if att[h] is None else att[h])

    qk = q * k
    outs = [[None] * n_seq for _ in range(nh)]
    st_new = [[None] * n_groups for _ in range(n_seq)]
    for grp in range(n_groups):
        gl = slice(grp * LANES, (grp + 1) * LANES)
        heads = []
        for i in range(hpg):
            h = grp * hpg + i
            vh = v[:, h * LANES:(h + 1) * LANES]
            sel = 1.0 if hpg == 1 else own[i]
            o = (_dot(att[h].astype(BF16), vh.astype(BF16))
                 + jnp.sum(qk[:, gl] * sel, axis=1, keepdims=True) * vh)
            heads.append((h, vh, o, qe[:, gl] * sel, kr[:, gl] * sel))
        for s in range(n_seq):
            rows = slice(s * ls, (s + 1) * ls)
            st = st_list[s][grp]
            if transposed_state:
                new = e_cum[(s + 1) * ls - 1:(s + 1) * ls, gl] * st
            else:
                g2 = jnp.concatenate([g_hi32[rows, gl], g_lo32[rows, gl]], axis=0).astype(BF16)
                new = jnp.exp2(_dot_tn(g2, jnp.ones((2 * ls, LANES), BF16))) * st
            for h, vh, o, qe_h, kr_h in heads:
                if transposed_state:
                    outs[h][s] = o[rows] + _dot_nt(qe_h[rows].astype(BF16), st.astype(BF16))
                    new = new + _dot_tn(vh[rows].astype(BF16), kr_h[rows].astype(BF16))
                else:
                    outs[h][s] = o[rows] + _dot(qe_h[rows].astype(BF16), st.astype(BF16))
                    new = new + _dot_tn(kr_h[rows].astype(BF16), vh[rows].astype(BF16))
            st_new[s][grp] = new
    outs = [parts[0] if n_seq == 1 else jnp.concatenate(parts, axis=0) for parts in outs]
    return outs, st_new


def _hgrn_inputs(refs, rows, q_scale):
    del q_scale
    qa_ref, ia_ref, ga_ref, fa_ref, lg_ref = refs
    lg = lg_ref[...]
    ex = jnp.exp(lg - jnp.max(lg, axis=0, keepdims=True))
    lb = ex[0:1] / jnp.sum(ex, axis=0, keepdims=True)
    sig = jax.nn.sigmoid(fa_ref[rows, :])
    q = _silu(qa_ref[rows, :].astype(F32))
    k = (1.0 - lb) * (1.0 - sig)
    g = jnp.log(lb + (1.0 - lb) * sig) * LOG2_E
    return q, k, ia_ref[rows, :].astype(F32), g, ga_ref[rows, :].astype(F32)


def _gla_inputs(refs, rows, q_scale):
    qb_ref, kb_ref, vb_ref, gb_ref, lr_ref, wgk_ref, bgk_ref = refs
    z = _dot(lr_ref[rows, :].astype(BF16), wgk_ref[...]) + bgk_ref[...]
    g = (jnp.minimum(z, 0.0) - jnp.log1p(jnp.exp(-jnp.abs(z)))) * (LOG2_E / GLA_GATE_NORM)
    q = qb_ref[rows, :].astype(F32) * q_scale
    return q, kb_ref[rows, :].astype(F32), vb_ref[rows, :].astype(F32), g, gb_ref[rows, :].astype(F32)


def _head_out(outs, gate, gn):
    ys = [_rms(o, gn) * _silu(gate[:, h * LANES:(h + 1) * LANES]) for h, o in enumerate(outs)]
    return jnp.concatenate(ys, axis=1).astype(BF16)


def _stage_out_weight(wo_ref, wo_s, first):
    @pl.when(first)
    def _():
        wo_s[...] = wo_ref[0].astype(BF16)


def _ab_prompt_kernel(*refs, kind, n_in, chunk, q_scale):
    in_refs = refs[:n_in]
    gn_ref, w_ref, m_ref, u_ref, x_ref, wo_ref, xo_ref, s_ref, st_ref, y_s, wo_s = refs[n_in:]
    t = pl.program_id(1)
    n_groups = st_ref.shape[0]
    nh, dk = s_ref.shape[1:3]
    hpg = nh // n_groups
    _stage_out_weight(wo_ref, wo_s, jnp.logical_and(pl.program_id(0) == 0, t == 0))

    @pl.when(t == 0)
    def _():
        st_ref[...] = jnp.zeros_like(st_ref)

    load = _hgrn_inputs if kind == "hgrn" else _gla_inputs
    st = [st_ref[j] for j in range(n_groups)]
    for c in range(y_s.shape[0] // chunk):
        rows = slice(c * chunk, (c + 1) * chunk)
        q, k, v, g, gate = load(in_refs, rows, q_scale)
        outs, (st,) = _gla_block(q, k, v, g, [st], w_ref, m_ref, u_ref, chunk, True, dk)
        y_s[rows, :] = _head_out(outs, gate, gn_ref[...])
    for j in range(n_groups):
        st_ref[j] = st[j]
    xo_ref[...] = x_ref[...] + _dot(y_s[...], wo_s[...])

    @pl.when(t == pl.num_programs(1) - 1)
    def _():
        for h in range(nh):
            grp, i = divmod(h, hpg)
            s_ref[0, h] = st_ref[grp].T[i * dk:(i + 1) * dk, :]


def _ab_sample_kernel(*refs, kind, n_in, ls, q_scale):
    in_refs = refs[:n_in]
    gn_ref, w_ref, m_ref, u_ref, s0_ref, x_ref, wo_ref, xo_ref, s_ref, wo_s = refs[n_in:]
    _stage_out_weight(wo_ref, wo_s, pl.program_id(0) == 0)
    load = _hgrn_inputs if kind == "hgrn" else _gla_inputs
    n_seq, nh, dk = s0_ref.shape[:3]
    hpg = LANES // dk
    q, k, v, g, gate = load(in_refs, slice(None), q_scale)
    st0 = [[jnp.concatenate([s0_ref[s, j * hpg + i] for i in range(hpg)], axis=0) if hpg > 1 else s0_ref[s, j]
            for j in range(nh // hpg)] for s in range(n_seq)]
    outs, st = _gla_block(q, k, v, g, st0, w_ref, m_ref, u_ref, ls, False, dk)
    xo_ref[...] = x_ref[...] + _dot(_head_out(outs, gate, gn_ref[...]), wo_s[...])
    for s in range(n_seq):
        for h in range(nh):
            grp, i = divmod(h, hpg)
            s_ref[s, h] = st[s][grp][i * dk:(i + 1) * dk, :]


def _ab_specs(kind, p16, p32, extra, tok_block, tok_index, nh, dk):
    wide = nh * LANES
    kw = nh * dk

    def col(start, w):
        assert start % w == 0
        return pl.BlockSpec((tok_block, w), lambda *ids: (tok_index(*ids), start // w))

    if kind == "hgrn":
        (logits,) = extra
        args = [p16, p16, p16, p32, logits]
        specs = [col(0, wide), col(wide, wide), col(2 * wide, wide), col(0, kw), _resident(logits.shape)]
    else:
        w_gk, b_gk = extra
        a16 = 3 * H_A * LANES
        args = [p16, p16, p16, p16, p32, w_gk, b_gk]
        specs = [col(a16, kw), col(a16 + kw, kw), col(a16 + 2 * kw, wide), col(a16 + 2 * kw + wide, wide),
                 col(H_A * LANES, LANES), _resident(w_gk.shape), _resident(b_gk.shape)]
    return args, specs


def _out_weight_spec(w_out, layer, part, rows):
    return pl.BlockSpec((1, rows, w_out.shape[2]), lambda *_: (layer, part, 0), pipeline_mode=pl.Buffered(1))


def _ab_prompt_call(kind, p16, p32, extra, gn, x, w_out, layer, part, batch, seq, dk, name):
    nh = H_A if kind == "hgrn" else H_B
    d = x.shape[1]
    tt = GLA_STEP_TOKENS
    nt = seq // tt
    tok = lambda b, t: b * nt + t
    consts = _gla_constants(1, GLA_CHUNK, nh * dk)
    args, specs = _ab_specs(kind, p16, p32, extra, tt, tok, nh, dk)
    n_in = len(args)
    args += [gn.reshape(1, LANES), *consts, x, w_out]
    specs += [_resident((1, LANES))] + [_resident(c.shape) for c in consts]
    specs += [pl.BlockSpec((tt, d), lambda b, t: (tok(b, t), 0)), _out_weight_spec(w_out, layer, part, nh * LANES)]
    return pl.pallas_call(
        functools.partial(_ab_prompt_kernel, kind=kind, n_in=n_in, chunk=GLA_CHUNK, q_scale=dk ** -0.5),
        out_shape=[jax.ShapeDtypeStruct((batch * seq, d), F32),
                   jax.ShapeDtypeStruct((batch, nh, dk, LANES), F32)],
        grid=(batch, nt),
        in_specs=specs,
        out_specs=[pl.BlockSpec((tt, d), lambda b, t: (tok(b, t), 0)),
                   pl.BlockSpec((1, nh, dk, LANES), lambda b, t: (b, 0, 0, 0))],
        scratch_shapes=[pltpu.VMEM((nh * dk // LANES, LANES, LANES), F32), pltpu.VMEM((tt, nh * LANES), BF16),
                        pltpu.VMEM((nh * LANES, d), BF16)],
        compiler_params=_params(2, 40 << 20),
        name=name,
    )(*args)


def _ab_sample_call(kind, p16, p32, extra, gn, s0, row0, seq, x, x_row0, w_out, layer, part, name):
    batch, nh, dk, _ = s0.shape
    d = x.shape[1]
    nb = SAMPLE_SEQS
    rows = nb * seq
    blk0, xblk0 = row0 // rows, x_row0 // rows
    consts = _gla_constants(nb, seq, nh * dk)
    args, specs = _ab_specs(kind, p16, p32, extra, rows, lambda i: blk0 + i, nh, dk)
    n_in = len(args)
    args += [gn.reshape(1, LANES), *consts, s0, x, w_out]
    state_spec = pl.BlockSpec((nb, nh, dk, LANES), lambda i: (i, 0, 0, 0))
    specs += [_resident((1, LANES))] + [_resident(c.shape) for c in consts] + [state_spec]
    specs += [pl.BlockSpec((rows, d), lambda i: (xblk0 + i, 0)), _out_weight_spec(w_out, layer, part, nh * LANES)]
    return pl.pallas_call(
        functools.partial(_ab_sample_kernel, kind=kind, n_in=n_in, ls=seq, q_scale=dk ** -0.5),
        out_shape=[jax.ShapeDtypeStruct((batch * seq, d), F32),
                   jax.ShapeDtypeStruct((batch, nh, dk, LANES), F32)],
        grid=(batch // nb,),
        in_specs=specs,
        out_specs=[pl.BlockSpec((rows, d), lambda i: (i, 0)), state_spec],
        scratch_shapes=[pltpu.VMEM((nh * LANES, d), BF16)],
        compiler_params=_params(1, 40 << 20),
        name=name,
    )(*args)


def _ret_constants(n_seq, ls):
    c = n_seq * ls
    log_gamma = np.log1p(-np.exp2(-5.0 - np.arange(H_C, dtype=np.float64)))[:, None, None]
    t = np.arange(c)[:, None]
    s = np.arange(c)[None, :]
    causal = ((t // ls) == (s // ls)) & (s <= t)
    dmask = np.where(causal, np.exp(log_gamma * np.where(causal, t - s, 0)), 0.0)
    pos = (np.arange(c) % ls)[None, :, None]
    scales = np.stack([np.exp(log_gamma * (pos + 1)), np.exp(log_gamma * (ls - 1 - pos))], axis=1)
    scales = np.broadcast_to(scales, (H_C, 2, c, LANES))
    return jnp.asarray(dmask, F32), jnp.asarray(scales, F32)


def _rope_tables(positions, half):
    inv = ROPE_THETA ** (-jnp.arange(half, dtype=F32) / half)
    ang = positions.astype(F32)[:, None] * inv[None, :]
    return jnp.cos(ang), jnp.sin(ang)


def _rope(x, cos, sin):
    half = x.shape[1] // 2
    x1, x2 = x[:, :half], x[:, half:]
    return jnp.concatenate([x1 * cos - x2 * sin, x2 * cos + x1 * sin], axis=1)


def _ret_block(q16, k16, v, cos, sin, dmask, q_scale, k_scale, s_list, ls):
    dk = q16.shape[1]
    q = _rope(q16.astype(F32), cos, sin)
    k = _rope(k16.astype(F32), cos, sin) * (dk ** -0.5)
    reps = dk // LANES
    q_sc = jnp.concatenate([q_scale] * reps, axis=1)
    k_sc = jnp.concatenate([k_scale] * reps, axis=1)
    att = _dot_nt(q.astype(BF16), k.astype(BF16)) * dmask
    o = _dot(att.astype(BF16), v)
    qs = q * q_sc
    ks = k * k_sc
    decay = q_scale[ls - 1:ls, 0:1]
    n_seq = q.shape[0] // ls
    v32 = v if n_seq == 1 else v.astype(F32)
    o_parts, s_new = [], []
    for b in range(n_seq):
        rows = slice(b * ls, (b + 1) * ls)
        s = s_list[b]
        o_parts.append(o[rows] + _dot(qs[rows].astype(BF16), s.astype(BF16)))
        s_new.append(decay * s + _dot_tn(ks[rows].astype(BF16), v32[rows].astype(BF16)))
    o = o_parts[0] if n_seq == 1 else jnp.concatenate(o_parts, axis=0)
    return o, s_new


def _ret_head_out(o, gate, gn):
    oc = o - jnp.mean(o, axis=-1, keepdims=True)
    y = oc * lax.rsqrt(jnp.mean(oc * oc, axis=-1, keepdims=True) + EPS) * gn
    return (y * _silu(gate)).astype(BF16)


def _ret_prompt_kernel(q_ref, k_ref, v_ref, g_ref, cos_ref, sin_ref, dm_ref, sc_ref, gn_ref, x_ref, wo_ref,
                       xo_ref, s_ref, st_ref, y_s, wo_s):
    t = pl.program_id(1)
    nh, dk, dv = st_ref.shape
    c = dm_ref.shape[1]
    _stage_out_weight(wo_ref, wo_s, jnp.logical_and(pl.program_id(0) == 0, t == 0))

    @pl.when(t == 0)
    def _():
        st_ref[...] = jnp.zeros_like(st_ref)

    for h in range(nh):
        s = st_ref[h]
        for j in range(q_ref.shape[0] // c):
            rows = slice(j * c, (j + 1) * c)
            o, (s,) = _ret_block(q_ref[rows, h * dk:(h + 1) * dk], k_ref[rows, h * dk:(h + 1) * dk],
                                 v_ref[rows, h * dv:(h + 1) * dv], cos_ref[rows, :], sin_ref[rows, :], dm_ref[h],
                                 sc_ref[h, 0], sc_ref[h, 1], [s], c)
            y_s[rows, h * dv:(h + 1) * dv] = _ret_head_out(o, g_ref[rows, h * dv:(h + 1) * dv].astype(F32),
                                                           gn_ref[...])
        st_ref[h] = s
    xo_ref[...] = x_ref[...] + _dot(y_s[...], wo_s[...])

    @pl.when(t == pl.num_programs(1) - 1)
    def _():
        s_ref[0] = st_ref[...]


def _ret_sample_kernel(q_ref, k_ref, v_ref, g_ref, cos_ref, sin_ref, dm_ref, sc_ref, gn_ref, s0_ref, x_ref,
                       wo_ref, xo_ref, s_ref, wo_s, *, ls):
    h = pl.program_id(1)
    dv = v_ref.shape[1]
    _stage_out_weight(wo_ref, wo_s, jnp.logical_and(pl.program_id(0) == 0, h == 0))
    n_seq = q_ref.shape[0] // ls
    o, s = _ret_block(q_ref[...], k_ref[...], v_ref[...], cos_ref[...], sin_ref[...], dm_ref[0],
                      sc_ref[0, 0], sc_ref[0, 1], [s0_ref[b, 0] for b in range(n_seq)], ls)
    y = _ret_head_out(o, g_ref[...].astype(F32), gn_ref[...])
    proj = _dot(y, wo_s[pl.ds(pl.multiple_of(h * dv, dv), dv), :])

    @pl.when(h == 0)
    def _():
        xo_ref[...] = x_ref[...] + proj

    @pl.when(h > 0)
    def _():
        xo_ref[...] += proj

    for b in range(n_seq):
        s_ref[b, 0] = s[b]


def _ret_prompt_call(r16, gn, x, w_out, layer, batch, seq, dk, dv, name):
    c = RET_STEP_TOKENS
    nt = seq // c
    d = x.shape[1]
    dmask, scales = _ret_constants(1, RET_CHUNK)
    cos, sin = _rope_tables(jnp.arange(seq, dtype=jnp.int32), dk // 2)
    tok = lambda b, t: b * nt + t
    nq, nv = H_C * dk, H_C * dv
    args = [r16, r16, r16, r16, cos, sin, dmask, scales, gn.reshape(1, dv), x, w_out]
    specs = [pl.BlockSpec((c, nq), lambda b, t: (tok(b, t), 0)),
             pl.BlockSpec((c, nq), lambda b, t: (tok(b, t), 1)),
             pl.BlockSpec((c, nv), lambda b, t: (tok(b, t), 2 * nq // nv)),
             pl.BlockSpec((c, nv), lambda b, t: (tok(b, t), 2 * nq // nv + 1)),
             pl.BlockSpec((c, dk // 2), lambda b, t: (t, 0)), pl.BlockSpec((c, dk // 2), lambda b, t: (t, 0)),
             _resident(dmask.shape), _resident(scales.shape), _resident((1, dv)),
             pl.BlockSpec((c, d), lambda b, t: (tok(b, t), 0)), _out_weight_spec(w_out, layer, 0, nv)]
    return pl.pallas_call(
        _ret_prompt_kernel,
        out_shape=[jax.ShapeDtypeStruct((batch * seq, d), F32),
                   jax.ShapeDtypeStruct((batch, H_C, dk, dv), F32)],
        grid=(batch, nt),
        in_specs=specs,
        out_specs=[pl.BlockSpec((c, d), lambda b, t: (tok(b, t), 0)),
                   pl.BlockSpec((1, H_C, dk, dv), lambda b, t: (b, 0, 0, 0))],
        scratch_shapes=[pltpu.VMEM((H_C, dk, dv), F32), pltpu.VMEM((c, nv), BF16), pltpu.VMEM((nv, d), BF16)],
        compiler_params=_params(2, 48 << 20),
        name=name,
    )(*args)


def _ret_sample_call(r16, gn, s0, row0, batch, seq, dk, dv, x, w_out, layer, name):
    nb = SAMPLE_SEQS
    rows = nb * seq
    blk0 = row0 // rows
    d = x.shape[1]
    dmask, scales = _ret_constants(nb, seq)
    pos = PAST_LEN + (jnp.arange(rows, dtype=jnp.int32) % seq)
    cos, sin = _rope_tables(pos, dk // 2)
    nq = H_C * dk

    def col(width, base):
        return pl.BlockSpec((rows, width), lambda i, h: (blk0 + i, base + h))

    state_spec = pl.BlockSpec((nb, 1, dk, dv), lambda i, h: (i, h, 0, 0))
    args = [r16, r16, r16, r16, cos, sin, dmask, scales, gn.reshape(1, dv), s0, x, w_out]
    specs = [col(dk, 0), col(dk, H_C), col(dv, 2 * nq // dv), col(dv, 2 * nq // dv + H_C),
             _resident(cos.shape), _resident(sin.shape),
             pl.BlockSpec((1, rows, rows), lambda i, h: (h, 0, 0)),
             pl.BlockSpec((1, 2, rows, LANES), lambda i, h: (h, 0, 0, 0)), _resident((1, dv)), state_spec,
             pl.BlockSpec((rows, d), lambda i, h: (blk0 + i, 0)), _out_weight_spec(w_out, layer, 0, H_C * dv)]
    return pl.pallas_call(
        functools.partial(_ret_sample_kernel, ls=seq),
        out_shape=[jax.ShapeDtypeStruct((batch * seq, d), F32),
                   jax.ShapeDtypeStruct((batch, H_C, dk, dv), F32)],
        grid=(batch // nb, H_C),
        in_specs=specs,
        out_specs=[pl.BlockSpec((rows, d), lambda i, h: (i, 0)), state_spec],
        scratch_shapes=[pltpu.VMEM((H_C * dv, d), BF16)],
        compiler_params=_params(2, 56 << 20),
        name=name,
    )(*args)


def _prepare_ab_weights(w_in, w_gk2, b_gk, dk_a, dv_a, dk_b, dv_b):
    sizes = (H_A * dk_a, H_A * dk_a, H_A * dv_a, H_A * dv_a, H_B * dk_b, H_B * dk_b, H_B * dv_b,
             GLA_RANK, H_B * dv_b)
    qa, fa, ia, ga, qb, kb, vb, lrb, gb = jnp.split(w_in, [int(v) for v in np.cumsum(sizes)[:-1]], axis=1)
    w16 = jnp.concatenate([qa, ia, ga, qb, kb, vb, gb], axis=1).astype(BF16)
    w32 = jnp.concatenate([fa, jnp.pad(lrb, ((0, 0), (0, LANES - GLA_RANK)))], axis=1).astype(BF16)
    w_gk = jnp.pad(w_gk2, ((0, LANES - GLA_RANK), (0, 0))).astype(BF16)
    return w16, w32, w_gk, b_gk.reshape(1, -1)


def kernel(x_prompt, x_sample, state_hgrn, state_gla, state_ret, p_prompt, p_sample, norm_ffn1, ffn1_w_in, ffn1_w_out, norm_mix, ab_w_in, ab_w_gk2, ab_b_gk, hgrn_lb_logits, ab_gn_hgrn, ab_gn_gla, ab_w_out, ret_w_in, ret_gn, ret_w_out, norm_ffn2, ffn2_w_in, ffn2_w_out, norm_ple, ple_w_gate, ple_w_proj, norm_final):
    bp, lp, d = x_prompt.shape
    bs, ls, _ = x_sample.shape
    depth = norm_ffn1.shape[0]
    n_prompt = bp * lp
    dk_a, dv_a = state_hgrn.shape[-2:]
    dk_b, dv_b = state_gla.shape[-2:]
    dk_c, dv_c = state_ret.shape[-2:]
    assert dk_a == dv_a == dv_b == LANES and LANES % dk_b == 0 and H_B % (LANES // dk_b) == 0
    assert state_hgrn.shape[0] == 1

    n_rows = (n_prompt, bs * ls)
    xs = (x_prompt.reshape(n_prompt, d), x_sample.reshape(bs * ls, d))
    ps = (p_prompt.reshape(depth, n_prompt, -1), p_sample.reshape(depth, bs * ls, -1))

    new_hgrn_p, new_gla_p, new_ret_p, new_hgrn_s, new_gla_s, new_ret_s = [], [], [], [], [], []
    for i in range(depth):
        j = i // 2
        x = _ffn_call(xs if i == 0 else (x,), n_rows, norm_ffn1[i], ffn1_w_in, ffn1_w_out, i, name=f"ffn1_{i}")
        if i % 2 == 0:
            w16, w32, w_gk, b_gk = _prepare_ab_weights(ab_w_in[j], ab_w_gk2[j], ab_b_gk[j], dk_a, dv_a, dk_b, dv_b)
            p16, p32 = _inproj_call(x, norm_mix[i], [w16, w32], [BF16, F32], [w16.shape[1] // 2, w32.shape[1]],
                                    name=f"ab_in_{i}")
            hgrn = ("hgrn", p16, p32, (hgrn_lb_logits,), ab_gn_hgrn[j])
            gla = ("gla", p16, p32, (w_gk, b_gk), ab_gn_gla[j])
            xa_p, sa_p = _ab_prompt_call(*hgrn, x, ab_w_out, j, 0, bp, lp, dk_a, f"hgrn_p_{i}")
            xa_s, sa_s = _ab_sample_call(*hgrn, state_hgrn[j], n_prompt, ls, x, n_prompt, ab_w_out, j, 0,
                                         f"hgrn_s_{i}")
            x_p, sb_p = _ab_prompt_call(*gla, xa_p, ab_w_out, j, 1, bp, lp, dk_b, f"gla_p_{i}")
            x_s, sb_s = _ab_sample_call(*gla, state_gla[j], n_prompt, ls, xa_s, 0, ab_w_out, j, 1, f"gla_s_{i}")
            new_hgrn_p.append(sa_p)
            new_hgrn_s.append(sa_s)
            new_gla_p.append(sb_p)
            new_gla_s.append(sb_s)
        else:
            r16 = _inproj_staged_call(x, norm_mix[i], ret_w_in, j, 2 * V7X_MXU_COLS, name=f"ret_in_{i}")
            x_p, sc_p = _ret_prompt_call(r16, ret_gn[j], x, ret_w_out, j, bp, lp, dk_c, dv_c, f"ret_p_{i}")
            x_s, sc_s = _ret_sample_call(r16, ret_gn[j], state_ret[j], n_prompt, bs, ls, dk_c, dv_c, x, ret_w_out, j,
                                         f"ret_s_{i}")
            new_ret_p.append(sc_p)
            new_ret_s.append(sc_s)
        last = i == depth - 1
        x = _ffn_call((x_p, x_s), n_rows, norm_ffn2[i], ffn2_w_in, ffn2_w_out, i,
                      ple=(ps, norm_ple[i], ple_w_gate, ple_w_proj),
                      final_gain=norm_final if last else None, split_out=last, name=f"ffn2_{i}")

    y_prompt = x[0].reshape(bp, lp, d)
    y_sample = x[1].reshape(bs, ls, d)
    def layers(states):
        return states[0][None] if len(states) == 1 else jnp.stack(states)

    return (y_prompt, y_sample, layers(new_hgrn_p), layers(new_gla_p), layers(new_ret_p),
            layers(new_hgrn_s), layers(new_gla_s), layers(new_ret_s))
```

```python
import functools
import math

import numpy as np
import jax
import jax.numpy as jnp
from jax import lax
from jax.experimental import pallas as pl
from jax.experimental.pallas import tpu as pltpu

F32 = jnp.float32
BF16 = jnp.bfloat16

H_A = 4
H_B = 4
H_C = 4
GLA_RANK = 16
GLA_GATE_NORM = 16.0
ROPE_THETA = 10000.0
PAST_LEN = 16384
EPS = 1e-6
LOG2_E = math.log2(math.e)

LANES = 128
V7X_MXU_COLS = 256
V7X_VMEM_BUDGET_BYTES = 56 * 1024 * 1024

ROW_TILE = 512
GLA_CHUNK = 128
GLA_STEP_TOKENS = 1024
RET_CHUNK = 256
RET_STEP_TOKENS = 512
SAMPLE_SEQS = 16


def _params(n_axes, vmem_bytes):
    return pltpu.CompilerParams(dimension_semantics=("arbitrary",) * n_axes,
                                vmem_limit_bytes=min(int(vmem_bytes), V7X_VMEM_BUDGET_BYTES))


def _resident(shape):
    nd = len(shape)
    return pl.BlockSpec(shape, lambda *_: (0,) * nd, pipeline_mode=pl.Buffered(1))


def _resident_layer(shape, layer):
    nd = len(shape)
    return pl.BlockSpec((1,) + tuple(shape[1:]), lambda *_: (layer,) + (0,) * (nd - 1),
                        pipeline_mode=pl.Buffered(1))


def _row_maps(prompt_tiles, lead=0):
    stacked = lambda i: (jnp.maximum(i - lead, 0), 0)
    prompt = lambda i: (jnp.clip(i - lead, 0, prompt_tiles - 1), 0)
    sample = lambda i: (jnp.maximum(i - lead - prompt_tiles, 0), 0)
    return stacked, prompt, sample


def _rms(x, g):
    return x * lax.rsqrt(jnp.mean(x * x, axis=-1, keepdims=True) + EPS) * g


def _silu(x):
    return x * jax.nn.sigmoid(x)


def _dot(a, b):
    return jnp.dot(a, b, preferred_element_type=F32)


def _dot_nt(a, b):
    return lax.dot_general(a, b, (((1,), (1,)), ((), ())), preferred_element_type=F32)


def _dot_tn(a, b):
    return lax.dot_general(a, b, (((0,), (0,)), ((), ())), preferred_element_type=F32)


def _pick(is_prompt, refs, index=None):
    vals = [r[...] if index is None else r[index] for r in refs]
    return vals[0] if len(vals) == 1 else jnp.where(is_prompt, vals[0], vals[1])


def _ffn_kernel(*refs, n_x, has_ple, has_final, n_o, prompt_tiles, stage_steps, gate_steps):
    it = iter(refs)
    x_refs = [next(it) for _ in range(n_x)]
    g_ref, win_ref, wout_ref = next(it), next(it), next(it)
    if has_ple:
        p_refs = [next(it), next(it)]
        gp_ref, wg_ref, wp_ref = next(it), next(it), next(it)
    if has_final:
        gf_ref = next(it)
    o_refs = [next(it) for _ in range(n_o)]
    win_s, wout_s, act_ref = next(it), next(it), next(it)
    if has_ple:
        wg_s, wp_s = next(it), next(it)
    i = pl.program_id(0)
    cols = win_s.shape[2]
    n_half = win_s.shape[0] // 2

    @pl.when(i < stage_steps)
    def _():
        blk = win_ref[0]
        win_s[2 * i] = blk[:, :cols].astype(BF16)
        win_s[2 * i + 1] = blk[:, cols:].astype(BF16)
        wout_s[pl.ds(pl.multiple_of(i * cols, cols), cols), :] = wout_ref[0].astype(BF16)

    if has_ple:
        rows = wg_ref.shape[1]

        @pl.when(i < gate_steps)
        def _():
            wg_s[pl.ds(pl.multiple_of(i * rows, rows), rows), :] = wg_ref[0].astype(BF16)

        @pl.when(i == 0)
        def _():
            wp_s[...] = wp_ref[0].astype(BF16)

    @pl.when(i >= stage_steps)
    def _():
        is_prompt = i - stage_steps < prompt_tiles
        x = _pick(is_prompt, x_refs)
        xn = _rms(x, g_ref[...]).astype(BF16)
        for j in range(n_half):
            a = _dot(xn, win_s[j])
            b = _dot(xn, win_s[n_half + j])
            act_ref[:, j * cols:(j + 1) * cols] = (_silu(a) * b).astype(BF16)
        x = x + 0.5 * _dot(act_ref[...], wout_s[...])
        if has_ple:
            gate = jax.nn.sigmoid(_dot(_rms(x, gp_ref[...]).astype(BF16), wg_s[...]))
            x = x + gate * _dot(_pick(is_prompt, p_refs, 0).astype(BF16), wp_s[...])
        if has_final:
            x = _rms(x, gf_ref[...])
        if n_o == 1:
            o_refs[0][...] = x
        else:
            @pl.when(is_prompt)
            def _():
                o_refs[0][...] = x

            @pl.when(jnp.logical_not(is_prompt))
            def _():
                o_refs[1][...] = x


def _ffn_call(xs, n_rows, gain, w_in, w_out, layer, ple=None, final_gain=None, split_out=False, name="ffn"):
    d = xs[0].shape[1]
    d_ff = w_out.shape[1]
    tm = ROW_TILE
    cols = V7X_MXU_COLS
    assert n_rows[0] % tm == 0 and n_rows[1] % tm == 0 and d_ff % cols == 0
    pt, st = n_rows[0] // tm, n_rows[1] // tm
    stage = d_ff // cols
    stacked, prompt, sample = _row_maps(pt, stage)
    pair = [prompt, sample]
    last = stage - 1
    args = [*xs, gain.reshape(1, d), w_in, w_out]
    specs = [pl.BlockSpec((tm, d), m) for m in ([stacked] if len(xs) == 1 else pair)]
    specs += [_resident((1, d)),
              pl.BlockSpec((1, d, 2 * cols), lambda i: (layer, 0, jnp.minimum(i, last))),
              pl.BlockSpec((1, cols, d), lambda i: (layer, jnp.minimum(i, last), 0))]
    scratch = [pltpu.VMEM((2 * stage, d, cols), BF16), pltpu.VMEM((d_ff, d), BF16), pltpu.VMEM((tm, d_ff), BF16)]
    gate_steps = 0
    if ple is not None:
        ps, gp, wg, wp = ple
        gate_rows = LANES
        gate_steps = d // gate_rows
        assert gate_steps <= stage
        args += [*ps, gp.reshape(1, d), wg, wp]
        specs += [pl.BlockSpec((1, tm, ps[0].shape[2]), lambda i, m=m: (layer, *m(i))) for m in pair]
        specs += [_resident((1, d)),
                  pl.BlockSpec((1, gate_rows, d), lambda i: (layer, jnp.minimum(i, gate_steps - 1), 0)),
                  _resident_layer(wp.shape, layer)]
        scratch += [pltpu.VMEM((d, d), BF16), pltpu.VMEM(wp.shape[1:], BF16)]
    if final_gain is not None:
        args.append(final_gain.reshape(1, d))
        specs.append(_resident((1, d)))
    if split_out:
        out_shape = [jax.ShapeDtypeStruct((r, d), F32) for r in n_rows]
        out_specs = [pl.BlockSpec((tm, d), m) for m in pair]
    else:
        out_shape = jax.ShapeDtypeStruct((sum(n_rows), d), F32)
        out_specs = pl.BlockSpec((tm, d), stacked)
    return pl.pallas_call(
        functools.partial(_ffn_kernel, n_x=len(xs), has_ple=ple is not None, has_final=final_gain is not None,
                          n_o=2 if split_out else 1, prompt_tiles=pt, stage_steps=stage, gate_steps=gate_steps),
        out_shape=out_shape,
        grid=(stage + pt + st,),
        in_specs=specs,
        out_specs=out_specs,
        scratch_shapes=scratch,
        compiler_params=_params(1, V7X_VMEM_BUDGET_BYTES),
        name=name,
    )(*args)


def _inproj_kernel(*refs, n_out, chunk_cols):
    x_ref, g_ref = refs[0], refs[1]
    w_refs = refs[2:2 + n_out]
    o_refs = refs[2 + n_out:2 + 2 * n_out]
    xn = _rms(x_ref[...], g_ref[...]).astype(BF16)
    for w_ref, o_ref, cw in zip(w_refs, o_refs, chunk_cols):
        for c in range(w_ref.shape[1] // cw):
            o_ref[:, c * cw:(c + 1) * cw] = _dot(xn, w_ref[:, c * cw:(c + 1) * cw]).astype(o_ref.dtype)


def _inproj_call(x, gain, weights, out_dtypes, chunk_cols, name):
    n, d = x.shape
    tm = ROW_TILE
    row = lambda i: (i, 0)
    specs = [pl.BlockSpec((tm, d), row), _resident((1, d))] + [_resident(w.shape) for w in weights]
    out_shape = [jax.ShapeDtypeStruct((n, w.shape[1]), dt) for w, dt in zip(weights, out_dtypes)]
    out_specs = [pl.BlockSpec((tm, w.shape[1]), row) for w in weights]
    vmem = (sum(2 * w.size for w in weights) + 4 * tm * d * 4
            + sum(2 * tm * w.shape[1] * jnp.dtype(dt).itemsize for w, dt in zip(weights, out_dtypes))
            + 2 * tm * max(chunk_cols) * 4 + (4 << 20))
    return pl.pallas_call(
        functools.partial(_inproj_kernel, n_out=len(weights), chunk_cols=tuple(chunk_cols)),
        out_shape=out_shape,
        grid=(n // tm,),
        in_specs=specs,
        out_specs=out_specs,
        compiler_params=_params(1, vmem),
        name=name,
    )(x, gain.reshape(1, d), *weights)


def _inproj_staged_kernel(x_ref, g_ref, w_ref, o_ref, w_s, *, stage_steps):
    i = pl.program_id(0)
    cols = w_s.shape[2]

    @pl.when(i < stage_steps)
    def _():
        w_s[i] = w_ref[0].astype(BF16)

    @pl.when(i >= stage_steps)
    def _():
        xn = _rms(x_ref[...], g_ref[...]).astype(BF16)
        for c in range(stage_steps):
            o_ref[:, c * cols:(c + 1) * cols] = _dot(xn, w_s[c]).astype(o_ref.dtype)


def _inproj_staged_call(x, gain, w, layer, cols, name):
    n, d = x.shape
    n_out = w.shape[2]
    tm = ROW_TILE
    assert n_out % cols == 0
    stage = n_out // cols
    row = lambda i: (jnp.maximum(i - stage, 0), 0)
    vmem = 2 * d * n_out + 2 * d * cols * 4 + 4 * tm * d * 4 + 2 * tm * n_out * 2 + 2 * tm * cols * 4 + (4 << 20)
    return pl.pallas_call(
        functools.partial(_inproj_staged_kernel, stage_steps=stage),
        out_shape=jax.ShapeDtypeStruct((n, n_out), BF16),
        grid=(stage + n // tm,),
        in_specs=[pl.BlockSpec((tm, d), row), _resident((1, d)),
                  pl.BlockSpec((1, d, cols), lambda i: (layer, 0, jnp.minimum(i, stage - 1)))],
        out_specs=pl.BlockSpec((tm, n_out), row),
        scratch_shapes=[pltpu.VMEM((stage, d, cols), BF16)],
        compiler_params=_params(1, vmem),
        name=name,
    )(x, gain.reshape(1, d), w)


GLA_MATMUL_LEVELS = (1, 2)


def _gla_constants(n_seq, ls, width):
    c = n_seq * ls
    nlev = int(round(math.log2(ls)))
    assert 2 ** nlev == ls and nlev > max(GLA_MATMUL_LEVELS)
    t = np.arange(c)[:, None]
    i = np.arange(c)[None, :]
    w_rows = [((t // ls) == (i // ls)) & (i <= t)]
    masks, uppers = [], []
    for l in range(nlev):
        m = 2 ** l
        upper = ((t // m) % 2) == 1
        r = (t // (2 * m)) * (2 * m) + m - 1
        if l in GLA_MATMUL_LEVELS:
            w_rows.append(np.where(upper, (i > r) & (i <= t), (i > t) & (i <= r)))
        masks.append(((t // (2 * m)) == (i // (2 * m))) & upper & (((i // m) % 2) == 0))
        uppers.append(np.broadcast_to(upper, (c, width)))
    w = jnp.asarray(np.concatenate(w_rows, 0).astype(np.float32), BF16)
    return w, jnp.asarray(np.stack(masks).astype(np.float32)), jnp.asarray(np.stack(uppers).astype(np.float32))


def _gla_block(q, k, v, g, st_list, w_ref, m_ref, u_ref, ls, transposed_state, dk):
    c, width = q.shape
    nh = v.shape[1] // LANES
    hpg = LANES // dk
    n_groups = width // LANES
    n_seq = c // ls
    nlev = m_ref.shape[0]
    lane = lax.broadcasted_iota(jnp.int32, (1, LANES), 1)
    own = [jnp.where((lane >= i * dk) & (lane < (i + 1) * dk), 1.0, 0.0) for i in range(hpg)]
    g_hi32 = g.astype(BF16).astype(F32)
    g_lo32 = g - g_hi32
    g_hi, g_lo = g_hi32.astype(BF16), g_lo32.astype(BF16)
    ex2 = _dot(w_ref[...], jnp.concatenate([g_hi, g_lo], axis=1))
    ex = ex2[:, :width] + ex2[:, width:]
    b = ex[:c]

    def row(r, n):
        return jnp.broadcast_to(b[r:r + 1, :], (n, width))

    b_end = jnp.concatenate([row((s + 1) * ls - 1, ls) for s in range(n_seq)], axis=0) if n_seq > 1 else row(c - 1, c)
    e_cum = jnp.exp2(b)
    qe = q * e_cum
    kr = k * jnp.exp2(b_end - b)

    att = [None] * nh
    for l in range(nlev):
        m = 2 ** l
        if l == 0:
            x = jnp.where(u_ref[l] > 0.5, q * jnp.exp2(g), k)
        elif l in GLA_MATMUL_LEVELS:
            j = 1 + GLA_MATMUL_LEVELS.index(l)
            x = jnp.where(u_ref[l] > 0.5, q, k) * jnp.exp2(ex[j * c:(j + 1) * c])
        else:
            pieces = []
            for p in range(c // (2 * m)):
                lo, mid, hi = p * 2 * m, p * 2 * m + m, (p + 1) * 2 * m
                pivot = row(mid - 1, m)
                pieces.append(k[lo:mid] * jnp.exp2(pivot - b[lo:mid]))
                pieces.append(q[mid:hi] * jnp.exp2(b[mid:hi] - pivot))
            x = jnp.concatenate(pieces, axis=0)
        x = x.astype(BF16)
        mask = m_ref[l] > 0.5
        for h in range(nh):
            grp, i = divmod(h, hpg)
            xg = x[:, grp * LANES:(grp + 1) * LANES]
            xl = xg if hpg == 1 else xg * own[i].astype(BF16)
            att[h] = jnp.where(mask, _dot_nt(xl, xg), 0.0 if att[h] is None else att[h])

    qk = q * k
    outs = [[None] * n_seq for _ in range(nh)]
    st_new = [[None] * n_groups for _ in range(n_seq)]
    for grp in range(n_groups):
        gl = slice(grp * LANES, (grp + 1) * LANES)
        heads = []
        for i in range(hpg):
            h = grp * hpg + i
            vh = v[:, h * LANES:(h + 1) * LANES]
            sel = 1.0 if hpg == 1 else own[i]
            o = (_dot(att[h].astype(BF16), vh.astype(BF16))
                 + jnp.sum(qk[:, gl] * sel, axis=1, keepdims=True) * vh)
            heads.append((h, vh, o, qe[:, gl] * sel, kr[:, gl] * sel))
        for s in range(n_seq):
            rows = slice(s * ls, (s + 1) * ls)
            st = st_list[s][grp]
            if transposed_state:
                new = e_cum[(s + 1) * ls - 1:(s + 1) * ls, gl] * st
            else:
                g2 = jnp.concatenate([g_hi32[rows, gl], g_lo32[rows, gl]], axis=0).astype(BF16)
                new = jnp.exp2(_dot_tn(g2, jnp.ones((2 * ls, LANES), BF16))) * st
            for h, vh, o, qe_h, kr_h in heads:
                if transposed_state:
                    outs[h][s] = o[rows] + _dot_nt(qe_h[rows].astype(BF16), st.astype(BF16))
                    new = new + _dot_tn(vh[rows].astype(BF16), kr_h[rows].astype(BF16))
                else:
                    outs[h][s] = o[rows] + _dot(qe_h[rows].astype(BF16), st.astype(BF16))
                    new = new + _dot_tn(kr_h[rows].astype(BF16), vh[rows].astype(BF16))
            st_new[s][grp] = new
    outs = [parts[0] if n_seq == 1 else jnp.concatenate(parts, axis=0) for parts in outs]
    return outs, st_new


def _hgrn_inputs(refs, rows, q_scale):
    del q_scale
    qa_ref, ia_ref, ga_ref, fa_ref, lg_ref = refs
    lg = lg_ref[...]
    ex = jnp.exp(lg - jnp.max(lg, axis=0, keepdims=True))
    lb = ex[0:1] / jnp.sum(ex, axis=0, keepdims=True)
    sig = jax.nn.sigmoid(fa_ref[rows, :])
    q = _silu(qa_ref[rows, :].astype(F32))
    k = (1.0 - lb) * (1.0 - sig)
    g = jnp.log(lb + (1.0 - lb) * sig) * LOG2_E
    return q, k, ia_ref[rows, :].astype(F32), g, ga_ref[rows, :].astype(F32)


def _gla_inputs(refs, rows, q_scale):
    qb_ref, kb_ref, vb_ref, gb_ref, lr_ref, wgk_ref, bgk_ref = refs
    z = _dot(lr_ref[rows, :].astype(BF16), wgk_ref[...]) + bgk_ref[...]
    g = (jnp.minimum(z, 0.0) - jnp.log1p(jnp.exp(-jnp.abs(z)))) * (LOG2_E / GLA_GATE_NORM)
    q = qb_ref[rows, :].astype(F32) * q_scale
    return q, kb_ref[rows, :].astype(F32), vb_ref[rows, :].astype(F32), g, gb_ref[rows, :].astype(F32)


def _head_out(outs, gate, gn):
    ys = [_rms(o, gn) * _silu(gate[:, h * LANES:(h + 1) * LANES]) for h, o in enumerate(outs)]
    return jnp.concatenate(ys, axis=1).astype(BF16)


def _stage_out_weight(wo_ref, wo_s, first):
    @pl.when(first)
    def _():
        wo_s[...] = wo_ref[0].astype(BF16)


def _ab_prompt_kernel(*refs, kind, n_in, chunk, q_scale):
    in_refs = refs[:n_in]
    gn_ref, w_ref, m_ref, u_ref, x_ref, wo_ref, xo_ref, s_ref, st_ref, y_s, wo_s = refs[n_in:]
    t = pl.program_id(1)
    n_groups = st_ref.shape[0]
    nh, dk = s_ref.shape[1:3]
    hpg = nh // n_groups
    _stage_out_weight(wo_ref, wo_s, jnp.logical_and(pl.program_id(0) == 0, t == 0))

    @pl.when(t == 0)
    def _():
        st_ref[...] = jnp.zeros_like(st_ref)

    load = _hgrn_inputs if kind == "hgrn" else _gla_inputs
    st = [st_ref[j] for j in range(n_groups)]
    for c in range(y_s.shape[0] // chunk):
        rows = slice(c * chunk, (c + 1) * chunk)
        q, k, v, g, gate = load(in_refs, rows, q_scale)
        outs, (st,) = _gla_block(q, k, v, g, [st], w_ref, m_ref, u_ref, chunk, True, dk)
        y_s[rows, :] = _head_out(outs, gate, gn_ref[...])
    for j in range(n_groups):
        st_ref[j] = st[j]
    xo_ref[...] = x_ref[...] + _dot(y_s[...], wo_s[...])

    @pl.when(t == pl.num_programs(1) - 1)
    def _():
        for h in range(nh):
            grp, i = divmod(h, hpg)
            s_ref[0, h] = st_ref[grp].T[i * dk:(i + 1) * dk, :]


def _ab_sample_kernel(*refs, kind, n_in, ls, q_scale):
    in_refs = refs[:n_in]
    gn_ref, w_ref, m_ref, u_ref, s0_ref, x_ref, wo_ref, xo_ref, s_ref, wo_s = refs[n_in:]
    _stage_out_weight(wo_ref, wo_s, pl.program_id(0) == 0)
    load = _hgrn_inputs if kind == "hgrn" else _gla_inputs
    n_seq, nh, dk = s0_ref.shape[:3]
    hpg = LANES // dk
    q, k, v, g, gate = load(in_refs, slice(None), q_scale)
    st0 = [[jnp.concatenate([s0_ref[s, j * hpg + i] for i in range(hpg)], axis=0) if hpg > 1 else s0_ref[s, j]
            for j in range(nh // hpg)] for s in range(n_seq)]
    outs, st = _gla_block(q, k, v, g, st0, w_ref, m_ref, u_ref, ls, False, dk)
    xo_ref[...] = x_ref[...] + _dot(_head_out(outs, gate, gn_ref[...]), wo_s[...])
    for s in range(n_seq):
        for h in range(nh):
            grp, i = divmod(h, hpg)
            s_ref[s, h] = st[s][grp][i * dk:(i + 1) * dk, :]


def _ab_specs(kind, p16, p32, extra, tok_block, tok_index, nh, dk):
    wide = nh * LANES
    kw = nh * dk

    def col(start, w):
        assert start % w == 0
        return pl.BlockSpec((tok_block, w), lambda *ids: (tok_index(*ids), start // w))

    if kind == "hgrn":
        (logits,) = extra
        args = [p16, p16, p16, p32, logits]
        specs = [col(0, wide), col(wide, wide), col(2 * wide, wide), col(0, kw), _resident(logits.shape)]
    else:
        w_gk, b_gk = extra
        a16 = 3 * H_A * LANES
        args = [p16, p16, p16, p16, p32, w_gk, b_gk]
        specs = [col(a16, kw), col(a16 + kw, kw), col(a16 + 2 * kw, wide), col(a16 + 2 * kw + wide, wide),
                 col(H_A * LANES, LANES), _resident(w_gk.shape), _resident(b_gk.shape)]
    return args, specs


def _out_weight_spec(w_out, layer, part, rows):
    return pl.BlockSpec((1, rows, w_out.shape[2]), lambda *_: (layer, part, 0), pipeline_mode=pl.Buffered(1))


def _ab_prompt_call(kind, p16, p32, extra, gn, x, w_out, layer, part, batch, seq, dk, name):
    nh = H_A if kind == "hgrn" else H_B
    d = x.shape[1]
    tt = GLA_STEP_TOKENS
    nt = seq // tt
    tok = lambda b, t: b * nt + t
    consts = _gla_constants(1, GLA_CHUNK, nh * dk)
    args, specs = _ab_specs(kind, p16, p32, extra, tt, tok, nh, dk)
    n_in = len(args)
    args += [gn.reshape(1, LANES), *consts, x, w_out]
    specs += [_resident((1, LANES))] + [_resident(c.shape) for c in consts]
    specs += [pl.BlockSpec((tt, d), lambda b, t: (tok(b, t), 0)), _out_weight_spec(w_out, layer, part, nh * LANES)]
    return pl.pallas_call(
        functools.partial(_ab_prompt_kernel, kind=kind, n_in=n_in, chunk=GLA_CHUNK, q_scale=dk ** -0.5),
        out_shape=[jax.ShapeDtypeStruct((batch * seq, d), F32),
                   jax.ShapeDtypeStruct((batch, nh, dk, LANES), F32)],
        grid=(batch, nt),
        in_specs=specs,
        out_specs=[pl.BlockSpec((tt, d), lambda b, t: (tok(b, t), 0)),
                   pl.BlockSpec((1, nh, dk, LANES), lambda b, t: (b, 0, 0, 0))],
        scratch_shapes=[pltpu.VMEM((nh * dk // LANES, LANES, LANES), F32), pltpu.VMEM((tt, nh * LANES), BF16),
                        pltpu.VMEM((nh * LANES, d), BF16)],
        compiler_params=_params(2, 40 << 20),
        name=name,
    )(*args)


def _ab_sample_call(kind, p16, p32, extra, gn, s0, row0, seq, x, x_row0, w_out, layer, part, name):
    batch, nh, dk, _ = s0.shape
    d = x.shape[1]
    nb = SAMPLE_SEQS
    rows = nb * seq
    blk0, xblk0 = row0 // rows, x_row0 // rows
    consts = _gla_constants(nb, seq, nh * dk)
    args, specs = _ab_specs(kind, p16, p32, extra, rows, lambda i: blk0 + i, nh, dk)
    n_in = len(args)
    args += [gn.reshape(1, LANES), *consts, s0, x, w_out]
    state_spec = pl.BlockSpec((nb, nh, dk, LANES), lambda i: (i, 0, 0, 0))
    specs += [_resident((1, LANES))] + [_resident(c.shape) for c in consts] + [state_spec]
    specs += [pl.BlockSpec((rows, d), lambda i: (xblk0 + i, 0)), _out_weight_spec(w_out, layer, part, nh * LANES)]
    return pl.pallas_call(
        functools.partial(_ab_sample_kernel, kind=kind, n_in=n_in, ls=seq, q_scale=dk ** -0.5),
        out_shape=[jax.ShapeDtypeStruct((batch * seq, d), F32),
                   jax.ShapeDtypeStruct((batch, nh, dk, LANES), F32)],
        grid=(batch // nb,),
        in_specs=specs,
        out_specs=[pl.BlockSpec((rows, d), lambda i: (i, 0)), state_spec],
        scratch_shapes=[pltpu.VMEM((nh * LANES, d), BF16)],
        compiler_params=_params(1, 40 << 20),
        name=name,
    )(*args)


def _ret_constants(n_seq, ls):
    c = n_seq * ls
    log_gamma = np.log1p(-np.exp2(-5.0 - np.arange(H_C, dtype=np.float64)))[:, None, None]
    t = np.arange(c)[:, None]
    s = np.arange(c)[None, :]
    causal = ((t // ls) == (s // ls)) & (s <= t)
    dmask = np.where(causal, np.exp(log_gamma * np.where(causal, t - s, 0)), 0.0)
    pos = (np.arange(c) % ls)[None, :, None]
    scales = np.stack([np.exp(log_gamma * (pos + 1)), np.exp(log_gamma * (ls - 1 - pos))], axis=1)
    scales = np.broadcast_to(scales, (H_C, 2, c, LANES))
    return jnp.asarray(dmask, F32), jnp.asarray(scales, F32)


def _rope_tables(positions, half):
    inv = ROPE_THETA ** (-jnp.arange(half, dtype=F32) / half)
    ang = positions.astype(F32)[:, None] * inv[None, :]
    return jnp.cos(ang), jnp.sin(ang)


def _rope(x, cos, sin):
    half = x.shape[1] // 2
    x1, x2 = x[:, :half], x[:, half:]
    return jnp.concatenate([x1 * cos - x2 * sin, x2 * cos + x1 * sin], axis=1)


def _ret_block(q16, k16, v, cos, sin, dmask, q_scale, k_scale, s_list, ls):
    dk = q16.shape[1]
    q = _rope(q16.astype(F32), cos, sin)
    k = _rope(k16.astype(F32), cos, sin) * (dk ** -0.5)
    reps = dk // LANES
    q_sc = jnp.concatenate([q_scale] * reps, axis=1)
    k_sc = jnp.concatenate([k_scale] * reps, axis=1)
    att = _dot_nt(q.astype(BF16), k.astype(BF16)) * dmask
    o = _dot(att.astype(BF16), v)
    qs = q * q_sc
    ks = k * k_sc
    decay = q_scale[ls - 1:ls, 0:1]
    n_seq = q.shape[0] // ls
    v32 = v if n_seq == 1 else v.astype(F32)
    o_parts, s_new = [], []
    for b in range(n_seq):
        rows = slice(b * ls, (b + 1) * ls)
        s = s_list[b]
        o_parts.append(o[rows] + _dot(qs[rows].astype(BF16), s.astype(BF16)))
        s_new.append(decay * s + _dot_tn(ks[rows].astype(BF16), v32[rows].astype(BF16)))
    o = o_parts[0] if n_seq == 1 else jnp.concatenate(o_parts, axis=0)
    return o, s_new


def _ret_head_out(o, gate, gn):
    oc = o - jnp.mean(o, axis=-1, keepdims=True)
    y = oc * lax.rsqrt(jnp.mean(oc * oc, axis=-1, keepdims=True) + EPS) * gn
    return (y * _silu(gate)).astype(BF16)


def _ret_prompt_kernel(q_ref, k_ref, v_ref, g_ref, cos_ref, sin_ref, dm_ref, sc_ref, gn_ref, x_ref, wo_ref,
                       xo_ref, s_ref, st_ref, y_s, wo_s):
    t = pl.program_id(1)
    nh, dk, dv = st_ref.shape
    c = dm_ref.shape[1]
    _stage_out_weight(wo_ref, wo_s, jnp.logical_and(pl.program_id(0) == 0, t == 0))

    @pl.when(t == 0)
    def _():
        st_ref[...] = jnp.zeros_like(st_ref)

    for h in range(nh):
        s = st_ref[h]
        for j in range(q_ref.shape[0] // c):
            rows = slice(j * c, (j + 1) * c)
            o, (s,) = _ret_block(q_ref[rows, h * dk:(h + 1) * dk], k_ref[rows, h * dk:(h + 1) * dk],
                                 v_ref[rows, h * dv:(h + 1) * dv], cos_ref[rows, :], sin_ref[rows, :], dm_ref[h],
                                 sc_ref[h, 0], sc_ref[h, 1], [s], c)
            y_s[rows, h * dv:(h + 1) * dv] = _ret_head_out(o, g_ref[rows, h * dv:(h + 1) * dv].astype(F32),
                                                           gn_ref[...])
        st_ref[h] = s
    xo_ref[...] = x_ref[...] + _dot(y_s[...], wo_s[...])

    @pl.when(t == pl.num_programs(1) - 1)
    def _():
        s_ref[0] = st_ref[...]


def _ret_sample_kernel(q_ref, k_ref, v_ref, g_ref, cos_ref, sin_ref, dm_ref, sc_ref, gn_ref, s0_ref, x_ref,
                       wo_ref, xo_ref, s_ref, wo_s, *, ls):
    h = pl.program_id(1)
    dv = v_ref.shape[1]
    _stage_out_weight(wo_ref, wo_s, jnp.logical_and(pl.program_id(0) == 0, h == 0))
    n_seq = q_ref.shape[0] // ls
    o, s = _ret_block(q_ref[...], k_ref[...], v_ref[...], cos_ref[...], sin_ref[...], dm_ref[0],
                      sc_ref[0, 0], sc_ref[0, 1], [s0_ref[b, 0] for b in range(n_seq)], ls)
    y = _ret_head_out(o, g_ref[...].astype(F32), gn_ref[...])
    proj = _dot(y, wo_s[pl.ds(pl.multiple_of(h * dv, dv), dv), :])

    @pl.when(h == 0)
    def _():
        xo_ref[...] = x_ref[...] + proj

    @pl.when(h > 0)
    def _():
        xo_ref[...] += proj

    for b in range(n_seq):
        s_ref[b, 0] = s[b]


def _ret_prompt_call(r16, gn, x, w_out, layer, batch, seq, dk, dv, name):
    c = RET_STEP_TOKENS
    nt = seq // c
    d = x.shape[1]
    dmask, scales = _ret_constants(1, RET_CHUNK)
    cos, sin = _rope_tables(jnp.arange(seq, dtype=jnp.int32), dk // 2)
    tok = lambda b, t: b * nt + t
    nq, nv = H_C * dk, H_C * dv
    args = [r16, r16, r16, r16, cos, sin, dmask, scales, gn.reshape(1, dv), x, w_out]
    specs = [pl.BlockSpec((c, nq), lambda b, t: (tok(b, t), 0)),
             pl.BlockSpec((c, nq), lambda b, t: (tok(b, t), 1)),
             pl.BlockSpec((c, nv), lambda b, t: (tok(b, t), 2 * nq // nv)),
             pl.BlockSpec((c, nv), lambda b, t: (tok(b, t), 2 * nq // nv + 1)),
             pl.BlockSpec((c, dk // 2), lambda b, t: (t, 0)), pl.BlockSpec((c, dk // 2), lambda b, t: (t, 0)),
             _resident(dmask.shape), _resident(scales.shape), _resident((1, dv)),
             pl.BlockSpec((c, d), lambda b, t: (tok(b, t), 0)), _out_weight_spec(w_out, layer, 0, nv)]
    return pl.pallas_call(
        _ret_prompt_kernel,
        out_shape=[jax.ShapeDtypeStruct((batch * seq, d), F32),
                   jax.ShapeDtypeStruct((batch, H_C, dk, dv), F32)],
        grid=(batch, nt),
        in_specs=specs,
        out_specs=[pl.BlockSpec((c, d), lambda b, t: (tok(b, t), 0)),
                   pl.BlockSpec((1, H_C, dk, dv), lambda b, t: (b, 0, 0, 0))],
        scratch_shapes=[pltpu.VMEM((H_C, dk, dv), F32), pltpu.VMEM((c, nv), BF16), pltpu.VMEM((nv, d), BF16)],
        compiler_params=_params(2, 48 << 20),
        name=name,
    )(*args)


def _ret_sample_call(r16, gn, s0, row0, batch, seq, dk, dv, x, w_out, layer, name):
    nb = SAMPLE_SEQS
    rows = nb * seq
    blk0 = row0 // rows
    d = x.shape[1]
    dmask, scales = _ret_constants(nb, seq)
    pos = PAST_LEN + (jnp.arange(rows, dtype=jnp.int32) % seq)
    cos, sin = _rope_tables(pos, dk // 2)
    nq = H_C * dk

    def col(width, base):
        return pl.BlockSpec((rows, width), lambda i, h: (blk0 + i, base + h))

    state_spec = pl.BlockSpec((nb, 1, dk, dv), lambda i, h: (i, h, 0, 0))
    args = [r16, r16, r16, r16, cos, sin, dmask, scales, gn.reshape(1, dv), s0, x, w_out]
    specs = [col(dk, 0), col(dk, H_C), col(dv, 2 * nq // dv), col(dv, 2 * nq // dv + H_C),
             _resident(cos.shape), _resident(sin.shape),
             pl.BlockSpec((1, rows, rows), lambda i, h: (h, 0, 0)),
             pl.BlockSpec((1, 2, rows, LANES), lambda i, h: (h, 0, 0, 0)), _resident((1, dv)), state_spec,
             pl.BlockSpec((rows, d), lambda i, h: (blk0 + i, 0)), _out_weight_spec(w_out, layer, 0, H_C * dv)]
    return pl.pallas_call(
        functools.partial(_ret_sample_kernel, ls=seq),
        out_shape=[jax.ShapeDtypeStruct((batch * seq, d), F32),
                   jax.ShapeDtypeStruct((batch, H_C, dk, dv), F32)],
        grid=(batch // nb, H_C),
        in_specs=specs,
        out_specs=[pl.BlockSpec((rows, d), lambda i, h: (i, 0)), state_spec],
        scratch_shapes=[pltpu.VMEM((H_C * dv, d), BF16)],
        compiler_params=_params(2, 56 << 20),
        name=name,
    )(*args)


def _prepare_ab_weights(w_in, w_gk2, b_gk, dk_a, dv_a, dk_b, dv_b):
    sizes = (H_A * dk_a, H_A * dk_a, H_A * dv_a, H_A * dv_a, H_B * dk_b, H_B * dk_b, H_B * dv_b,
             GLA_RANK, H_B * dv_b)
    qa, fa, ia, ga, qb, kb, vb, lrb, gb = jnp.split(w_in, [int(v) for v in np.cumsum(sizes)[:-1]], axis=1)
    w16 = jnp.concatenate([qa, ia, ga, qb, kb, vb, gb], axis=1).astype(BF16)
    w32 = jnp.concatenate([fa, jnp.pad(lrb, ((0, 0), (0, LANES - GLA_RANK)))], axis=1).astype(BF16)
    w_gk = jnp.pad(w_gk2, ((0, LANES - GLA_RANK), (0, 0))).astype(BF16)
    return w16, w32, w_gk, b_gk.reshape(1, -1)


def kernel(x_prompt, x_sample, state_hgrn, state_gla, state_ret, p_prompt, p_sample, norm_ffn1, ffn1_w_in, ffn1_w_out, norm_mix, ab_w_in, ab_w_gk2, ab_b_gk, hgrn_lb_logits, ab_gn_hgrn, ab_gn_gla, ab_w_out, ret_w_in, ret_gn, ret_w_out, norm_ffn2, ffn2_w_in, ffn2_w_out, norm_ple, ple_w_gate, ple_w_proj, norm_final):
    bp, lp, d = x_prompt.shape
    bs, ls, _ = x_sample.shape
    depth = norm_ffn1.shape[0]
    n_prompt = bp * lp
    dk_a, dv_a = state_hgrn.shape[-2:]
    dk_b, dv_b = state_gla.shape[-2:]
    dk_c, dv_c = state_ret.shape[-2:]
    assert dk_a == dv_a == dv_b == LANES and LANES % dk_b == 0 and H_B % (LANES // dk_b) == 0
    assert state_hgrn.shape[0] == 1

    n_rows = (n_prompt, bs * ls)
    xs = (x_prompt.reshape(n_prompt, d), x_sample.reshape(bs * ls, d))
    ps = (p_prompt.reshape(depth, n_prompt, -1), p_sample.reshape(depth, bs * ls, -1))

    new_hgrn_p, new_gla_p, new_ret_p, new_hgrn_s, new_gla_s, new_ret_s = [], [], [], [], [], []
    for i in range(depth):
        j = i // 2
        x = _ffn_call(xs if i == 0 else (x,), n_rows, norm_ffn1[i], ffn1_w_in, ffn1_w_out, i, name=f"ffn1_{i}")
        if i % 2 == 0:
            w16, w32, w_gk, b_gk = _prepare_ab_weights(ab_w_in[j], ab_w_gk2[j], ab_b_gk[j], dk_a, dv_a, dk_b, dv_b)
            p16, p32 = _inproj_call(x, norm_mix[i], [w16, w32], [BF16, F32], [w16.shape[1] // 2, w32.shape[1]],
                                    name=f"ab_in_{i}")
            hgrn = ("hgrn", p16, p32, (hgrn_lb_logits,), ab_gn_hgrn[j])
            gla = ("gla", p16, p32, (w_gk, b_gk), ab_gn_gla[j])
            xa_p, sa_p = _ab_prompt_call(*hgrn, x, ab_w_out, j, 0, bp, lp, dk_a, f"hgrn_p_{i}")
            xa_s, sa_s = _ab_sample_call(*hgrn, state_hgrn[j], n_prompt, ls, x, n_prompt, ab_w_out, j, 0,
                                         f"hgrn_s_{i}")
            x_p, sb_p = _ab_prompt_call(*gla, xa_p, ab_w_out, j, 1, bp, lp, dk_b, f"gla_p_{i}")
            x_s, sb_s = _ab_sample_call(*gla, state_gla[j], n_prompt, ls, xa_s, 0, ab_w_out, j, 1, f"gla_s_{i}")
            new_hgrn_p.append(sa_p)
            new_hgrn_s.append(sa_s)
            new_gla_p.append(sb_p)
            new_gla_s.append(sb_s)
        else:
            r16 = _inproj_staged_call(x, norm_mix[i], ret_w_in, j, 2 * V7X_MXU_COLS, name=f"ret_in_{i}")
            x_p, sc_p = _ret_prompt_call(r16, ret_gn[j], x, ret_w_out, j, bp, lp, dk_c, dv_c, f"ret_p_{i}")
            x_s, sc_s = _ret_sample_call(r16, ret_gn[j], state_ret[j], n_prompt, bs, ls, dk_c, dv_c, x, ret_w_out, j,
                                         f"ret_s_{i}")
            new_ret_p.append(sc_p)
            new_ret_s.append(sc_s)
        last = i == depth - 1
        x = _ffn_call((x_p, x_s), n_rows, norm_ffn2[i], ffn2_w_in, ffn2_w_out, i,
                      ple=(ps, norm_ple[i], ple_w_gate, ple_w_proj),
                      final_gain=norm_final if last else None, split_out=last, name=f"ffn2_{i}")

    y_prompt = x[0].reshape(bp, lp, d)
    y_sample = x[1].reshape(bs, ls, d)
    def layers(states):
        return states[0][None] if len(states) == 1 else jnp.stack(states)

    return (y_prompt, y_sample, layers(new_hgrn_p), layers(new_gla_p), layers(new_ret_p),
            layers(new_hgrn_s), layers(new_gla_s), layers(new_ret_s))
```

```python
import functools
import math

import numpy as np
import jax
import jax.numpy as jnp
from jax import lax
from jax.experimental import pallas as pl
from jax.experimental.pallas import tpu as pltpu

F32 = jnp.float32
BF16 = jnp.bfloat16

H_A = 4
H_B = 4
H_C = 4
GLA_RANK = 16
GLA_GATE_NORM = 16.0
ROPE_THETA = 10000.0
PAST_LEN = 16384
EPS = 1e-6
LOG2_E = math.log2(math.e)

LANES = 128
V7X_MXU_COLS = 256
V7X_VMEM_BUDGET_BYTES = 56 * 1024 * 1024

ROW_TILE = 512
GLA_CHUNK = 128
GLA_STEP_TOKENS = 1024
RET_CHUNK = 256
RET_STEP_TOKENS = 512
SAMPLE_SEQS = 16


def _params(n_axes, vmem_bytes):
    return pltpu.CompilerParams(dimension_semantics=("arbitrary",) * n_axes,
                                vmem_limit_bytes=min(int(vmem_bytes), V7X_VMEM_BUDGET_BYTES))


def _resident(shape):
    nd = len(shape)
    return pl.BlockSpec(shape, lambda *_: (0,) * nd, pipeline_mode=pl.Buffered(1))


def _resident_layer(shape, layer):
    nd = len(shape)
    return pl.BlockSpec((1,) + tuple(shape[1:]), lambda *_: (layer,) + (0,) * (nd - 1),
                        pipeline_mode=pl.Buffered(1))


def _row_maps(prompt_tiles, lead=0):
    stacked = lambda i: (jnp.maximum(i - lead, 0), 0)
    prompt = lambda i: (jnp.clip(i - lead, 0, prompt_tiles - 1), 0)
    sample = lambda i: (jnp.maximum(i - lead - prompt_tiles, 0), 0)
    return stacked, prompt, sample


def _rms(x, g):
    return x * lax.rsqrt(jnp.mean(x * x, axis=-1, keepdims=True) + EPS) * g


def _silu(x):
    return x * jax.nn.sigmoid(x)


def _dot(a, b):
    return jnp.dot(a, b, preferred_element_type=F32)


def _dot_nt(a, b):
    return lax.dot_general(a, b, (((1,), (1,)), ((), ())), preferred_element_type=F32)


def _dot_tn(a, b):
    return lax.dot_general(a, b, (((0,), (0,)), ((), ())), preferred_element_type=F32)


def _pick(is_prompt, refs, index=None):
    vals = [r[...] if index is None else r[index] for r in refs]
    return vals[0] if len(vals) == 1 else jnp.where(is_prompt, vals[0], vals[1])


def _ffn_kernel(*refs, n_x, has_ple, has_final, n_o, prompt_tiles, stage_steps, gate_steps):
    it = iter(refs)
    x_refs = [next(it) for _ in range(n_x)]
    g_ref, win_ref, wout_ref = next(it), next(it), next(it)
    if has_ple:
        p_refs = [next(it), next(it)]
        gp_ref, wg_ref, wp_ref = next(it), next(it), next(it)
    if has_final:
        gf_ref = next(it)
    o_refs = [next(it) for _ in range(n_o)]
    win_s, wout_s, act_ref = next(it), next(it), next(it)
    if has_ple:
        wg_s, wp_s = next(it), next(it)
    i = pl.program_id(0)
    cols = win_s.shape[2]
    n_half = win_s.shape[0] // 2

    @pl.when(i < stage_steps)
    def _():
        blk = win_ref[0]
        win_s[2 * i] = blk[:, :cols].astype(BF16)
        win_s[2 * i + 1] = blk[:, cols:].astype(BF16)
        wout_s[pl.ds(pl.multiple_of(i * cols, cols), cols), :] = wout_ref[0].astype(BF16)

    if has_ple:
        rows = wg_ref.shape[1]

        @pl.when(i < gate_steps)
        def _():
            wg_s[pl.ds(pl.multiple_of(i * rows, rows), rows), :] = wg_ref[0].astype(BF16)

        @pl.when(i == 0)
        def _():
            wp_s[...] = wp_ref[0].astype(BF16)

    @pl.when(i >= stage_steps)
    def _():
        is_prompt = i - stage_steps < prompt_tiles
        x = _pick(is_prompt, x_refs)
        xn = _rms(x, g_ref[...]).astype(BF16)
        for j in range(n_half):
            a = _dot(xn, win_s[j])
            b = _dot(xn, win_s[n_half + j])
            act_ref[:, j * cols:(j + 1) * cols] = (_silu(a) * b).astype(BF16)
        x = x + 0.5 * _dot(act_ref[...], wout_s[...])
        if has_ple:
            gate = jax.nn.sigmoid(_dot(_rms(x, gp_ref[...]).astype(BF16), wg_s[...]))
            x = x + gate * _dot(_pick(is_prompt, p_refs, 0).astype(BF16), wp_s[...])
        if has_final:
            x = _rms(x, gf_ref[...])
        if n_o == 1:
            o_refs[0][...] = x
        else:
            @pl.when(is_prompt)
            def _():
                o_refs[0][...] = x

            @pl.when(jnp.logical_not(is_prompt))
            def _():
                o_refs[1][...] = x


def _ffn_call(xs, n_rows, gain, w_in, w_out, layer, ple=None, final_gain=None, split_out=False, name="ffn"):
    d = xs[0].shape[1]
    d_ff = w_out.shape[1]
    tm = ROW_TILE
    cols = V7X_MXU_COLS
    assert n_rows[0] % tm == 0 and n_rows[1] % tm == 0 and d_ff % cols == 0
    pt, st = n_rows[0] // tm, n_rows[1] // tm
    stage = d_ff // cols
    stacked, prompt, sample = _row_maps(pt, stage)
    pair = [prompt, sample]
    last = stage - 1
    args = [*xs, gain.reshape(1, d), w_in, w_out]
    specs = [pl.BlockSpec((tm, d), m) for m in ([stacked] if len(xs) == 1 else pair)]
    specs += [_resident((1, d)),
              pl.BlockSpec((1, d, 2 * cols), lambda i: (layer, 0, jnp.minimum(i, last))),
              pl.BlockSpec((1, cols, d), lambda i: (layer, jnp.minimum(i, last), 0))]
    scratch = [pltpu.VMEM((2 * stage, d, cols), BF16), pltpu.VMEM((d_ff, d), BF16), pltpu.VMEM((tm, d_ff), BF16)]
    gate_steps = 0
    if ple is not None:
        ps, gp, wg, wp = ple
        gate_rows = LANES
        gate_steps = d // gate_rows
        assert gate_steps <= stage
        args += [*ps, gp.reshape(1, d), wg, wp]
        specs += [pl.BlockSpec((1, tm, ps[0].shape[2]), lambda i, m=m: (layer, *m(i))) for m in pair]
        specs += [_resident((1, d)),
                  pl.BlockSpec((1, gate_rows, d), lambda i: (layer, jnp.minimum(i, gate_steps - 1), 0)),
                  _resident_layer(wp.shape, layer)]
        scratch += [pltpu.VMEM((d, d), BF16), pltpu.VMEM(wp.shape[1:], BF16)]
    if final_gain is not None:
        args.append(final_gain.reshape(1, d))
        specs.append(_resident((1, d)))
    if split_out:
        out_shape = [jax.ShapeDtypeStruct((r, d), F32) for r in n_rows]
        out_specs = [pl.BlockSpec((tm, d), m) for m in pair]
    else:
        out_shape = jax.ShapeDtypeStruct((sum(n_rows), d), F32)
        out_specs = pl.BlockSpec((tm, d), stacked)
    return pl.pallas_call(
        functools.partial(_ffn_kernel, n_x=len(xs), has_ple=ple is not None, has_final=final_gain is not None,
                          n_o=2 if split_out else 1, prompt_tiles=pt, stage_steps=stage, gate_steps=gate_steps),
        out_shape=out_shape,
        grid=(stage + pt + st,),
        in_specs=specs,
        out_specs=out_specs,
        scratch_shapes=scratch,
        compiler_params=_params(1, V7X_VMEM_BUDGET_BYTES),
        name=name,
    )(*args)


def _inproj_kernel(*refs, n_out, chunk_cols):
    x_ref, g_ref = refs[0], refs[1]
    w_refs = refs[2:2 + n_out]
    o_refs = refs[2 + n_out:2 + 2 * n_out]
    xn = _rms(x_ref[...], g_ref[...]).astype(BF16)
    for w_ref, o_ref, cw in zip(w_refs, o_refs, chunk_cols):
        for c in range(w_ref.shape[1] // cw):
            o_ref[:, c * cw:(c + 1) * cw] = _dot(xn, w_ref[:, c * cw:(c + 1) * cw]).astype(o_ref.dtype)


def _inproj_call(x, gain, weights, out_dtypes, chunk_cols, name):
    n, d = x.shape
    tm = ROW_TILE
    row = lambda i: (i, 0)
    specs = [pl.BlockSpec((tm, d), row), _resident((1, d))] + [_resident(w.shape) for w in weights]
    out_shape = [jax.ShapeDtypeStruct((n, w.shape[1]), dt) for w, dt in zip(weights, out_dtypes)]
    out_specs = [pl.BlockSpec((tm, w.shape[1]), row) for w in weights]
    vmem = (sum(2 * w.size for w in weights) + 4 * tm * d * 4
            + sum(2 * tm * w.shape[1] * jnp.dtype(dt).itemsize for w, dt in zip(weights, out_dtypes))
            + 2 * tm * max(chunk_cols) * 4 + (4 << 20))
    return pl.pallas_call(
        functools.partial(_inproj_kernel, n_out=len(weights), chunk_cols=tuple(chunk_cols)),
        out_shape=out_shape,
        grid=(n // tm,),
        in_specs=specs,
        out_specs=out_specs,
        compiler_params=_params(1, vmem),
        name=name,
    )(x, gain.reshape(1, d), *weights)


def _inproj_staged_kernel(x_ref, g_ref, w_ref, o_ref, w_s, *, stage_steps):
    i = pl.program_id(0)
    cols = w_s.shape[2]

    @pl.when(i < stage_steps)
    def _():
        w_s[i] = w_ref[0].astype(BF16)

    @pl.when(i >= stage_steps)
    def _():
        xn = _rms(x_ref[...], g_ref[...]).astype(BF16)
        for c in range(stage_steps):
            o_ref[:, c * cols:(c + 1) * cols] = _dot(xn, w_s[c]).astype(o_ref.dtype)


def _inproj_staged_call(x, gain, w, layer, cols, name):
    n, d = x.shape
    n_out = w.shape[2]
    tm = ROW_TILE
    assert n_out % cols == 0
    stage = n_out // cols
    row = lambda i: (jnp.maximum(i - stage, 0), 0)
    vmem = 2 * d * n_out + 2 * d * cols * 4 + 4 * tm * d * 4 + 2 * tm * n_out * 2 + 2 * tm * cols * 4 + (4 << 20)
    return pl.pallas_call(
        functools.partial(_inproj_staged_kernel, stage_steps=stage),
        out_shape=jax.ShapeDtypeStruct((n, n_out), BF16),
        grid=(stage + n // tm,),
        in_specs=[pl.BlockSpec((tm, d), row), _resident((1, d)),
                  pl.BlockSpec((1, d, cols), lambda i: (layer, 0, jnp.minimum(i, stage - 1)))],
        out_specs=pl.BlockSpec((tm, n_out), row),
        scratch_shapes=[pltpu.VMEM((stage, d, cols), BF16)],
        compiler_params=_params(1, vmem),
        name=name,
    )(x, gain.reshape(1, d), w)


GLA_MATMUL_LEVELS = (1, 2)


def _gla_constants(n_seq, ls, width):
    c = n_seq * ls
    nlev = int(round(math.log2(ls)))
    assert 2 ** nlev == ls and nlev > max(GLA_MATMUL_LEVELS)
    t = np.arange(c)[:, None]
    i = np.arange(c)[None, :]
    w_rows = [((t // ls) == (i // ls)) & (i <= t)]
    masks, uppers = [], []
    for l in range(nlev):
        m = 2 ** l
        upper = ((t // m) % 2) == 1
        r = (t // (2 * m)) * (2 * m) + m - 1
        if l in GLA_MATMUL_LEVELS:
            w_rows.append(np.where(upper, (i > r) & (i <= t), (i > t) & (i <= r)))
        masks.append(((t // (2 * m)) == (i // (2 * m))) & upper & (((i // m) % 2) == 0))
        uppers.append(np.broadcast_to(upper, (c, width)))
    w = jnp.asarray(np.concatenate(w_rows, 0).astype(np.float32), BF16)
    return w, jnp.asarray(np.stack(masks).astype(np.float32)), jnp.asarray(np.stack(uppers).astype(np.float32))


def _gla_block(q, k, v, g, st_list, w_ref, m_ref, u_ref, ls, transposed_state, dk):
    c, width = q.shape
    nh = v.shape[1] // LANES
    hpg = LANES // dk
    n_groups = width // LANES
    n_seq = c // ls
    nlev = m_ref.shape[0]
    lane = lax.broadcasted_iota(jnp.int32, (1, LANES), 1)
    own = [jnp.where((lane >= i * dk) & (lane < (i + 1) * dk), 1.0, 0.0) for i in range(hpg)]
    g_hi32 = g.astype(BF16).astype(F32)
    g_lo32 = g - g_hi32
    g_hi, g_lo = g_hi32.astype(BF16), g_lo32.astype(BF16)
    ex2 = _dot(w_ref[...], jnp.concatenate([g_hi, g_lo], axis=1))
    ex = ex2[:, :width] + ex2[:, width:]
    b = ex[:c]

    def row(r, n):
        return jnp.broadcast_to(b[r:r + 1, :], (n, width))

    b_end = jnp.concatenate([row((s + 1) * ls - 1, ls) for s in range(n_seq)], axis=0) if n_seq > 1 else row(c - 1, c)
    e_cum = jnp.exp2(b)
    qe = q * e_cum
    kr = k * jnp.exp2(b_end - b)

    att = [None] * nh
    for l in range(nlev):
        m = 2 ** l
        if l == 0:
            x = jnp.where(u_ref[l] > 0.5, q * jnp.exp2(g), k)
        elif l in GLA_MATMUL_LEVELS:
            j = 1 + GLA_MATMUL_LEVELS.index(l)
            x = jnp.where(u_ref[l] > 0.5, q, k) * jnp.exp2(ex[j * c:(j + 1) * c])
        else:
            pieces = []
            for p in range(c // (2 * m)):
                lo, mid, hi = p * 2 * m, p * 2 * m + m, (p + 1) * 2 * m
                pivot = row(mid - 1, m)
                pieces.append(k[lo:mid] * jnp.exp2(pivot - b[lo:mid]))
                pieces.append(q[mid:hi] * jnp.exp2(b[mid:hi] - pivot))
            x = jnp.concatenate(pieces, axis=0)
        x = x.astype(BF16)
        mask = m_ref[l] > 0.5
        for h in range(nh):
            grp, i = divmod(h, hpg)
            xg = x[:, grp * LANES:(grp + 1) * LANES]
            xl = xg if hpg == 1 else xg * own[i].astype(BF16)
            att[h] = jnp.where(mask, _dot_nt(xl, xg), 0.0 if att[h] is None else att[h])

    qk = q * k
    outs = [[None] * n_seq for _ in range(nh)]
    st_new = [[None] * n_groups for _ in range(n_seq)]
    for grp in range(n_groups):
        gl = slice(grp * LANES, (grp + 1) * LANES)
        heads = []
        for i in range(hpg):
            h = grp * hpg + i
            vh = v[:, h * LANES:(h + 1) * LANES]
            sel = 1.0 if hpg == 1 else own[i]
            o = (_dot(att[h].astype(BF16), vh.astype(BF16))
                 + jnp.sum(qk[:, gl] * sel, axis=1, keepdims=True) * vh)
            heads.append((h, vh, o, qe[:, gl] * sel, kr[:, gl] * sel))
        for s in range(n_seq):
            rows = slice(s * ls, (s + 1) * ls)
            st = st_list[s][grp]
            if transposed_state:
                new = e_cum[(s + 1) * ls - 1:(s + 1) * ls, gl] * st
            else:
                g2 = jnp.concatenate([g_hi32[rows, gl], g_lo32[rows, gl]], axis=0).astype(BF16)
                new = jnp.exp2(_dot_tn(g2, jnp.ones((2 * ls, LANES), BF16))) * st
            for h, vh, o, qe_h, kr_h in heads:
                if transposed_state:
                    outs[h][s] = o[rows] + _dot_nt(qe_h[rows].astype(BF16), st.astype(BF16))
                    new = new + _dot_tn(vh[rows].astype(BF16), kr_h[rows].astype(BF16))
                else:
                    outs[h][s] = o[rows] + _dot(qe_h[rows].astype(BF16), st.astype(BF16))
                    new = new + _dot_tn(kr_h[rows].astype(BF16), vh[rows].astype(BF16))
            st_new[s][grp] = new
    outs = [parts[0] if n_seq == 1 else jnp.concatenate(parts, axis=0) for parts in outs]
    return outs, st_new


def _hgrn_inputs(refs, rows, q_scale):
    del q_scale
    qa_ref, ia_ref, ga_ref, fa_ref, lg_ref = refs
    lg = lg_ref[...]
    ex = jnp.exp(lg - jnp.max(lg, axis=0, keepdims=True))
    lb = ex[0:1] / jnp.sum(ex, axis=0, keepdims=True)
    sig = jax.nn.sigmoid(fa_ref[rows, :])
    q = _silu(qa_ref[rows, :].astype(F32))
    k = (1.0 - lb) * (1.0 - sig)
    g = jnp.log(lb + (1.0 - lb) * sig) * LOG2_E
    return q, k, ia_ref[rows, :].astype(F32), g, ga_ref[rows, :].astype(F32)


def _gla_inputs(refs, rows, q_scale):
    qb_ref, kb_ref, vb_ref, gb_ref, lr_ref, wgk_ref, bgk_ref = refs
    z = _dot(lr_ref[rows, :].astype(BF16), wgk_ref[...]) + bgk_ref[...]
    g = (jnp.minimum(z, 0.0) - jnp.log1p(jnp.exp(-jnp.abs(z)))) * (LOG2_E / GLA_GATE_NORM)
    q = qb_ref[rows, :].astype(F32) * q_scale
    return q, kb_ref[rows, :].astype(F32), vb_ref[rows, :].astype(F32), g, gb_ref[rows, :].astype(F32)


def _head_out(outs, gate, gn):
    ys = [_rms(o, gn) * _silu(gate[:, h * LANES:(h + 1) * LANES]) for h, o in enumerate(outs)]
    return jnp.concatenate(ys, axis=1).astype(BF16)


def _stage_out_weight(wo_ref, wo_s, first):
    @pl.when(first)
    def _():
        wo_s[...] = wo_ref[0].astype(BF16)


def _project_onto_residual(x_ref, y_prev_ref, y, wo_s):
    k_prev = y_prev_ref.shape[1]
    return x_ref[...] + _dot(y_prev_ref[...], wo_s[:k_prev, :]) + _dot(y, wo_s[k_prev:, :])


def _ab_prompt_kernel(*refs, kind, n_in, chunk, q_scale, project):
    in_refs = refs[:n_in]
    if project:
        gn_ref, w_ref, m_ref, u_ref, yp_ref, x_ref, wo_ref, o_ref, s_ref, st_ref, y_s, wo_s = refs[n_in:]
    else:
        gn_ref, w_ref, m_ref, u_ref, o_ref, s_ref, st_ref = refs[n_in:]
        y_s = o_ref
    t = pl.program_id(1)
    n_groups = st_ref.shape[0]
    nh, dk = s_ref.shape[1:3]
    hpg = nh // n_groups
    if project:
        _stage_out_weight(wo_ref, wo_s, jnp.logical_and(pl.program_id(0) == 0, t == 0))

    @pl.when(t == 0)
    def _():
        st_ref[...] = jnp.zeros_like(st_ref)

    load = _hgrn_inputs if kind == "hgrn" else _gla_inputs
    st = [st_ref[j] for j in range(n_groups)]
    for c in range(y_s.shape[0] // chunk):
        rows = slice(c * chunk, (c + 1) * chunk)
        q, k, v, g, gate = load(in_refs, rows, q_scale)
        outs, (st,) = _gla_block(q, k, v, g, [st], w_ref, m_ref, u_ref, chunk, True, dk)
        y_s[rows, :] = _head_out(outs, gate, gn_ref[...])
    for j in range(n_groups):
        st_ref[j] = st[j]
    if project:
        o_ref[...] = _project_onto_residual(x_ref, yp_ref, y_s[...], wo_s)

    @pl.when(t == pl.num_programs(1) - 1)
    def _():
        for h in range(nh):
            grp, i = divmod(h, hpg)
            s_ref[0, h] = st_ref[grp].T[i * dk:(i + 1) * dk, :]


def _ab_sample_kernel(*refs, kind, n_in, ls, q_scale, project):
    in_refs = refs[:n_in]
    if project:
        gn_ref, w_ref, m_ref, u_ref, s0_ref, yp_ref, x_ref, wo_ref, o_ref, s_ref, wo_s = refs[n_in:]
        _stage_out_weight(wo_ref, wo_s, pl.program_id(0) == 0)
    else:
        gn_ref, w_ref, m_ref, u_ref, s0_ref, o_ref, s_ref = refs[n_in:]
    load = _hgrn_inputs if kind == "hgrn" else _gla_inputs
    n_seq, nh, dk = s0_ref.shape[:3]
    hpg = LANES // dk
    q, k, v, g, gate = load(in_refs, slice(None), q_scale)
    st0 = [[jnp.concatenate([s0_ref[s, j * hpg + i] for i in range(hpg)], axis=0) if hpg > 1 else s0_ref[s, j]
            for j in range(nh // hpg)] for s in range(n_seq)]
    outs, st = _gla_block(q, k, v, g, st0, w_ref, m_ref, u_ref, ls, False, dk)
    y = _head_out(outs, gate, gn_ref[...])
    o_ref[...] = _project_onto_residual(x_ref, yp_ref, y, wo_s) if project else y
    for s in range(n_seq):
        for h in range(nh):
            grp, i = divmod(h, hpg)
            s_ref[s, h] = st[s][grp][i * dk:(i + 1) * dk, :]


def _ab_specs(kind, p16, p32, extra, tok_block, tok_index, nh, dk):
    wide = nh * LANES
    kw = nh * dk

    def col(start, w):
        assert start % w == 0
        return pl.BlockSpec((tok_block, w), lambda *ids: (tok_index(*ids), start // w))

    if kind == "hgrn":
        (logits,) = extra
        args = [p16, p16, p16, p32, logits]
        specs = [col(0, wide), col(wide, wide), col(2 * wide, wide), col(0, kw), _resident(logits.shape)]
    else:
        w_gk, b_gk = extra
        a16 = 3 * H_A * LANES
        args = [p16, p16, p16, p16, p32, w_gk, b_gk]
        specs = [col(a16, kw), col(a16 + kw, kw), col(a16 + 2 * kw, wide), col(a16 + 2 * kw + wide, wide),
                 col(H_A * LANES, LANES), _resident(w_gk.shape), _resident(b_gk.shape)]
    return args, specs


def _out_weight_spec(w_out, layer, part, rows):
    return pl.BlockSpec((1, rows, w_out.shape[2]), lambda *_: (layer, part, 0), pipeline_mode=pl.Buffered(1))


def _ab_prompt_call(kind, p16, p32, extra, gn, batch, seq, dk, name, project=None):
    nh = H_A if kind == "hgrn" else H_B
    tt = GLA_STEP_TOKENS
    nt = seq // tt
    tok = lambda b, t: (b * nt + t, 0)
    consts = _gla_constants(1, GLA_CHUNK, nh * dk)
    args, specs = _ab_specs(kind, p16, p32, extra, tt, lambda b, t: b * nt + t, nh, dk)
    n_in = len(args)
    args += [gn.reshape(1, LANES), *consts]
    specs += [_resident((1, LANES))] + [_resident(c.shape) for c in consts]
    scratch = [pltpu.VMEM((nh * dk // LANES, LANES, LANES), F32)]
    if project is None:
        out = jax.ShapeDtypeStruct((batch * seq, nh * LANES), BF16)
    else:
        y_prev, x, w_out, layer = project
        d = x.shape[1]
        args += [y_prev, x, w_out]
        specs += [pl.BlockSpec((tt, y_prev.shape[1]), tok), pl.BlockSpec((tt, d), tok),
                  _out_weight_spec(w_out, layer, 0, w_out.shape[1])]
        scratch += [pltpu.VMEM((tt, nh * LANES), BF16), pltpu.VMEM((w_out.shape[1], d), BF16)]
        out = jax.ShapeDtypeStruct((batch * seq, d), F32)
    return pl.pallas_call(
        functools.partial(_ab_prompt_kernel, kind=kind, n_in=n_in, chunk=GLA_CHUNK, q_scale=dk ** -0.5,
                          project=project is not None),
        out_shape=[out, jax.ShapeDtypeStruct((batch, nh, dk, LANES), F32)],
        grid=(batch, nt),
        in_specs=specs,
        out_specs=[pl.BlockSpec((tt, out.shape[1]), tok),
                   pl.BlockSpec((1, nh, dk, LANES), lambda b, t: (b, 0, 0, 0))],
        scratch_shapes=scratch,
        compiler_params=_params(2, 40 << 20),
        name=name,
    )(*args)


def _ab_sample_call(kind, p16, p32, extra, gn, s0, row0, seq, name, project=None):
    batch, nh, dk, _ = s0.shape
    nb = SAMPLE_SEQS
    rows = nb * seq
    blk0 = row0 // rows
    consts = _gla_constants(nb, seq, nh * dk)
    args, specs = _ab_specs(kind, p16, p32, extra, rows, lambda i: blk0 + i, nh, dk)
    n_in = len(args)
    args += [gn.reshape(1, LANES), *consts, s0]
    state_spec = pl.BlockSpec((nb, nh, dk, LANES), lambda i: (i, 0, 0, 0))
    specs += [_resident((1, LANES))] + [_resident(c.shape) for c in consts] + [state_spec]
    scratch = []
    if project is None:
        out = jax.ShapeDtypeStruct((batch * seq, nh * LANES), BF16)
    else:
        y_prev, x, w_out, layer = project
        d = x.shape[1]
        args += [y_prev, x, w_out]
        specs += [pl.BlockSpec((rows, y_prev.shape[1]), lambda i: (i, 0)),
                  pl.BlockSpec((rows, d), lambda i: (blk0 + i, 0)), _out_weight_spec(w_out, layer, 0, w_out.shape[1])]
        scratch = [pltpu.VMEM((w_out.shape[1], d), BF16)]
        out = jax.ShapeDtypeStruct((batch * seq, d), F32)
    return pl.pallas_call(
        functools.partial(_ab_sample_kernel, kind=kind, n_in=n_in, ls=seq, q_scale=dk ** -0.5,
                          project=project is not None),
        out_shape=[out, jax.ShapeDtypeStruct((batch, nh, dk, LANES), F32)],
        grid=(batch // nb,),
        in_specs=specs,
        out_specs=[pl.BlockSpec((rows, out.shape[1]), lambda i: (i, 0)), state_spec],
        scratch_shapes=scratch,
        compiler_params=_params(1, 40 << 20),
        name=name,
    )(*args)


def _ret_constants(n_seq, ls):
    c = n_seq * ls
    log_gamma = np.log1p(-np.exp2(-5.0 - np.arange(H_C, dtype=np.float64)))[:, None, None]
    t = np.arange(c)[:, None]
    s = np.arange(c)[None, :]
    causal = ((t // ls) == (s // ls)) & (s <= t)
    dmask = np.where(causal, np.exp(log_gamma * np.where(causal, t - s, 0)), 0.0)
    pos = (np.arange(c) % ls)[None, :, None]
    scales = np.stack([np.exp(log_gamma * (pos + 1)), np.exp(log_gamma * (ls - 1 - pos))], axis=1)
    scales = np.broadcast_to(scales, (H_C, 2, c, LANES))
    return jnp.asarray(dmask, F32), jnp.asarray(scales, F32)


def _rope_tables(positions, half):
    inv = ROPE_THETA ** (-jnp.arange(half, dtype=F32) / half)
    ang = positions.astype(F32)[:, None] * inv[None, :]
    return jnp.cos(ang), jnp.sin(ang)


def _rope(x, cos, sin):
    half = x.shape[1] // 2
    x1, x2 = x[:, :half], x[:, half:]
    return jnp.concatenate([x1 * cos - x2 * sin, x2 * cos + x1 * sin], axis=1)


def _ret_block(q16, k16, v, cos, sin, dmask, q_scale, k_scale, s_list, ls):
    dk = q16.shape[1]
    q = _rope(q16.astype(F32), cos, sin)
    k = _rope(k16.astype(F32), cos, sin) * (dk ** -0.5)
    reps = dk // LANES
    q_sc = jnp.concatenate([q_scale] * reps, axis=1)
    k_sc = jnp.concatenate([k_scale] * reps, axis=1)
    att = _dot_nt(q.astype(BF16), k.astype(BF16)) * dmask
    o = _dot(att.astype(BF16), v)
    qs = q * q_sc
    ks = k * k_sc
    decay = q_scale[ls - 1:ls, 0:1]
    n_seq = q.shape[0] // ls
    v32 = v if n_seq == 1 else v.astype(F32)
    o_parts, s_new = [], []
    for b in range(n_seq):
        rows = slice(b * ls, (b + 1) * ls)
        s = s_list[b]
        o_parts.append(o[rows] + _dot(qs[rows].astype(BF16), s.astype(BF16)))
        s_new.append(decay * s + _dot_tn(ks[rows].astype(BF16), v32[rows].astype(BF16)))
    o = o_parts[0] if n_seq == 1 else jnp.concatenate(o_parts, axis=0)
    return o, s_new


def _ret_head_out(o, gate, gn):
    oc = o - jnp.mean(o, axis=-1, keepdims=True)
    y = oc * lax.rsqrt(jnp.mean(oc * oc, axis=-1, keepdims=True) + EPS) * gn
    return (y * _silu(gate)).astype(BF16)


def _ret_prompt_kernel(q_ref, k_ref, v_ref, g_ref, cos_ref, sin_ref, dm_ref, sc_ref, gn_ref, x_ref, wo_ref,
                       xo_ref, s_ref, st_ref, y_s, wo_s):
    t = pl.program_id(1)
    nh, dk, dv = st_ref.shape
    c = dm_ref.shape[1]
    _stage_out_weight(wo_ref, wo_s, jnp.logical_and(pl.program_id(0) == 0, t == 0))

    @pl.when(t == 0)
    def _():
        st_ref[...] = jnp.zeros_like(st_ref)

    for h in range(nh):
        s = st_ref[h]
        for j in range(q_ref.shape[0] // c):
            rows = slice(j * c, (j + 1) * c)
            o, (s,) = _ret_block(q_ref[rows, h * dk:(h + 1) * dk], k_ref[rows, h * dk:(h + 1) * dk],
                                 v_ref[rows, h * dv:(h + 1) * dv], cos_ref[rows, :], sin_ref[rows, :], dm_ref[h],
                                 sc_ref[h, 0], sc_ref[h, 1], [s], c)
            y_s[rows, h * dv:(h + 1) * dv] = _ret_head_out(o, g_ref[rows, h * dv:(h + 1) * dv].astype(F32),
                                                           gn_ref[...])
        st_ref[h] = s
    xo_ref[...] = x_ref[...] + _dot(y_s[...], wo_s[...])

    @pl.when(t == pl.num_programs(1) - 1)
    def _():
        s_ref[0] = st_ref[...]


def _ret_sample_kernel(q_ref, k_ref, v_ref, g_ref, cos_ref, sin_ref, dm_ref, sc_ref, gn_ref, s0_ref, x_ref,
                       wo_ref, xo_ref, s_ref, wo_s, *, ls):
    h = pl.program_id(1)
    dv = v_ref.shape[1]
    _stage_out_weight(wo_ref, wo_s, jnp.logical_and(pl.program_id(0) == 0, h == 0))
    n_seq = q_ref.shape[0] // ls
    o, s = _ret_block(q_ref[...], k_ref[...], v_ref[...], cos_ref[...], sin_ref[...], dm_ref[0],
                      sc_ref[0, 0], sc_ref[0, 1], [s0_ref[b, 0] for b in range(n_seq)], ls)
    y = _ret_head_out(o, g_ref[...].astype(F32), gn_ref[...])
    proj = _dot(y, wo_s[pl.ds(pl.multiple_of(h * dv, dv), dv), :])

    @pl.when(h == 0)
    def _():
        xo_ref[...] = x_ref[...] + proj

    @pl.when(h > 0)
    def _():
        xo_ref[...] += proj

    for b in range(n_seq):
        s_ref[b, 0] = s[b]


def _ret_prompt_call(r16, gn, x, w_out, layer, batch, seq, dk, dv, name):
    c = RET_STEP_TOKENS
    nt = seq // c
    d = x.shape[1]
    dmask, scales = _ret_constants(1, RET_CHUNK)
    cos, sin = _rope_tables(jnp.arange(seq, dtype=jnp.int32), dk // 2)
    tok = lambda b, t: b * nt + t
    nq, nv = H_C * dk, H_C * dv
    args = [r16, r16, r16, r16, cos, sin, dmask, scales, gn.reshape(1, dv), x, w_out]
    specs = [pl.BlockSpec((c, nq), lambda b, t: (tok(b, t), 0)),
             pl.BlockSpec((c, nq), lambda b, t: (tok(b, t), 1)),
             pl.BlockSpec((c, nv), lambda b, t: (tok(b, t), 2 * nq // nv)),
             pl.BlockSpec((c, nv), lambda b, t: (tok(b, t), 2 * nq // nv + 1)),
             pl.BlockSpec((c, dk // 2), lambda b, t: (t, 0)), pl.BlockSpec((c, dk // 2), lambda b, t: (t, 0)),
             _resident(dmask.shape), _resident(scales.shape), _resident((1, dv)),
             pl.BlockSpec((c, d), lambda b, t: (tok(b, t), 0)), _out_weight_spec(w_out, layer, 0, nv)]
    return pl.pallas_call(
        _ret_prompt_kernel,
        out_shape=[jax.ShapeDtypeStruct((batch * seq, d), F32),
                   jax.ShapeDtypeStruct((batch, H_C, dk, dv), F32)],
        grid=(batch, nt),
        in_specs=specs,
        out_specs=[pl.BlockSpec((c, d), lambda b, t: (tok(b, t), 0)),
                   pl.BlockSpec((1, H_C, dk, dv), lambda b, t: (b, 0, 0, 0))],
        scratch_shapes=[pltpu.VMEM((H_C, dk, dv), F32), pltpu.VMEM((c, nv), BF16), pltpu.VMEM((nv, d), BF16)],
        compiler_params=_params(2, 48 << 20),
        name=name,
    )(*args)


def _ret_sample_call(r16, gn, s0, row0, batch, seq, dk, dv, x, w_out, layer, name):
    nb = SAMPLE_SEQS
    rows = nb * seq
    blk0 = row0 // rows
    d = x.shape[1]
    dmask, scales = _ret_constants(nb, seq)
    pos = PAST_LEN + (jnp.arange(rows, dtype=jnp.int32) % seq)
    cos, sin = _rope_tables(pos, dk // 2)
    nq = H_C * dk

    def col(width, base):
        return pl.BlockSpec((rows, width), lambda i, h: (blk0 + i, base + h))

    state_spec = pl.BlockSpec((nb, 1, dk, dv), lambda i, h: (i, h, 0, 0))
    args = [r16, r16, r16, r16, cos, sin, dmask, scales, gn.reshape(1, dv), s0, x, w_out]
    specs = [col(dk, 0), col(dk, H_C), col(dv, 2 * nq // dv), col(dv, 2 * nq // dv + H_C),
             _resident(cos.shape), _resident(sin.shape),
             pl.BlockSpec((1, rows, rows), lambda i, h: (h, 0, 0)),
             pl.BlockSpec((1, 2, rows, LANES), lambda i, h: (h, 0, 0, 0)), _resident((1, dv)), state_spec,
             pl.BlockSpec((rows, d), lambda i, h: (blk0 + i, 0)), _out_weight_spec(w_out, layer, 0, H_C * dv)]
    return pl.pallas_call(
        functools.partial(_ret_sample_kernel, ls=seq),
        out_shape=[jax.ShapeDtypeStruct((batch * seq, d), F32),
                   jax.ShapeDtypeStruct((batch, H_C, dk, dv), F32)],
        grid=(batch // nb, H_C),
        in_specs=specs,
        out_specs=[pl.BlockSpec((rows, d), lambda i, h: (i, 0)), state_spec],
        scratch_shapes=[pltpu.VMEM((H_C * dv, d), BF16)],
        compiler_params=_params(2, 56 << 20),
        name=name,
    )(*args)


def _prepare_ab_weights(w_in, w_gk2, b_gk, dk_a, dv_a, dk_b, dv_b):
    sizes = (H_A * dk_a, H_A * dk_a, H_A * dv_a, H_A * dv_a, H_B * dk_b, H_B * dk_b, H_B * dv_b,
             GLA_RANK, H_B * dv_b)
    qa, fa, ia, ga, qb, kb, vb, lrb, gb = jnp.split(w_in, [int(v) for v in np.cumsum(sizes)[:-1]], axis=1)
    w16 = jnp.concatenate([qa, ia, ga, qb, kb, vb, gb], axis=1).astype(BF16)
    w32 = jnp.concatenate([fa, jnp.pad(lrb, ((0, 0), (0, LANES - GLA_RANK)))], axis=1).astype(BF16)
    w_gk = jnp.pad(w_gk2, ((0, LANES - GLA_RANK), (0, 0))).astype(BF16)
    return w16, w32, w_gk, b_gk.reshape(1, -1)


def kernel(x_prompt, x_sample, state_hgrn, state_gla, state_ret, p_prompt, p_sample, norm_ffn1, ffn1_w_in, ffn1_w_out, norm_mix, ab_w_in, ab_w_gk2, ab_b_gk, hgrn_lb_logits, ab_gn_hgrn, ab_gn_gla, ab_w_out, ret_w_in, ret_gn, ret_w_out, norm_ffn2, ffn2_w_in, ffn2_w_out, norm_ple, ple_w_gate, ple_w_proj, norm_final):
    bp, lp, d = x_prompt.shape
    bs, ls, _ = x_sample.shape
    depth = norm_ffn1.shape[0]
    n_prompt = bp * lp
    dk_a, dv_a = state_hgrn.shape[-2:]
    dk_b, dv_b = state_gla.shape[-2:]
    dk_c, dv_c = state_ret.shape[-2:]
    assert dk_a == dv_a == dv_b == LANES and LANES % dk_b == 0 and H_B % (LANES // dk_b) == 0
    assert state_hgrn.shape[0] == 1

    n_rows = (n_prompt, bs * ls)
    xs = (x_prompt.reshape(n_prompt, d), x_sample.reshape(bs * ls, d))
    ps = (p_prompt.reshape(depth, n_prompt, -1), p_sample.reshape(depth, bs * ls, -1))

    new_hgrn_p, new_gla_p, new_ret_p, new_hgrn_s, new_gla_s, new_ret_s = [], [], [], [], [], []
    for i in range(depth):
        j = i // 2
        x = _ffn_call(xs if i == 0 else (x,), n_rows, norm_ffn1[i], ffn1_w_in, ffn1_w_out, i, name=f"ffn1_{i}")
        if i % 2 == 0:
            w16, w32, w_gk, b_gk = _prepare_ab_weights(ab_w_in[j], ab_w_gk2[j], ab_b_gk[j], dk_a, dv_a, dk_b, dv_b)
            p16, p32 = _inproj_call(x, norm_mix[i], [w16, w32], [BF16, F32], [w16.shape[1] // 2, w32.shape[1]],
                                    name=f"ab_in_{i}")
            hgrn = ("hgrn", p16, p32, (hgrn_lb_logits,), ab_gn_hgrn[j])
            gla = ("gla", p16, p32, (w_gk, b_gk), ab_gn_gla[j])
            ya_p, sa_p = _ab_prompt_call(*hgrn, bp, lp, dk_a, f"hgrn_p_{i}")
            ya_s, sa_s = _ab_sample_call(*hgrn, state_hgrn[j], n_prompt, ls, f"hgrn_s_{i}")
            x_p, sb_p = _ab_prompt_call(*gla, bp, lp, dk_b, f"gla_p_{i}", project=(ya_p, x, ab_w_out, j))
            x_s, sb_s = _ab_sample_call(*gla, state_gla[j], n_prompt, ls, f"gla_s_{i}",
                                        project=(ya_s, x, ab_w_out, j))
            new_hgrn_p.append(sa_p)
            new_hgrn_s.append(sa_s)
            new_gla_p.append(sb_p)
            new_gla_s.append(sb_s)
        else:
            r16 = _inproj_staged_call(x, norm_mix[i], ret_w_in, j, 2 * V7X_MXU_COLS, name=f"ret_in_{i}")
            x_p, sc_p = _ret_prompt_call(r16, ret_gn[j], x, ret_w_out, j, bp, lp, dk_c, dv_c, f"ret_p_{i}")
            x_s, sc_s = _ret_sample_call(r16, ret_gn[j], state_ret[j], n_prompt, bs, ls, dk_c, dv_c, x, ret_w_out, j,
                                         f"ret_s_{i}")
            new_ret_p.append(sc_p)
            new_ret_s.append(sc_s)
        last = i == depth - 1
        x = _ffn_call((x_p, x_s), n_rows, norm_ffn2[i], ffn2_w_in, ffn2_w_out, i,
                      ple=(ps, norm_ple[i], ple_w_gate, ple_w_proj),
                      final_gain=norm_final if last else None, split_out=last, name=f"ffn2_{i}")

    y_prompt = x[0].reshape(bp, lp, d)
    y_sample = x[1].reshape(bs, ls, d)
    def layers(states):
        return states[0][None] if len(states) == 1 else jnp.stack(states)

    return (y_prompt, y_sample, layers(new_hgrn_p), layers(new_gla_p), layers(new_ret_p),
            layers(new_hgrn_s), layers(new_gla_s), layers(new_ret_s))
```

```python
import functools
import math

import numpy as np
import jax
import jax.numpy as jnp
from jax import lax
from jax.experimental import pallas as pl
from jax.experimental.pallas import tpu as pltpu

F32 = jnp.float32
BF16 = jnp.bfloat16

H_A = 4
H_B = 4
H_C = 4
GLA_RANK = 16
GLA_GATE_NORM = 16.0
ROPE_THETA = 10000.0
PAST_LEN = 16384
EPS = 1e-6
LOG2_E = math.log2(math.e)

LANES = 128
V7X_MXU_COLS = 256
V7X_VMEM_BUDGET_BYTES = 56 * 1024 * 1024

ROW_TILE = 512
GLA_CHUNK = 128
GLA_STEP_TOKENS = 512
RET_CHUNK = 256
RET_STEP_TOKENS = 512
SAMPLE_SEQS = 16


def _params(n_axes, vmem_bytes):
    return pltpu.CompilerParams(dimension_semantics=("arbitrary",) * n_axes,
                                vmem_limit_bytes=min(int(vmem_bytes), V7X_VMEM_BUDGET_BYTES))


def _resident(shape):
    nd = len(shape)
    return pl.BlockSpec(shape, lambda *_: (0,) * nd, pipeline_mode=pl.Buffered(1))


def _resident_layer(shape, layer):
    nd = len(shape)
    return pl.BlockSpec((1,) + tuple(shape[1:]), lambda *_: (layer,) + (0,) * (nd - 1),
                        pipeline_mode=pl.Buffered(1))


def _row_maps(prompt_tiles, lead=0):
    stacked = lambda i: (jnp.maximum(i - lead, 0), 0)
    prompt = lambda i: (jnp.clip(i - lead, 0, prompt_tiles - 1), 0)
    sample = lambda i: (jnp.maximum(i - lead - prompt_tiles, 0), 0)
    return stacked, prompt, sample


def _rms(x, g):
    return x * lax.rsqrt(jnp.mean(x * x, axis=-1, keepdims=True) + EPS) * g


def _silu(x):
    return x * jax.nn.sigmoid(x)


def _dot(a, b):
    return jnp.dot(a, b, preferred_element_type=F32)


def _dot_nt(a, b):
    return lax.dot_general(a, b, (((1,), (1,)), ((), ())), preferred_element_type=F32)


def _dot_tn(a, b):
    return lax.dot_general(a, b, (((0,), (0,)), ((), ())), preferred_element_type=F32)


def _pick(is_prompt, refs, index=None):
    vals = [r[...] if index is None else r[index] for r in refs]
    return vals[0] if len(vals) == 1 else jnp.where(is_prompt, vals[0], vals[1])


def _ffn_kernel(*refs, n_x, has_ple, has_final, n_o, prompt_tiles, stage_steps, gate_steps):
    it = iter(refs)
    x_refs = [next(it) for _ in range(n_x)]
    g_ref, win_ref, wout_ref = next(it), next(it), next(it)
    if has_ple:
        p_refs = [next(it), next(it)]
        gp_ref, wg_ref, wp_ref = next(it), next(it), next(it)
    if has_final:
        gf_ref = next(it)
    o_refs = [next(it) for _ in range(n_o)]
    win_s, wout_s, act_ref = next(it), next(it), next(it)
    if has_ple:
        wg_s, wp_s = next(it), next(it)
    i = pl.program_id(0)
    cols = win_s.shape[2]
    n_half = win_s.shape[0] // 2

    @pl.when(i < stage_steps)
    def _():
        blk = win_ref[0]
        win_s[2 * i] = blk[:, :cols].astype(BF16)
        win_s[2 * i + 1] = blk[:, cols:].astype(BF16)
        wout_s[pl.ds(pl.multiple_of(i * cols, cols), cols), :] = wout_ref[0].astype(BF16)

    if has_ple:
        rows = wg_ref.shape[1]

        @pl.when(i < gate_steps)
        def _():
            wg_s[pl.ds(pl.multiple_of(i * rows, rows), rows), :] = wg_ref[0].astype(BF16)

        @pl.when(i == 0)
        def _():
            wp_s[...] = wp_ref[0].astype(BF16)

    @pl.when(i >= stage_steps)
    def _():
        is_prompt = i - stage_steps < prompt_tiles
        x = _pick(is_prompt, x_refs)
        xn = _rms(x, g_ref[...]).astype(BF16)
        for j in range(n_half):
            a = _dot(xn, win_s[j])
            b = _dot(xn, win_s[n_half + j])
            act_ref[:, j * cols:(j + 1) * cols] = (_silu(a) * b).astype(BF16)
        x = x + 0.5 * _dot(act_ref[...], wout_s[...])
        if has_ple:
            gate = jax.nn.sigmoid(_dot(_rms(x, gp_ref[...]).astype(BF16), wg_s[...]))
            x = x + gate * _dot(_pick(is_prompt, p_refs, 0).astype(BF16), wp_s[...])
        if has_final:
            x = _rms(x, gf_ref[...])
        if n_o == 1:
            o_refs[0][...] = x
        else:
            @pl.when(is_prompt)
            def _():
                o_refs[0][...] = x

            @pl.when(jnp.logical_not(is_prompt))
            def _():
                o_refs[1][...] = x


def _ffn_call(xs, n_rows, gain, w_in, w_out, layer, ple=None, final_gain=None, split_out=False, name="ffn"):
    d = xs[0].shape[1]
    d_ff = w_out.shape[1]
    tm = ROW_TILE
    cols = V7X_MXU_COLS
    assert n_rows[0] % tm == 0 and n_rows[1] % tm == 0 and d_ff % cols == 0
    pt, st = n_rows[0] // tm, n_rows[1] // tm
    stage = d_ff // cols
    stacked, prompt, sample = _row_maps(pt, stage)
    pair = [prompt, sample]
    last = stage - 1
    args = [*xs, gain.reshape(1, d), w_in, w_out]
    specs = [pl.BlockSpec((tm, d), m) for m in ([stacked] if len(xs) == 1 else pair)]
    specs += [_resident((1, d)),
              pl.BlockSpec((1, d, 2 * cols), lambda i: (layer, 0, jnp.minimum(i, last))),
              pl.BlockSpec((1, cols, d), lambda i: (layer, jnp.minimum(i, last), 0))]
    scratch = [pltpu.VMEM((2 * stage, d, cols), BF16), pltpu.VMEM((d_ff, d), BF16), pltpu.VMEM((tm, d_ff), BF16)]
    gate_steps = 0
    if ple is not None:
        ps, gp, wg, wp = ple
        gate_rows = LANES
        gate_steps = d // gate_rows
        assert gate_steps <= stage
        args += [*ps, gp.reshape(1, d), wg, wp]
        specs += [pl.BlockSpec((1, tm, ps[0].shape[2]), lambda i, m=m: (layer, *m(i))) for m in pair]
        specs += [_resident((1, d)),
                  pl.BlockSpec((1, gate_rows, d), lambda i: (layer, jnp.minimum(i, gate_steps - 1), 0)),
                  _resident_layer(wp.shape, layer)]
        scratch += [pltpu.VMEM((d, d), BF16), pltpu.VMEM(wp.shape[1:], BF16)]
    if final_gain is not None:
        args.append(final_gain.reshape(1, d))
        specs.append(_resident((1, d)))
    if split_out:
        out_shape = [jax.ShapeDtypeStruct((r, d), F32) for r in n_rows]
        out_specs = [pl.BlockSpec((tm, d), m) for m in pair]
    else:
        out_shape = jax.ShapeDtypeStruct((sum(n_rows), d), F32)
        out_specs = pl.BlockSpec((tm, d), stacked)
    return pl.pallas_call(
        functools.partial(_ffn_kernel, n_x=len(xs), has_ple=ple is not None, has_final=final_gain is not None,
                          n_o=2 if split_out else 1, prompt_tiles=pt, stage_steps=stage, gate_steps=gate_steps),
        out_shape=out_shape,
        grid=(stage + pt + st,),
        in_specs=specs,
        out_specs=out_specs,
        scratch_shapes=scratch,
        compiler_params=_params(1, V7X_VMEM_BUDGET_BYTES),
        name=name,
    )(*args)


def _inproj_kernel(*refs, n_out, chunk_cols):
    x_ref, g_ref = refs[0], refs[1]
    w_refs = refs[2:2 + n_out]
    o_refs = refs[2 + n_out:2 + 2 * n_out]
    xn = _rms(x_ref[...], g_ref[...]).astype(BF16)
    for w_ref, o_ref, cw in zip(w_refs, o_refs, chunk_cols):
        for c in range(w_ref.shape[1] // cw):
            o_ref[:, c * cw:(c + 1) * cw] = _dot(xn, w_ref[:, c * cw:(c + 1) * cw]).astype(o_ref.dtype)


def _inproj_call(x, gain, weights, out_dtypes, chunk_cols, name):
    n, d = x.shape
    tm = ROW_TILE
    row = lambda i: (i, 0)
    specs = [pl.BlockSpec((tm, d), row), _resident((1, d))] + [_resident(w.shape) for w in weights]
    out_shape = [jax.ShapeDtypeStruct((n, w.shape[1]), dt) for w, dt in zip(weights, out_dtypes)]
    out_specs = [pl.BlockSpec((tm, w.shape[1]), row) for w in weights]
    vmem = (sum(2 * w.size for w in weights) + 4 * tm * d * 4
            + sum(2 * tm * w.shape[1] * jnp.dtype(dt).itemsize for w, dt in zip(weights, out_dtypes))
            + 2 * tm * max(chunk_cols) * 4 + (4 << 20))
    return pl.pallas_call(
        functools.partial(_inproj_kernel, n_out=len(weights), chunk_cols=tuple(chunk_cols)),
        out_shape=out_shape,
        grid=(n // tm,),
        in_specs=specs,
        out_specs=out_specs,
        compiler_params=_params(1, vmem),
        name=name,
    )(x, gain.reshape(1, d), *weights)


def _inproj_staged_kernel(x_ref, g_ref, w_ref, o_ref, w_s, *, stage_steps):
    i = pl.program_id(0)
    cols = w_s.shape[2]

    @pl.when(i < stage_steps)
    def _():
        w_s[i] = w_ref[0].astype(BF16)

    @pl.when(i >= stage_steps)
    def _():
        xn = _rms(x_ref[...], g_ref[...]).astype(BF16)
        for c in range(stage_steps):
            o_ref[:, c * cols:(c + 1) * cols] = _dot(xn, w_s[c]).astype(o_ref.dtype)


def _inproj_staged_call(x, gain, w, layer, cols, name):
    n, d = x.shape
    n_out = w.shape[2]
    tm = ROW_TILE
    assert n_out % cols == 0
    stage = n_out // cols
    row = lambda i: (jnp.maximum(i - stage, 0), 0)
    vmem = 2 * d * n_out + 2 * d * cols * 4 + 4 * tm * d * 4 + 2 * tm * n_out * 2 + 2 * tm * cols * 4 + (4 << 20)
    return pl.pallas_call(
        functools.partial(_inproj_staged_kernel, stage_steps=stage),
        out_shape=jax.ShapeDtypeStruct((n, n_out), BF16),
        grid=(stage + n // tm,),
        in_specs=[pl.BlockSpec((tm, d), row), _resident((1, d)),
                  pl.BlockSpec((1, d, cols), lambda i: (layer, 0, jnp.minimum(i, stage - 1)))],
        out_specs=pl.BlockSpec((tm, n_out), row),
        scratch_shapes=[pltpu.VMEM((stage, d, cols), BF16)],
        compiler_params=_params(1, vmem),
        name=name,
    )(x, gain.reshape(1, d), w)


GLA_MATMUL_LEVELS = (1, 2)


def _gla_constants(n_seq, ls, width):
    c = n_seq * ls
    nlev = int(round(math.log2(ls)))
    assert 2 ** nlev == ls and nlev > max(GLA_MATMUL_LEVELS)
    t = np.arange(c)[:, None]
    i = np.arange(c)[None, :]
    w_rows = [((t // ls) == (i // ls)) & (i <= t)]
    masks, uppers = [], []
    for l in range(nlev):
        m = 2 ** l
        upper = ((t // m) % 2) == 1
        r = (t // (2 * m)) * (2 * m) + m - 1
        if l in GLA_MATMUL_LEVELS:
            w_rows.append(np.where(upper, (i > r) & (i <= t), (i > t) & (i <= r)))
        masks.append(((t // (2 * m)) == (i // (2 * m))) & upper & (((i // m) % 2) == 0))
        uppers.append(np.broadcast_to(upper, (c, width)))
    w = jnp.asarray(np.concatenate(w_rows, 0).astype(np.float32), BF16)
    return w, jnp.asarray(np.stack(masks).astype(np.float32)), jnp.asarray(np.stack(uppers).astype(np.float32))


def _gla_block(q, k, v, g, st_list, w_ref, m_ref, u_ref, ls, transposed_state, dk):
    c, width = q.shape
    nh = v.shape[1] // LANES
    hpg = LANES // dk
    n_groups = width // LANES
    n_seq = c // ls
    nlev = m_ref.shape[0]
    lane = lax.broadcasted_iota(jnp.int32, (1, LANES), 1)
    own = [jnp.where((lane >= i * dk) & (lane < (i + 1) * dk), 1.0, 0.0) for i in range(hpg)]
    g_hi32 = g.astype(BF16).astype(F32)
    g_lo32 = g - g_hi32
    g_hi, g_lo = g_hi32.astype(BF16), g_lo32.astype(BF16)
    ex2 = _dot(w_ref[...], jnp.concatenate([g_hi, g_lo], axis=1))
    ex = ex2[:, :width] + ex2[:, width:]
    b = ex[:c]

    def row(r, n):
        return jnp.broadcast_to(b[r:r + 1, :], (n, width))

    b_end = jnp.concatenate([row((s + 1) * ls - 1, ls) for s in range(n_seq)], axis=0) if n_seq > 1 else row(c - 1, c)
    e_cum = jnp.exp2(b)
    qe = q * e_cum
    kr = k * jnp.exp2(b_end - b)

    att = [None] * nh
    for l in range(nlev):
        m = 2 ** l
        if l == 0:
            x = jnp.where(u_ref[l] > 0.5, q * jnp.exp2(g), k)
        elif l in GLA_MATMUL_LEVELS:
            j = 1 + GLA_MATMUL_LEVELS.index(l)
            x = jnp.where(u_ref[l] > 0.5, q, k) * jnp.exp2(ex[j * c:(j + 1) * c])
        else:
            pieces = []
            for p in range(c // (2 * m)):
                lo, mid, hi = p * 2 * m, p * 2 * m + m, (p + 1) * 2 * m
                pivot = row(mid - 1, m)
                pieces.append(k[lo:mid] * jnp.exp2(pivot - b[lo:mid]))
                pieces.append(q[mid:hi] * jnp.exp2(b[mid:hi] - pivot))
            x = jnp.concatenate(pieces, axis=0)
        x = x.astype(BF16)
        mask = m_ref[l] > 0.5
        for h in range(nh):
            grp, i = divmod(h, hpg)
            xg = x[:, grp * LANES:(grp + 1) * LANES]
            xl = xg if hpg == 1 else xg * own[i].astype(BF16)
            att[h] = jnp.where(mask, _dot_nt(xl, xg), 0.0 if att[h] is None else att[h])

    qk = q * k
    outs = [[None] * n_seq for _ in range(nh)]
    st_new = [[None] * n_groups for _ in range(n_seq)]
    for grp in range(n_groups):
        gl = slice(grp * LANES, (grp + 1) * LANES)
        heads = []
        for i in range(hpg):
            h = grp * hpg + i
            vh = v[:, h * LANES:(h + 1) * LANES]
            sel = 1.0 if hpg == 1 else own[i]
            o = (_dot(att[h].astype(BF16), vh.astype(BF16))
                 + jnp.sum(qk[:, gl] * sel, axis=1, keepdims=True) * vh)
            heads.append((h, vh, o, qe[:, gl] * sel, kr[:, gl] * sel))
        for s in range(n_seq):
            rows = slice(s * ls, (s + 1) * ls)
            st = st_list[s][grp]
            if transposed_state:
                new = e_cum[(s + 1) * ls - 1:(s + 1) * ls, gl] * st
            else:
                g2 = jnp.concatenate([g_hi32[rows, gl], g_lo32[rows, gl]], axis=0).astype(BF16)
                new = jnp.exp2(_dot_tn(g2, jnp.ones((2 * ls, LANES), BF16))) * st
            for h, vh, o, qe_h, kr_h in heads:
                if transposed_state:
                    outs[h][s] = o[rows] + _dot_nt(qe_h[rows].astype(BF16), st.astype(BF16))
                    new = new + _dot_tn(vh[rows].astype(BF16), kr_h[rows].astype(BF16))
                else:
                    outs[h][s] = o[rows] + _dot(qe_h[rows].astype(BF16), st.astype(BF16))
                    new = new + _dot_tn(kr_h[rows].astype(BF16), vh[rows].astype(BF16))
            st_new[s][grp] = new
    outs = [parts[0] if n_seq == 1 else jnp.concatenate(parts, axis=0) for parts in outs]
    return outs, st_new


def _hgrn_inputs(refs, rows, q_scale):
    del q_scale
    qa_ref, ia_ref, ga_ref, fa_ref, lg_ref = refs
    lg = lg_ref[...]
    ex = jnp.exp(lg - jnp.max(lg, axis=0, keepdims=True))
    lb = ex[0:1] / jnp.sum(ex, axis=0, keepdims=True)
    sig = jax.nn.sigmoid(fa_ref[rows, :])
    q = _silu(qa_ref[rows, :].astype(F32))
    k = (1.0 - lb) * (1.0 - sig)
    g = jnp.log(lb + (1.0 - lb) * sig) * LOG2_E
    return q, k, ia_ref[rows, :].astype(F32), g, ga_ref[rows, :].astype(F32)


def _gla_inputs(refs, rows, q_scale):
    qb_ref, kb_ref, vb_ref, gb_ref, lr_ref, wgk_ref, bgk_ref = refs
    z = _dot(lr_ref[rows, :].astype(BF16), wgk_ref[...]) + bgk_ref[...]
    g = (jnp.minimum(z, 0.0) - jnp.log1p(jnp.exp(-jnp.abs(z)))) * (LOG2_E / GLA_GATE_NORM)
    q = qb_ref[rows, :].astype(F32) * q_scale
    return q, kb_ref[rows, :].astype(F32), vb_ref[rows, :].astype(F32), g, gb_ref[rows, :].astype(F32)


def _head_out(outs, gate, gn):
    ys = [_rms(o, gn) * _silu(gate[:, h * LANES:(h + 1) * LANES]) for h, o in enumerate(outs)]
    return jnp.concatenate(ys, axis=1).astype(BF16)


def _stage_out_weight(wo_ref, wo_s, first):
    @pl.when(first)
    def _():
        wo_s[...] = wo_ref[0].astype(BF16)


def _project_onto_residual(x_ref, y_prev_ref, y, wo_s):
    k_prev = y_prev_ref.shape[1]
    return x_ref[...] + _dot(y_prev_ref[...], wo_s[:k_prev, :]) + _dot(y, wo_s[k_prev:, :])


def _ab_prompt_kernel(*refs, kind, n_in, chunk, q_scale, project):
    in_refs = refs[:n_in]
    if project:
        gn_ref, w_ref, m_ref, u_ref, yp_ref, x_ref, wo_ref, o_ref, s_ref, st_ref, y_s, wo_s = refs[n_in:]
    else:
        gn_ref, w_ref, m_ref, u_ref, o_ref, s_ref, st_ref = refs[n_in:]
        y_s = o_ref
    t = pl.program_id(1)
    n_groups = st_ref.shape[0]
    nh, dk = s_ref.shape[1:3]
    hpg = nh // n_groups
    if project:
        _stage_out_weight(wo_ref, wo_s, jnp.logical_and(pl.program_id(0) == 0, t == 0))

    @pl.when(t == 0)
    def _():
        st_ref[...] = jnp.zeros_like(st_ref)

    load = _hgrn_inputs if kind == "hgrn" else _gla_inputs
    st = [st_ref[j] for j in range(n_groups)]
    for c in range(y_s.shape[0] // chunk):
        rows = slice(c * chunk, (c + 1) * chunk)
        q, k, v, g, gate = load(in_refs, rows, q_scale)
        outs, (st,) = _gla_block(q, k, v, g, [st], w_ref, m_ref, u_ref, chunk, True, dk)
        y_s[rows, :] = _head_out(outs, gate, gn_ref[...])
    for j in range(n_groups):
        st_ref[j] = st[j]
    if project:
        o_ref[...] = _project_onto_residual(x_ref, yp_ref, y_s[...], wo_s)

    @pl.when(t == pl.num_programs(1) - 1)
    def _():
        for h in range(nh):
            grp, i = divmod(h, hpg)
            s_ref[0, h] = st_ref[grp].T[i * dk:(i + 1) * dk, :]


def _ab_sample_kernel(*refs, kind, n_in, ls, q_scale, project):
    in_refs = refs[:n_in]
    if project:
        gn_ref, w_ref, m_ref, u_ref, s0_ref, yp_ref, x_ref, wo_ref, o_ref, s_ref, wo_s = refs[n_in:]
        _stage_out_weight(wo_ref, wo_s, pl.program_id(0) == 0)
    else:
        gn_ref, w_ref, m_ref, u_ref, s0_ref, o_ref, s_ref = refs[n_in:]
    load = _hgrn_inputs if kind == "hgrn" else _gla_inputs
    n_seq, nh, dk = s0_ref.shape[:3]
    hpg = LANES // dk
    q, k, v, g, gate = load(in_refs, slice(None), q_scale)
    st0 = [[jnp.concatenate([s0_ref[s, j * hpg + i] for i in range(hpg)], axis=0) if hpg > 1 else s0_ref[s, j]
            for j in range(nh // hpg)] for s in range(n_seq)]
    outs, st = _gla_block(q, k, v, g, st0, w_ref, m_ref, u_ref, ls, False, dk)
    y = _head_out(outs, gate, gn_ref[...])
    o_ref[...] = _project_onto_residual(x_ref, yp_ref, y, wo_s) if project else y
    for s in range(n_seq):
        for h in range(nh):
            grp, i = divmod(h, hpg)
            s_ref[s, h] = st[s][grp][i * dk:(i + 1) * dk, :]


def _ab_specs(kind, p16, p32, extra, tok_block, tok_index, nh, dk):
    wide = nh * LANES
    kw = nh * dk

    def col(start, w):
        assert start % w == 0
        return pl.BlockSpec((tok_block, w), lambda *ids: (tok_index(*ids), start // w))

    if kind == "hgrn":
        (logits,) = extra
        args = [p16, p16, p16, p32, logits]
        specs = [col(0, wide), col(wide, wide), col(2 * wide, wide), col(0, kw), _resident(logits.shape)]
    else:
        w_gk, b_gk = extra
        a16 = 3 * H_A * LANES
        args = [p16, p16, p16, p16, p32, w_gk, b_gk]
        specs = [col(a16, kw), col(a16 + kw, kw), col(a16 + 2 * kw, wide), col(a16 + 2 * kw + wide, wide),
                 col(H_A * LANES, LANES), _resident(w_gk.shape), _resident(b_gk.shape)]
    return args, specs


def _out_weight_spec(w_out, layer, part, rows):
    return pl.BlockSpec((1, rows, w_out.shape[2]), lambda *_: (layer, part, 0), pipeline_mode=pl.Buffered(1))


def _ab_prompt_call(kind, p16, p32, extra, gn, batch, seq, dk, name, project=None):
    nh = H_A if kind == "hgrn" else H_B
    tt = GLA_STEP_TOKENS
    nt = seq // tt
    tok = lambda b, t: (b * nt + t, 0)
    consts = _gla_constants(1, GLA_CHUNK, nh * dk)
    args, specs = _ab_specs(kind, p16, p32, extra, tt, lambda b, t: b * nt + t, nh, dk)
    n_in = len(args)
    args += [gn.reshape(1, LANES), *consts]
    specs += [_resident((1, LANES))] + [_resident(c.shape) for c in consts]
    scratch = [pltpu.VMEM((nh * dk // LANES, LANES, LANES), F32)]
    if project is None:
        out = jax.ShapeDtypeStruct((batch * seq, nh * LANES), BF16)
    else:
        y_prev, x, w_out, layer = project
        d = x.shape[1]
        args += [y_prev, x, w_out]
        specs += [pl.BlockSpec((tt, y_prev.shape[1]), tok), pl.BlockSpec((tt, d), tok),
                  _out_weight_spec(w_out, layer, 0, w_out.shape[1])]
        scratch += [pltpu.VMEM((tt, nh * LANES), BF16), pltpu.VMEM((w_out.shape[1], d), BF16)]
        out = jax.ShapeDtypeStruct((batch * seq, d), F32)
    return pl.pallas_call(
        functools.partial(_ab_prompt_kernel, kind=kind, n_in=n_in, chunk=GLA_CHUNK, q_scale=dk ** -0.5,
                          project=project is not None),
        out_shape=[out, jax.ShapeDtypeStruct((batch, nh, dk, LANES), F32)],
        grid=(batch, nt),
        in_specs=specs,
        out_specs=[pl.BlockSpec((tt, out.shape[1]), tok),
                   pl.BlockSpec((1, nh, dk, LANES), lambda b, t: (b, 0, 0, 0))],
        scratch_shapes=scratch,
        compiler_params=_params(2, 40 << 20),
        name=name,
    )(*args)


def _ab_sample_call(kind, p16, p32, extra, gn, s0, row0, seq, name, project=None):
    batch, nh, dk, _ = s0.shape
    nb = SAMPLE_SEQS
    rows = nb * seq
    blk0 = row0 // rows
    consts = _gla_constants(nb, seq, nh * dk)
    args, specs = _ab_specs(kind, p16, p32, extra, rows, lambda i: blk0 + i, nh, dk)
    n_in = len(args)
    args += [gn.reshape(1, LANES), *consts, s0]
    state_spec = pl.BlockSpec((nb, nh, dk, LANES), lambda i: (i, 0, 0, 0))
    specs += [_resident((1, LANES))] + [_resident(c.shape) for c in consts] + [state_spec]
    scratch = []
    if project is None:
        out = jax.ShapeDtypeStruct((batch * seq, nh * LANES), BF16)
    else:
        y_prev, x, w_out, layer = project
        d = x.shape[1]
        args += [y_prev, x, w_out]
        specs += [pl.BlockSpec((rows, y_prev.shape[1]), lambda i: (i, 0)),
                  pl.BlockSpec((rows, d), lambda i: (blk0 + i, 0)), _out_weight_spec(w_out, layer, 0, w_out.shape[1])]
        scratch = [pltpu.VMEM((w_out.shape[1], d), BF16)]
        out = jax.ShapeDtypeStruct((batch * seq, d), F32)
    return pl.pallas_call(
        functools.partial(_ab_sample_kernel, kind=kind, n_in=n_in, ls=seq, q_scale=dk ** -0.5,
                          project=project is not None),
        out_shape=[out, jax.ShapeDtypeStruct((batch, nh, dk, LANES), F32)],
        grid=(batch // nb,),
        in_specs=specs,
        out_specs=[pl.BlockSpec((rows, out.shape[1]), lambda i: (i, 0)), state_spec],
        scratch_shapes=scratch,
        compiler_params=_params(1, 40 << 20),
        name=name,
    )(*args)


def _ret_constants(n_seq, ls):
    c = n_seq * ls
    log_gamma = np.log1p(-np.exp2(-5.0 - np.arange(H_C, dtype=np.float64)))[:, None, None]
    t = np.arange(c)[:, None]
    s = np.arange(c)[None, :]
    causal = ((t // ls) == (s // ls)) & (s <= t)
    dmask = np.where(causal, np.exp(log_gamma * np.where(causal, t - s, 0)), 0.0)
    pos = (np.arange(c) % ls)[None, :, None]
    scales = np.stack([np.exp(log_gamma * (pos + 1)), np.exp(log_gamma * (ls - 1 - pos))], axis=1)
    scales = np.broadcast_to(scales, (H_C, 2, c, LANES))
    return jnp.asarray(dmask, F32), jnp.asarray(scales, F32)


def _rope_tables(positions, half):
    inv = ROPE_THETA ** (-jnp.arange(half, dtype=F32) / half)
    ang = positions.astype(F32)[:, None] * inv[None, :]
    return jnp.cos(ang), jnp.sin(ang)


def _rope(x, cos, sin):
    half = x.shape[1] // 2
    x1, x2 = x[:, :half], x[:, half:]
    return jnp.concatenate([x1 * cos - x2 * sin, x2 * cos + x1 * sin], axis=1)


def _ret_block(q16, k16, v, cos, sin, dmask, q_scale, k_scale, s_list, ls):
    dk = q16.shape[1]
    q = _rope(q16.astype(F32), cos, sin)
    k = _rope(k16.astype(F32), cos, sin) * (dk ** -0.5)
    reps = dk // LANES
    q_sc = jnp.concatenate([q_scale] * reps, axis=1)
    k_sc = jnp.concatenate([k_scale] * reps, axis=1)
    att = _dot_nt(q.astype(BF16), k.astype(BF16)) * dmask
    o = _dot(att.astype(BF16), v)
    qs = q * q_sc
    ks = k * k_sc
    decay = q_scale[ls - 1:ls, 0:1]
    n_seq = q.shape[0] // ls
    v32 = v if n_seq == 1 else v.astype(F32)
    o_parts, s_new = [], []
    for b in range(n_seq):
        rows = slice(b * ls, (b + 1) * ls)
        s = s_list[b]
        o_parts.append(o[rows] + _dot(qs[rows].astype(BF16), s.astype(BF16)))
        s_new.append(decay * s + _dot_tn(ks[rows].astype(BF16), v32[rows].astype(BF16)))
    o = o_parts[0] if n_seq == 1 else jnp.concatenate(o_parts, axis=0)
    return o, s_new


def _ret_head_out(o, gate, gn):
    oc = o - jnp.mean(o, axis=-1, keepdims=True)
    y = oc * lax.rsqrt(jnp.mean(oc * oc, axis=-1, keepdims=True) + EPS) * gn
    return (y * _silu(gate)).astype(BF16)


def _ret_prompt_kernel(q_ref, k_ref, v_ref, g_ref, cos_ref, sin_ref, dm_ref, sc_ref, gn_ref, x_ref, wo_ref,
                       xo_ref, s_ref, st_ref, y_s, wo_s):
    t = pl.program_id(1)
    nh, dk, dv = st_ref.shape
    c = dm_ref.shape[1]
    _stage_out_weight(wo_ref, wo_s, jnp.logical_and(pl.program_id(0) == 0, t == 0))

    @pl.when(t == 0)
    def _():
        st_ref[...] = jnp.zeros_like(st_ref)

    for h in range(nh):
        s = st_ref[h]
        for j in range(q_ref.shape[0] // c):
            rows = slice(j * c, (j + 1) * c)
            o, (s,) = _ret_block(q_ref[rows, h * dk:(h + 1) * dk], k_ref[rows, h * dk:(h + 1) * dk],
                                 v_ref[rows, h * dv:(h + 1) * dv], cos_ref[rows, :], sin_ref[rows, :], dm_ref[h],
                                 sc_ref[h, 0], sc_ref[h, 1], [s], c)
            y_s[rows, h * dv:(h + 1) * dv] = _ret_head_out(o, g_ref[rows, h * dv:(h + 1) * dv].astype(F32),
                                                           gn_ref[...])
        st_ref[h] = s
    xo_ref[...] = x_ref[...] + _dot(y_s[...], wo_s[...])

    @pl.when(t == pl.num_programs(1) - 1)
    def _():
        s_ref[0] = st_ref[...]


def _ret_sample_kernel(q_ref, k_ref, v_ref, g_ref, cos_ref, sin_ref, dm_ref, sc_ref, gn_ref, s0_ref, x_ref,
                       wo_ref, xo_ref, s_ref, wo_s, *, ls):
    h = pl.program_id(1)
    dv = v_ref.shape[1]
    _stage_out_weight(wo_ref, wo_s, jnp.logical_and(pl.program_id(0) == 0, h == 0))
    n_seq = q_ref.shape[0] // ls
    o, s = _ret_block(q_ref[...], k_ref[...], v_ref[...], cos_ref[...], sin_ref[...], dm_ref[0],
                      sc_ref[0, 0], sc_ref[0, 1], [s0_ref[b, 0] for b in range(n_seq)], ls)
    y = _ret_head_out(o, g_ref[...].astype(F32), gn_ref[...])
    proj = _dot(y, wo_s[pl.ds(pl.multiple_of(h * dv, dv), dv), :])

    @pl.when(h == 0)
    def _():
        xo_ref[...] = x_ref[...] + proj

    @pl.when(h > 0)
    def _():
        xo_ref[...] += proj

    for b in range(n_seq):
        s_ref[b, 0] = s[b]


def _ret_prompt_call(r16, gn, x, w_out, layer, batch, seq, dk, dv, name):
    c = RET_STEP_TOKENS
    nt = seq // c
    d = x.shape[1]
    dmask, scales = _ret_constants(1, RET_CHUNK)
    cos, sin = _rope_tables(jnp.arange(seq, dtype=jnp.int32), dk // 2)
    tok = lambda b, t: b * nt + t
    nq, nv = H_C * dk, H_C * dv
    args = [r16, r16, r16, r16, cos, sin, dmask, scales, gn.reshape(1, dv), x, w_out]
    specs = [pl.BlockSpec((c, nq), lambda b, t: (tok(b, t), 0)),
             pl.BlockSpec((c, nq), lambda b, t: (tok(b, t), 1)),
             pl.BlockSpec((c, nv), lambda b, t: (tok(b, t), 2 * nq // nv)),
             pl.BlockSpec((c, nv), lambda b, t: (tok(b, t), 2 * nq // nv + 1)),
             pl.BlockSpec((c, dk // 2), lambda b, t: (t, 0)), pl.BlockSpec((c, dk // 2), lambda b, t: (t, 0)),
             _resident(dmask.shape), _resident(scales.shape), _resident((1, dv)),
             pl.BlockSpec((c, d), lambda b, t: (tok(b, t), 0)), _out_weight_spec(w_out, layer, 0, nv)]
    return pl.pallas_call(
        _ret_prompt_kernel,
        out_shape=[jax.ShapeDtypeStruct((batch * seq, d), F32),
                   jax.ShapeDtypeStruct((batch, H_C, dk, dv), F32)],
        grid=(batch, nt),
        in_specs=specs,
        out_specs=[pl.BlockSpec((c, d), lambda b, t: (tok(b, t), 0)),
                   pl.BlockSpec((1, H_C, dk, dv), lambda b, t: (b, 0, 0, 0))],
        scratch_shapes=[pltpu.VMEM((H_C, dk, dv), F32), pltpu.VMEM((c, nv), BF16), pltpu.VMEM((nv, d), BF16)],
        compiler_params=_params(2, 48 << 20),
        name=name,
    )(*args)


def _ret_sample_call(r16, gn, s0, row0, batch, seq, dk, dv, x, w_out, layer, name):
    nb = SAMPLE_SEQS
    rows = nb * seq
    blk0 = row0 // rows
    d = x.shape[1]
    dmask, scales = _ret_constants(nb, seq)
    pos = PAST_LEN + (jnp.arange(rows, dtype=jnp.int32) % seq)
    cos, sin = _rope_tables(pos, dk // 2)
    nq = H_C * dk

    def col(width, base):
        return pl.BlockSpec((rows, width), lambda i, h: (blk0 + i, base + h))

    state_spec = pl.BlockSpec((nb, 1, dk, dv), lambda i, h: (i, h, 0, 0))
    args = [r16, r16, r16, r16, cos, sin, dmask, scales, gn.reshape(1, dv), s0, x, w_out]
    specs = [col(dk, 0), col(dk, H_C), col(dv, 2 * nq // dv), col(dv, 2 * nq // dv + H_C),
             _resident(cos.shape), _resident(sin.shape),
             pl.BlockSpec((1, rows, rows), lambda i, h: (h, 0, 0)),
             pl.BlockSpec((1, 2, rows, LANES), lambda i, h: (h, 0, 0, 0)), _resident((1, dv)), state_spec,
             pl.BlockSpec((rows, d), lambda i, h: (blk0 + i, 0)), _out_weight_spec(w_out, layer, 0, H_C * dv)]
    return pl.pallas_call(
        functools.partial(_ret_sample_kernel, ls=seq),
        out_shape=[jax.ShapeDtypeStruct((batch * seq, d), F32),
                   jax.ShapeDtypeStruct((batch, H_C, dk, dv), F32)],
        grid=(batch // nb, H_C),
        in_specs=specs,
        out_specs=[pl.BlockSpec((rows, d), lambda i, h: (i, 0)), state_spec],
        scratch_shapes=[pltpu.VMEM((H_C * dv, d), BF16)],
        compiler_params=_params(2, 56 << 20),
        name=name,
    )(*args)


def _prepare_ab_weights(w_in, w_gk2, b_gk, dk_a, dv_a, dk_b, dv_b):
    sizes = (H_A * dk_a, H_A * dk_a, H_A * dv_a, H_A * dv_a, H_B * dk_b, H_B * dk_b, H_B * dv_b,
             GLA_RANK, H_B * dv_b)
    qa, fa, ia, ga, qb, kb, vb, lrb, gb = jnp.split(w_in, [int(v) for v in np.cumsum(sizes)[:-1]], axis=1)
    w16 = jnp.concatenate([qa, ia, ga, qb, kb, vb, gb], axis=1).astype(BF16)
    w32 = jnp.concatenate([fa, jnp.pad(lrb, ((0, 0), (0, LANES - GLA_RANK)))], axis=1).astype(BF16)
    w_gk = jnp.pad(w_gk2, ((0, LANES - GLA_RANK), (0, 0))).astype(BF16)
    return w16, w32, w_gk, b_gk.reshape(1, -1)


def kernel(x_prompt, x_sample, state_hgrn, state_gla, state_ret, p_prompt, p_sample, norm_ffn1, ffn1_w_in, ffn1_w_out, norm_mix, ab_w_in, ab_w_gk2, ab_b_gk, hgrn_lb_logits, ab_gn_hgrn, ab_gn_gla, ab_w_out, ret_w_in, ret_gn, ret_w_out, norm_ffn2, ffn2_w_in, ffn2_w_out, norm_ple, ple_w_gate, ple_w_proj, norm_final):
    bp, lp, d = x_prompt.shape
    bs, ls, _ = x_sample.shape
    depth = norm_ffn1.shape[0]
    n_prompt = bp * lp
    dk_a, dv_a = state_hgrn.shape[-2:]
    dk_b, dv_b = state_gla.shape[-2:]
    dk_c, dv_c = state_ret.shape[-2:]
    assert dk_a == dv_a == dv_b == LANES and LANES % dk_b == 0 and H_B % (LANES // dk_b) == 0
    assert state_hgrn.shape[0] == 1

    n_rows = (n_prompt, bs * ls)
    xs = (x_prompt.reshape(n_prompt, d), x_sample.reshape(bs * ls, d))
    ps = (p_prompt.reshape(depth, n_prompt, -1), p_sample.reshape(depth, bs * ls, -1))

    new_hgrn_p, new_gla_p, new_ret_p, new_hgrn_s, new_gla_s, new_ret_s = [], [], [], [], [], []
    for i in range(depth):
        j = i // 2
        x = _ffn_call(xs if i == 0 else (x,), n_rows, norm_ffn1[i], ffn1_w_in, ffn1_w_out, i, name=f"ffn1_{i}")
        if i % 2 == 0:
            w16, w32, w_gk, b_gk = _prepare_ab_weights(ab_w_in[j], ab_w_gk2[j], ab_b_gk[j], dk_a, dv_a, dk_b, dv_b)
            p16, p32 = _inproj_call(x, norm_mix[i], [w16, w32], [BF16, F32], [w16.shape[1] // 2, w32.shape[1]],
                                    name=f"ab_in_{i}")
            hgrn = ("hgrn", p16, p32, (hgrn_lb_logits,), ab_gn_hgrn[j])
            gla = ("gla", p16, p32, (w_gk, b_gk), ab_gn_gla[j])
            ya_p, sa_p = _ab_prompt_call(*hgrn, bp, lp, dk_a, f"hgrn_p_{i}")
            ya_s, sa_s = _ab_sample_call(*hgrn, state_hgrn[j], n_prompt, ls, f"hgrn_s_{i}")
            x_p, sb_p = _ab_prompt_call(*gla, bp, lp, dk_b, f"gla_p_{i}", project=(ya_p, x, ab_w_out, j))
            x_s, sb_s = _ab_sample_call(*gla, state_gla[j], n_prompt, ls, f"gla_s_{i}",
                                        project=(ya_s, x, ab_w_out, j))
            new_hgrn_p.append(sa_p)
            new_hgrn_s.append(sa_s)
            new_gla_p.append(sb_p)
            new_gla_s.append(sb_s)
        else:
            r16 = _inproj_staged_call(x, norm_mix[i], ret_w_in, j, 2 * V7X_MXU_COLS, name=f"ret_in_{i}")
            x_p, sc_p = _ret_prompt_call(r16, ret_gn[j], x, ret_w_out, j, bp, lp, dk_c, dv_c, f"ret_p_{i}")
            x_s, sc_s = _ret_sample_call(r16, ret_gn[j], state_ret[j], n_prompt, bs, ls, dk_c, dv_c, x, ret_w_out, j,
                                         f"ret_s_{i}")
            new_ret_p.append(sc_p)
            new_ret_s.append(sc_s)
        last = i == depth - 1
        x = _ffn_call((x_p, x_s), n_rows, norm_ffn2[i], ffn2_w_in, ffn2_w_out, i,
                      ple=(ps, norm_ple[i], ple_w_gate, ple_w_proj),
                      final_gain=norm_final if last else None, split_out=last, name=f"ffn2_{i}")

    y_prompt = x[0].reshape(bp, lp, d)
    y_sample = x[1].reshape(bs, ls, d)
    def layers(states):
        return states[0][None] if len(states) == 1 else jnp.stack(states)

    return (y_prompt, y_sample, layers(new_hgrn_p), layers(new_gla_p), layers(new_ret_p),
            layers(new_hgrn_s), layers(new_gla_s), layers(new_ret_s))
```

```python
import functools
import math

import numpy as np
import jax
import jax.numpy as jnp
from jax import lax
from jax.experimental import pallas as pl
from jax.experimental.pallas import tpu as pltpu

F32 = jnp.float32
BF16 = jnp.bfloat16

H_A = 4
H_B = 4
H_C = 4
GLA_RANK = 16
GLA_GATE_NORM = 16.0
ROPE_THETA = 10000.0
PAST_LEN = 16384
EPS = 1e-6
LOG2_E = math.log2(math.e)

LANES = 128
V7X_MXU_COLS = 256
V7X_VMEM_BUDGET_BYTES = 56 * 1024 * 1024

ROW_TILE = 512
GLA_CHUNK = 128
GLA_STEP_TOKENS = 1024
RET_CHUNK = 256
RET_STEP_TOKENS = 512
ROW_PIECE = 64
SAMPLE_SEQS = 16


def _params(n_axes, vmem_bytes):
    return pltpu.CompilerParams(dimension_semantics=("arbitrary",) * n_axes,
                                vmem_limit_bytes=min(int(vmem_bytes), V7X_VMEM_BUDGET_BYTES))


def _resident(shape):
    nd = len(shape)
    return pl.BlockSpec(shape, lambda *_: (0,) * nd, pipeline_mode=pl.Buffered(1))


def _resident_layer(shape, layer):
    nd = len(shape)
    return pl.BlockSpec((1,) + tuple(shape[1:]), lambda *_: (layer,) + (0,) * (nd - 1),
                        pipeline_mode=pl.Buffered(1))


def _row_maps(prompt_tiles, lead=0):
    stacked = lambda i: (jnp.maximum(i - lead, 0), 0)
    prompt = lambda i: (jnp.clip(i - lead, 0, prompt_tiles - 1), 0)
    sample = lambda i: (jnp.maximum(i - lead - prompt_tiles, 0), 0)
    return stacked, prompt, sample


def _rms(x, g):
    return x * lax.rsqrt(jnp.mean(x * x, axis=-1, keepdims=True) + EPS) * g


def _silu(x):
    return x * jax.nn.sigmoid(x)


def _dot(a, b):
    return jnp.dot(a, b, preferred_element_type=F32)


def _dot_nt(a, b):
    return lax.dot_general(a, b, (((1,), (1,)), ((), ())), preferred_element_type=F32)


def _dot_tn(a, b):
    return lax.dot_general(a, b, (((0,), (0,)), ((), ())), preferred_element_type=F32)


def _pick(is_prompt, refs, index=None):
    vals = [r[...] if index is None else r[index] for r in refs]
    return vals[0] if len(vals) == 1 else jnp.where(is_prompt, vals[0], vals[1])


def _ffn_kernel(*refs, n_x, has_ple, has_final, n_o, prompt_tiles, stage_steps, gate_steps):
    it = iter(refs)
    x_refs = [next(it) for _ in range(n_x)]
    g_ref, win_ref, wout_ref = next(it), next(it), next(it)
    if has_ple:
        p_refs = [next(it), next(it)]
        gp_ref, wg_ref, wp_ref = next(it), next(it), next(it)
    if has_final:
        gf_ref = next(it)
    o_refs = [next(it) for _ in range(n_o)]
    win_s, wout_s, act_ref = next(it), next(it), next(it)
    if has_ple:
        wg_s, wp_s = next(it), next(it)
    i = pl.program_id(0)
    cols = win_s.shape[2]
    n_half = win_s.shape[0] // 2

    @pl.when(i < stage_steps)
    def _():
        blk = win_ref[0]
        win_s[2 * i] = blk[:, :cols].astype(BF16)
        win_s[2 * i + 1] = blk[:, cols:].astype(BF16)
        wout_s[pl.ds(pl.multiple_of(i * cols, cols), cols), :] = wout_ref[0].astype(BF16)

    if has_ple:
        rows = wg_ref.shape[1]

        @pl.when(i < gate_steps)
        def _():
            wg_s[pl.ds(pl.multiple_of(i * rows, rows), rows), :] = wg_ref[0].astype(BF16)

        @pl.when(i == 0)
        def _():
            wp_s[...] = wp_ref[0].astype(BF16)

    @pl.when(i >= stage_steps)
    def _():
        is_prompt = i - stage_steps < prompt_tiles
        x = _pick(is_prompt, x_refs)
        xn = _rms(x, g_ref[...]).astype(BF16)
        for j in range(n_half):
            a = _dot(xn, win_s[j])
            b = _dot(xn, win_s[n_half + j])
            act_ref[:, j * cols:(j + 1) * cols] = (_silu(a) * b).astype(BF16)
        x = x + 0.5 * _dot(act_ref[...], wout_s[...])
        if has_ple:
            gate = jax.nn.sigmoid(_dot(_rms(x, gp_ref[...]).astype(BF16), wg_s[...]))
            x = x + gate * _dot(_pick(is_prompt, p_refs, 0).astype(BF16), wp_s[...])
        if has_final:
            x = _rms(x, gf_ref[...])
        if n_o == 1:
            o_refs[0][...] = x
        else:
            @pl.when(is_prompt)
            def _():
                o_refs[0][...] = x

            @pl.when(jnp.logical_not(is_prompt))
            def _():
                o_refs[1][...] = x


def _ffn_call(xs, n_rows, gain, w_in, w_out, layer, ple=None, final_gain=None, split_out=False, name="ffn"):
    d = xs[0].shape[1]
    d_ff = w_out.shape[1]
    tm = ROW_TILE
    cols = V7X_MXU_COLS
    assert n_rows[0] % tm == 0 and n_rows[1] % tm == 0 and d_ff % cols == 0
    pt, st = n_rows[0] // tm, n_rows[1] // tm
    stage = d_ff // cols
    stacked, prompt, sample = _row_maps(pt, stage)
    pair = [prompt, sample]
    last = stage - 1
    args = [*xs, gain.reshape(1, d), w_in, w_out]
    specs = [pl.BlockSpec((tm, d), m) for m in ([stacked] if len(xs) == 1 else pair)]
    specs += [_resident((1, d)),
              pl.BlockSpec((1, d, 2 * cols), lambda i: (layer, 0, jnp.minimum(i, last))),
              pl.BlockSpec((1, cols, d), lambda i: (layer, jnp.minimum(i, last), 0))]
    scratch = [pltpu.VMEM((2 * stage, d, cols), BF16), pltpu.VMEM((d_ff, d), BF16), pltpu.VMEM((tm, d_ff), BF16)]
    gate_steps = 0
    if ple is not None:
        ps, gp, wg, wp = ple
        gate_rows = LANES
        gate_steps = d // gate_rows
        assert gate_steps <= stage
        args += [*ps, gp.reshape(1, d), wg, wp]
        specs += [pl.BlockSpec((1, tm, ps[0].shape[2]), lambda i, m=m: (layer, *m(i))) for m in pair]
        specs += [_resident((1, d)),
                  pl.BlockSpec((1, gate_rows, d), lambda i: (layer, jnp.minimum(i, gate_steps - 1), 0)),
                  _resident_layer(wp.shape, layer)]
        scratch += [pltpu.VMEM((d, d), BF16), pltpu.VMEM(wp.shape[1:], BF16)]
    if final_gain is not None:
        args.append(final_gain.reshape(1, d))
        specs.append(_resident((1, d)))
    if split_out:
        out_shape = [jax.ShapeDtypeStruct((r, d), F32) for r in n_rows]
        out_specs = [pl.BlockSpec((tm, d), m) for m in pair]
    else:
        out_shape = jax.ShapeDtypeStruct((sum(n_rows), d), F32)
        out_specs = pl.BlockSpec((tm, d), stacked)
    return pl.pallas_call(
        functools.partial(_ffn_kernel, n_x=len(xs), has_ple=ple is not None, has_final=final_gain is not None,
                          n_o=2 if split_out else 1, prompt_tiles=pt, stage_steps=stage, gate_steps=gate_steps),
        out_shape=out_shape,
        grid=(stage + pt + st,),
        in_specs=specs,
        out_specs=out_specs,
        scratch_shapes=scratch,
        compiler_params=_params(1, V7X_VMEM_BUDGET_BYTES),
        name=name,
    )(*args)


def _inproj_kernel(*refs, n_out, chunk_cols):
    x_ref, g_ref = refs[0], refs[1]
    w_refs = refs[2:2 + n_out]
    o_refs = refs[2 + n_out:2 + 2 * n_out]
    xn = _rms(x_ref[...], g_ref[...]).astype(BF16)
    for w_ref, o_ref, cw in zip(w_refs, o_refs, chunk_cols):
        for c in range(w_ref.shape[1] // cw):
            o_ref[:, c * cw:(c + 1) * cw] = _dot(xn, w_ref[:, c * cw:(c + 1) * cw]).astype(o_ref.dtype)


def _inproj_call(x, gain, weights, out_dtypes, chunk_cols, name):
    n, d = x.shape
    tm = ROW_TILE
    row = lambda i: (i, 0)
    specs = [pl.BlockSpec((tm, d), row), _resident((1, d))] + [_resident(w.shape) for w in weights]
    out_shape = [jax.ShapeDtypeStruct((n, w.shape[1]), dt) for w, dt in zip(weights, out_dtypes)]
    out_specs = [pl.BlockSpec((tm, w.shape[1]), row) for w in weights]
    vmem = (sum(2 * w.size for w in weights) + 4 * tm * d * 4
            + sum(2 * tm * w.shape[1] * jnp.dtype(dt).itemsize for w, dt in zip(weights, out_dtypes))
            + 2 * tm * max(chunk_cols) * 4 + (4 << 20))
    return pl.pallas_call(
        functools.partial(_inproj_kernel, n_out=len(weights), chunk_cols=tuple(chunk_cols)),
        out_shape=out_shape,
        grid=(n // tm,),
        in_specs=specs,
        out_specs=out_specs,
        compiler_params=_params(1, vmem),
        name=name,
    )(x, gain.reshape(1, d), *weights)


def _inproj_staged_kernel(x_ref, g_ref, w_ref, o_ref, w_s, *, stage_steps):
    i = pl.program_id(0)
    cols = w_s.shape[2]

    @pl.when(i < stage_steps)
    def _():
        w_s[i] = w_ref[0].astype(BF16)

    @pl.when(i >= stage_steps)
    def _():
        xn = _rms(x_ref[...], g_ref[...]).astype(BF16)
        for c in range(stage_steps):
            o_ref[:, c * cols:(c + 1) * cols] = _dot(xn, w_s[c]).astype(o_ref.dtype)


def _inproj_staged_call(x, gain, w, layer, cols, name):
    n, d = x.shape
    n_out = w.shape[2]
    tm = ROW_TILE
    assert n_out % cols == 0
    stage = n_out // cols
    row = lambda i: (jnp.maximum(i - stage, 0), 0)
    vmem = 2 * d * n_out + 2 * d * cols * 4 + 4 * tm * d * 4 + 2 * tm * n_out * 2 + 2 * tm * cols * 4 + (4 << 20)
    return pl.pallas_call(
        functools.partial(_inproj_staged_kernel, stage_steps=stage),
        out_shape=jax.ShapeDtypeStruct((n, n_out), BF16),
        grid=(stage + n // tm,),
        in_specs=[pl.BlockSpec((tm, d), row), _resident((1, d)),
                  pl.BlockSpec((1, d, cols), lambda i: (layer, 0, jnp.minimum(i, stage - 1)))],
        out_specs=pl.BlockSpec((tm, n_out), row),
        scratch_shapes=[pltpu.VMEM((stage, d, cols), BF16)],
        compiler_params=_params(1, vmem),
        name=name,
    )(x, gain.reshape(1, d), w)


GLA_MATMUL_LEVELS = (1, 2)


def _gla_constants(n_seq, ls, width):
    c = n_seq * ls
    nlev = int(round(math.log2(ls)))
    assert 2 ** nlev == ls and nlev > max(GLA_MATMUL_LEVELS)
    t = np.arange(c)[:, None]
    i = np.arange(c)[None, :]
    w_rows = [((t // ls) == (i // ls)) & (i <= t)]
    masks, uppers = [], []
    for l in range(nlev):
        m = 2 ** l
        upper = ((t // m) % 2) == 1
        r = (t // (2 * m)) * (2 * m) + m - 1
        if l in GLA_MATMUL_LEVELS:
            w_rows.append(np.where(upper, (i > r) & (i <= t), (i > t) & (i <= r)))
        masks.append(((t // (2 * m)) == (i // (2 * m))) & upper & (((i // m) % 2) == 0))
        uppers.append(np.broadcast_to(upper, (c, width)))
    w = jnp.asarray(np.concatenate(w_rows, 0).astype(np.float32), BF16)
    return w, jnp.asarray(np.stack(masks).astype(np.float32)), jnp.asarray(np.stack(uppers).astype(np.float32))


def _gla_block(q, k, v, g, st_list, w_ref, m_ref, u_ref, ls, transposed_state, dk):
    c, width = q.shape
    nh = v.shape[1] // LANES
    hpg = LANES // dk
    n_groups = width // LANES
    n_seq = c // ls
    nlev = m_ref.shape[0]
    lane = lax.broadcasted_iota(jnp.int32, (1, LANES), 1)
    own = [jnp.where((lane >= i * dk) & (lane < (i + 1) * dk), 1.0, 0.0) for i in range(hpg)]
    g_hi32 = g.astype(BF16).astype(F32)
    g_lo32 = g - g_hi32
    g_hi, g_lo = g_hi32.astype(BF16), g_lo32.astype(BF16)
    ex2 = _dot(w_ref[...], jnp.concatenate([g_hi, g_lo], axis=1))
    ex = ex2[:, :width] + ex2[:, width:]
    b = ex[:c]

    def row(r, n):
        return jnp.broadcast_to(b[r:r + 1, :], (n, width))

    b_end = jnp.concatenate([row((s + 1) * ls - 1, ls) for s in range(n_seq)], axis=0) if n_seq > 1 else row(c - 1, c)
    e_cum = jnp.exp2(b)
    qe = q * e_cum
    kr = k * jnp.exp2(b_end - b)

    att = [None] * nh
    for l in range(nlev):
        m = 2 ** l
        if l == 0:
            x = jnp.where(u_ref[l] > 0.5, q * jnp.exp2(g), k)
        elif l in GLA_MATMUL_LEVELS:
            j = 1 + GLA_MATMUL_LEVELS.index(l)
            x = jnp.where(u_ref[l] > 0.5, q, k) * jnp.exp2(ex[j * c:(j + 1) * c])
        else:
            pieces = []
            for p in range(c // (2 * m)):
                lo, mid, hi = p * 2 * m, p * 2 * m + m, (p + 1) * 2 * m
                pivot = row(mid - 1, m)
                pieces.append(k[lo:mid] * jnp.exp2(pivot - b[lo:mid]))
                pieces.append(q[mid:hi] * jnp.exp2(b[mid:hi] - pivot))
            x = jnp.concatenate(pieces, axis=0)
        x = x.astype(BF16)
        mask = m_ref[l] > 0.5
        for h in range(nh):
            grp, i = divmod(h, hpg)
            xg = x[:, grp * LANES:(grp + 1) * LANES]
            xl = xg if hpg == 1 else xg * own[i].astype(BF16)
            att[h] = jnp.where(mask, _dot_nt(xl, xg), 0.0 if att[h] is None else att[h])

    qk = q * k
    outs = [[None] * n_seq for _ in range(nh)]
    st_new = [[None] * n_groups for _ in range(n_seq)]
    for grp in range(n_groups):
        gl = slice(grp * LANES, (grp + 1) * LANES)
        heads = []
        for i in range(hpg):
            h = grp * hpg + i
            vh = v[:, h * LANES:(h + 1) * LANES]
            sel = 1.0 if hpg == 1 else own[i]
            o = (_dot(att[h].astype(BF16), vh.astype(BF16))
                 + jnp.sum(qk[:, gl] * sel, axis=1, keepdims=True) * vh)
            heads.append((h, vh, o, qe[:, gl] * sel, kr[:, gl] * sel))
        for s in range(n_seq):
            rows = slice(s * ls, (s + 1) * ls)
            st = st_list[s][grp]
            if transposed_state:
                new = e_cum[(s + 1) * ls - 1:(s + 1) * ls, gl] * st
            else:
                g2 = jnp.concatenate([g_hi32[rows, gl], g_lo32[rows, gl]], axis=0).astype(BF16)
                new = jnp.exp2(_dot_tn(g2, jnp.ones((2 * ls, LANES), BF16))) * st
            for h, vh, o, qe_h, kr_h in heads:
                if transposed_state:
                    outs[h][s] = o[rows] + _dot_nt(qe_h[rows].astype(BF16), st.astype(BF16))
                    new = new + _dot_tn(vh[rows].astype(BF16), kr_h[rows].astype(BF16))
                else:
                    outs[h][s] = o[rows] + _dot(qe_h[rows].astype(BF16), st.astype(BF16))
                    new = new + _dot_tn(kr_h[rows].astype(BF16), vh[rows].astype(BF16))
            st_new[s][grp] = new
    outs = [parts[0] if n_seq == 1 else jnp.concatenate(parts, axis=0) for parts in outs]
    return outs, st_new


def _hgrn_inputs(refs, rows, q_scale):
    del q_scale
    qa_ref, ia_ref, ga_ref, fa_ref, lg_ref = refs
    lg = lg_ref[...]
    ex = jnp.exp(lg - jnp.max(lg, axis=0, keepdims=True))
    lb = ex[0:1] / jnp.sum(ex, axis=0, keepdims=True)
    sig = jax.nn.sigmoid(fa_ref[rows, :])
    q = _silu(qa_ref[rows, :].astype(F32))
    k = (1.0 - lb) * (1.0 - sig)
    g = jnp.log(lb + (1.0 - lb) * sig) * LOG2_E
    return q, k, ia_ref[rows, :].astype(F32), g, ga_ref[rows, :].astype(F32)


def _gla_inputs(refs, rows, q_scale):
    qb_ref, kb_ref, vb_ref, gb_ref, lr_ref, wgk_ref, bgk_ref = refs
    z = _dot(lr_ref[rows, :].astype(BF16), wgk_ref[...]) + bgk_ref[...]
    g = (jnp.minimum(z, 0.0) - jnp.log1p(jnp.exp(-jnp.abs(z)))) * (LOG2_E / GLA_GATE_NORM)
    q = qb_ref[rows, :].astype(F32) * q_scale
    return q, kb_ref[rows, :].astype(F32), vb_ref[rows, :].astype(F32), g, gb_ref[rows, :].astype(F32)


def _head_out(outs, gate, gn):
    ys = [_rms(o, gn) * _silu(gate[:, h * LANES:(h + 1) * LANES]) for h, o in enumerate(outs)]
    return jnp.concatenate(ys, axis=1).astype(BF16)


def _stage_out_weight(wo_ref, wo_s, first):
    @pl.when(first)
    def _():
        wo_s[...] = wo_ref[0].astype(BF16)


def _project_onto_residual(x_ref, y_prev_ref, y, wo_s):
    k_prev = y_prev_ref.shape[1]
    return x_ref[...] + _dot(y_prev_ref[...], wo_s[:k_prev, :]) + _dot(y, wo_s[k_prev:, :])


def _ab_prompt_kernel(*refs, kind, n_in, chunk, q_scale, project):
    in_refs = refs[:n_in]
    if project:
        gn_ref, w_ref, m_ref, u_ref, yp_ref, x_ref, wo_ref, o_ref, s_ref, st_ref, y_s, wo_s = refs[n_in:]
    else:
        gn_ref, w_ref, m_ref, u_ref, o_ref, s_ref, st_ref = refs[n_in:]
        y_s = o_ref
    t = pl.program_id(1)
    n_groups = st_ref.shape[0]
    nh, dk = s_ref.shape[1:3]
    hpg = nh // n_groups
    if project:
        _stage_out_weight(wo_ref, wo_s, jnp.logical_and(pl.program_id(0) == 0, t == 0))

    @pl.when(t == 0)
    def _():
        st_ref[...] = jnp.zeros_like(st_ref)

    load = _hgrn_inputs if kind == "hgrn" else _gla_inputs
    st = [st_ref[j] for j in range(n_groups)]
    for c in range(y_s.shape[0] // chunk):
        rows = slice(c * chunk, (c + 1) * chunk)
        q, k, v, g, gate = load(in_refs, rows, q_scale)
        outs, (st,) = _gla_block(q, k, v, g, [st], w_ref, m_ref, u_ref, chunk, True, dk)
        y_s[rows, :] = _head_out(outs, gate, gn_ref[...])
    for j in range(n_groups):
        st_ref[j] = st[j]
    if project:
        o_ref[...] = _project_onto_residual(x_ref, yp_ref, y_s[...], wo_s)

    @pl.when(t == pl.num_programs(1) - 1)
    def _():
        for h in range(nh):
            grp, i = divmod(h, hpg)
            s_ref[0, h] = st_ref[grp].T[i * dk:(i + 1) * dk, :]


def _ab_sample_kernel(*refs, kind, n_in, ls, q_scale, project):
    in_refs = refs[:n_in]
    if project:
        gn_ref, w_ref, m_ref, u_ref, s0_ref, yp_ref, x_ref, wo_ref, o_ref, s_ref, wo_s = refs[n_in:]
        _stage_out_weight(wo_ref, wo_s, pl.program_id(0) == 0)
    else:
        gn_ref, w_ref, m_ref, u_ref, s0_ref, o_ref, s_ref = refs[n_in:]
    load = _hgrn_inputs if kind == "hgrn" else _gla_inputs
    n_seq, nh, dk = s0_ref.shape[:3]
    hpg = LANES // dk
    q, k, v, g, gate = load(in_refs, slice(None), q_scale)
    st0 = [[jnp.concatenate([s0_ref[s, j * hpg + i] for i in range(hpg)], axis=0) if hpg > 1 else s0_ref[s, j]
            for j in range(nh // hpg)] for s in range(n_seq)]
    outs, st = _gla_block(q, k, v, g, st0, w_ref, m_ref, u_ref, ls, False, dk)
    y = _head_out(outs, gate, gn_ref[...])
    o_ref[...] = _project_onto_residual(x_ref, yp_ref, y, wo_s) if project else y
    for s in range(n_seq):
        for h in range(nh):
            grp, i = divmod(h, hpg)
            s_ref[s, h] = st[s][grp][i * dk:(i + 1) * dk, :]


def _ab_specs(kind, p16, p32, extra, tok_block, tok_index, nh, dk):
    wide = nh * LANES
    kw = nh * dk

    def col(start, w):
        assert start % w == 0
        return pl.BlockSpec((tok_block, w), lambda *ids: (tok_index(*ids), start // w))

    if kind == "hgrn":
        (logits,) = extra
        args = [p16, p16, p16, p32, logits]
        specs = [col(0, wide), col(wide, wide), col(2 * wide, wide), col(0, kw), _resident(logits.shape)]
    else:
        w_gk, b_gk = extra
        a16 = 3 * H_A * LANES
        args = [p16, p16, p16, p16, p32, w_gk, b_gk]
        specs = [col(a16, kw), col(a16 + kw, kw), col(a16 + 2 * kw, wide), col(a16 + 2 * kw + wide, wide),
                 col(H_A * LANES, LANES), _resident(w_gk.shape), _resident(b_gk.shape)]
    return args, specs


def _out_weight_spec(w_out, layer, part, rows):
    return pl.BlockSpec((1, rows, w_out.shape[2]), lambda *_: (layer, part, 0), pipeline_mode=pl.Buffered(1))


def _ab_prompt_call(kind, p16, p32, extra, gn, batch, seq, dk, name, project=None):
    nh = H_A if kind == "hgrn" else H_B
    tt = GLA_STEP_TOKENS
    nt = seq // tt
    tok = lambda b, t: (b * nt + t, 0)
    consts = _gla_constants(1, GLA_CHUNK, nh * dk)
    args, specs = _ab_specs(kind, p16, p32, extra, tt, lambda b, t: b * nt + t, nh, dk)
    n_in = len(args)
    args += [gn.reshape(1, LANES), *consts]
    specs += [_resident((1, LANES))] + [_resident(c.shape) for c in consts]
    scratch = [pltpu.VMEM((nh * dk // LANES, LANES, LANES), F32)]
    if project is None:
        out = jax.ShapeDtypeStruct((batch * seq, nh * LANES), BF16)
    else:
        y_prev, x, w_out, layer = project
        d = x.shape[1]
        args += [y_prev, x, w_out]
        specs += [pl.BlockSpec((tt, y_prev.shape[1]), tok), pl.BlockSpec((tt, d), tok),
                  _out_weight_spec(w_out, layer, 0, w_out.shape[1])]
        scratch += [pltpu.VMEM((tt, nh * LANES), BF16), pltpu.VMEM((w_out.shape[1], d), BF16)]
        out = jax.ShapeDtypeStruct((batch * seq, d), F32)
    return pl.pallas_call(
        functools.partial(_ab_prompt_kernel, kind=kind, n_in=n_in, chunk=GLA_CHUNK, q_scale=dk ** -0.5,
                          project=project is not None),
        out_shape=[out, jax.ShapeDtypeStruct((batch, nh, dk, LANES), F32)],
        grid=(batch, nt),
        in_specs=specs,
        out_specs=[pl.BlockSpec((tt, out.shape[1]), tok),
                   pl.BlockSpec((1, nh, dk, LANES), lambda b, t: (b, 0, 0, 0))],
        scratch_shapes=scratch,
        compiler_params=_params(2, 40 << 20),
        name=name,
    )(*args)


def _ab_sample_call(kind, p16, p32, extra, gn, s0, row0, seq, name, project=None):
    batch, nh, dk, _ = s0.shape
    nb = SAMPLE_SEQS
    rows = nb * seq
    blk0 = row0 // rows
    consts = _gla_constants(nb, seq, nh * dk)
    args, specs = _ab_specs(kind, p16, p32, extra, rows, lambda i: blk0 + i, nh, dk)
    n_in = len(args)
    args += [gn.reshape(1, LANES), *consts, s0]
    state_spec = pl.BlockSpec((nb, nh, dk, LANES), lambda i: (i, 0, 0, 0))
    specs += [_resident((1, LANES))] + [_resident(c.shape) for c in consts] + [state_spec]
    scratch = []
    if project is None:
        out = jax.ShapeDtypeStruct((batch * seq, nh * LANES), BF16)
    else:
        y_prev, x, w_out, layer = project
        d = x.shape[1]
        args += [y_prev, x, w_out]
        specs += [pl.BlockSpec((rows, y_prev.shape[1]), lambda i: (i, 0)),
                  pl.BlockSpec((rows, d), lambda i: (blk0 + i, 0)), _out_weight_spec(w_out, layer, 0, w_out.shape[1])]
        scratch = [pltpu.VMEM((w_out.shape[1], d), BF16)]
        out = jax.ShapeDtypeStruct((batch * seq, d), F32)
    return pl.pallas_call(
        functools.partial(_ab_sample_kernel, kind=kind, n_in=n_in, ls=seq, q_scale=dk ** -0.5,
                          project=project is not None),
        out_shape=[out, jax.ShapeDtypeStruct((batch, nh, dk, LANES), F32)],
        grid=(batch // nb,),
        in_specs=specs,
        out_specs=[pl.BlockSpec((rows, out.shape[1]), lambda i: (i, 0)), state_spec],
        scratch_shapes=scratch,
        compiler_params=_params(1, 40 << 20),
        name=name,
    )(*args)


def _ret_constants(n_seq, ls):
    c = n_seq * ls
    log_gamma = np.log1p(-np.exp2(-5.0 - np.arange(H_C, dtype=np.float64)))[:, None, None]
    t = np.arange(c)[:, None]
    s = np.arange(c)[None, :]
    causal = ((t // ls) == (s // ls)) & (s <= t)
    dmask = np.where(causal, np.exp(log_gamma * np.where(causal, t - s, 0)), 0.0)
    pos = (np.arange(c) % ls)[None, :, None]
    scales = np.stack([np.exp(log_gamma * (pos + 1)), np.exp(log_gamma * (ls - 1 - pos))], axis=1)
    scales = np.broadcast_to(scales, (H_C, 2, c, LANES))
    return jnp.asarray(dmask, F32), jnp.asarray(scales, F32)


def _rope_tables(positions, half):
    inv = ROPE_THETA ** (-jnp.arange(half, dtype=F32) / half)
    ang = positions.astype(F32)[:, None] * inv[None, :]
    return jnp.cos(ang), jnp.sin(ang)


def _rope(x, cos, sin):
    half = x.shape[1] // 2
    x1, x2 = x[:, :half], x[:, half:]
    return jnp.concatenate([x1 * cos - x2 * sin, x2 * cos + x1 * sin], axis=1)


def _ret_block(q16, k16, v, cos, sin, dmask, q_scale, k_scale, s_list, ls):
    c, dk = q16.shape
    reps = dk // LANES
    n_seq = c // ls
    scaled_dtype = BF16 if n_seq == 1 else F32
    qb, kb, qs, ks = [], [], [], []
    for r in range(0, c, ROW_PIECE):
        rows = slice(r, r + ROW_PIECE)
        q = _rope(q16[rows].astype(F32), cos[rows], sin[rows])
        k = _rope(k16[rows].astype(F32), cos[rows], sin[rows]) * (dk ** -0.5)
        qb.append(q.astype(BF16))
        kb.append(k.astype(BF16))
        qs.append((q * jnp.concatenate([q_scale[rows]] * reps, axis=1)).astype(scaled_dtype))
        ks.append((k * jnp.concatenate([k_scale[rows]] * reps, axis=1)).astype(scaled_dtype))
    qb, kb, qs, ks = (jnp.concatenate(p, axis=0) for p in (qb, kb, qs, ks))
    att = _dot_nt(qb, kb) * dmask
    o = _dot(att.astype(BF16), v)
    decay = q_scale[ls - 1:ls, 0:1]
    v32 = v if n_seq == 1 else v.astype(F32)
    o_parts, s_new = [], []
    for b in range(n_seq):
        rows = slice(b * ls, (b + 1) * ls)
        s = s_list[b]
        o_parts.append(o[rows] + _dot(qs[rows].astype(BF16), s.astype(BF16)))
        s_new.append(decay * s + _dot_tn(ks[rows].astype(BF16), v32[rows].astype(BF16)))
    o = o_parts[0] if n_seq == 1 else jnp.concatenate(o_parts, axis=0)
    return o, s_new


def _ret_head_out(o, gate, gn):
    pieces = []
    for r in range(0, o.shape[0], ROW_PIECE):
        op = o[r:r + ROW_PIECE]
        oc = op - jnp.mean(op, axis=-1, keepdims=True)
        y = oc * lax.rsqrt(jnp.mean(oc * oc, axis=-1, keepdims=True) + EPS) * gn
        pieces.append((y * _silu(gate[r:r + ROW_PIECE].astype(F32))).astype(BF16))
    return jnp.concatenate(pieces, axis=0)


def _ret_prompt_kernel(q_ref, k_ref, v_ref, g_ref, cos_ref, sin_ref, dm_ref, sc_ref, gn_ref, x_ref, wo_ref,
                       xo_ref, s_ref, st_ref, y_s, wo_s):
    t = pl.program_id(1)
    nh, dk, dv = st_ref.shape
    c = dm_ref.shape[1]
    _stage_out_weight(wo_ref, wo_s, jnp.logical_and(pl.program_id(0) == 0, t == 0))

    @pl.when(t == 0)
    def _():
        st_ref[...] = jnp.zeros_like(st_ref)

    for h in range(nh):
        s = st_ref[h]
        for j in range(q_ref.shape[0] // c):
            rows = slice(j * c, (j + 1) * c)
            o, (s,) = _ret_block(q_ref[rows, h * dk:(h + 1) * dk], k_ref[rows, h * dk:(h + 1) * dk],
                                 v_ref[rows, h * dv:(h + 1) * dv], cos_ref[rows, :], sin_ref[rows, :], dm_ref[h],
                                 sc_ref[h, 0], sc_ref[h, 1], [s], c)
            y_s[rows, h * dv:(h + 1) * dv] = _ret_head_out(o, g_ref[rows, h * dv:(h + 1) * dv],
                                                           gn_ref[...])
        st_ref[h] = s
    xo_ref[...] = x_ref[...] + _dot(y_s[...], wo_s[...])

    @pl.when(t == pl.num_programs(1) - 1)
    def _():
        s_ref[0] = st_ref[...]


def _ret_sample_kernel(q_ref, k_ref, v_ref, g_ref, cos_ref, sin_ref, dm_ref, sc_ref, gn_ref, s0_ref, x_ref,
                       wo_ref, xo_ref, s_ref, wo_s, *, ls):
    h = pl.program_id(1)
    dv = v_ref.shape[1]
    _stage_out_weight(wo_ref, wo_s, jnp.logical_and(pl.program_id(0) == 0, h == 0))
    n_seq = q_ref.shape[0] // ls
    o, s = _ret_block(q_ref[...], k_ref[...], v_ref[...], cos_ref[...], sin_ref[...], dm_ref[0],
                      sc_ref[0, 0], sc_ref[0, 1], [s0_ref[b, 0] for b in range(n_seq)], ls)
    y = _ret_head_out(o, g_ref[...], gn_ref[...])
    proj = _dot(y, wo_s[pl.ds(pl.multiple_of(h * dv, dv), dv), :])

    @pl.when(h == 0)
    def _():
        xo_ref[...] = x_ref[...] + proj

    @pl.when(h > 0)
    def _():
        xo_ref[...] += proj

    for b in range(n_seq):
        s_ref[b, 0] = s[b]


def _ret_prompt_call(r16, gn, x, w_out, layer, batch, seq, dk, dv, name):
    c = RET_STEP_TOKENS
    nt = seq // c
    d = x.shape[1]
    dmask, scales = _ret_constants(1, RET_CHUNK)
    cos, sin = _rope_tables(jnp.arange(seq, dtype=jnp.int32), dk // 2)
    tok = lambda b, t: b * nt + t
    nq, nv = H_C * dk, H_C * dv
    args = [r16, r16, r16, r16, cos, sin, dmask, scales, gn.reshape(1, dv), x, w_out]
    specs = [pl.BlockSpec((c, nq), lambda b, t: (tok(b, t), 0)),
             pl.BlockSpec((c, nq), lambda b, t: (tok(b, t), 1)),
             pl.BlockSpec((c, nv), lambda b, t: (tok(b, t), 2 * nq // nv)),
             pl.BlockSpec((c, nv), lambda b, t: (tok(b, t), 2 * nq // nv + 1)),
             pl.BlockSpec((c, dk // 2), lambda b, t: (t, 0)), pl.BlockSpec((c, dk // 2), lambda b, t: (t, 0)),
             _resident(dmask.shape), _resident(scales.shape), _resident((1, dv)),
             pl.BlockSpec((c, d), lambda b, t: (tok(b, t), 0)), _out_weight_spec(w_out, layer, 0, nv)]
    return pl.pallas_call(
        _ret_prompt_kernel,
        out_shape=[jax.ShapeDtypeStruct((batch * seq, d), F32),
                   jax.ShapeDtypeStruct((batch, H_C, dk, dv), F32)],
        grid=(batch, nt),
        in_specs=specs,
        out_specs=[pl.BlockSpec((c, d), lambda b, t: (tok(b, t), 0)),
                   pl.BlockSpec((1, H_C, dk, dv), lambda b, t: (b, 0, 0, 0))],
        scratch_shapes=[pltpu.VMEM((H_C, dk, dv), F32), pltpu.VMEM((c, nv), BF16), pltpu.VMEM((nv, d), BF16)],
        compiler_params=_params(2, 48 << 20),
        name=name,
    )(*args)


def _ret_sample_call(r16, gn, s0, row0, batch, seq, dk, dv, x, w_out, layer, name):
    nb = SAMPLE_SEQS
    rows = nb * seq
    blk0 = row0 // rows
    d = x.shape[1]
    dmask, scales = _ret_constants(nb, seq)
    pos = PAST_LEN + (jnp.arange(rows, dtype=jnp.int32) % seq)
    cos, sin = _rope_tables(pos, dk // 2)
    nq = H_C * dk

    def col(width, base):
        return pl.BlockSpec((rows, width), lambda i, h: (blk0 + i, base + h))

    state_spec = pl.BlockSpec((nb, 1, dk, dv), lambda i, h: (i, h, 0, 0))
    args = [r16, r16, r16, r16, cos, sin, dmask, scales, gn.reshape(1, dv), s0, x, w_out]
    specs = [col(dk, 0), col(dk, H_C), col(dv, 2 * nq // dv), col(dv, 2 * nq // dv + H_C),
             _resident(cos.shape), _resident(sin.shape),
             pl.BlockSpec((1, rows, rows), lambda i, h: (h, 0, 0)),
             pl.BlockSpec((1, 2, rows, LANES), lambda i, h: (h, 0, 0, 0)), _resident((1, dv)), state_spec,
             pl.BlockSpec((rows, d), lambda i, h: (blk0 + i, 0)), _out_weight_spec(w_out, layer, 0, H_C * dv)]
    return pl.pallas_call(
        functools.partial(_ret_sample_kernel, ls=seq),
        out_shape=[jax.ShapeDtypeStruct((batch * seq, d), F32),
                   jax.ShapeDtypeStruct((batch, H_C, dk, dv), F32)],
        grid=(batch // nb, H_C),
        in_specs=specs,
        out_specs=[pl.BlockSpec((rows, d), lambda i, h: (i, 0)), state_spec],
        scratch_shapes=[pltpu.VMEM((H_C * dv, d), BF16)],
        compiler_params=_params(2, 56 << 20),
        name=name,
    )(*args)


def _prepare_ab_weights(w_in, w_gk2, b_gk, dk_a, dv_a, dk_b, dv_b):
    sizes = (H_A * dk_a, H_A * dk_a, H_A * dv_a, H_A * dv_a, H_B * dk_b, H_B * dk_b, H_B * dv_b,
             GLA_RANK, H_B * dv_b)
    qa, fa, ia, ga, qb, kb, vb, lrb, gb = jnp.split(w_in, [int(v) for v in np.cumsum(sizes)[:-1]], axis=1)
    w16 = jnp.concatenate([qa, ia, ga, qb, kb, vb, gb], axis=1).astype(BF16)
    w32 = jnp.concatenate([fa, jnp.pad(lrb, ((0, 0), (0, LANES - GLA_RANK)))], axis=1).astype(BF16)
    w_gk = jnp.pad(w_gk2, ((0, LANES - GLA_RANK), (0, 0))).astype(BF16)
    return w16, w32, w_gk, b_gk.reshape(1, -1)


def kernel(x_prompt, x_sample, state_hgrn, state_gla, state_ret, p_prompt, p_sample, norm_ffn1, ffn1_w_in, ffn1_w_out, norm_mix, ab_w_in, ab_w_gk2, ab_b_gk, hgrn_lb_logits, ab_gn_hgrn, ab_gn_gla, ab_w_out, ret_w_in, ret_gn, ret_w_out, norm_ffn2, ffn2_w_in, ffn2_w_out, norm_ple, ple_w_gate, ple_w_proj, norm_final):
    bp, lp, d = x_prompt.shape
    bs, ls, _ = x_sample.shape
    depth = norm_ffn1.shape[0]
    n_prompt = bp * lp
    dk_a, dv_a = state_hgrn.shape[-2:]
    dk_b, dv_b = state_gla.shape[-2:]
    dk_c, dv_c = state_ret.shape[-2:]
    assert dk_a == dv_a == dv_b == LANES and LANES % dk_b == 0 and H_B % (LANES // dk_b) == 0
    assert state_hgrn.shape[0] == 1

    n_rows = (n_prompt, bs * ls)
    xs = (x_prompt.reshape(n_prompt, d), x_sample.reshape(bs * ls, d))
    ps = (p_prompt.reshape(depth, n_prompt, -1), p_sample.reshape(depth, bs * ls, -1))

    new_hgrn_p, new_gla_p, new_ret_p, new_hgrn_s, new_gla_s, new_ret_s = [], [], [], [], [], []
    for i in range(depth):
        j = i // 2
        x = _ffn_call(xs if i == 0 else (x,), n_rows, norm_ffn1[i], ffn1_w_in, ffn1_w_out, i, name=f"ffn1_{i}")
        if i % 2 == 0:
            w16, w32, w_gk, b_gk = _prepare_ab_weights(ab_w_in[j], ab_w_gk2[j], ab_b_gk[j], dk_a, dv_a, dk_b, dv_b)
            p16, p32 = _inproj_call(x, norm_mix[i], [w16, w32], [BF16, F32], [w16.shape[1] // 2, w32.shape[1]],
                                    name=f"ab_in_{i}")
            hgrn = ("hgrn", p16, p32, (hgrn_lb_logits,), ab_gn_hgrn[j])
            gla = ("gla", p16, p32, (w_gk, b_gk), ab_gn_gla[j])
            ya_p, sa_p = _ab_prompt_call(*hgrn, bp, lp, dk_a, f"hgrn_p_{i}")
            ya_s, sa_s = _ab_sample_call(*hgrn, state_hgrn[j], n_prompt, ls, f"hgrn_s_{i}")
            x_p, sb_p = _ab_prompt_call(*gla, bp, lp, dk_b, f"gla_p_{i}", project=(ya_p, x, ab_w_out, j))
            x_s, sb_s = _ab_sample_call(*gla, state_gla[j], n_prompt, ls, f"gla_s_{i}",
                                        project=(ya_s, x, ab_w_out, j))
            new_hgrn_p.append(sa_p)
            new_hgrn_s.append(sa_s)
            new_gla_p.append(sb_p)
            new_gla_s.append(sb_s)
        else:
            r16 = _inproj_staged_call(x, norm_mix[i], ret_w_in, j, 2 * V7X_MXU_COLS, name=f"ret_in_{i}")
            x_p, sc_p = _ret_prompt_call(r16, ret_gn[j], x, ret_w_out, j, bp, lp, dk_c, dv_c, f"ret_p_{i}")
            x_s, sc_s = _ret_sample_call(r16, ret_gn[j], state_ret[j], n_prompt, bs, ls, dk_c, dv_c, x, ret_w_out, j,
                                         f"ret_s_{i}")
            new_ret_p.append(sc_p)
            new_ret_s.append(sc_s)
        last = i == depth - 1
        x = _ffn_call((x_p, x_s), n_rows, norm_ffn2[i], ffn2_w_in, ffn2_w_out, i,
                      ple=(ps, norm_ple[i], ple_w_gate, ple_w_proj),
                      final_gain=norm_final if last else None, split_out=last, name=f"ffn2_{i}")

    y_prompt = x[0].reshape(bp, lp, d)
    y_sample = x[1].reshape(bs, ls, d)
    def layers(states):
        return states[0][None] if len(states) == 1 else jnp.stack(states)

    return (y_prompt, y_sample, layers(new_hgrn_p), layers(new_gla_p), layers(new_ret_p),
            layers(new_hgrn_s), layers(new_gla_s), layers(new_ret_s))
```
